```python
import math
import jax, jax.numpy as jnp
from jax import lax
import numpy as np

D_MODEL = 2048
BATCH = 2
SEQ = 8192
DEPTH = 2

N_MIXERS = 2
N_FOX_LAYERS = (DEPTH + 1) // 2
N_GDN_LAYERS = DEPTH // 2
EPS = 1e-6

FOX_HEAD_DIM = 128
FOX_HEADS = D_MODEL // FOX_HEAD_DIM
FOX_Q = FOX_HEADS * FOX_HEAD_DIM
FOX_IN = 3 * FOX_Q + FOX_HEADS + FOX_Q
FOX_F_BIAS_INIT = 3.0
Q_BLOCK = 128

GDN_HEAD_DIM_K = 128
GDN_HEAD_DIM_V = 128
GDN_QK_HEADS = D_MODEL // 128
GDN_V_HEADS = 2 * GDN_QK_HEADS
GDN_KD = GDN_QK_HEADS * GDN_HEAD_DIM_K
GDN_VD = GDN_V_HEADS * GDN_HEAD_DIM_V
GDN_CONV_DIM = 2 * GDN_KD + GDN_VD
GDN_IN = 2 * GDN_KD + 2 * GDN_VD + 2 * GDN_V_HEADS
GDN_CONV = 4
GDN_CHUNK = 64

D_FF = 11 * D_MODEL // 4
FFN_CONV = 3

kernel_name = "fox_gdn_interleaved_convglu_adaln_sandwich"


def rmsnorm(x, g):
    xf = x.astype(jnp.float32)
    y = xf * lax.rsqrt(jnp.mean(xf * xf, axis=-1, keepdims=True) + EPS)
    return (y * g.astype(jnp.float32)).astype(x.dtype)


def l2norm(x):
    return x * lax.rsqrt(jnp.sum(x * x, axis=-1, keepdims=True) + EPS)


def causal_dwconv(x, w):
    K = w.shape[0]
    S = x.shape[1]
    xp = jnp.pad(x, ((0, 0), (K - 1, 0), (0, 0)))
    return sum(xp[:, k:k + S] * w[k] for k in range(K))


def fox_attention(h, w_in, f_bias, q_norm_g, k_norm_g, w_o):
    B, S, _ = h.shape
    H, Dh = FOX_HEADS, FOX_HEAD_DIM
    proj = h @ w_in
    q, k, v, f_logit, o_gate = jnp.split(proj, [FOX_Q, 2 * FOX_Q, 3 * FOX_Q, 3 * FOX_Q + H], axis=-1)
    qh = rmsnorm(q.reshape(B, S, H, Dh), q_norm_g).transpose(0, 2, 1, 3)
    kh = rmsnorm(k.reshape(B, S, H, Dh), k_norm_g).transpose(0, 2, 1, 3)
    vh = v.reshape(B, S, H, Dh).transpose(0, 2, 1, 3)
    log_f = jax.nn.log_sigmoid(f_logit.astype(jnp.float32) + f_bias.astype(jnp.float32))
    f_cum = jnp.cumsum(log_f, axis=1).transpose(0, 2, 1)
    scale = Dh ** -0.5
    key_pos = jnp.arange(S)

    def block(i):
        start = i * Q_BLOCK
        qb = lax.dynamic_slice_in_dim(qh, start, Q_BLOCK, axis=2)
        fq = lax.dynamic_slice_in_dim(f_cum, start, Q_BLOCK, axis=2)
        s = jnp.einsum('bhqd,bhkd->bhqk', qb, kh, preferred_element_type=jnp.float32) * scale
        s = s + fq[..., :, None] - f_cum[..., None, :]
        q_pos = start + jnp.arange(Q_BLOCK)
        s = jnp.where(key_pos[None, :] <= q_pos[:, None], s, -jnp.inf)
        p = jax.nn.softmax(s, axis=-1)
        return jnp.einsum('bhqk,bhkd->bhqd', p.astype(vh.dtype), vh)

    out = lax.map(block, jnp.arange(S // Q_BLOCK))
    out = out.transpose(1, 0, 3, 2, 4).reshape(B, S, H * Dh)
    out = out * jax.nn.sigmoid(o_gate)
    return out @ w_o


def chunk_gated_delta_rule(q, k, v, g, beta):
    B, S, H, Dk = q.shape
    Dv = v.shape[-1]
    C = GDN_CHUNK
    N = S // C

    def to_chunks(t):
        return t.reshape(B, N, C, H, -1).transpose(0, 3, 1, 2, 4)

    q, k, v = to_chunks(q), to_chunks(k), to_chunks(v)
    beta = beta.reshape(B, N, C, H).transpose(0, 3, 1, 2)
    g = jnp.cumsum(g.reshape(B, N, C, H).transpose(0, 3, 1, 2), axis=-1)
    idx = jnp.arange(C)
    lower_incl = idx[:, None] >= idx[None, :]
    strict = idx[:, None] > idx[None, :]
    decay_mat = jnp.exp(jnp.where(lower_incl, g[..., :, None] - g[..., None, :], -jnp.inf))
    kk = jnp.einsum('bhncd,bhnjd->bhncj', k, k)
    A = jnp.where(strict, kk * decay_mat * beta[..., :, None], 0.0)
    eye = jnp.eye(C, dtype=jnp.float32)
    rhs = jnp.concatenate([v * beta[..., None], k * (beta * jnp.exp(g))[..., None]], axis=-1)
    sol = lax.linalg.triangular_solve(eye + A, rhs, left_side=True, lower=True)
    u, w = jnp.split(sol, [Dv], axis=-1)
    qk = jnp.where(lower_incl, jnp.einsum('bhncd,bhnjd->bhncj', q, k) * decay_mat, 0.0)
    q_dec = q * jnp.exp(g)[..., None]
    g_last = g[..., -1]
    k_dec = k * jnp.exp(g_last[..., None] - g)[..., None]

    def step(state, inp):
        q_d, k_d, w_c, u_c, qk_c, gl = inp
        v_new = u_c - jnp.einsum('bhcd,bhde->bhce', w_c, state)
        o = jnp.einsum('bhcd,bhde->bhce', q_d, state) + jnp.einsum('bhcj,bhje->bhce', qk_c, v_new)
        state = state * jnp.exp(gl)[..., None, None] + jnp.einsum('bhcd,bhce->bhde', k_d, v_new)
        return state, o

    xs = (jnp.moveaxis(q_dec, 2, 0), jnp.moveaxis(k_dec, 2, 0), jnp.moveaxis(w, 2, 0),
          jnp.moveaxis(u, 2, 0), jnp.moveaxis(qk, 2, 0), jnp.moveaxis(g_last, 2, 0))
    state0 = jnp.zeros((B, H, Dk, Dv), jnp.float32)
    _, o = lax.scan(step, state0, xs)
    return o.transpose(1, 0, 3, 2, 4).reshape(B, S, H, Dv)


def gated_deltanet(h, w_in, conv_w, a_log, dt_bias, out_norm_g, w_o):
    B, S, _ = h.shape
    Hk, Hv, Dk, Dv = GDN_QK_HEADS, GDN_V_HEADS, GDN_HEAD_DIM_K, GDN_HEAD_DIM_V
    proj = h @ w_in
    qkv, z, a, b = jnp.split(proj, [GDN_CONV_DIM, GDN_CONV_DIM + GDN_VD, GDN_CONV_DIM + GDN_VD + Hv], axis=-1)
    qkv = jax.nn.silu(causal_dwconv(qkv, conv_w))
    q, k, v = jnp.split(qkv, [GDN_KD, 2 * GDN_KD], axis=-1)
    q = l2norm(q.reshape(B, S, Hk, Dk).astype(jnp.float32)) * (Dk ** -0.5)
    k = l2norm(k.reshape(B, S, Hk, Dk).astype(jnp.float32))
    q = jnp.repeat(q, Hv // Hk, axis=2)
    k = jnp.repeat(k, Hv // Hk, axis=2)
    v = v.reshape(B, S, Hv, Dv).astype(jnp.float32)
    beta = jax.nn.sigmoid(b.astype(jnp.float32))
    g = -jnp.exp(a_log.astype(jnp.float32)) * jax.nn.softplus(a.astype(jnp.float32) + dt_bias.astype(jnp.float32))
    o = chunk_gated_delta_rule(q, k, v, g, beta)
    o = rmsnorm(o, out_norm_g) * jax.nn.silu(z.reshape(B, S, Hv, Dv).astype(jnp.float32))
    return o.reshape(B, S, GDN_VD).astype(h.dtype) @ w_o


def conv_glu(h, w_up, conv_w, conv_b, w_down):
    gate, val = jnp.split(h @ w_up, 2, axis=-1)
    gate = causal_dwconv(gate, conv_w) + conv_b
    return (jax.nn.gelu(gate, approximate=False) * val) @ w_down


def modulate(x, g_pre, mod):
    shift, scale, gate = jnp.split(mod[:, None, :], 3, axis=-1)
    return rmsnorm(x, g_pre) * (1 + scale) + shift, gate


def setup_inputs(seed: int = 0) -> dict:
    key = jax.random.key(seed)
    ks = iter(jax.random.split(key, 32))
    NF, NG = N_FOX_LAYERS, N_GDN_LAYERS

    def nrm(shape, scale):
        return jax.random.normal(next(ks), shape, jnp.float32) * scale

    def gain(shape):
        return 1.0 + 0.02 * jax.random.normal(next(ks), shape, jnp.float32)

    x = nrm((BATCH, SEQ, D_MODEL), 1.0)
    c = nrm((BATCH, D_MODEL), 1.0)
    ada_w = nrm((DEPTH, 2, D_MODEL, 3 * D_MODEL), 0.5 * D_MODEL ** -0.5)
    ada_b = nrm((DEPTH, 2, 3 * D_MODEL), 0.02)
    norm_g = gain((DEPTH, 4, D_MODEL))
    fox_w_in = nrm((NF, D_MODEL, FOX_IN), D_MODEL ** -0.5)
    fox_f_bias = FOX_F_BIAS_INIT + nrm((NF, FOX_HEADS), 0.1)
    fox_q_norm = gain((NF, FOX_HEAD_DIM))
    fox_k_norm = gain((NF, FOX_HEAD_DIM))
    fox_w_o = nrm((NF, FOX_Q, D_MODEL), FOX_Q ** -0.5)
    gdn_w_in = nrm((NG, D_MODEL, GDN_IN), D_MODEL ** -0.5)
    gdn_conv_w = nrm((NG, GDN_CONV, GDN_CONV_DIM), GDN_CONV ** -0.5)
    gdn_a_log = jnp.log(jax.random.uniform(next(ks), (NG, GDN_V_HEADS), jnp.float32, 1.0, 16.0))
    dt = jnp.exp(jax.random.uniform(next(ks), (NG, GDN_V_HEADS), jnp.float32, math.log(1e-3), math.log(1e-1)))
    gdn_dt_bias = dt + jnp.log(-jnp.expm1(-dt))
    gdn_out_norm = gain((NG, GDN_HEAD_DIM_V))
    gdn_w_o = nrm((NG, GDN_VD, D_MODEL), GDN_VD ** -0.5)
    ffn_w_up = nrm((DEPTH, D_MODEL, 2 * D_FF), D_MODEL ** -0.5)
    ffn_conv_w = nrm((DEPTH, FFN_CONV, D_FF), FFN_CONV ** -0.5)
    ffn_conv_b = nrm((DEPTH, D_FF), 0.02)
    ffn_w_down = nrm((DEPTH, D_FF, D_MODEL), D_FF ** -0.5)
    return {"x": x, "c": c, "ada_w": ada_w, "ada_b": ada_b, "norm_g": norm_g,
            "fox_w_in": fox_w_in, "fox_f_bias": fox_f_bias, "fox_q_norm": fox_q_norm,
            "fox_k_norm": fox_k_norm, "fox_w_o": fox_w_o,
            "gdn_w_in": gdn_w_in, "gdn_conv_w": gdn_conv_w, "gdn_a_log": gdn_a_log,
            "gdn_dt_bias": gdn_dt_bias, "gdn_out_norm": gdn_out_norm, "gdn_w_o": gdn_w_o,
            "ffn_w_up": ffn_w_up, "ffn_conv_w": ffn_conv_w, "ffn_conv_b": ffn_conv_b,
            "ffn_w_down": ffn_w_down}


def reference(x, c, ada_w, ada_b, norm_g, fox_w_in, fox_f_bias, fox_q_norm, fox_k_norm, fox_w_o,
              gdn_w_in, gdn_conv_w, gdn_a_log, gdn_dt_bias, gdn_out_norm, gdn_w_o,
              ffn_w_up, ffn_conv_w, ffn_conv_b, ffn_w_down):
    c_act = jax.nn.silu(c)
    for i in range(DEPTH):
        mods = jnp.einsum('bd,sde->sbe', c_act, ada_w[i]) + ada_b[i][:, None, :]
        h, gate = modulate(x, norm_g[i, 0], mods[0])
        j = i // N_MIXERS
        if i % N_MIXERS == 0:
            y = fox_attention(h, fox_w_in[j], fox_f_bias[j], fox_q_norm[j], fox_k_norm[j], fox_w_o[j])
        else:
            y = gated_deltanet(h, gdn_w_in[j], gdn_conv_w[j], gdn_a_log[j], gdn_dt_bias[j],
                               gdn_out_norm[j], gdn_w_o[j])
        x = x + gate * rmsnorm(y, norm_g[i, 1])
        h, gate = modulate(x, norm_g[i, 2], mods[1])
        y = conv_glu(h, ffn_w_up[i], ffn_conv_w[i], ffn_conv_b[i], ffn_w_down[i])
        x = x + gate * rmsnorm(y, norm_g[i, 3])
    return x
```

```python
import functools

import jax
import jax.numpy as jnp
from jax import lax
from jax.experimental import pallas as pl
from jax.experimental.pallas import tpu as pltpu

F32 = jnp.float32
BF16 = jnp.bfloat16
EPS = 1e-6
HEAD = 128
CHUNK = 64
SUBLANES = 8
VMEM_LIMIT = 56 * 1024 * 1024
INV_SQRT2 = 0.7071067811865476


def _cparams(*sem):
    return pltpu.CompilerParams(dimension_semantics=sem, vmem_limit_bytes=VMEM_LIMIT)


def _dot(a, b):
    return jnp.dot(a, b, preferred_element_type=F32)


def _dot_nt(a, b):
    return lax.dot_general(a, b, (((1,), (1,)), ((), ())), preferred_element_type=F32)


def _dot_tn(a, b):
    return lax.dot_general(a, b, (((0,), (0,)), ((), ())), preferred_element_type=F32)


def _dot_f32(a, b):
    return jnp.dot(a, b, preferred_element_type=F32, precision=lax.Precision.HIGHEST)


def _softplus(x):
    return jnp.maximum(x, 0.0) + jnp.log1p(jnp.exp(-jnp.abs(x)))


def _ada_kernel(c_ref, w_ref, b_ref, o_ref):
    c = c_ref[...]
    o_ref[0] = _dot_f32(c * jax.nn.sigmoid(c), w_ref[0]) + b_ref[0]


def _ada_mods(c, ada_w, ada_b):
    nb, d = c.shape
    n = ada_w.shape[0] * ada_w.shape[1]
    w = ada_w.reshape(n, d, 3 * d)
    b = ada_b.reshape(n, 1, 3 * d)
    cp = jnp.zeros((SUBLANES, d), F32).at[:nb].set(c)
    tn = 1024
    out = pl.pallas_call(
        _ada_kernel,
        grid=(n, 3 * d // tn),
        in_specs=[pl.BlockSpec((SUBLANES, d), lambda s, j: (0, 0)),
                  pl.BlockSpec((1, d, tn), lambda s, j: (s, 0, j)),
                  pl.BlockSpec((1, 1, tn), lambda s, j: (s, 0, j))],
        out_specs=pl.BlockSpec((1, SUBLANES, tn), lambda s, j: (s, 0, j)),
        out_shape=jax.ShapeDtypeStruct((n, SUBLANES, 3 * d), F32),
        compiler_params=_cparams("parallel", "parallel"),
        name="ada_mods",
    )(cp, w, b)
    return out[:, :nb].reshape(n, nb, 1, 3 * d)


def _norm_mod(h_ref, x_ref, g_ref, sh_ref, sc_ref):
    x = x_ref[...]
    ms = jnp.mean(x * x, axis=-1, keepdims=True)
    y = x * lax.rsqrt(ms + EPS) * g_ref[...]
    h_ref[...] = (y * (1.0 + sc_ref[...]) + sh_ref[...]).astype(BF16)


def _head_norm_store(o_ref, y, gain, mean):
    for c in range(y.shape[1] // HEAD):
        sl = slice(c * HEAD, (c + 1) * HEAD)
        t = y[:, sl]
        ss = jnp.sum(t * t, axis=-1, keepdims=True)
        if mean:
            ss = ss * (1.0 / HEAD)
        gc = gain[:, sl] if getattr(gain, "ndim", 0) == 2 else gain
        o_ref[:, sl] = (t * lax.rsqrt(ss + EPS) * gc).astype(o_ref.dtype)


def _causal_conv(acc, cw_ref, buf_ref, carry_ref, j, first, taps):
    tm = acc.shape[0]
    prev = jnp.where(first, 0.0, carry_ref[j])
    buf_ref[0:SUBLANES, :] = prev
    buf_ref[SUBLANES:SUBLANES + tm, :] = acc
    carry_ref[j] = acc[tm - SUBLANES:tm, :]
    y = cw_ref[taps - 1:taps, :] * acc
    for k in range(taps - 1):
        y = y + cw_ref[k:k + 1, :] * buf_ref[pl.ds(SUBLANES - (taps - 1) + k, tm), :]
    return y


def _fox_in_kernel(x_ref, g_ref, sh_ref, sc_ref, w_ref, gn_ref, ws_ref, o_ref, os_ref, h_ref, *, n_norm):
    j = pl.program_id(1)

    @pl.when(j == 0)
    def _():
        _norm_mod(h_ref, x_ref, g_ref, sh_ref, sc_ref)
        os_ref[...] = _dot(h_ref[...], ws_ref[...])

    acc = _dot(h_ref[...], w_ref[...])

    @pl.when(j < n_norm)
    def _():
        _head_norm_store(o_ref, acc, gn_ref[...], mean=True)

    @pl.when(j >= n_norm)
    def _():
        o_ref[...] = acc.astype(o_ref.dtype)


def _gdn_in_kernel(x_ref, g_ref, sh_ref, sc_ref, w_ref, cw_ref, ws_ref, o_ref, os_ref,
                   h_ref, buf_ref, carry_ref, *, n_q, n_qk, n_conv, blocks_per_seq, taps):
    i = pl.program_id(0)
    j = pl.program_id(1)

    @pl.when(j == 0)
    def _():
        _norm_mod(h_ref, x_ref, g_ref, sh_ref, sc_ref)
        os_ref[...] = _dot(h_ref[...], ws_ref[...])

    acc = _dot(h_ref[...], w_ref[...])

    @pl.when(j < n_conv)
    def _():
        first = (i % blocks_per_seq) == 0
        y = _causal_conv(acc, cw_ref, buf_ref, carry_ref, j, first, taps)
        y = y * jax.nn.sigmoid(y)

        @pl.when(j < n_qk)
        def _():
            qscale = jnp.where(j < n_q, HEAD ** -0.5, 1.0).astype(F32)
            _head_norm_store(o_ref, y, qscale, mean=False)

        @pl.when(j >= n_qk)
        def _():
            o_ref[...] = y.astype(o_ref.dtype)

    @pl.when(j >= n_conv)
    def _():
        o_ref[...] = acc.astype(o_ref.dtype)


def _ffn_up_kernel(x_ref, g_ref, sh_ref, sc_ref, wg_ref, wv_ref, cw_ref, cb_ref, o_ref,
                   h_ref, buf_ref, carry_ref, *, blocks_per_seq, taps):
    i = pl.program_id(0)
    j = pl.program_id(1)

    @pl.when(j == 0)
    def _():
        _norm_mod(h_ref, x_ref, g_ref, sh_ref, sc_ref)

    h = h_ref[...]
    gate = _dot(h, wg_ref[...])
    first = (i % blocks_per_seq) == 0
    gate = _causal_conv(gate, cw_ref, buf_ref, carry_ref, j, first, taps) + cb_ref[...]
    act = 0.5 * gate * (1.0 + lax.erf(gate * INV_SQRT2))
    o_ref[...] = (act * _dot(h, wv_ref[...])).astype(o_ref.dtype)


def _mod_specs(d, rows_per_batch_blocks):
    return [pl.BlockSpec((1, d), lambda i, j: (0, 0)),
            pl.BlockSpec((None, 1, d), lambda i, j: (i // rows_per_batch_blocks, 0, 0)),
            pl.BlockSpec((None, 1, d), lambda i, j: (i // rows_per_batch_blocks, 0, 1))]


def _fox_in_proj(x2, g_pre, mod, w_main, gn, w_small, seq, tm=1024, tn=512):
    t, d = x2.shape
    n = w_main.shape[1]
    bps = seq // tm
    n_norm = gn.shape[1] // tn
    return pl.pallas_call(
        functools.partial(_fox_in_kernel, n_norm=n_norm),
        grid=(t // tm, n // tn),
        in_specs=[pl.BlockSpec((tm, d), lambda i, j: (i, 0))] + _mod_specs(d, bps) + [
            pl.BlockSpec((d, tn), lambda i, j: (0, j)),
            pl.BlockSpec((1, tn), lambda i, j: (0, jnp.minimum(j, n_norm - 1))),
            pl.BlockSpec((d, HEAD), lambda i, j: (0, 0))],
        out_specs=[pl.BlockSpec((tm, tn), lambda i, j: (i, j)),
                   pl.BlockSpec((tm, HEAD), lambda i, j: (i, 0))],
        out_shape=[jax.ShapeDtypeStruct((t, n), BF16), jax.ShapeDtypeStruct((t, HEAD), F32)],
        scratch_shapes=[pltpu.VMEM((tm, d), BF16)],
        compiler_params=_cparams("parallel", "arbitrary"),
        name="fox_in_proj",
    )(x2, g_pre, mod, mod, w_main, gn, w_small)


def _gdn_in_proj(x2, g_pre, mod, w_main, conv_w, w_small, seq, n_q, n_qk, n_conv, tm=1024, tn=512):
    t, d = x2.shape
    n = w_main.shape[1]
    bps = seq // tm
    taps = conv_w.shape[0]
    return pl.pallas_call(
        functools.partial(_gdn_in_kernel, n_q=n_q // tn, n_qk=n_qk // tn, n_conv=n_conv // tn,
                          blocks_per_seq=bps, taps=taps),
        grid=(t // tm, n // tn),
        in_specs=[pl.BlockSpec((tm, d), lambda i, j: (i, 0))] + _mod_specs(d, bps) + [
            pl.BlockSpec((d, tn), lambda i, j: (0, j)),
            pl.BlockSpec((taps, tn), lambda i, j: (0, jnp.minimum(j, n_conv // tn - 1))),
            pl.BlockSpec((d, HEAD), lambda i, j: (0, 0))],
        out_specs=[pl.BlockSpec((tm, tn), lambda i, j: (i, j)),
                   pl.BlockSpec((tm, HEAD), lambda i, j: (i, 0))],
        out_shape=[jax.ShapeDtypeStruct((t, n), BF16), jax.ShapeDtypeStruct((t, HEAD), F32)],
        scratch_shapes=[pltpu.VMEM((tm, d), BF16),
                        pltpu.VMEM((tm + SUBLANES, tn), F32),
                        pltpu.VMEM((n_conv // tn, SUBLANES, tn), F32)],
        compiler_params=_cparams("arbitrary", "arbitrary"),
        name="gdn_in_proj",
    )(x2, g_pre, mod, mod, w_main, conv_w, w_small)


def _ffn_up(x2, g_pre, mod, w_up, conv_w, conv_b, seq, tm=1024, tn=512):
    t, d = x2.shape
    dff = w_up.shape[1] // 2
    nj = dff // tn
    bps = seq // tm
    taps = conv_w.shape[0]
    return pl.pallas_call(
        functools.partial(_ffn_up_kernel, blocks_per_seq=bps, taps=taps),
        grid=(t // tm, nj),
        in_specs=[pl.BlockSpec((tm, d), lambda i, j: (i, 0))] + _mod_specs(d, bps) + [
            pl.BlockSpec((d, tn), lambda i, j: (0, j)),
            pl.BlockSpec((d, tn), lambda i, j: (0, j + nj)),
            pl.BlockSpec((taps, tn), lambda i, j: (0, j)),
            pl.BlockSpec((1, tn), lambda i, j: (0, j))],
        out_specs=pl.BlockSpec((tm, tn), lambda i, j: (i, j)),
        out_shape=jax.ShapeDtypeStruct((t, dff), BF16),
        scratch_shapes=[pltpu.VMEM((tm, d), BF16),
                        pltpu.VMEM((tm + SUBLANES, tn), F32),
                        pltpu.VMEM((nj, SUBLANES, tn), F32)],
        compiler_params=_cparams("arbitrary", "arbitrary"),
        name="ffn_up",
    )(x2, g_pre, mod, mod, w_up, w_up, conv_w, conv_b)


def _out_proj_kernel(a_ref, w_ref, x_ref, g_ref, gate_ref, o_ref, *, nk):
    k = pl.program_id(1)
    part = _dot(a_ref[...], w_ref[...])

    @pl.when(k == 0)
    def _():
        o_ref[...] = part

    @pl.when(k > 0)
    def _():
        o_ref[...] += part

    @pl.when(k == nk - 1)
    def _():
        y = o_ref[...]
        ms = jnp.mean(y * y, axis=-1, keepdims=True)
        o_ref[...] = x_ref[...] + gate_ref[...] * (y * lax.rsqrt(ms + EPS) * g_ref[...])


def _out_proj(a, w, x2, g_post, mod, seq, tm=512, tk=512):
    t, kd = a.shape
    d = w.shape[1]
    bps = seq // tm
    nk = kd // tk
    return pl.pallas_call(
        functools.partial(_out_proj_kernel, nk=nk),
        grid=(t // tm, nk),
        in_specs=[pl.BlockSpec((tm, tk), lambda i, k: (i, k)),
                  pl.BlockSpec((tk, d), lambda i, k: (k, 0)),
                  pl.BlockSpec((tm, d), lambda i, k: (i, 0)),
                  pl.BlockSpec((1, d), lambda i, k: (0, 0)),
                  pl.BlockSpec((None, 1, d), lambda i, k: (i // bps, 0, 2))],
        out_specs=pl.BlockSpec((tm, d), lambda i, k: (i, 0)),
        out_shape=jax.ShapeDtypeStruct((t, d), F32),
        compiler_params=_cparams("parallel", "arbitrary"),
        name="out_proj",
    )(a, w, x2, g_post, mod)


def _fox_gate_kernel(f_ref, fb_ref, o_ref, carry_ref):
    tm = f_ref.shape[0]

    @pl.when(pl.program_id(1) == 0)
    def _():
        carry_ref[...] = jnp.zeros_like(carry_ref)

    x = f_ref[...] + fb_ref[...]
    logf = jnp.minimum(x, 0.0) - jnp.log1p(jnp.exp(-jnp.abs(x)))
    r = lax.broadcasted_iota(jnp.int32, (tm, tm), 0)
    c = lax.broadcasted_iota(jnp.int32, (tm, tm), 1)
    tri = (r >= c).astype(F32)
    cum = _dot_f32(tri, logf) + carry_ref[0:1, :]
    o_ref[...] = cum
    carry_ref[0:1, :] = cum[tm - 1:tm, :]


def _fox_gate_cumsum(f_logit, f_bias_row, nb, seq, tm=256):
    return pl.pallas_call(
        _fox_gate_kernel,
        grid=(nb, seq // tm),
        in_specs=[pl.BlockSpec((None, tm, HEAD), lambda b, i: (b, i, 0)),
                  pl.BlockSpec((1, HEAD), lambda b, i: (0, 0))],
        out_specs=pl.BlockSpec((None, tm, HEAD), lambda b, i: (b, i, 0)),
        out_shape=jax.ShapeDtypeStruct((nb, seq, HEAD), F32),
        scratch_shapes=[pltpu.VMEM((SUBLANES, HEAD), F32)],
        compiler_params=_cparams("parallel", "arbitrary"),
        name="fox_gate_cumsum",
    )(f_logit.reshape(nb, seq, HEAD), f_bias_row)


def _attn_kernel(q_ref, k_ref, v_ref, og_ref, fk_ref, o_ref, *, tq):
    h = pl.program_id(1)
    qi = pl.program_id(2)
    q = q_ref[...]

    def step(ki, carry, masked):
        m, l, acc = carry
        ks = pl.multiple_of(ki * tq, tq)
        k = k_ref[pl.ds(ks, tq), :]
        v = v_ref[pl.ds(ks, tq), :]
        fk = fk_ref[pl.ds(h, 1), pl.ds(ks, tq)]
        s = _dot_nt(q, k) - fk
        if masked:
            r = lax.broadcasted_iota(jnp.int32, (tq, tq), 0)
            c = lax.broadcasted_iota(jnp.int32, (tq, tq), 1)
            s = jnp.where(c <= r, s, -jnp.inf)
        m_new = jnp.maximum(m, jnp.max(s, axis=-1, keepdims=True))
        p = jnp.exp(s - m_new)
        alpha = jnp.exp(m - m_new)
        l = alpha * l + jnp.sum(p, axis=-1, keepdims=True)
        acc = alpha * acc + _dot(p.astype(BF16), v)
        return m_new, l, acc

    init = (jnp.full((tq, 1), -jnp.inf, F32), jnp.zeros((tq, 1), F32), jnp.zeros((tq, HEAD), F32))
    carry = lax.fori_loop(0, qi, lambda ki, cr: step(ki, cr, False), init)
    _, l, acc = step(qi, carry, True)
    o_ref[...] = (acc / l * jax.nn.sigmoid(og_ref[...].astype(F32))).astype(o_ref.dtype)


def _fox_attention(proj3, fk, n_heads, tq=512):
    nb, seq, _ = proj3.shape
    return pl.pallas_call(
        functools.partial(_attn_kernel, tq=tq),
        grid=(nb, n_heads, seq // tq),
        in_specs=[pl.BlockSpec((None, tq, HEAD), lambda b, h, i: (b, i, h)),
                  pl.BlockSpec((None, seq, HEAD), lambda b, h, i: (b, 0, n_heads + h)),
                  pl.BlockSpec((None, seq, HEAD), lambda b, h, i: (b, 0, 2 * n_heads + h)),
                  pl.BlockSpec((None, tq, HEAD), lambda b, h, i: (b, i, 3 * n_heads + h)),
                  pl.BlockSpec((None, n_heads, seq), lambda b, h, i: (b, 0, 0))],
        out_specs=pl.BlockSpec((None, tq, HEAD), lambda b, h, i: (b, i, h)),
        out_shape=jax.ShapeDtypeStruct((nb, seq, n_heads * HEAD), BF16),
        compiler_params=_cparams("parallel", "parallel", "arbitrary"),
        name="fox_attention",
    )(proj3, proj3, proj3, proj3, fk)


def _gdn_gate_kernel(ab_ref, alog_ref, dt_ref, o_ref, *, n_heads):
    tm = ab_ref.shape[0]
    x = ab_ref[...]
    lane = lax.broadcasted_iota(jnp.int32, x.shape, 1)
    g = -jnp.exp(alog_ref[...]) * _softplus(x + dt_ref[...])
    r = lax.broadcasted_iota(jnp.int32, (tm, tm), 0)
    c = lax.broadcasted_iota(jnp.int32, (tm, tm), 1)
    tri = ((r >= c) & ((r // CHUNK) == (c // CHUNK))).astype(F32)
    gc = _dot_f32(tri, jnp.where(lane < n_heads, g, 0.0))
    o_ref[...] = jnp.where(lane < n_heads, gc, jax.nn.sigmoid(x))


def _gdn_gates(ab, alog_row, dt_row, n_heads, tm=256):
    t = ab.shape[0]
    return pl.pallas_call(
        functools.partial(_gdn_gate_kernel, n_heads=n_heads),
        grid=(t // tm,),
        in_specs=[pl.BlockSpec((tm, HEAD), lambda i: (i, 0)),
                  pl.BlockSpec((1, HEAD), lambda i: (0, 0)),
                  pl.BlockSpec((1, HEAD), lambda i: (0, 0))],
        out_specs=pl.BlockSpec((tm, HEAD), lambda i: (i, 0)),
        out_shape=jax.ShapeDtypeStruct((t, HEAD), F32),
        compiler_params=_cparams("parallel"),
        name="gdn_gates",
    )(ab, alog_row, dt_row)


def _block_diag2(p, half):
    return jnp.concatenate([jnp.where(half, 0.0, p), jnp.where(half, p, 0.0)], axis=0)


def _gdn_chunk_kernel(q_ref, k_ref, v_ref, gb_ref, grow_ref, u_ref, w_ref, qd_ref, kd_ref, qk_ref, egl_ref,
                      *, n_vheads, chunks):
    hk = pl.program_id(1)
    lane = lax.broadcasted_iota(jnp.int32, (CHUNK, HEAD), 1)
    row = lax.broadcasted_iota(jnp.int32, (CHUNK, HEAD), 0)
    col = lane & (CHUNK - 1)
    half = lane >= CHUNK
    lower = row >= col
    strict = row > col
    eye = (row == col).astype(F32)
    level_masks = []
    for lvl in range(6):
        level_masks.append((((row >> lvl) & 1) == 1) & (((col >> lvl) & 1) == 0)
                           & ((row >> (lvl + 1)) == (col >> (lvl + 1))))
    zeros = jnp.zeros((CHUNK, HEAD), F32)

    def pick(gb, idx):
        return jnp.sum(jnp.where(lane == idx, gb, 0.0), axis=-1, keepdims=True)

    for c in range(chunks):
        sl = slice(c * CHUNK, (c + 1) * CHUNK)
        q = q_ref[sl, :]
        k = k_ref[sl, :]
        gb = gb_ref[sl, :]
        gc0, gc1 = pick(gb, 2 * hk), pick(gb, 2 * hk + 1)
        be0, be1 = pick(gb, n_vheads + 2 * hk), pick(gb, n_vheads + 2 * hk + 1)
        gcol = jnp.where(half, gc1, gc0)
        bcol = jnp.where(half, be1, be0)
        decay = jnp.exp(jnp.where(lower, gcol - grow_ref[c:c + 1, :], -jnp.inf))
        qkk = _dot_nt(jnp.concatenate([q, k], axis=0), jnp.concatenate([k, k], axis=0))
        a = jnp.where(strict, qkk[CHUNK:, :] * decay * bcol, 0.0)
        qk_ref[sl, :] = (qkk[:CHUNK, :] * decay).astype(qk_ref.dtype)

        x = eye - jnp.where(level_masks[0], a, 0.0)
        for lvl in range(1, 6):
            bl = jnp.where(level_masks[lvl], a, 0.0)
            y = _dot(x.astype(BF16), _block_diag2(bl, half).astype(BF16))
            x = x - _dot(y.astype(BF16), _block_diag2(x, half).astype(BF16))

        kf = k.astype(F32)
        qf = q.astype(F32)
        v = v_ref[sl, :].astype(F32)
        eg0, eg1 = jnp.exp(gc0), jnp.exp(gc1)
        rhs = jnp.concatenate([
            jnp.concatenate([v[:, :HEAD] * be0, kf * (be0 * eg0), zeros, zeros], axis=1),
            jnp.concatenate([zeros, zeros, v[:, HEAD:] * be1, kf * (be1 * eg1)], axis=1)], axis=0)
        sol = _dot(x.astype(BF16), rhs.astype(BF16))
        u_ref[sl, :] = jnp.concatenate([sol[:, 0:HEAD], sol[:, 2 * HEAD:3 * HEAD]], axis=1).astype(u_ref.dtype)
        w_ref[sl, :] = jnp.concatenate([sol[:, HEAD:2 * HEAD], sol[:, 3 * HEAD:]], axis=1).astype(w_ref.dtype)
        qd_ref[sl, :] = jnp.concatenate([qf * eg0, qf * eg1], axis=1).astype(qd_ref.dtype)
        gl0, gl1 = gc0[CHUNK - 1:CHUNK, :], gc1[CHUNK - 1:CHUNK, :]
        kd_ref[sl, :] = jnp.concatenate([kf * jnp.exp(gl0 - gc0), kf * jnp.exp(gl1 - gc1)],
                                        axis=1).astype(kd_ref.dtype)
        egl_ref[0, c:c + 1, :] = jnp.broadcast_to(jnp.exp(gl0), (1, HEAD))
        egl_ref[1, c:c + 1, :] = jnp.broadcast_to(jnp.exp(gl1), (1, HEAD))


def _gdn_chunks(proj3, gb3, grow, n_kheads, n_vheads, tm=512):
    nb, seq, _ = proj3.shape
    chunks = tm // CHUNK
    vd = n_vheads * HEAD
    big = jax.ShapeDtypeStruct((nb, seq, vd), BF16)
    pair_spec = pl.BlockSpec((None, tm, 2 * HEAD), lambda b, h, i: (b, i, h))
    return pl.pallas_call(
        functools.partial(_gdn_chunk_kernel, n_vheads=n_vheads, chunks=chunks),
        grid=(nb, n_kheads, seq // tm),
        in_specs=[pl.BlockSpec((None, tm, HEAD), lambda b, h, i: (b, i, h)),
                  pl.BlockSpec((None, tm, HEAD), lambda b, h, i: (b, i, n_kheads + h)),
                  pl.BlockSpec((None, tm, 2 * HEAD), lambda b, h, i: (b, i, n_kheads + h)),
                  pl.BlockSpec((None, tm, HEAD), lambda b, h, i: (b, i, 0)),
                  pl.BlockSpec((None, None, chunks, HEAD), lambda b, h, i: (b, h, i, 0))],
        out_specs=[pair_spec, pair_spec, pair_spec, pair_spec,
                   pl.BlockSpec((None, tm, HEAD), lambda b, h, i: (b, i, h)),
                   pl.BlockSpec((None, 2, chunks, HEAD), lambda b, h, i: (b, h, i, 0))],
        out_shape=[big, big, big, big,
                   jax.ShapeDtypeStruct((nb, seq, n_kheads * HEAD), BF16),
                   jax.ShapeDtypeStruct((nb, n_vheads, seq // CHUNK, HEAD), F32)],
        compiler_params=_cparams("parallel", "parallel", "parallel"),
        name="gdn_chunks",
    )(proj3, proj3, proj3, gb3, grow)


def _gdn_scan_kernel(u_ref, w_ref, qd_ref, kd_ref, qk_ref, egl_ref, z_ref, gn_ref, o_ref, s_ref,
                     *, heads, chunks):
    @pl.when(pl.program_id(2) == 0)
    def _():
        s_ref[...] = jnp.zeros_like(s_ref)

    lane = lax.broadcasted_iota(jnp.int32, (CHUNK, HEAD), 1)
    half = lane >= CHUNK
    zeros = jnp.zeros((CHUNK, HEAD), F32)
    gn = gn_ref[...]
    for c in range(chunks):
        sl = slice(c * CHUNK, (c + 1) * CHUNK)
        for pr in range(heads // 2):
            vnew, r_q = [], []
            for p in range(2):
                g = 2 * pr + p
                cs = slice(g * HEAD, (g + 1) * HEAD)
                wq = jnp.concatenate([w_ref[sl, cs], qd_ref[sl, cs]], axis=0)
                r = _dot(wq, s_ref[g].astype(BF16))
                vnew.append(u_ref[sl, cs].astype(F32) - r[:CHUNK, :])
                r_q.append(r[CHUNK:, :])
            vb = [t.astype(BF16) for t in vnew]
            v2 = jnp.concatenate([jnp.concatenate([vnew[0], zeros], axis=1),
                                  jnp.concatenate([zeros, vnew[1]], axis=1)], axis=0).astype(BF16)
            intra = _dot(qk_ref[sl, pr * HEAD:(pr + 1) * HEAD], v2)
            for p in range(2):
                g = 2 * pr + p
                cs = slice(g * HEAD, (g + 1) * HEAD)
                s_ref[g] = s_ref[g] * egl_ref[g, c:c + 1, :] + _dot_tn(kd_ref[sl, cs], vb[p])
                o = r_q[p] + intra[:, p * HEAD:(p + 1) * HEAD]
                ms = jnp.mean(o * o, axis=-1, keepdims=True)
                z = z_ref[sl, cs].astype(F32)
                o_ref[sl, cs] = (o * lax.rsqrt(ms + EPS) * gn * (z * jax.nn.sigmoid(z))).astype(o_ref.dtype)


def _gdn_scan(u, w, qd, kd, qkm, egl, proj3, gn_row, n_vheads, z_col0, heads=8, tm=512):
    nb, seq, vd = u.shape
    chunks = tm // CHUNK
    wide = pl.BlockSpec((None, tm, heads * HEAD), lambda b, g, i: (b, i, g))
    zoff = z_col0 // (heads * HEAD)
    return pl.pallas_call(
        functools.partial(_gdn_scan_kernel, heads=heads, chunks=chunks),
        grid=(nb, n_vheads // heads, seq // tm),
        in_specs=[wide, wide, wide, wide,
                  pl.BlockSpec((None, tm, heads // 2 * HEAD), lambda b, g, i: (b, i, g)),
                  pl.BlockSpec((None, heads, chunks, HEAD), lambda b, g, i: (b, g, i, 0)),
                  pl.BlockSpec((None, tm, heads * HEAD), lambda b, g, i: (b, i, zoff + g)),
                  pl.BlockSpec((1, HEAD), lambda b, g, i: (0, 0))],
        out_specs=wide,
        out_shape=jax.ShapeDtypeStruct((nb, seq, vd), BF16),
        scratch_shapes=[pltpu.VMEM((heads, HEAD, HEAD), F32)],
        compiler_params=_cparams("parallel", "parallel", "arbitrary"),
        name="gdn_scan",
    )(u, w, qd, kd, qkm, egl, proj3, gn_row)


def _pad_cols(w, n):
    return jnp.pad(w, ((0, 0), (0, n - w.shape[1])))


def _row(v, n=HEAD):
    return jnp.pad(v.astype(F32), (0, n - v.shape[0])).reshape(1, n)


def _fox_layer(x2, nb, seq, g_pre, g_post, mod, w_in, f_bias, q_norm, k_norm, w_o):
    d = x2.shape[1]
    nh = d // HEAD
    qd = nh * HEAD
    w_main = jnp.concatenate([w_in[:, :3 * qd], w_in[:, 3 * qd + nh:]], axis=1).astype(BF16)
    w_small = _pad_cols(w_in[:, 3 * qd:3 * qd + nh], HEAD).astype(BF16)
    gn = jnp.concatenate([jnp.tile(q_norm.astype(F32) * HEAD ** -0.5, nh),
                          jnp.tile(k_norm.astype(F32), nh)]).reshape(1, 2 * qd)
    proj, f_logit = _fox_in_proj(x2, g_pre, mod, w_main, gn, w_small, seq)
    fcum = _fox_gate_cumsum(f_logit, _row(f_bias), nb, seq)
    fk = jnp.transpose(fcum[:, :, :nh], (0, 2, 1))
    attn = _fox_attention(proj.reshape(nb, seq, 4 * qd), fk, nh)
    return _out_proj(attn.reshape(nb * seq, qd), w_o.astype(BF16), x2, g_post, mod, seq)


def _gdn_layer(x2, nb, seq, g_pre, g_post, mod, w_in, conv_w, a_log, dt_bias, out_norm, w_o):
    d = x2.shape[1]
    nk = d // HEAD
    nv = 2 * nk
    kd, vd = nk * HEAD, nv * HEAD
    n_main = 2 * kd + 2 * vd
    w_main = w_in[:, :n_main].astype(BF16)
    w_small = _pad_cols(w_in[:, n_main:], HEAD).astype(BF16)
    proj, ab = _gdn_in_proj(x2, g_pre, mod, w_main, conv_w.astype(F32), w_small, seq,
                            n_q=kd, n_qk=2 * kd, n_conv=2 * kd + vd)
    gb = _gdn_gates(ab, _row(a_log), _row(dt_bias), nv)
    n_chunks = seq // CHUNK
    grow = gb[:, :nv].reshape(nb, n_chunks, CHUNK, nk, 2)
    grow = jnp.transpose(grow, (0, 3, 1, 4, 2)).reshape(nb, nk, n_chunks, 2 * CHUNK)
    proj3 = proj.reshape(nb, seq, n_main)
    u, w, qdec, kdec, qkm, egl = _gdn_chunks(proj3, gb.reshape(nb, seq, HEAD), grow, nk, nv)
    a = _gdn_scan(u, w, qdec, kdec, qkm, egl, proj3, _row(out_norm), nv, z_col0=2 * kd + vd)
    return _out_proj(a.reshape(nb * seq, vd), w_o.astype(BF16), x2, g_post, mod, seq)


def _ffn_layer(x2, seq, g_pre, g_post, mod, w_up, conv_w, conv_b, w_down):
    u = _ffn_up(x2, g_pre, mod, w_up.astype(BF16), conv_w.astype(F32), conv_b.reshape(1, -1).astype(F32), seq)
    return _out_proj(u, w_down.astype(BF16), x2, g_post, mod, seq)


def kernel(x, c, ada_w, ada_b, norm_g, fox_w_in, fox_f_bias, fox_q_norm, fox_k_norm, fox_w_o,
           gdn_w_in, gdn_conv_w, gdn_a_log, gdn_dt_bias, gdn_out_norm, gdn_w_o,
           ffn_w_up, ffn_conv_w, ffn_conv_b, ffn_w_down):
    nb, seq, d = x.shape
    depth = ada_w.shape[0]
    mods = _ada_mods(c, ada_w, ada_b)
    x2 = x.reshape(nb * seq, d)
    for i in range(depth):
        g = [norm_g[i, r].reshape(1, d).astype(F32) for r in range(4)]
        j = i // 2
        if i % 2 == 0:
            x2 = _fox_layer(x2, nb, seq, g[0], g[1], mods[2 * i], fox_w_in[j], fox_f_bias[j],
                            fox_q_norm[j], fox_k_norm[j], fox_w_o[j])
        else:
            x2 = _gdn_layer(x2, nb, seq, g[0], g[1], mods[2 * i], gdn_w_in[j], gdn_conv_w[j],
                            gdn_a_log[j], gdn_dt_bias[j], gdn_out_norm[j], gdn_w_o[j])
        x2 = _ffn_layer(x2, seq, g[2], g[3], mods[2 * i + 1], ffn_w_up[i], ffn_conv_w[i],
                        ffn_conv_b[i], ffn_w_down[i])
    return x2.reshape(nb, seq, d)
```

```python
import functools

import jax
import jax.numpy as jnp
from jax import lax
from jax.experimental import pallas as pl
from jax.experimental.pallas import tpu as pltpu

F32 = jnp.float32
BF16 = jnp.bfloat16
EPS = 1e-6
HEAD = 128
CHUNK = 64
SUBLANES = 8
VMEM_LIMIT = 56 * 1024 * 1024
INV_SQRT2 = 0.7071067811865476
LOG2E = 1.4426950408889634


def _cparams(*sem):
    return pltpu.CompilerParams(dimension_semantics=sem, vmem_limit_bytes=VMEM_LIMIT)


def _dot(a, b):
    return jnp.dot(a, b, preferred_element_type=F32)


def _dot_nt(a, b):
    return lax.dot_general(a, b, (((1,), (1,)), ((), ())), preferred_element_type=F32)


def _dot_tn(a, b):
    return lax.dot_general(a, b, (((0,), (0,)), ((), ())), preferred_element_type=F32)


def _dot_f32(a, b):
    return jnp.dot(a, b, preferred_element_type=F32, precision=lax.Precision.HIGHEST)


def _softplus(x):
    return jnp.maximum(x, 0.0) + jnp.log1p(jnp.exp(-jnp.abs(x)))


def _ada_kernel(c_ref, w_ref, b_ref, o_ref):
    c = c_ref[...]
    o_ref[0] = _dot_f32(c * jax.nn.sigmoid(c), w_ref[0]) + b_ref[0]


def _ada_mods(c, ada_w, ada_b):
    nb, d = c.shape
    n = ada_w.shape[0] * ada_w.shape[1]
    w = ada_w.reshape(n, d, 3 * d)
    b = ada_b.reshape(n, 1, 3 * d)
    cp = jnp.zeros((SUBLANES, d), F32).at[:nb].set(c)
    tn = 1024
    out = pl.pallas_call(
        _ada_kernel,
        grid=(n, 3 * d // tn),
        in_specs=[pl.BlockSpec((SUBLANES, d), lambda s, j: (0, 0)),
                  pl.BlockSpec((1, d, tn), lambda s, j: (s, 0, j)),
                  pl.BlockSpec((1, 1, tn), lambda s, j: (s, 0, j))],
        out_specs=pl.BlockSpec((1, SUBLANES, tn), lambda s, j: (s, 0, j)),
        out_shape=jax.ShapeDtypeStruct((n, SUBLANES, 3 * d), F32),
        compiler_params=_cparams("parallel", "parallel"),
        name="ada_mods",
    )(cp, w, b)
    return out[:, :nb].reshape(n, nb, 1, 3 * d)


def _norm_mod(h_ref, x_ref, g_ref, sh_ref, sc_ref):
    x = x_ref[...]
    ms = jnp.mean(x * x, axis=-1, keepdims=True)
    y = x * lax.rsqrt(ms + EPS) * g_ref[...]
    h_ref[...] = (y * (1.0 + sc_ref[...]) + sh_ref[...]).astype(BF16)


def _head_norm_store(o_ref, y, gain, mean):
    for c in range(y.shape[1] // HEAD):
        sl = slice(c * HEAD, (c + 1) * HEAD)
        t = y[:, sl]
        ss = jnp.sum(t * t, axis=-1, keepdims=True)
        if mean:
            ss = ss * (1.0 / HEAD)
        gc = gain[:, sl] if getattr(gain, "ndim", 0) == 2 else gain
        o_ref[:, sl] = (t * lax.rsqrt(ss + EPS) * gc).astype(o_ref.dtype)


def _causal_conv(acc, cw_ref, buf_ref, carry_ref, j, first, taps):
    tm = acc.shape[0]
    @pl.when(first)
    def _():
        buf_ref[0:SUBLANES, :] = jnp.zeros((SUBLANES, acc.shape[1]), F32)

    @pl.when(jnp.logical_not(first))
    def _():
        buf_ref[0:SUBLANES, :] = carry_ref[j]

    buf_ref[SUBLANES:SUBLANES + tm, :] = acc
    carry_ref[j] = acc[tm - SUBLANES:tm, :]
    y = cw_ref[taps - 1:taps, :] * acc
    for k in range(taps - 1):
        y = y + cw_ref[k:k + 1, :] * buf_ref[pl.ds(SUBLANES - (taps - 1) + k, tm), :]
    return y


def _fox_in_kernel(x_ref, g_ref, sh_ref, sc_ref, w_ref, gn_ref, ws_ref, o_ref, os_ref, h_ref, *, n_norm):
    j = pl.program_id(1)

    @pl.when(j == 0)
    def _():
        _norm_mod(h_ref, x_ref, g_ref, sh_ref, sc_ref)
        os_ref[...] = _dot(h_ref[...], ws_ref[...])

    acc = _dot(h_ref[...], w_ref[...])

    @pl.when(j < n_norm)
    def _():
        _head_norm_store(o_ref, acc, gn_ref[...], mean=True)

    @pl.when(j >= n_norm)
    def _():
        o_ref[...] = acc.astype(o_ref.dtype)


def _gdn_in_kernel(x_ref, g_ref, sh_ref, sc_ref, w_ref, cw_ref, ws_ref, o_ref, os_ref,
                   h_ref, buf_ref, carry_ref, *, n_q, n_qk, n_conv, blocks_per_seq, taps):
    i = pl.program_id(0)
    j = pl.program_id(1)

    @pl.when(j == 0)
    def _():
        _norm_mod(h_ref, x_ref, g_ref, sh_ref, sc_ref)
        os_ref[...] = _dot(h_ref[...], ws_ref[...])

    acc = _dot(h_ref[...], w_ref[...])

    @pl.when(j < n_conv)
    def _():
        first = (i % blocks_per_seq) == 0
        y = _causal_conv(acc, cw_ref, buf_ref, carry_ref, j, first, taps)
        y = y * jax.nn.sigmoid(y)

        @pl.when(j < n_qk)
        def _():
            qscale = jnp.where(j < n_q, HEAD ** -0.5, 1.0).astype(F32)
            _head_norm_store(o_ref, y, qscale, mean=False)

        @pl.when(j >= n_qk)
        def _():
            o_ref[...] = y.astype(o_ref.dtype)

    @pl.when(j >= n_conv)
    def _():
        o_ref[...] = acc.astype(o_ref.dtype)


def _ffn_up_kernel(x_ref, g_ref, sh_ref, sc_ref, wg_ref, wv_ref, cw_ref, cb_ref, o_ref,
                   h_ref, buf_ref, carry_ref, *, blocks_per_seq, taps):
    i = pl.program_id(0)
    j = pl.program_id(1)

    @pl.when(j == 0)
    def _():
        _norm_mod(h_ref, x_ref, g_ref, sh_ref, sc_ref)

    h = h_ref[...]
    gate = _dot(h, wg_ref[...])
    first = (i % blocks_per_seq) == 0
    gate = _causal_conv(gate, cw_ref, buf_ref, carry_ref, j, first, taps) + cb_ref[...]
    act = 0.5 * gate * (1.0 + lax.erf(gate * INV_SQRT2))
    o_ref[...] = (act * _dot(h, wv_ref[...])).astype(o_ref.dtype)


def _mod_specs(d, rows_per_batch_blocks):
    return [pl.BlockSpec((1, d), lambda i, j: (0, 0)),
            pl.BlockSpec((None, 1, d), lambda i, j: (i // rows_per_batch_blocks, 0, 0)),
            pl.BlockSpec((None, 1, d), lambda i, j: (i // rows_per_batch_blocks, 0, 1))]


def _fox_in_proj(x2, g_pre, mod, w_main, gn, w_small, seq, tm=1024, tn=512):
    t, d = x2.shape
    n = w_main.shape[1]
    bps = seq // tm
    n_norm = gn.shape[1] // tn
    return pl.pallas_call(
        functools.partial(_fox_in_kernel, n_norm=n_norm),
        grid=(t // tm, n // tn),
        in_specs=[pl.BlockSpec((tm, d), lambda i, j: (i, 0))] + _mod_specs(d, bps) + [
            pl.BlockSpec((d, tn), lambda i, j: (0, j)),
            pl.BlockSpec((1, tn), lambda i, j: (0, jnp.minimum(j, n_norm - 1))),
            pl.BlockSpec((d, HEAD), lambda i, j: (0, 0))],
        out_specs=[pl.BlockSpec((tm, tn), lambda i, j: (i, j)),
                   pl.BlockSpec((tm, HEAD), lambda i, j: (i, 0))],
        out_shape=[jax.ShapeDtypeStruct((t, n), BF16), jax.ShapeDtypeStruct((t, HEAD), F32)],
        scratch_shapes=[pltpu.VMEM((tm, d), BF16)],
        compiler_params=_cparams("parallel", "arbitrary"),
        name="fox_in_proj",
    )(x2, g_pre, mod, mod, w_main, gn, w_small)


def _gdn_in_proj(x2, g_pre, mod, w_main, conv_w, w_small, seq, n_q, n_qk, n_conv, tm=1024, tn=512):
    t, d = x2.shape
    n = w_main.shape[1]
    bps = seq // tm
    taps = conv_w.shape[0]
    return pl.pallas_call(
        functools.partial(_gdn_in_kernel, n_q=n_q // tn, n_qk=n_qk // tn, n_conv=n_conv // tn,
                          blocks_per_seq=bps, taps=taps),
        grid=(t // tm, n // tn),
        in_specs=[pl.BlockSpec((tm, d), lambda i, j: (i, 0))] + _mod_specs(d, bps) + [
            pl.BlockSpec((d, tn), lambda i, j: (0, j)),
            pl.BlockSpec((taps, tn), lambda i, j: (0, jnp.minimum(j, n_conv // tn - 1))),
            pl.BlockSpec((d, HEAD), lambda i, j: (0, 0))],
        out_specs=[pl.BlockSpec((tm, tn), lambda i, j: (i, j)),
                   pl.BlockSpec((tm, HEAD), lambda i, j: (i, 0))],
        out_shape=[jax.ShapeDtypeStruct((t, n), BF16), jax.ShapeDtypeStruct((t, HEAD), F32)],
        scratch_shapes=[pltpu.VMEM((tm, d), BF16),
                        pltpu.VMEM((tm + SUBLANES, tn), F32),
                        pltpu.VMEM((n_conv // tn, SUBLANES, tn), F32)],
        compiler_params=_cparams("arbitrary", "arbitrary"),
        name="gdn_in_proj",
    )(x2, g_pre, mod, mod, w_main, conv_w, w_small)


def _ffn_up(x2, g_pre, mod, w_up, conv_w, conv_b, seq, tm=1024, tn=512):
    t, d = x2.shape
    dff = w_up.shape[1] // 2
    nj = dff // tn
    bps = seq // tm
    taps = conv_w.shape[0]
    return pl.pallas_call(
        functools.partial(_ffn_up_kernel, blocks_per_seq=bps, taps=taps),
        grid=(t // tm, nj),
        in_specs=[pl.BlockSpec((tm, d), lambda i, j: (i, 0))] + _mod_specs(d, bps) + [
            pl.BlockSpec((d, tn), lambda i, j: (0, j)),
            pl.BlockSpec((d, tn), lambda i, j: (0, j + nj)),
            pl.BlockSpec((taps, tn), lambda i, j: (0, j)),
            pl.BlockSpec((1, tn), lambda i, j: (0, j))],
        out_specs=pl.BlockSpec((tm, tn), lambda i, j: (i, j)),
        out_shape=jax.ShapeDtypeStruct((t, dff), BF16),
        scratch_shapes=[pltpu.VMEM((tm, d), BF16),
                        pltpu.VMEM((tm + SUBLANES, tn), F32),
                        pltpu.VMEM((nj, SUBLANES, tn), F32)],
        compiler_params=_cparams("arbitrary", "arbitrary"),
        name="ffn_up",
    )(x2, g_pre, mod, mod, w_up, w_up, conv_w, conv_b)


def _out_proj_kernel(a_ref, w_ref, x_ref, g_ref, gate_ref, o_ref, *, sub):
    tm = a_ref.shape[0]
    ys = [_dot(a_ref[r * sub:(r + 1) * sub, :], w_ref[...]) for r in range(tm // sub)]
    for r, y in enumerate(ys):
        rows = slice(r * sub, (r + 1) * sub)
        ms = jnp.mean(y * y, axis=-1, keepdims=True)
        o_ref[rows, :] = x_ref[rows, :] + gate_ref[...] * (y * lax.rsqrt(ms + EPS) * g_ref[...])


def _out_proj(a, w, x2, g_post, mod, seq, tm, sub):
    t, kd = a.shape
    d = w.shape[1]
    bps = seq // tm
    return pl.pallas_call(
        functools.partial(_out_proj_kernel, sub=sub),
        grid=(t // tm,),
        in_specs=[pl.BlockSpec((tm, kd), lambda i: (i, 0)),
                  pl.BlockSpec((kd, d), lambda i: (0, 0), pipeline_mode=pl.Buffered(1)),
                  pl.BlockSpec((tm, d), lambda i: (i, 0)),
                  pl.BlockSpec((1, d), lambda i: (0, 0)),
                  pl.BlockSpec((None, 1, d), lambda i: (i // bps, 0, 2))],
        out_specs=pl.BlockSpec((tm, d), lambda i: (i, 0)),
        out_shape=jax.ShapeDtypeStruct((t, d), F32),
        compiler_params=_cparams("parallel"),
        name="out_proj",
    )(a, w, x2, g_post, mod)


def _fox_gate_kernel(f_ref, fb_ref, o_ref, carry_ref):
    tm = f_ref.shape[0]

    @pl.when(pl.program_id(1) == 0)
    def _():
        carry_ref[...] = jnp.zeros_like(carry_ref)

    x = f_ref[...] + fb_ref[...]
    logf = (jnp.minimum(x, 0.0) - jnp.log1p(jnp.exp(-jnp.abs(x)))) * LOG2E
    r = lax.broadcasted_iota(jnp.int32, (tm, tm), 0)
    c = lax.broadcasted_iota(jnp.int32, (tm, tm), 1)
    tri = (r >= c).astype(F32)
    cum = _dot_f32(tri, logf) + carry_ref[0:1, :]
    o_ref[...] = cum
    carry_ref[0:1, :] = cum[tm - 1:tm, :]


def _fox_gate_cumsum(f_logit, f_bias_row, nb, seq, tm=256):
    return pl.pallas_call(
        _fox_gate_kernel,
        grid=(nb, seq // tm),
        in_specs=[pl.BlockSpec((None, tm, HEAD), lambda b, i: (b, i, 0)),
                  pl.BlockSpec((1, HEAD), lambda b, i: (0, 0))],
        out_specs=pl.BlockSpec((None, tm, HEAD), lambda b, i: (b, i, 0)),
        out_shape=jax.ShapeDtypeStruct((nb, seq, HEAD), F32),
        scratch_shapes=[pltpu.VMEM((SUBLANES, HEAD), F32)],
        compiler_params=_cparams("parallel", "arbitrary"),
        name="fox_gate_cumsum",
    )(f_logit.reshape(nb, seq, HEAD), f_bias_row)


def _attn_kernel(q_ref, k_ref, v_ref, og_ref, fk_ref, o_ref,
                 s0_ref, s1_ref, m_ref, l_ref, acc_ref, *, tq):
    h = pl.program_id(1)
    qi = pl.program_id(2)
    m_ref[...] = jnp.full(m_ref.shape, -jnp.inf, F32)
    l_ref[...] = jnp.zeros(l_ref.shape, F32)
    acc_ref[...] = jnp.zeros(acc_ref.shape, F32)

    def scores(s_ref, ki):
        ks = pl.multiple_of(ki * tq, tq)
        s_ref[...] = _dot_nt(q_ref[...], k_ref[pl.ds(ks, tq), :]) - fk_ref[pl.ds(h, 1), pl.ds(ks, tq)]

    def update(s_ref, ki, masked):
        ks = pl.multiple_of(ki * tq, tq)
        s = s_ref[...]
        if masked:
            row = lax.broadcasted_iota(jnp.int32, (tq, tq), 0)
            col = lax.broadcasted_iota(jnp.int32, (tq, tq), 1)
            s = jnp.where(col <= row, s, -jnp.inf)
        m_old = m_ref[...]
        m_new = jnp.maximum(m_old, jnp.max(s, axis=-1, keepdims=True))
        p = jnp.exp2(s - m_new)
        alpha = jnp.exp2(m_old - m_new)
        l_ref[...] = alpha * l_ref[...] + jnp.sum(p, axis=-1, keepdims=True)
        m_ref[...] = m_new
        acc_ref[...] = alpha * acc_ref[...] + _dot(p.astype(BF16), v_ref[pl.ds(ks, tq), :])

    scores(s0_ref, 0)

    def pair(pi, carry):
        scores(s1_ref, 2 * pi + 1)
        update(s0_ref, 2 * pi, False)
        scores(s0_ref, 2 * pi + 2)
        update(s1_ref, 2 * pi + 1, False)
        return carry

    lax.fori_loop(0, qi // 2, pair, 0)

    @pl.when(qi % 2 == 1)
    def _():
        scores(s1_ref, qi)
        update(s0_ref, qi - 1, False)
        update(s1_ref, qi, True)

    @pl.when(qi % 2 == 0)
    def _():
        update(s0_ref, qi, True)

    o_ref[...] = (acc_ref[...] / l_ref[...] * jax.nn.sigmoid(og_ref[...].astype(F32))).astype(o_ref.dtype)


def _fox_attention(proj3, fk, n_heads, tq=512):
    nb, seq, _ = proj3.shape
    return pl.pallas_call(
        functools.partial(_attn_kernel, tq=tq),
        grid=(nb, n_heads, seq // tq),
        in_specs=[pl.BlockSpec((None, tq, HEAD), lambda b, h, i: (b, i, h)),
                  pl.BlockSpec((None, seq, HEAD), lambda b, h, i: (b, 0, n_heads + h)),
                  pl.BlockSpec((None, seq, HEAD), lambda b, h, i: (b, 0, 2 * n_heads + h)),
                  pl.BlockSpec((None, tq, HEAD), lambda b, h, i: (b, i, 3 * n_heads + h)),
                  pl.BlockSpec((None, n_heads, seq), lambda b, h, i: (b, 0, 0))],
        out_specs=pl.BlockSpec((None, tq, HEAD), lambda b, h, i: (b, i, h)),
        out_shape=jax.ShapeDtypeStruct((nb, seq, n_heads * HEAD), BF16),
        scratch_shapes=[pltpu.VMEM((tq, tq), F32), pltpu.VMEM((tq, tq), F32),
                        pltpu.VMEM((tq, 1), F32), pltpu.VMEM((tq, 1), F32), pltpu.VMEM((tq, HEAD), F32)],
        compiler_params=_cparams("parallel", "parallel", "arbitrary"),
        name="fox_attention",
    )(proj3, proj3, proj3, proj3, fk)


def _gdn_gate_kernel(ab_ref, alog_ref, dt_ref, o_ref, *, n_heads):
    tm = ab_ref.shape[0]
    x = ab_ref[...]
    lane = lax.broadcasted_iota(jnp.int32, x.shape, 1)
    g = -jnp.exp(alog_ref[...]) * _softplus(x + dt_ref[...])
    r = lax.broadcasted_iota(jnp.int32, (tm, tm), 0)
    c = lax.broadcasted_iota(jnp.int32, (tm, tm), 1)
    tri = ((r >= c) & ((r // CHUNK) == (c // CHUNK))).astype(F32)
    gc = _dot_f32(tri, jnp.where(lane < n_heads, g, 0.0))
    o_ref[...] = jnp.where(lane < n_heads, gc, jax.nn.sigmoid(x))


def _gdn_gates(ab, alog_row, dt_row, n_heads, tm=256):
    t = ab.shape[0]
    return pl.pallas_call(
        functools.partial(_gdn_gate_kernel, n_heads=n_heads),
        grid=(t // tm,),
        in_specs=[pl.BlockSpec((tm, HEAD), lambda i: (i, 0)),
                  pl.BlockSpec((1, HEAD), lambda i: (0, 0)),
                  pl.BlockSpec((1, HEAD), lambda i: (0, 0))],
        out_specs=pl.BlockSpec((tm, HEAD), lambda i: (i, 0)),
        out_shape=jax.ShapeDtypeStruct((t, HEAD), F32),
        compiler_params=_cparams("parallel"),
        name="gdn_gates",
    )(ab, alog_row, dt_row)


def _block_diag2(p, half):
    return jnp.concatenate([jnp.where(half, 0.0, p), jnp.where(half, p, 0.0)], axis=0)


def _gdn_chunk_kernel(q_ref, k_ref, v_ref, gb_ref, grow_ref, u_ref, w_ref, qd_ref, kd_ref, qk_ref, egl_ref,
                      *, n_vheads, chunks):
    hk = pl.program_id(1)
    lane = lax.broadcasted_iota(jnp.int32, (CHUNK, HEAD), 1)
    row = lax.broadcasted_iota(jnp.int32, (CHUNK, HEAD), 0)
    col = lane & (CHUNK - 1)
    half = lane >= CHUNK
    lower = row >= col
    strict = row > col
    eye = (row == col).astype(F32)
    level_masks = []
    for lvl in range(6):
        level_masks.append((((row >> lvl) & 1) == 1) & (((col >> lvl) & 1) == 0)
                           & ((row >> (lvl + 1)) == (col >> (lvl + 1))))
    zeros = jnp.zeros((CHUNK, HEAD), F32)

    def pick(gb, idx):
        return jnp.sum(jnp.where(lane == idx, gb, 0.0), axis=-1, keepdims=True)

    sls = [slice(c * CHUNK, (c + 1) * CHUNK) for c in range(chunks)]
    gcs, bes, a_s, xs = [], [], [], []
    for c, sl in enumerate(sls):
        q = q_ref[sl, :]
        k = k_ref[sl, :]
        gb = gb_ref[sl, :]
        gc0, gc1 = pick(gb, 2 * hk), pick(gb, 2 * hk + 1)
        be0, be1 = pick(gb, n_vheads + 2 * hk), pick(gb, n_vheads + 2 * hk + 1)
        gcol = jnp.where(half, gc1, gc0)
        bcol = jnp.where(half, be1, be0)
        decay = jnp.exp(jnp.where(lower, gcol - grow_ref[c:c + 1, :], -jnp.inf))
        qkk = _dot_nt(jnp.concatenate([q, k], axis=0), jnp.concatenate([k, k], axis=0))
        a = jnp.where(strict, qkk[CHUNK:, :] * decay * bcol, 0.0)
        qk_ref[sl, :] = (qkk[:CHUNK, :] * decay).astype(qk_ref.dtype)
        gcs.append((gc0, gc1))
        bes.append((be0, be1))
        a_s.append(a)
        xs.append(eye - jnp.where(level_masks[0], a, 0.0))

    for lvl in range(1, 6):
        ys = [_dot(xs[c].astype(BF16), _block_diag2(jnp.where(level_masks[lvl], a_s[c], 0.0), half).astype(BF16))
              for c in range(chunks)]
        xs = [xs[c] - _dot(ys[c].astype(BF16), _block_diag2(xs[c], half).astype(BF16)) for c in range(chunks)]

    for c, sl in enumerate(sls):
        (gc0, gc1), (be0, be1) = gcs[c], bes[c]
        kf = k_ref[sl, :].astype(F32)
        qf = q_ref[sl, :].astype(F32)
        v = v_ref[sl, :].astype(F32)
        eg0, eg1 = jnp.exp(gc0), jnp.exp(gc1)
        rhs = jnp.concatenate([
            jnp.concatenate([v[:, :HEAD] * be0, kf * (be0 * eg0), zeros, zeros], axis=1),
            jnp.concatenate([zeros, zeros, v[:, HEAD:] * be1, kf * (be1 * eg1)], axis=1)], axis=0)
        sol = _dot(xs[c].astype(BF16), rhs.astype(BF16))
        u_ref[sl, :] = jnp.concatenate([sol[:, 0:HEAD], sol[:, 2 * HEAD:3 * HEAD]], axis=1).astype(u_ref.dtype)
        w_ref[sl, :] = jnp.concatenate([sol[:, HEAD:2 * HEAD], sol[:, 3 * HEAD:]], axis=1).astype(w_ref.dtype)
        qd_ref[sl, :] = jnp.concatenate([qf * eg0, qf * eg1], axis=1).astype(qd_ref.dtype)
        gl0, gl1 = gc0[CHUNK - 1:CHUNK, :], gc1[CHUNK - 1:CHUNK, :]
        kd_ref[sl, :] = jnp.concatenate([kf * jnp.exp(gl0 - gc0), kf * jnp.exp(gl1 - gc1)],
                                        axis=1).astype(kd_ref.dtype)
        egl_ref[0, c:c + 1, :] = jnp.broadcast_to(jnp.exp(gl0), (1, HEAD))
        egl_ref[1, c:c + 1, :] = jnp.broadcast_to(jnp.exp(gl1), (1, HEAD))


def _gdn_chunks(proj3, gb3, grow, n_kheads, n_vheads, tm=512):
    nb, seq, _ = proj3.shape
    chunks = tm // CHUNK
    vd = n_vheads * HEAD
    big = jax.ShapeDtypeStruct((nb, seq, vd), BF16)
    pair_spec = pl.BlockSpec((None, tm, 2 * HEAD), lambda b, h, i: (b, i, h))
    return pl.pallas_call(
        functools.partial(_gdn_chunk_kernel, n_vheads=n_vheads, chunks=chunks),
        grid=(nb, n_kheads, seq // tm),
        in_specs=[pl.BlockSpec((None, tm, HEAD), lambda b, h, i: (b, i, h)),
                  pl.BlockSpec((None, tm, HEAD), lambda b, h, i: (b, i, n_kheads + h)),
                  pl.BlockSpec((None, tm, 2 * HEAD), lambda b, h, i: (b, i, n_kheads + h)),
                  pl.BlockSpec((None, tm, HEAD), lambda b, h, i: (b, i, 0)),
                  pl.BlockSpec((None, None, chunks, HEAD), lambda b, h, i: (b, h, i, 0))],
        out_specs=[pair_spec, pair_spec, pair_spec, pair_spec,
                   pl.BlockSpec((None, tm, HEAD), lambda b, h, i: (b, i, h)),
                   pl.BlockSpec((None, 2, chunks, HEAD), lambda b, h, i: (b, h, i, 0))],
        out_shape=[big, big, big, big,
                   jax.ShapeDtypeStruct((nb, seq, n_kheads * HEAD), BF16),
                   jax.ShapeDtypeStruct((nb, n_vheads, seq // CHUNK, HEAD), F32)],
        compiler_params=_cparams("parallel", "parallel", "parallel"),
        name="gdn_chunks",
    )(proj3, proj3, proj3, gb3, grow)


def _gdn_scan_kernel(u_ref, w_ref, qd_ref, kd_ref, qk_ref, egl_ref, z_ref, gn_ref, o_ref, s_ref,
                     *, heads, chunks):
    @pl.when(pl.program_id(2) == 0)
    def _():
        s_ref[...] = jnp.zeros_like(s_ref)

    lane = lax.broadcasted_iota(jnp.int32, (CHUNK, HEAD), 1)
    half = lane >= CHUNK
    zeros = jnp.zeros((CHUNK, HEAD), F32)
    gn = gn_ref[...]
    for c in range(chunks):
        sl = slice(c * CHUNK, (c + 1) * CHUNK)
        for pr in range(heads // 2):
            vnew, r_q = [], []
            for p in range(2):
                g = 2 * pr + p
                cs = slice(g * HEAD, (g + 1) * HEAD)
                wq = jnp.concatenate([w_ref[sl, cs], qd_ref[sl, cs]], axis=0)
                r = _dot(wq, s_ref[g].astype(BF16))
                vnew.append(u_ref[sl, cs].astype(F32) - r[:CHUNK, :])
                r_q.append(r[CHUNK:, :])
            vb = [t.astype(BF16) for t in vnew]
            v2 = jnp.concatenate([jnp.concatenate([vnew[0], zeros], axis=1),
                                  jnp.concatenate([zeros, vnew[1]], axis=1)], axis=0).astype(BF16)
            intra = _dot(qk_ref[sl, pr * HEAD:(pr + 1) * HEAD], v2)
            for p in range(2):
                g = 2 * pr + p
                cs = slice(g * HEAD, (g + 1) * HEAD)
                s_ref[g] = s_ref[g] * egl_ref[g, c:c + 1, :] + _dot_tn(kd_ref[sl, cs], vb[p])
                o = r_q[p] + intra[:, p * HEAD:(p + 1) * HEAD]
                ms = jnp.mean(o * o, axis=-1, keepdims=True)
                z = z_ref[sl, cs].astype(F32)
                o_ref[sl, cs] = (o * lax.rsqrt(ms + EPS) * gn * (z * jax.nn.sigmoid(z))).astype(o_ref.dtype)


def _gdn_scan(u, w, qd, kd, qkm, egl, proj3, gn_row, n_vheads, z_col0, heads=8, tm=512):
    nb, seq, vd = u.shape
    chunks = tm // CHUNK
    wide = pl.BlockSpec((None, tm, heads * HEAD), lambda b, g, i: (b, i, g))
    zoff = z_col0 // (heads * HEAD)
    return pl.pallas_call(
        functools.partial(_gdn_scan_kernel, heads=heads, chunks=chunks),
        grid=(nb, n_vheads // heads, seq // tm),
        in_specs=[wide, wide, wide, wide,
                  pl.BlockSpec((None, tm, heads // 2 * HEAD), lambda b, g, i: (b, i, g)),
                  pl.BlockSpec((None, heads, chunks, HEAD), lambda b, g, i: (b, g, i, 0)),
                  pl.BlockSpec((None, tm, heads * HEAD), lambda b, g, i: (b, i, zoff + g)),
                  pl.BlockSpec((1, HEAD), lambda b, g, i: (0, 0))],
        out_specs=wide,
        out_shape=jax.ShapeDtypeStruct((nb, seq, vd), BF16),
        scratch_shapes=[pltpu.VMEM((heads, HEAD, HEAD), F32)],
        compiler_params=_cparams("parallel", "parallel", "arbitrary"),
        name="gdn_scan",
    )(u, w, qd, kd, qkm, egl, proj3, gn_row)


def _pad_cols(w, n):
    return jnp.pad(w, ((0, 0), (0, n - w.shape[1])))


def _row(v, n=HEAD):
    return jnp.pad(v.astype(F32), (0, n - v.shape[0])).reshape(1, n)


def _fox_layer(x2, nb, seq, g_pre, g_post, mod, w_in, f_bias, q_norm, k_norm, w_o):
    d = x2.shape[1]
    nh = d // HEAD
    qd = nh * HEAD
    w_main = jnp.concatenate([w_in[:, :3 * qd], w_in[:, 3 * qd + nh:]], axis=1).astype(BF16)
    w_small = _pad_cols(w_in[:, 3 * qd:3 * qd + nh], HEAD).astype(BF16)
    gn = jnp.concatenate([jnp.tile(q_norm.astype(F32) * (HEAD ** -0.5 * LOG2E), nh),
                          jnp.tile(k_norm.astype(F32), nh)]).reshape(1, 2 * qd)
    proj, f_logit = _fox_in_proj(x2, g_pre, mod, w_main, gn, w_small, seq)
    fcum = _fox_gate_cumsum(f_logit, _row(f_bias), nb, seq)
    fk = jnp.transpose(fcum[:, :, :nh], (0, 2, 1))
    attn = _fox_attention(proj.reshape(nb, seq, 4 * qd), fk, nh)
    return _out_proj(attn.reshape(nb * seq, qd), w_o.astype(BF16), x2, g_post, mod, seq, tm=512, sub=512)


def _gdn_layer(x2, nb, seq, g_pre, g_post, mod, w_in, conv_w, a_log, dt_bias, out_norm, w_o):
    d = x2.shape[1]
    nk = d // HEAD
    nv = 2 * nk
    kd, vd = nk * HEAD, nv * HEAD
    n_main = 2 * kd + 2 * vd
    w_main = w_in[:, :n_main].astype(BF16)
    w_small = _pad_cols(w_in[:, n_main:], HEAD).astype(BF16)
    proj, ab = _gdn_in_proj(x2, g_pre, mod, w_main, conv_w.astype(F32), w_small, seq,
                            n_q=kd, n_qk=2 * kd, n_conv=2 * kd + vd)
    gb = _gdn_gates(ab, _row(a_log), _row(dt_bias), nv)
    n_chunks = seq // CHUNK
    grow = gb[:, :nv].reshape(nb, n_chunks, CHUNK, nk, 2)
    grow = jnp.transpose(grow, (0, 3, 1, 4, 2)).reshape(nb, nk, n_chunks, 2 * CHUNK)
    proj3 = proj.reshape(nb, seq, n_main)
    u, w, qdec, kdec, qkm, egl = _gdn_chunks(proj3, gb.reshape(nb, seq, HEAD), grow, nk, nv)
    a = _gdn_scan(u, w, qdec, kdec, qkm, egl, proj3, _row(out_norm), nv, z_col0=2 * kd + vd)
    return _out_proj(a.reshape(nb * seq, vd), w_o.astype(BF16), x2, g_post, mod, seq, tm=512, sub=512)


def _ffn_layer(x2, seq, g_pre, g_post, mod, w_up, conv_w, conv_b, w_down):
    u = _ffn_up(x2, g_pre, mod, w_up.astype(BF16), conv_w.astype(F32), conv_b.reshape(1, -1).astype(F32), seq)
    return _out_proj(u, w_down.astype(BF16), x2, g_post, mod, seq, tm=256, sub=256)


def kernel(x, c, ada_w, ada_b, norm_g, fox_w_in, fox_f_bias, fox_q_norm, fox_k_norm, fox_w_o,
           gdn_w_in, gdn_conv_w, gdn_a_log, gdn_dt_bias, gdn_out_norm, gdn_w_o,
           ffn_w_up, ffn_conv_w, ffn_conv_b, ffn_w_down):
    nb, seq, d = x.shape
    depth = ada_w.shape[0]
    mods = _ada_mods(c, ada_w, ada_b)
    x2 = x.reshape(nb * seq, d)
    for i in range(depth):
        g = [norm_g[i, r].reshape(1, d).astype(F32) for r in range(4)]
        j = i // 2
        if i % 2 == 0:
            x2 = _fox_layer(x2, nb, seq, g[0], g[1], mods[2 * i], fox_w_in[j], fox_f_bias[j],
                            fox_q_norm[j], fox_k_norm[j], fox_w_o[j])
        else:
            x2 = _gdn_layer(x2, nb, seq, g[0], g[1], mods[2 * i], gdn_w_in[j], gdn_conv_w[j],
                            gdn_a_log[j], gdn_dt_bias[j], gdn_out_norm[j], gdn_w_o[j])
        x2 = _ffn_layer(x2, seq, g[2], g[3], mods[2 * i + 1], ffn_w_up[i], ffn_conv_w[i],
                        ffn_conv_b[i], ffn_w_down[i])
    return x2.reshape(nb, seq, d)
```

```python
import functools

import jax
import jax.numpy as jnp
from jax import lax
from jax.experimental import pallas as pl
from jax.experimental.pallas import tpu as pltpu

F32 = jnp.float32
BF16 = jnp.bfloat16
EPS = 1e-6
HEAD = 128
CHUNK = 64
SUBLANES = 8
MXU_N = 256
VMEM_LIMIT = 56 * 1024 * 1024
INV_SQRT2 = 0.7071067811865476
LOG2E = 1.4426950408889634


def _cparams(*sem):
    return pltpu.CompilerParams(dimension_semantics=sem, vmem_limit_bytes=VMEM_LIMIT)


def _dot(a, b):
    return jnp.dot(a, b, preferred_element_type=F32)


def _dot_nt(a, b):
    return lax.dot_general(a, b, (((1,), (1,)), ((), ())), preferred_element_type=F32)


def _dot_tn(a, b):
    return lax.dot_general(a, b, (((0,), (0,)), ((), ())), preferred_element_type=F32)


def _dot_f32(a, b):
    return jnp.dot(a, b, preferred_element_type=F32, precision=lax.Precision.HIGHEST)


def _softplus(x):
    return jnp.maximum(x, 0.0) + jnp.log1p(jnp.exp(-jnp.abs(x)))


def _ada_kernel(c_ref, w_ref, b_ref, o_ref):
    c = c_ref[...]
    o_ref[0] = _dot_f32(c * jax.nn.sigmoid(c), w_ref[0]) + b_ref[0]


def _ada_mods(c, ada_w, ada_b):
    nb, d = c.shape
    n = ada_w.shape[0] * ada_w.shape[1]
    w = ada_w.reshape(n, d, 3 * d)
    b = ada_b.reshape(n, 1, 3 * d)
    cp = jnp.zeros((SUBLANES, d), F32).at[:nb].set(c)
    tn = 1024
    out = pl.pallas_call(
        _ada_kernel,
        grid=(n, 3 * d // tn),
        in_specs=[pl.BlockSpec((SUBLANES, d), lambda s, j: (0, 0)),
                  pl.BlockSpec((1, d, tn), lambda s, j: (s, 0, j)),
                  pl.BlockSpec((1, 1, tn), lambda s, j: (s, 0, j))],
        out_specs=pl.BlockSpec((1, SUBLANES, tn), lambda s, j: (s, 0, j)),
        out_shape=jax.ShapeDtypeStruct((n, SUBLANES, 3 * d), F32),
        compiler_params=_cparams("parallel", "parallel"),
        name="ada_mods",
    )(cp, w, b)
    return out[:, :nb].reshape(n, nb, 1, 3 * d)


def _norm_mod(h_ref, x_ref, g_ref, sh_ref, sc_ref):
    x = x_ref[...]
    ms = jnp.mean(x * x, axis=-1, keepdims=True)
    y = x * lax.rsqrt(ms + EPS) * g_ref[...]
    h_ref[...] = (y * (1.0 + sc_ref[...]) + sh_ref[...]).astype(BF16)


def _sub_dots(h_ref, w_ref):
    return [_dot(h_ref[...], w_ref[:, c * MXU_N:(c + 1) * MXU_N]) for c in range(w_ref.shape[1] // MXU_N)]


def _head_norm_store(o_ref, c, y, gain_ref, mean):
    for g in range(MXU_N // HEAD):
        sl = slice(c * MXU_N + g * HEAD, c * MXU_N + (g + 1) * HEAD)
        t = y[:, g * HEAD:(g + 1) * HEAD]
        ss = jnp.sum(t * t, axis=-1, keepdims=True)
        if mean:
            ss = ss * (1.0 / HEAD)
        o_ref[:, sl] = (t * lax.rsqrt(ss + EPS) * gain_ref[:, sl]).astype(o_ref.dtype)


def _causal_conv(acc, cw, buf_ref, carry_ref, slot, first):
    tm = acc.shape[0]
    taps = cw.shape[0]
    prev = jnp.where(first, 0.0, carry_ref[slot])
    carry_ref[slot] = acc[tm - SUBLANES:tm, :]
    row = lax.broadcasted_iota(jnp.int32, prev.shape, 0)
    y = cw[taps - 1:taps, :] * acc
    for k in range(taps - 1):
        d = taps - 1 - k
        r = pltpu.roll(acc, d, axis=0)
        top = jnp.where(row < d, pltpu.roll(prev, d, axis=0), r[0:SUBLANES, :])
        y = y + cw[k:k + 1, :] * jnp.concatenate([top, r[SUBLANES:, :]], axis=0)
    return y


def _zero_carry_at_start(carry_ref):
    @pl.when((pl.program_id(0) == 0) & (pl.program_id(1) == 0))
    def _():
        carry_ref[...] = jnp.zeros_like(carry_ref)


def _fox_qk_kernel(x_ref, g_ref, sh_ref, sc_ref, w_ref, gn_ref, ws_ref, o_ref, os_ref, h_ref):
    @pl.when(pl.program_id(1) == 0)
    def _():
        _norm_mod(h_ref, x_ref, g_ref, sh_ref, sc_ref)
        os_ref[...] = _dot(h_ref[...], ws_ref[...])

    for c, acc in enumerate(_sub_dots(h_ref, w_ref)):
        _head_norm_store(o_ref, c, acc, gn_ref, mean=True)


def _plain_proj_kernel(h_ref, w_ref, o_ref):
    o_ref[...] = _dot(h_ref[...], w_ref[...]).astype(o_ref.dtype)


def _gdn_qk_kernel(x_ref, g_ref, sh_ref, sc_ref, w_ref, cw_ref, qs_ref, ws_ref, o_ref, os_ref, h_ref,
                   bufs_ref, carry_ref, *, blocks_per_seq):
    i = pl.program_id(0)
    j = pl.program_id(1)
    _zero_carry_at_start(carry_ref)

    @pl.when(j == 0)
    def _():
        _norm_mod(h_ref, x_ref, g_ref, sh_ref, sc_ref)
        os_ref[...] = _dot(h_ref[...], ws_ref[...])

    first = (i % blocks_per_seq) == 0
    accs = _sub_dots(h_ref, w_ref)
    for c, acc in enumerate(accs):
        y = _causal_conv(acc, cw_ref[:, c * MXU_N:(c + 1) * MXU_N], bufs_ref.at[c], carry_ref,
                         j * len(accs) + c, first)
        _head_norm_store(o_ref, c, y * jax.nn.sigmoid(y), qs_ref, mean=False)


def _gdn_v_kernel(h_ref, w_ref, cw_ref, o_ref, bufs_ref, carry_ref, *, blocks_per_seq):
    i = pl.program_id(0)
    j = pl.program_id(1)
    _zero_carry_at_start(carry_ref)
    first = (i % blocks_per_seq) == 0
    n_sub = w_ref.shape[1] // MXU_N

    def epilogue(c, acc):
        cs = slice(c * MXU_N, (c + 1) * MXU_N)
        y = _causal_conv(acc, cw_ref[:, cs], bufs_ref.at[c], carry_ref, j * n_sub + c, first)
        o_ref[:, cs] = (y * jax.nn.sigmoid(y)).astype(o_ref.dtype)

    prev = None
    for c in range(n_sub):
        acc = _dot(h_ref[...], w_ref[:, c * MXU_N:(c + 1) * MXU_N])
        if prev is not None:
            epilogue(c - 1, prev)
        prev = acc
    epilogue(n_sub - 1, prev)


def _ffn_up_kernel(x_ref, g_ref, sh_ref, sc_ref, wg_ref, wv_ref, cw_ref, cb_ref, o_ref,
                   h_ref, bufs_ref, carry_ref, *, blocks_per_seq):
    i = pl.program_id(0)
    j = pl.program_id(1)
    _zero_carry_at_start(carry_ref)

    @pl.when(j == 0)
    def _():
        _norm_mod(h_ref, x_ref, g_ref, sh_ref, sc_ref)

    first = (i % blocks_per_seq) == 0
    n_sub = wg_ref.shape[1] // MXU_N
    pairs = []
    for c in range(n_sub):
        cs = slice(c * MXU_N, (c + 1) * MXU_N)
        pairs.append((_dot(h_ref[...], wg_ref[:, cs]), _dot(h_ref[...], wv_ref[:, cs])))
    for c, (gate, val) in enumerate(pairs):
        cs = slice(c * MXU_N, (c + 1) * MXU_N)
        gate = _causal_conv(gate, cw_ref[:, cs], bufs_ref.at[c], carry_ref, j * n_sub + c, first) + cb_ref[:, cs]
        act = 0.5 * gate * (1.0 + lax.erf(gate * INV_SQRT2))
        o_ref[:, cs] = (act * val).astype(o_ref.dtype)


def _x_mod_specs(tm, d, bps):
    return [pl.BlockSpec((tm, d), lambda i, j: (i, 0)),
            pl.BlockSpec((1, d), lambda i, j: (0, 0)),
            pl.BlockSpec((None, 1, d), lambda i, j: (i // bps, 0, 0)),
            pl.BlockSpec((None, 1, d), lambda i, j: (i // bps, 0, 1))]


def _conv_scratch(tm, tn, nj):
    n_sub = tn // MXU_N
    return [pltpu.VMEM((n_sub, tm + SUBLANES, MXU_N), F32), pltpu.VMEM((nj * n_sub, SUBLANES, MXU_N), F32)]


def _fox_qk_proj(x2, g_pre, mod, w_main, gn, w_small, seq, tm=1024, tn=1024):
    t, d = x2.shape
    n = gn.shape[1]
    return pl.pallas_call(
        _fox_qk_kernel,
        grid=(t // tm, n // tn),
        in_specs=_x_mod_specs(tm, d, seq // tm) + [
            pl.BlockSpec((d, tn), lambda i, j: (0, j)),
            pl.BlockSpec((1, tn), lambda i, j: (0, j)),
            pl.BlockSpec((d, HEAD), lambda i, j: (0, 0))],
        out_specs=[pl.BlockSpec((tm, tn), lambda i, j: (i, j)),
                   pl.BlockSpec((tm, HEAD), lambda i, j: (i, 0)),
                   pl.BlockSpec((tm, d), lambda i, j: (i, 0))],
        out_shape=[jax.ShapeDtypeStruct((t, n), BF16), jax.ShapeDtypeStruct((t, HEAD), F32),
                   jax.ShapeDtypeStruct((t, d), BF16)],
        compiler_params=_cparams("parallel", "arbitrary"),
        name="fox_qk_proj",
    )(x2, g_pre, mod, mod, w_main, gn, w_small)


def _plain_proj(h, w_main, col0, n, tm=1024, tn=1024):
    t, d = h.shape
    off = col0 // tn
    return pl.pallas_call(
        _plain_proj_kernel,
        grid=(t // tm, n // tn),
        in_specs=[pl.BlockSpec((tm, d), lambda i, j: (i, 0)),
                  pl.BlockSpec((d, tn), lambda i, j: (0, off + j))],
        out_specs=pl.BlockSpec((tm, tn), lambda i, j: (i, j)),
        out_shape=jax.ShapeDtypeStruct((t, n), BF16),
        compiler_params=_cparams("parallel", "arbitrary"),
        name="plain_proj",
    )(h, w_main)


def _gdn_qk_proj(x2, g_pre, mod, w_main, conv_w, qscale, w_small, seq, tm=1024, tn=1024):
    t, d = x2.shape
    n = qscale.shape[1]
    taps = conv_w.shape[0]
    return pl.pallas_call(
        functools.partial(_gdn_qk_kernel, blocks_per_seq=seq // tm),
        grid=(t // tm, n // tn),
        in_specs=_x_mod_specs(tm, d, seq // tm) + [
            pl.BlockSpec((d, tn), lambda i, j: (0, j)),
            pl.BlockSpec((taps, tn), lambda i, j: (0, j)),
            pl.BlockSpec((1, tn), lambda i, j: (0, j)),
            pl.BlockSpec((d, HEAD), lambda i, j: (0, 0))],
        out_specs=[pl.BlockSpec((tm, tn), lambda i, j: (i, j)),
                   pl.BlockSpec((tm, HEAD), lambda i, j: (i, 0)),
                   pl.BlockSpec((tm, d), lambda i, j: (i, 0))],
        out_shape=[jax.ShapeDtypeStruct((t, n), BF16), jax.ShapeDtypeStruct((t, HEAD), F32),
                   jax.ShapeDtypeStruct((t, d), BF16)],
        scratch_shapes=_conv_scratch(tm, tn, n // tn),
        compiler_params=_cparams("arbitrary", "arbitrary"),
        name="gdn_qk_proj",
    )(x2, g_pre, mod, mod, w_main, conv_w, qscale, w_small)


def _gdn_v_proj(h, w_main, conv_w, col0, n, seq, tm=1024, tn=1024):
    t, d = h.shape
    taps = conv_w.shape[0]
    off = col0 // tn
    return pl.pallas_call(
        functools.partial(_gdn_v_kernel, blocks_per_seq=seq // tm),
        grid=(t // tm, n // tn),
        in_specs=[pl.BlockSpec((tm, d), lambda i, j: (i, 0)),
                  pl.BlockSpec((d, tn), lambda i, j: (0, off + j)),
                  pl.BlockSpec((taps, tn), lambda i, j: (0, off + j))],
        out_specs=pl.BlockSpec((tm, tn), lambda i, j: (i, j)),
        out_shape=jax.ShapeDtypeStruct((t, n), BF16),
        scratch_shapes=_conv_scratch(tm, tn, n // tn),
        compiler_params=_cparams("arbitrary", "arbitrary"),
        name="gdn_v_proj",
    )(h, w_main, conv_w)


def _ffn_up(x2, g_pre, mod, w_up, conv_w, conv_b, seq, tm=1024, tn=512):
    t, d = x2.shape
    dff = w_up.shape[1] // 2
    nj = dff // tn
    taps = conv_w.shape[0]
    return pl.pallas_call(
        functools.partial(_ffn_up_kernel, blocks_per_seq=seq // tm),
        grid=(t // tm, nj),
        in_specs=_x_mod_specs(tm, d, seq // tm) + [
            pl.BlockSpec((d, tn), lambda i, j: (0, j)),
            pl.BlockSpec((d, tn), lambda i, j: (0, j + nj)),
            pl.BlockSpec((taps, tn), lambda i, j: (0, j)),
            pl.BlockSpec((1, tn), lambda i, j: (0, j))],
        out_specs=pl.BlockSpec((tm, tn), lambda i, j: (i, j)),
        out_shape=jax.ShapeDtypeStruct((t, dff), BF16),
        scratch_shapes=[pltpu.VMEM((tm, d), BF16)] + _conv_scratch(tm, tn, nj),
        compiler_params=_cparams("arbitrary", "arbitrary"),
        name="ffn_up",
    )(x2, g_pre, mod, mod, w_up, w_up, conv_w, conv_b)


def _out_proj_kernel(a_ref, w_ref, x_ref, g_ref, gate_ref, o_ref):
    y = _dot(a_ref[...], w_ref[...])
    ms = jnp.mean(y * y, axis=-1, keepdims=True)
    o_ref[...] = x_ref[...] + gate_ref[...] * (y * lax.rsqrt(ms + EPS) * g_ref[...])


def _out_proj(a, w, x2, g_post, mod, seq, tm):
    t, kd = a.shape
    d = w.shape[1]
    bps = seq // tm
    return pl.pallas_call(
        _out_proj_kernel,
        grid=(t // tm,),
        in_specs=[pl.BlockSpec((tm, kd), lambda i: (i, 0)),
                  pl.BlockSpec((kd, d), lambda i: (0, 0), pipeline_mode=pl.Buffered(1)),
                  pl.BlockSpec((tm, d), lambda i: (i, 0)),
                  pl.BlockSpec((1, d), lambda i: (0, 0)),
                  pl.BlockSpec((None, 1, d), lambda i: (i // bps, 0, 2))],
        out_specs=pl.BlockSpec((tm, d), lambda i: (i, 0)),
        out_shape=jax.ShapeDtypeStruct((t, d), F32),
        compiler_params=_cparams("parallel"),
        name="out_proj",
    )(a, w, x2, g_post, mod)


def _fox_gate_kernel(f_ref, fb_ref, o_ref, carry_ref):
    tm = f_ref.shape[0]

    @pl.when(pl.program_id(1) == 0)
    def _():
        carry_ref[...] = jnp.zeros_like(carry_ref)

    x = f_ref[...] + fb_ref[...]
    logf = (jnp.minimum(x, 0.0) - jnp.log1p(jnp.exp(-jnp.abs(x)))) * LOG2E
    r = lax.broadcasted_iota(jnp.int32, (tm, tm), 0)
    c = lax.broadcasted_iota(jnp.int32, (tm, tm), 1)
    tri = (r >= c).astype(F32)
    cum = _dot_f32(tri, logf) + carry_ref[0:1, :]
    o_ref[...] = cum
    carry_ref[0:1, :] = cum[tm - 1:tm, :]


def _fox_gate_cumsum(f_logit, f_bias_row, nb, seq, tm=256):
    return pl.pallas_call(
        _fox_gate_kernel,
        grid=(nb, seq // tm),
        in_specs=[pl.BlockSpec((None, tm, HEAD), lambda b, i: (b, i, 0)),
                  pl.BlockSpec((1, HEAD), lambda b, i: (0, 0))],
        out_specs=pl.BlockSpec((None, tm, HEAD), lambda b, i: (b, i, 0)),
        out_shape=jax.ShapeDtypeStruct((nb, seq, HEAD), F32),
        scratch_shapes=[pltpu.VMEM((SUBLANES, HEAD), F32)],
        compiler_params=_cparams("parallel", "arbitrary"),
        name="fox_gate_cumsum",
    )(f_logit.reshape(nb, seq, HEAD), f_bias_row)


def _attn_kernel(q_ref, k_ref, v_ref, og_ref, fk_ref, o_ref,
                 s0_ref, s1_ref, m_ref, l_ref, acc_ref, *, tq):
    h = pl.program_id(1)
    qi = pl.program_id(2)
    m_ref[...] = jnp.full(m_ref.shape, -jnp.inf, F32)
    l_ref[...] = jnp.zeros(l_ref.shape, F32)
    acc_ref[...] = jnp.zeros(acc_ref.shape, F32)

    def scores(s_ref, ki):
        ks = pl.multiple_of(ki * tq, tq)
        s_ref[...] = _dot_nt(q_ref[...], k_ref[pl.ds(ks, tq), :]) - fk_ref[pl.ds(h, 1), pl.ds(ks, tq)]

    def update(s_ref, ki, masked):
        ks = pl.multiple_of(ki * tq, tq)
        s = s_ref[...]
        if masked:
            row = lax.broadcasted_iota(jnp.int32, (tq, tq), 0)
            col = lax.broadcasted_iota(jnp.int32, (tq, tq), 1)
            s = jnp.where(col <= row, s, -jnp.inf)
        m_old = m_ref[...]
        m_new = jnp.maximum(m_old, jnp.max(s, axis=-1, keepdims=True))
        p = jnp.exp2(s - m_new)
        alpha = jnp.exp2(m_old - m_new)
        l_ref[...] = alpha * l_ref[...] + jnp.sum(p, axis=-1, keepdims=True)
        m_ref[...] = m_new
        acc_ref[...] = alpha * acc_ref[...] + _dot(p.astype(BF16), v_ref[pl.ds(ks, tq), :])

    scores(s0_ref, 0)

    def pair(pi, carry):
        scores(s1_ref, 2 * pi + 1)
        update(s0_ref, 2 * pi, False)
        scores(s0_ref, 2 * pi + 2)
        update(s1_ref, 2 * pi + 1, False)
        return carry

    lax.fori_loop(0, qi // 2, pair, 0)

    @pl.when(qi % 2 == 1)
    def _():
        scores(s1_ref, qi)
        update(s0_ref, qi - 1, False)
        update(s1_ref, qi, True)

    @pl.when(qi % 2 == 0)
    def _():
        update(s0_ref, qi, True)

    o_ref[...] = (acc_ref[...] / l_ref[...] * jax.nn.sigmoid(og_ref[...].astype(F32))).astype(o_ref.dtype)


def _fox_attention(qk3, vo3, fk, n_heads, tq=512):
    nb, seq, _ = qk3.shape
    return pl.pallas_call(
        functools.partial(_attn_kernel, tq=tq),
        grid=(nb, n_heads, seq // tq),
        in_specs=[pl.BlockSpec((None, tq, HEAD), lambda b, h, i: (b, i, h)),
                  pl.BlockSpec((None, seq, HEAD), lambda b, h, i: (b, 0, n_heads + h)),
                  pl.BlockSpec((None, seq, HEAD), lambda b, h, i: (b, 0, h)),
                  pl.BlockSpec((None, tq, HEAD), lambda b, h, i: (b, i, n_heads + h)),
                  pl.BlockSpec((None, n_heads, seq), lambda b, h, i: (b, 0, 0))],
        out_specs=pl.BlockSpec((None, tq, HEAD), lambda b, h, i: (b, i, h)),
        out_shape=jax.ShapeDtypeStruct((nb, seq, n_heads * HEAD), BF16),
        scratch_shapes=[pltpu.VMEM((tq, tq), F32), pltpu.VMEM((tq, tq), F32),
                        pltpu.VMEM((tq, 1), F32), pltpu.VMEM((tq, 1), F32), pltpu.VMEM((tq, HEAD), F32)],
        compiler_params=_cparams("parallel", "parallel", "arbitrary"),
        name="fox_attention",
    )(qk3, qk3, vo3, vo3, fk)


def _gdn_gate_kernel(ab_ref, alog_ref, dt_ref, o_ref, *, n_heads):
    tm = ab_ref.shape[0]
    x = ab_ref[...]
    lane = lax.broadcasted_iota(jnp.int32, x.shape, 1)
    g = -jnp.exp(alog_ref[...]) * _softplus(x + dt_ref[...])
    r = lax.broadcasted_iota(jnp.int32, (tm, tm), 0)
    c = lax.broadcasted_iota(jnp.int32, (tm, tm), 1)
    tri = ((r >= c) & ((r // CHUNK) == (c // CHUNK))).astype(F32)
    gc = _dot_f32(tri, jnp.where(lane < n_heads, g, 0.0))
    o_ref[...] = jnp.where(lane < n_heads, gc, jax.nn.sigmoid(x))


def _gdn_gates(ab, alog_row, dt_row, n_heads, tm=256):
    t = ab.shape[0]
    return pl.pallas_call(
        functools.partial(_gdn_gate_kernel, n_heads=n_heads),
        grid=(t // tm,),
        in_specs=[pl.BlockSpec((tm, HEAD), lambda i: (i, 0)),
                  pl.BlockSpec((1, HEAD), lambda i: (0, 0)),
                  pl.BlockSpec((1, HEAD), lambda i: (0, 0))],
        out_specs=pl.BlockSpec((tm, HEAD), lambda i: (i, 0)),
        out_shape=jax.ShapeDtypeStruct((t, HEAD), F32),
        compiler_params=_cparams("parallel"),
        name="gdn_gates",
    )(ab, alog_row, dt_row)


def _block_diag2(p, half):
    return jnp.concatenate([jnp.where(half, 0.0, p), jnp.where(half, p, 0.0)], axis=0)


def _gdn_chunk_kernel(q_ref, k_ref, v_ref, gb_ref, grow_ref, u_ref, w_ref, qd_ref, kd_ref, qk_ref, egl_ref,
                      *, n_vheads, chunks):
    hk = pl.program_id(1)
    lane = lax.broadcasted_iota(jnp.int32, (CHUNK, HEAD), 1)
    row = lax.broadcasted_iota(jnp.int32, (CHUNK, HEAD), 0)
    col = lane & (CHUNK - 1)
    half = lane >= CHUNK
    lower = row >= col
    strict = row > col
    eye = (row == col).astype(F32)
    level_masks = []
    for lvl in range(6):
        level_masks.append((((row >> lvl) & 1) == 1) & (((col >> lvl) & 1) == 0)
                           & ((row >> (lvl + 1)) == (col >> (lvl + 1))))
    zeros = jnp.zeros((CHUNK, HEAD), F32)

    def pick(gb, idx):
        return jnp.sum(jnp.where(lane == idx, gb, 0.0), axis=-1, keepdims=True)

    sls = [slice(c * CHUNK, (c + 1) * CHUNK) for c in range(chunks)]
    gcs, bes, a_s, xs = [], [], [], []
    for c, sl in enumerate(sls):
        q = q_ref[sl, :]
        k = k_ref[sl, :]
        gb = gb_ref[sl, :]
        gc0, gc1 = pick(gb, 2 * hk), pick(gb, 2 * hk + 1)
        be0, be1 = pick(gb, n_vheads + 2 * hk), pick(gb, n_vheads + 2 * hk + 1)
        gcol = jnp.where(half, gc1, gc0)
        bcol = jnp.where(half, be1, be0)
        decay = jnp.exp(jnp.where(lower, gcol - grow_ref[c:c + 1, :], -jnp.inf))
        qkk = _dot_nt(jnp.concatenate([q, k], axis=0), jnp.concatenate([k, k], axis=0))
        a = jnp.where(strict, qkk[CHUNK:, :] * decay * bcol, 0.0)
        qk_ref[sl, :] = (qkk[:CHUNK, :] * decay).astype(qk_ref.dtype)
        gcs.append((gc0, gc1))
        bes.append((be0, be1))
        a_s.append(a)
        xs.append(eye - jnp.where(level_masks[0], a, 0.0))

    for lvl in range(1, 6):
        ys = [_dot(xs[c].astype(BF16), _block_diag2(jnp.where(level_masks[lvl], a_s[c], 0.0), half).astype(BF16))
              for c in range(chunks)]
        xs = [xs[c] - _dot(ys[c].astype(BF16), _block_diag2(xs[c], half).astype(BF16)) for c in range(chunks)]

    for c, sl in enumerate(sls):
        (gc0, gc1), (be0, be1) = gcs[c], bes[c]
        kf = k_ref[sl, :].astype(F32)
        qf = q_ref[sl, :].astype(F32)
        v = v_ref[sl, :].astype(F32)
        eg0, eg1 = jnp.exp(gc0), jnp.exp(gc1)
        rhs = jnp.concatenate([
            jnp.concatenate([v[:, :HEAD] * be0, kf * (be0 * eg0), zeros, zeros], axis=1),
            jnp.concatenate([zeros, zeros, v[:, HEAD:] * be1, kf * (be1 * eg1)], axis=1)], axis=0)
        sol = _dot(xs[c].astype(BF16), rhs.astype(BF16))
        u_ref[sl, :] = jnp.concatenate([sol[:, 0:HEAD], sol[:, 2 * HEAD:3 * HEAD]], axis=1).astype(u_ref.dtype)
        w_ref[sl, :] = jnp.concatenate([sol[:, HEAD:2 * HEAD], sol[:, 3 * HEAD:]], axis=1).astype(w_ref.dtype)
        qd_ref[sl, :] = jnp.concatenate([qf * eg0, qf * eg1], axis=1).astype(qd_ref.dtype)
        gl0, gl1 = gc0[CHUNK - 1:CHUNK, :], gc1[CHUNK - 1:CHUNK, :]
        kd_ref[sl, :] = jnp.concatenate([kf * jnp.exp(gl0 - gc0), kf * jnp.exp(gl1 - gc1)],
                                        axis=1).astype(kd_ref.dtype)
        egl_ref[0, c:c + 1, :] = jnp.broadcast_to(jnp.exp(gl0), (1, HEAD))
        egl_ref[1, c:c + 1, :] = jnp.broadcast_to(jnp.exp(gl1), (1, HEAD))


def _gdn_chunks(qk3, v3, gb3, grow, n_kheads, n_vheads, tm=512):
    nb, seq, _ = qk3.shape
    chunks = tm // CHUNK
    vd = n_vheads * HEAD
    big = jax.ShapeDtypeStruct((nb, seq, vd), BF16)
    pair_spec = pl.BlockSpec((None, tm, 2 * HEAD), lambda b, h, i: (b, i, h))
    return pl.pallas_call(
        functools.partial(_gdn_chunk_kernel, n_vheads=n_vheads, chunks=chunks),
        grid=(nb, n_kheads, seq // tm),
        in_specs=[pl.BlockSpec((None, tm, HEAD), lambda b, h, i: (b, i, h)),
                  pl.BlockSpec((None, tm, HEAD), lambda b, h, i: (b, i, n_kheads + h)),
                  pair_spec,
                  pl.BlockSpec((None, tm, HEAD), lambda b, h, i: (b, i, 0)),
                  pl.BlockSpec((None, None, chunks, HEAD), lambda b, h, i: (b, h, i, 0))],
        out_specs=[pair_spec, pair_spec, pair_spec, pair_spec,
                   pl.BlockSpec((None, tm, HEAD), lambda b, h, i: (b, i, h)),
                   pl.BlockSpec((None, 2, chunks, HEAD), lambda b, h, i: (b, h, i, 0))],
        out_shape=[big, big, big, big,
                   jax.ShapeDtypeStruct((nb, seq, n_kheads * HEAD), BF16),
                   jax.ShapeDtypeStruct((nb, n_vheads, seq // CHUNK, HEAD), F32)],
        compiler_params=_cparams("parallel", "parallel", "parallel"),
        name="gdn_chunks",
    )(qk3, qk3, v3, gb3, grow)


def _gdn_scan_kernel(u_ref, w_ref, qd_ref, kd_ref, qk_ref, egl_ref, z_ref, gn_ref, o_ref, s_ref,
                     *, heads, chunks):
    @pl.when(pl.program_id(2) == 0)
    def _():
        s_ref[...] = jnp.zeros_like(s_ref)

    zeros = jnp.zeros((CHUNK, HEAD), F32)
    gn = gn_ref[...]
    cols = [slice(g * HEAD, (g + 1) * HEAD) for g in range(heads)]
    for c in range(chunks):
        sl = slice(c * CHUNK, (c + 1) * CHUNK)
        rs = [_dot(jnp.concatenate([w_ref[sl, cols[g]], qd_ref[sl, cols[g]]], axis=0), s_ref[g].astype(BF16))
              for g in range(heads)]
        vnew = [u_ref[sl, cols[g]].astype(F32) - rs[g][:CHUNK, :] for g in range(heads)]
        intra = []
        for pr in range(heads // 2):
            v2 = jnp.concatenate([jnp.concatenate([vnew[2 * pr], zeros], axis=1),
                                  jnp.concatenate([zeros, vnew[2 * pr + 1]], axis=1)], axis=0).astype(BF16)
            intra.append(_dot(qk_ref[sl, pr * HEAD:(pr + 1) * HEAD], v2))
        for g in range(heads):
            s_ref[g] = s_ref[g] * egl_ref[g, c:c + 1, :] + _dot_tn(kd_ref[sl, cols[g]], vnew[g].astype(BF16))
        for g in range(heads):
            o = rs[g][CHUNK:, :] + intra[g // 2][:, (g % 2) * HEAD:(g % 2 + 1) * HEAD]
            ms = jnp.mean(o * o, axis=-1, keepdims=True)
            z = z_ref[sl, cols[g]].astype(F32)
            o_ref[sl, cols[g]] = (o * lax.rsqrt(ms + EPS) * gn * (z * jax.nn.sigmoid(z))).astype(o_ref.dtype)


def _gdn_scan(u, w, qd, kd, qkm, egl, z3, gn_row, n_vheads, heads=8, tm=512):
    nb, seq, vd = u.shape
    chunks = tm // CHUNK
    wide = pl.BlockSpec((None, tm, heads * HEAD), lambda b, g, i: (b, i, g))
    return pl.pallas_call(
        functools.partial(_gdn_scan_kernel, heads=heads, chunks=chunks),
        grid=(nb, n_vheads // heads, seq // tm),
        in_specs=[wide, wide, wide, wide,
                  pl.BlockSpec((None, tm, heads // 2 * HEAD), lambda b, g, i: (b, i, g)),
                  pl.BlockSpec((None, heads, chunks, HEAD), lambda b, g, i: (b, g, i, 0)),
                  wide,
                  pl.BlockSpec((1, HEAD), lambda b, g, i: (0, 0))],
        out_specs=wide,
        out_shape=jax.ShapeDtypeStruct((nb, seq, vd), BF16),
        scratch_shapes=[pltpu.VMEM((heads, HEAD, HEAD), F32)],
        compiler_params=_cparams("parallel", "parallel", "arbitrary"),
        name="gdn_scan",
    )(u, w, qd, kd, qkm, egl, z3, gn_row)


def _pad_cols(w, n):
    return jnp.pad(w, ((0, 0), (0, n - w.shape[1])))


def _row(v, n=HEAD):
    return jnp.pad(v.astype(F32), (0, n - v.shape[0])).reshape(1, n)


def _fox_layer(x2, nb, seq, g_pre, g_post, mod, w_in, f_bias, q_norm, k_norm, w_o):
    d = x2.shape[1]
    nh = d // HEAD
    qd = nh * HEAD
    w_main = jnp.concatenate([w_in[:, :3 * qd], w_in[:, 3 * qd + nh:]], axis=1).astype(BF16)
    w_small = _pad_cols(w_in[:, 3 * qd:3 * qd + nh], HEAD).astype(BF16)
    gn = jnp.concatenate([jnp.tile(q_norm.astype(F32) * (HEAD ** -0.5 * LOG2E), nh),
                          jnp.tile(k_norm.astype(F32), nh)]).reshape(1, 2 * qd)
    qk, f_logit, h = _fox_qk_proj(x2, g_pre, mod, w_main, gn, w_small, seq)
    vo = _plain_proj(h, w_main, 2 * qd, 2 * qd)
    fcum = _fox_gate_cumsum(f_logit, _row(f_bias), nb, seq)
    fk = jnp.transpose(fcum[:, :, :nh], (0, 2, 1))
    attn = _fox_attention(qk.reshape(nb, seq, 2 * qd), vo.reshape(nb, seq, 2 * qd), fk, nh)
    return _out_proj(attn.reshape(nb * seq, qd), w_o.astype(BF16), x2, g_post, mod, seq, tm=512)


def _gdn_layer(x2, nb, seq, g_pre, g_post, mod, w_in, conv_w, a_log, dt_bias, out_norm, w_o):
    d = x2.shape[1]
    nk = d // HEAD
    nv = 2 * nk
    kd, vd = nk * HEAD, nv * HEAD
    n_main = 2 * kd + 2 * vd
    w_main = w_in[:, :n_main].astype(BF16)
    w_small = _pad_cols(w_in[:, n_main:], HEAD).astype(BF16)
    conv_w = conv_w.astype(F32)
    qscale = jnp.concatenate([jnp.full((kd,), HEAD ** -0.5, F32), jnp.ones((kd,), F32)]).reshape(1, 2 * kd)
    qk, ab, h = _gdn_qk_proj(x2, g_pre, mod, w_main, conv_w, qscale, w_small, seq)
    v = _gdn_v_proj(h, w_main, conv_w, 2 * kd, vd, seq)
    z = _plain_proj(h, w_main, 2 * kd + vd, vd)
    gb = _gdn_gates(ab, _row(a_log), _row(dt_bias), nv)
    n_chunks = seq // CHUNK
    grow = gb[:, :nv].reshape(nb, n_chunks, CHUNK, nk, 2)
    grow = jnp.transpose(grow, (0, 3, 1, 4, 2)).reshape(nb, nk, n_chunks, 2 * CHUNK)
    u, w, qdec, kdec, qkm, egl = _gdn_chunks(qk.reshape(nb, seq, 2 * kd), v.reshape(nb, seq, vd),
                                             gb.reshape(nb, seq, HEAD), grow, nk, nv)
    a = _gdn_scan(u, w, qdec, kdec, qkm, egl, z.reshape(nb, seq, vd), _row(out_norm), nv)
    return _out_proj(a.reshape(nb * seq, vd), w_o.astype(BF16), x2, g_post, mod, seq, tm=512)


def _ffn_layer(x2, seq, g_pre, g_post, mod, w_up, conv_w, conv_b, w_down):
    u = _ffn_up(x2, g_pre, mod, w_up.astype(BF16), conv_w.astype(F32), conv_b.reshape(1, -1).astype(F32), seq)
    return _out_proj(u, w_down.astype(BF16), x2, g_post, mod, seq, tm=256)


def kernel(x, c, ada_w, ada_b, norm_g, fox_w_in, fox_f_bias, fox_q_norm, fox_k_norm, fox_w_o,
           gdn_w_in, gdn_conv_w, gdn_a_log, gdn_dt_bias, gdn_out_norm, gdn_w_o,
           ffn_w_up, ffn_conv_w, ffn_conv_b, ffn_w_down):
    nb, seq, d = x.shape
    depth = ada_w.shape[0]
    mods = _ada_mods(c, ada_w, ada_b)
    x2 = x.reshape(nb * seq, d)
    for i in range(depth):
        g = [norm_g[i, r].reshape(1, d).astype(F32) for r in range(4)]
        j = i // 2
        if i % 2 == 0:
            x2 = _fox_layer(x2, nb, seq, g[0], g[1], mods[2 * i], fox_w_in[j], fox_f_bias[j],
                            fox_q_norm[j], fox_k_norm[j], fox_w_o[j])
        else:
            x2 = _gdn_layer(x2, nb, seq, g[0], g[1], mods[2 * i], gdn_w_in[j], gdn_conv_w[j],
                            gdn_a_log[j], gdn_dt_bias[j], gdn_out_norm[j], gdn_w_o[j])
        x2 = _ffn_layer(x2, seq, g[2], g[3], mods[2 * i + 1], ffn_w_up[i], ffn_conv_w[i],
                        ffn_conv_b[i], ffn_w_down[i])
    return x2.reshape(nb, seq, d)
```

```python
import functools

import jax
import jax.numpy as jnp
from jax import lax
from jax.experimental import pallas as pl
from jax.experimental.pallas import tpu as pltpu

F32 = jnp.float32
BF16 = jnp.bfloat16
EPS = 1e-6
HEAD = 128
CHUNK = 64
SUBLANES = 8
MXU_N = 256
VMEM_LIMIT = 56 * 1024 * 1024
INV_SQRT2 = 0.7071067811865476
LOG2E = 1.4426950408889634


def _cparams(*sem):
    return pltpu.CompilerParams(dimension_semantics=sem, vmem_limit_bytes=VMEM_LIMIT)


def _dot(a, b):
    return jnp.dot(a, b, preferred_element_type=F32)


def _dot_nt(a, b):
    return lax.dot_general(a, b, (((1,), (1,)), ((), ())), preferred_element_type=F32)


def _dot_tn(a, b):
    return lax.dot_general(a, b, (((0,), (0,)), ((), ())), preferred_element_type=F32)


def _dot_f32(a, b):
    return jnp.dot(a, b, preferred_element_type=F32, precision=lax.Precision.HIGHEST)


def _softplus(x):
    return jnp.maximum(x, 0.0) + jnp.log1p(jnp.exp(-jnp.abs(x)))


def _ada_kernel(c_ref, w_ref, b_ref, o_ref):
    c = c_ref[...]
    o_ref[0] = _dot_f32(c * jax.nn.sigmoid(c), w_ref[0]) + b_ref[0]


def _ada_mods(c, ada_w, ada_b):
    nb, d = c.shape
    n = ada_w.shape[0] * ada_w.shape[1]
    w = ada_w.reshape(n, d, 3 * d)
    b = ada_b.reshape(n, 1, 3 * d)
    cp = jnp.zeros((SUBLANES, d), F32).at[:nb].set(c)
    tn = 1024
    out = pl.pallas_call(
        _ada_kernel,
        grid=(n, 3 * d // tn),
        in_specs=[pl.BlockSpec((SUBLANES, d), lambda s, j: (0, 0)),
                  pl.BlockSpec((1, d, tn), lambda s, j: (s, 0, j)),
                  pl.BlockSpec((1, 1, tn), lambda s, j: (s, 0, j))],
        out_specs=pl.BlockSpec((1, SUBLANES, tn), lambda s, j: (s, 0, j)),
        out_shape=jax.ShapeDtypeStruct((n, SUBLANES, 3 * d), F32),
        compiler_params=_cparams("parallel", "parallel"),
        name="ada_mods",
    )(cp, w, b)
    return out[:, :nb].reshape(n, nb, 1, 3 * d)


def _norm_mod(h_ref, x_ref, g_ref, sh_ref, sc_ref):
    x = x_ref[...]
    ms = jnp.mean(x * x, axis=-1, keepdims=True)
    y = x * lax.rsqrt(ms + EPS) * g_ref[...]
    h_ref[...] = (y * (1.0 + sc_ref[...]) + sh_ref[...]).astype(BF16)


def _sub_dots(h_ref, w_ref):
    return [_dot(h_ref[...], w_ref[:, c * MXU_N:(c + 1) * MXU_N]) for c in range(w_ref.shape[1] // MXU_N)]


def _head_norm_store(o_ref, c, y, gain_ref, mean):
    for g in range(MXU_N // HEAD):
        sl = slice(c * MXU_N + g * HEAD, c * MXU_N + (g + 1) * HEAD)
        t = y[:, g * HEAD:(g + 1) * HEAD]
        ss = jnp.sum(t * t, axis=-1, keepdims=True)
        if mean:
            ss = ss * (1.0 / HEAD)
        o_ref[:, sl] = (t * lax.rsqrt(ss + EPS) * gain_ref[:, sl]).astype(o_ref.dtype)


def _causal_conv(acc, cw, buf_ref, carry_ref, slot, first):
    tm = acc.shape[0]
    taps = cw.shape[0]
    prev = jnp.where(first, 0.0, carry_ref[slot])
    carry_ref[slot] = acc[tm - SUBLANES:tm, :]
    row = lax.broadcasted_iota(jnp.int32, prev.shape, 0)
    y = cw[taps - 1:taps, :] * acc
    for k in range(taps - 1):
        d = taps - 1 - k
        r = pltpu.roll(acc, d, axis=0)
        top = jnp.where(row < d, pltpu.roll(prev, d, axis=0), r[0:SUBLANES, :])
        y = y + cw[k:k + 1, :] * jnp.concatenate([top, r[SUBLANES:, :]], axis=0)
    return y


def _zero_carry_at_start(carry_ref):
    @pl.when((pl.program_id(0) == 0) & (pl.program_id(1) == 0))
    def _():
        carry_ref[...] = jnp.zeros_like(carry_ref)


def _fox_qk_kernel(x_ref, g_ref, sh_ref, sc_ref, w_ref, gn_ref, ws_ref, o_ref, os_ref, h_ref):
    @pl.when(pl.program_id(1) == 0)
    def _():
        _norm_mod(h_ref, x_ref, g_ref, sh_ref, sc_ref)
        os_ref[...] = _dot(h_ref[...], ws_ref[...])

    for c, acc in enumerate(_sub_dots(h_ref, w_ref)):
        _head_norm_store(o_ref, c, acc, gn_ref, mean=True)


def _plain_proj_kernel(h_ref, w_ref, o_ref):
    o_ref[...] = _dot(h_ref[...], w_ref[...]).astype(o_ref.dtype)


def _gdn_qk_kernel(x_ref, g_ref, sh_ref, sc_ref, w_ref, cw_ref, qs_ref, ws_ref, o_ref, os_ref, h_ref,
                   bufs_ref, carry_ref, *, blocks_per_seq):
    i = pl.program_id(0)
    j = pl.program_id(1)
    _zero_carry_at_start(carry_ref)

    @pl.when(j == 0)
    def _():
        _norm_mod(h_ref, x_ref, g_ref, sh_ref, sc_ref)
        os_ref[...] = _dot(h_ref[...], ws_ref[...])

    first = (i % blocks_per_seq) == 0
    accs = _sub_dots(h_ref, w_ref)
    for c, acc in enumerate(accs):
        y = _causal_conv(acc, cw_ref[:, c * MXU_N:(c + 1) * MXU_N], bufs_ref.at[c], carry_ref,
                         j * len(accs) + c, first)
        _head_norm_store(o_ref, c, y * jax.nn.sigmoid(y), qs_ref, mean=False)


def _gdn_v_kernel(h_ref, w_ref, cw_ref, o_ref, bufs_ref, carry_ref, *, blocks_per_seq):
    i = pl.program_id(0)
    j = pl.program_id(1)
    _zero_carry_at_start(carry_ref)
    first = (i % blocks_per_seq) == 0
    n_sub = w_ref.shape[1] // MXU_N

    def epilogue(c, acc):
        cs = slice(c * MXU_N, (c + 1) * MXU_N)
        y = _causal_conv(acc, cw_ref[:, cs], bufs_ref.at[c], carry_ref, j * n_sub + c, first)
        o_ref[:, cs] = (y * jax.nn.sigmoid(y)).astype(o_ref.dtype)

    prev = None
    for c in range(n_sub):
        acc = _dot(h_ref[...], w_ref[:, c * MXU_N:(c + 1) * MXU_N])
        if prev is not None:
            epilogue(c - 1, prev)
        prev = acc
    epilogue(n_sub - 1, prev)


def _ffn_up_kernel(x_ref, g_ref, sh_ref, sc_ref, wg_ref, wv_ref, cw_ref, cb_ref, o_ref,
                   h_ref, bufs_ref, carry_ref, *, blocks_per_seq):
    i = pl.program_id(0)
    j = pl.program_id(1)
    _zero_carry_at_start(carry_ref)

    @pl.when(j == 0)
    def _():
        _norm_mod(h_ref, x_ref, g_ref, sh_ref, sc_ref)

    first = (i % blocks_per_seq) == 0
    n_sub = wg_ref.shape[1] // MXU_N
    pairs = []
    for c in range(n_sub):
        cs = slice(c * MXU_N, (c + 1) * MXU_N)
        pairs.append((_dot(h_ref[...], wg_ref[:, cs]), _dot(h_ref[...], wv_ref[:, cs])))
    for c, (gate, val) in enumerate(pairs):
        cs = slice(c * MXU_N, (c + 1) * MXU_N)
        gate = _causal_conv(gate, cw_ref[:, cs], bufs_ref.at[c], carry_ref, j * n_sub + c, first) + cb_ref[:, cs]
        act = 0.5 * gate * (1.0 + lax.erf(gate * INV_SQRT2))
        o_ref[:, cs] = (act * val).astype(o_ref.dtype)


def _x_mod_specs(tm, d, bps):
    return [pl.BlockSpec((tm, d), lambda i, j: (i, 0)),
            pl.BlockSpec((1, d), lambda i, j: (0, 0)),
            pl.BlockSpec((None, 1, d), lambda i, j: (i // bps, 0, 0)),
            pl.BlockSpec((None, 1, d), lambda i, j: (i // bps, 0, 1))]


def _conv_scratch(tm, tn, nj):
    n_sub = tn // MXU_N
    return [pltpu.VMEM((n_sub, tm + SUBLANES, MXU_N), F32), pltpu.VMEM((nj * n_sub, SUBLANES, MXU_N), F32)]


def _fox_qk_proj(x2, g_pre, mod, w_main, gn, w_small, seq, tm=1024, tn=1024):
    t, d = x2.shape
    n = gn.shape[1]
    return pl.pallas_call(
        _fox_qk_kernel,
        grid=(t // tm, n // tn),
        in_specs=_x_mod_specs(tm, d, seq // tm) + [
            pl.BlockSpec((d, tn), lambda i, j: (0, j)),
            pl.BlockSpec((1, tn), lambda i, j: (0, j)),
            pl.BlockSpec((d, HEAD), lambda i, j: (0, 0))],
        out_specs=[pl.BlockSpec((tm, tn), lambda i, j: (i, j)),
                   pl.BlockSpec((tm, HEAD), lambda i, j: (i, 0)),
                   pl.BlockSpec((tm, d), lambda i, j: (i, 0))],
        out_shape=[jax.ShapeDtypeStruct((t, n), BF16), jax.ShapeDtypeStruct((t, HEAD), F32),
                   jax.ShapeDtypeStruct((t, d), BF16)],
        compiler_params=_cparams("parallel", "arbitrary"),
        name="fox_qk_proj",
    )(x2, g_pre, mod, mod, w_main, gn, w_small)


def _plain_proj(h, w_main, col0, n, tm=1024, tn=1024):
    t, d = h.shape
    off = col0 // tn
    return pl.pallas_call(
        _plain_proj_kernel,
        grid=(t // tm, n // tn),
        in_specs=[pl.BlockSpec((tm, d), lambda i, j: (i, 0)),
                  pl.BlockSpec((d, tn), lambda i, j: (0, off + j))],
        out_specs=pl.BlockSpec((tm, tn), lambda i, j: (i, j)),
        out_shape=jax.ShapeDtypeStruct((t, n), BF16),
        compiler_params=_cparams("parallel", "arbitrary"),
        name="plain_proj",
    )(h, w_main)


def _gdn_qk_proj(x2, g_pre, mod, w_main, conv_w, qscale, w_small, seq, tm=1024, tn=1024):
    t, d = x2.shape
    n = qscale.shape[1]
    taps = conv_w.shape[0]
    return pl.pallas_call(
        functools.partial(_gdn_qk_kernel, blocks_per_seq=seq // tm),
        grid=(t // tm, n // tn),
        in_specs=_x_mod_specs(tm, d, seq // tm) + [
            pl.BlockSpec((d, tn), lambda i, j: (0, j)),
            pl.BlockSpec((taps, tn), lambda i, j: (0, j)),
            pl.BlockSpec((1, tn), lambda i, j: (0, j)),
            pl.BlockSpec((d, HEAD), lambda i, j: (0, 0))],
        out_specs=[pl.BlockSpec((tm, tn), lambda i, j: (i, j)),
                   pl.BlockSpec((tm, HEAD), lambda i, j: (i, 0)),
                   pl.BlockSpec((tm, d), lambda i, j: (i, 0))],
        out_shape=[jax.ShapeDtypeStruct((t, n), BF16), jax.ShapeDtypeStruct((t, HEAD), F32),
                   jax.ShapeDtypeStruct((t, d), BF16)],
        scratch_shapes=_conv_scratch(tm, tn, n // tn),
        compiler_params=_cparams("arbitrary", "arbitrary"),
        name="gdn_qk_proj",
    )(x2, g_pre, mod, mod, w_main, conv_w, qscale, w_small)


def _gdn_v_proj(h, w_main, conv_w, col0, n, seq, tm=1024, tn=1024):
    t, d = h.shape
    taps = conv_w.shape[0]
    off = col0 // tn
    return pl.pallas_call(
        functools.partial(_gdn_v_kernel, blocks_per_seq=seq // tm),
        grid=(t // tm, n // tn),
        in_specs=[pl.BlockSpec((tm, d), lambda i, j: (i, 0)),
                  pl.BlockSpec((d, tn), lambda i, j: (0, off + j)),
                  pl.BlockSpec((taps, tn), lambda i, j: (0, off + j))],
        out_specs=pl.BlockSpec((tm, tn), lambda i, j: (i, j)),
        out_shape=jax.ShapeDtypeStruct((t, n), BF16),
        scratch_shapes=_conv_scratch(tm, tn, n // tn),
        compiler_params=_cparams("arbitrary", "arbitrary"),
        name="gdn_v_proj",
    )(h, w_main, conv_w)


def _ffn_up(x2, g_pre, mod, w_up, conv_w, conv_b, seq, tm=1024, tn=512):
    t, d = x2.shape
    dff = w_up.shape[1] // 2
    nj = dff // tn
    taps = conv_w.shape[0]
    return pl.pallas_call(
        functools.partial(_ffn_up_kernel, blocks_per_seq=seq // tm),
        grid=(t // tm, nj),
        in_specs=_x_mod_specs(tm, d, seq // tm) + [
            pl.BlockSpec((d, tn), lambda i, j: (0, j)),
            pl.BlockSpec((d, tn), lambda i, j: (0, j + nj)),
            pl.BlockSpec((taps, tn), lambda i, j: (0, j)),
            pl.BlockSpec((1, tn), lambda i, j: (0, j))],
        out_specs=pl.BlockSpec((tm, tn), lambda i, j: (i, j)),
        out_shape=jax.ShapeDtypeStruct((t, dff), BF16),
        scratch_shapes=[pltpu.VMEM((tm, d), BF16)] + _conv_scratch(tm, tn, nj),
        compiler_params=_cparams("arbitrary", "arbitrary"),
        name="ffn_up",
    )(x2, g_pre, mod, mod, w_up, w_up, conv_w, conv_b)


def _out_proj_kernel(a_ref, w_ref, x_ref, g_ref, gate_ref, o_ref):
    y = _dot(a_ref[...], w_ref[...])
    ms = jnp.mean(y * y, axis=-1, keepdims=True)
    o_ref[...] = x_ref[...] + gate_ref[...] * (y * lax.rsqrt(ms + EPS) * g_ref[...])


def _out_proj(a, w, x2, g_post, mod, seq, tm):
    t, kd = a.shape
    d = w.shape[1]
    bps = seq // tm
    return pl.pallas_call(
        _out_proj_kernel,
        grid=(t // tm,),
        in_specs=[pl.BlockSpec((tm, kd), lambda i: (i, 0)),
                  pl.BlockSpec((kd, d), lambda i: (0, 0), pipeline_mode=pl.Buffered(1)),
                  pl.BlockSpec((tm, d), lambda i: (i, 0)),
                  pl.BlockSpec((1, d), lambda i: (0, 0)),
                  pl.BlockSpec((None, 1, d), lambda i: (i // bps, 0, 2))],
        out_specs=pl.BlockSpec((tm, d), lambda i: (i, 0)),
        out_shape=jax.ShapeDtypeStruct((t, d), F32),
        compiler_params=_cparams("parallel"),
        name="out_proj",
    )(a, w, x2, g_post, mod)


def _fox_gate_kernel(f_ref, fb_ref, o_ref, carry_ref):
    tm = f_ref.shape[0]

    @pl.when(pl.program_id(1) == 0)
    def _():
        carry_ref[...] = jnp.zeros_like(carry_ref)

    x = f_ref[...] + fb_ref[...]
    logf = (jnp.minimum(x, 0.0) - jnp.log1p(jnp.exp(-jnp.abs(x)))) * LOG2E
    r = lax.broadcasted_iota(jnp.int32, (tm, tm), 0)
    c = lax.broadcasted_iota(jnp.int32, (tm, tm), 1)
    tri = (r >= c).astype(F32)
    cum = _dot_f32(tri, logf) + carry_ref[0:1, :]
    o_ref[...] = cum
    carry_ref[0:1, :] = cum[tm - 1:tm, :]


def _fox_gate_cumsum(f_logit, f_bias_row, nb, seq, tm=256):
    return pl.pallas_call(
        _fox_gate_kernel,
        grid=(nb, seq // tm),
        in_specs=[pl.BlockSpec((None, tm, HEAD), lambda b, i: (b, i, 0)),
                  pl.BlockSpec((1, HEAD), lambda b, i: (0, 0))],
        out_specs=pl.BlockSpec((None, tm, HEAD), lambda b, i: (b, i, 0)),
        out_shape=jax.ShapeDtypeStruct((nb, seq, HEAD), F32),
        scratch_shapes=[pltpu.VMEM((SUBLANES, HEAD), F32)],
        compiler_params=_cparams("parallel", "arbitrary"),
        name="fox_gate_cumsum",
    )(f_logit.reshape(nb, seq, HEAD), f_bias_row)


def _attn_kernel(q_ref, k_ref, v_ref, og_ref, fk_ref, o_ref,
                 s0_ref, s1_ref, m_ref, l_ref, acc_ref, *, tq):
    h = pl.program_id(1)
    qi = pl.program_id(2)
    m_ref[...] = jnp.full(m_ref.shape, -jnp.inf, F32)
    l_ref[...] = jnp.zeros(l_ref.shape, F32)
    acc_ref[...] = jnp.zeros(acc_ref.shape, F32)

    def scores(s_ref, ki):
        ks = pl.multiple_of(ki * tq, tq)
        s_ref[...] = _dot_nt(q_ref[...], k_ref[pl.ds(ks, tq), :]) - fk_ref[pl.ds(h, 1), pl.ds(ks, tq)]

    def update(s_ref, ki, masked):
        ks = pl.multiple_of(ki * tq, tq)
        s = s_ref[...]
        if masked:
            row = lax.broadcasted_iota(jnp.int32, (tq, tq), 0)
            col = lax.broadcasted_iota(jnp.int32, (tq, tq), 1)
            s = jnp.where(col <= row, s, -jnp.inf)
        m_old = m_ref[...]
        m_new = jnp.maximum(m_old, jnp.max(s, axis=-1, keepdims=True))
        p = jnp.exp2(s - m_new)
        alpha = jnp.exp2(m_old - m_new)
        l_ref[...] = alpha * l_ref[...] + jnp.sum(p, axis=-1, keepdims=True)
        m_ref[...] = m_new
        acc_ref[...] = alpha * acc_ref[...] + _dot(p.astype(BF16), v_ref[pl.ds(ks, tq), :])

    scores(s0_ref, 0)

    def pair(pi, carry):
        scores(s1_ref, 2 * pi + 1)
        update(s0_ref, 2 * pi, False)
        scores(s0_ref, 2 * pi + 2)
        update(s1_ref, 2 * pi + 1, False)
        return carry

    lax.fori_loop(0, qi // 2, pair, 0)

    @pl.when(qi % 2 == 1)
    def _():
        scores(s1_ref, qi)
        update(s0_ref, qi - 1, False)
        update(s1_ref, qi, True)

    @pl.when(qi % 2 == 0)
    def _():
        update(s0_ref, qi, True)

    o_ref[...] = (acc_ref[...] / l_ref[...] * jax.nn.sigmoid(og_ref[...].astype(F32))).astype(o_ref.dtype)


def _split3(x):
    hi = x.astype(BF16).astype(F32)
    mid = (x - hi).astype(BF16).astype(F32)
    return hi, mid, x - hi - mid


def _attn_fixed_shift_kernel(q_ref, k_ref, v_ref, og_ref, fc_ref, c_ref, o_ref,
                             ka_ref, va_ref, p0_ref, p1_ref, acc_ref, *, tq):
    h = pl.program_id(1)
    qi = pl.program_id(2)
    seq = k_ref.shape[0]

    def head_col(rows):
        lane = lax.broadcasted_iota(jnp.int32, (tq, HEAD), 1)
        return jnp.sum(jnp.where(lane == h, fc_ref[rows, :], 0.0), axis=-1, keepdims=True)

    def extra_cols(first, second):
        lane = lax.broadcasted_iota(jnp.int32, (tq, HEAD), 1)
        cols = jnp.zeros((tq, HEAD), F32)
        for n, term in enumerate(first + second):
            cols = jnp.where(lane == n, term, cols)
        return cols.astype(BF16)

    ones3 = (1.0, 1.0, 1.0)

    @pl.when(qi == 0)
    def _():
        def build(i, carry):
            rows = pl.ds(pl.multiple_of(i * tq, tq), tq)
            hi, mid, lo = _split3(-head_col(rows))
            ka_ref[rows, 0:HEAD] = k_ref[rows, :]
            ka_ref[rows, HEAD:2 * HEAD] = extra_cols((hi, mid, lo), ones3)
            va_ref[rows, 0:HEAD] = v_ref[rows, :]
            va_ref[rows, HEAD:2 * HEAD] = jnp.ones((tq, HEAD), BF16)
            return carry
        lax.fori_loop(0, seq // tq, build, 0)

    q_rows = pl.ds(pl.multiple_of(qi * tq, tq), tq)
    qa = jnp.concatenate([q_ref[...], extra_cols(ones3, _split3(head_col(q_rows) - c_ref[0:1, 0:1]))], axis=1)

    def key_rows(ki):
        return pl.ds(pl.multiple_of(ki * tq, tq), tq)

    def probs(p_ref, ki, masked):
        s = _dot_nt(qa, ka_ref[key_rows(ki), :])
        if masked:
            row = lax.broadcasted_iota(jnp.int32, (tq, tq), 0)
            col = lax.broadcasted_iota(jnp.int32, (tq, tq), 1)
            s = jnp.where(col <= row, s, -jnp.inf)
        p_ref[...] = jnp.exp2(s).astype(BF16)

    def accumulate(p_ref, ki):
        acc_ref[...] += _dot(p_ref[...], va_ref[key_rows(ki), :])

    acc_ref[...] = jnp.zeros(acc_ref.shape, F32)
    probs(p0_ref, qi, True)

    def pair(pi, carry):
        probs(p1_ref, 2 * pi, False)
        accumulate(p0_ref, jnp.where(pi == 0, qi, 2 * pi - 1))
        probs(p0_ref, 2 * pi + 1, False)
        accumulate(p1_ref, 2 * pi)
        return carry

    n_pairs = qi // 2
    lax.fori_loop(0, n_pairs, pair, 0)
    held = jnp.where(n_pairs == 0, qi, 2 * n_pairs - 1)

    @pl.when(qi % 2 == 1)
    def _():
        probs(p1_ref, qi - 1, False)
        accumulate(p0_ref, held)
        accumulate(p1_ref, qi - 1)

    @pl.when(qi % 2 == 0)
    def _():
        accumulate(p0_ref, held)

    acc = acc_ref[...]
    o_ref[...] = (acc[:, 0:HEAD] / acc[:, HEAD:HEAD + 1]
                  * jax.nn.sigmoid(og_ref[...].astype(F32))).astype(o_ref.dtype)


def _fox_attention_fixed_shift(qk3, vo3, fcum, c_row, n_heads, tq=512):
    nb, seq, _ = qk3.shape
    return pl.pallas_call(
        functools.partial(_attn_fixed_shift_kernel, tq=tq),
        grid=(nb, n_heads, seq // tq),
        in_specs=[pl.BlockSpec((None, tq, HEAD), lambda b, h, i: (b, i, h)),
                  pl.BlockSpec((None, seq, HEAD), lambda b, h, i: (b, 0, n_heads + h)),
                  pl.BlockSpec((None, seq, HEAD), lambda b, h, i: (b, 0, h)),
                  pl.BlockSpec((None, tq, HEAD), lambda b, h, i: (b, i, n_heads + h)),
                  pl.BlockSpec((None, seq, HEAD), lambda b, h, i: (b, 0, 0)),
                  pl.BlockSpec((1, HEAD), lambda b, h, i: (0, 0))],
        out_specs=pl.BlockSpec((None, tq, HEAD), lambda b, h, i: (b, i, h)),
        out_shape=jax.ShapeDtypeStruct((nb, seq, n_heads * HEAD), BF16),
        scratch_shapes=[pltpu.VMEM((seq, 2 * HEAD), BF16), pltpu.VMEM((seq, 2 * HEAD), BF16),
                        pltpu.VMEM((tq, tq), BF16), pltpu.VMEM((tq, tq), BF16),
                        pltpu.VMEM((tq, 2 * HEAD), F32)],
        compiler_params=_cparams("parallel", "parallel", "arbitrary"),
        name="fox_attention_fixed_shift",
    )(qk3, qk3, vo3, vo3, fcum, c_row)


MAX_SHIFT_GAP = 100.0


def _fox_attention(qk3, vo3, fcum, gn, n_heads):
    qd = n_heads * HEAD
    c = (HEAD * 1.01) * jnp.max(jnp.abs(gn[:, :qd])) * jnp.max(jnp.abs(gn[:, qd:]))
    c_row = jnp.full((1, HEAD), c, F32)

    def running_max(qk3, vo3, fcum, c_row):
        fk = jnp.transpose(fcum[:, :, :n_heads], (0, 2, 1))
        return _fox_attention_running_max(qk3, vo3, fk, n_heads)

    def fixed_shift(qk3, vo3, fcum, c_row):
        return _fox_attention_fixed_shift(qk3, vo3, fcum, c_row, n_heads)

    return lax.cond(2.0 * c <= MAX_SHIFT_GAP, fixed_shift, running_max, qk3, vo3, fcum, c_row)


def _fox_attention_running_max(qk3, vo3, fk, n_heads, tq=512):
    nb, seq, _ = qk3.shape
    return pl.pallas_call(
        functools.partial(_attn_kernel, tq=tq),
        grid=(nb, n_heads, seq // tq),
        in_specs=[pl.BlockSpec((None, tq, HEAD), lambda b, h, i: (b, i, h)),
                  pl.BlockSpec((None, seq, HEAD), lambda b, h, i: (b, 0, n_heads + h)),
                  pl.BlockSpec((None, seq, HEAD), lambda b, h, i: (b, 0, h)),
                  pl.BlockSpec((None, tq, HEAD), lambda b, h, i: (b, i, n_heads + h)),
                  pl.BlockSpec((None, n_heads, seq), lambda b, h, i: (b, 0, 0))],
        out_specs=pl.BlockSpec((None, tq, HEAD), lambda b, h, i: (b, i, h)),
        out_shape=jax.ShapeDtypeStruct((nb, seq, n_heads * HEAD), BF16),
        scratch_shapes=[pltpu.VMEM((tq, tq), F32), pltpu.VMEM((tq, tq), F32),
                        pltpu.VMEM((tq, 1), F32), pltpu.VMEM((tq, 1), F32), pltpu.VMEM((tq, HEAD), F32)],
        compiler_params=_cparams("parallel", "parallel", "arbitrary"),
        name="fox_attention",
    )(qk3, qk3, vo3, vo3, fk)


def _gdn_gate_kernel(ab_ref, alog_ref, dt_ref, o_ref, *, n_heads):
    tm = ab_ref.shape[0]
    x = ab_ref[...]
    lane = lax.broadcasted_iota(jnp.int32, x.shape, 1)
    g = -jnp.exp(alog_ref[...]) * _softplus(x + dt_ref[...])
    r = lax.broadcasted_iota(jnp.int32, (tm, tm), 0)
    c = lax.broadcasted_iota(jnp.int32, (tm, tm), 1)
    tri = ((r >= c) & ((r // CHUNK) == (c // CHUNK))).astype(F32)
    gc = _dot_f32(tri, jnp.where(lane < n_heads, g, 0.0))
    o_ref[...] = jnp.where(lane < n_heads, gc, jax.nn.sigmoid(x))


def _gdn_gates(ab, alog_row, dt_row, n_heads, tm=256):
    t = ab.shape[0]
    return pl.pallas_call(
        functools.partial(_gdn_gate_kernel, n_heads=n_heads),
        grid=(t // tm,),
        in_specs=[pl.BlockSpec((tm, HEAD), lambda i: (i, 0)),
                  pl.BlockSpec((1, HEAD), lambda i: (0, 0)),
                  pl.BlockSpec((1, HEAD), lambda i: (0, 0))],
        out_specs=pl.BlockSpec((tm, HEAD), lambda i: (i, 0)),
        out_shape=jax.ShapeDtypeStruct((t, HEAD), F32),
        compiler_params=_cparams("parallel"),
        name="gdn_gates",
    )(ab, alog_row, dt_row)


def _block_diag2(p, half):
    return jnp.concatenate([jnp.where(half, 0.0, p), jnp.where(half, p, 0.0)], axis=0)


def _gdn_chunk_kernel(q_ref, k_ref, v_ref, gb_ref, grow_ref, u_ref, w_ref, qd_ref, kd_ref, qk_ref, egl_ref,
                      *, n_vheads, chunks):
    hk = pl.program_id(1)
    lane = lax.broadcasted_iota(jnp.int32, (CHUNK, HEAD), 1)
    row = lax.broadcasted_iota(jnp.int32, (CHUNK, HEAD), 0)
    col = lane & (CHUNK - 1)
    half = lane >= CHUNK
    lower = row >= col
    strict = row > col
    eye = (row == col).astype(F32)
    level_masks = []
    for lvl in range(6):
        level_masks.append((((row >> lvl) & 1) == 1) & (((col >> lvl) & 1) == 0)
                           & ((row >> (lvl + 1)) == (col >> (lvl + 1))))
    zeros = jnp.zeros((CHUNK, HEAD), F32)

    def pick(gb, idx):
        return jnp.sum(jnp.where(lane == idx, gb, 0.0), axis=-1, keepdims=True)

    sls = [slice(c * CHUNK, (c + 1) * CHUNK) for c in range(chunks)]
    gcs, bes, a_s, xs = [], [], [], []
    for c, sl in enumerate(sls):
        q = q_ref[sl, :]
        k = k_ref[sl, :]
        gb = gb_ref[sl, :]
        gc0, gc1 = pick(gb, 2 * hk), pick(gb, 2 * hk + 1)
        be0, be1 = pick(gb, n_vheads + 2 * hk), pick(gb, n_vheads + 2 * hk + 1)
        gcol = jnp.where(half, gc1, gc0)
        bcol = jnp.where(half, be1, be0)
        decay = jnp.exp(jnp.where(lower, gcol - grow_ref[c:c + 1, :], -jnp.inf))
        qkk = _dot_nt(jnp.concatenate([q, k], axis=0), jnp.concatenate([k, k], axis=0))
        a = jnp.where(strict, qkk[CHUNK:, :] * decay * bcol, 0.0)
        qk_ref[sl, :] = (qkk[:CHUNK, :] * decay).astype(qk_ref.dtype)
        gcs.append((gc0, gc1))
        bes.append((be0, be1))
        a_s.append(a)
        xs.append(eye - jnp.where(level_masks[0], a, 0.0))

    for lvl in range(1, 6):
        ys = [_dot(xs[c].astype(BF16), _block_diag2(jnp.where(level_masks[lvl], a_s[c], 0.0), half).astype(BF16))
              for c in range(chunks)]
        xs = [xs[c] - _dot(ys[c].astype(BF16), _block_diag2(xs[c], half).astype(BF16)) for c in range(chunks)]

    for c, sl in enumerate(sls):
        (gc0, gc1), (be0, be1) = gcs[c], bes[c]
        kf = k_ref[sl, :].astype(F32)
        qf = q_ref[sl, :].astype(F32)
        v = v_ref[sl, :].astype(F32)
        eg0, eg1 = jnp.exp(gc0), jnp.exp(gc1)
        rhs = jnp.concatenate([
            jnp.concatenate([v[:, :HEAD] * be0, kf * (be0 * eg0), zeros, zeros], axis=1),
            jnp.concatenate([zeros, zeros, v[:, HEAD:] * be1, kf * (be1 * eg1)], axis=1)], axis=0)
        sol = _dot(xs[c].astype(BF16), rhs.astype(BF16))
        u_ref[sl, :] = jnp.concatenate([sol[:, 0:HEAD], sol[:, 2 * HEAD:3 * HEAD]], axis=1).astype(u_ref.dtype)
        w_ref[sl, :] = jnp.concatenate([sol[:, HEAD:2 * HEAD], sol[:, 3 * HEAD:]], axis=1).astype(w_ref.dtype)
        qd_ref[sl, :] = jnp.concatenate([qf * eg0, qf * eg1], axis=1).astype(qd_ref.dtype)
        gl0, gl1 = gc0[CHUNK - 1:CHUNK, :], gc1[CHUNK - 1:CHUNK, :]
        kd_ref[sl, :] = jnp.concatenate([kf * jnp.exp(gl0 - gc0), kf * jnp.exp(gl1 - gc1)],
                                        axis=1).astype(kd_ref.dtype)
        egl_ref[0, c:c + 1, :] = jnp.broadcast_to(jnp.exp(gl0), (1, HEAD))
        egl_ref[1, c:c + 1, :] = jnp.broadcast_to(jnp.exp(gl1), (1, HEAD))


def _gdn_chunks(qk3, v3, gb3, grow, n_kheads, n_vheads, tm=512):
    nb, seq, _ = qk3.shape
    chunks = tm // CHUNK
    vd = n_vheads * HEAD
    big = jax.ShapeDtypeStruct((nb, seq, vd), BF16)
    pair_spec = pl.BlockSpec((None, tm, 2 * HEAD), lambda b, h, i: (b, i, h))
    return pl.pallas_call(
        functools.partial(_gdn_chunk_kernel, n_vheads=n_vheads, chunks=chunks),
        grid=(nb, n_kheads, seq // tm),
        in_specs=[pl.BlockSpec((None, tm, HEAD), lambda b, h, i: (b, i, h)),
                  pl.BlockSpec((None, tm, HEAD), lambda b, h, i: (b, i, n_kheads + h)),
                  pair_spec,
                  pl.BlockSpec((None, tm, HEAD), lambda b, h, i: (b, i, 0)),
                  pl.BlockSpec((None, None, chunks, HEAD), lambda b, h, i: (b, h, i, 0))],
        out_specs=[pair_spec, pair_spec, pair_spec, pair_spec,
                   pl.BlockSpec((None, tm, HEAD), lambda b, h, i: (b, i, h)),
                   pl.BlockSpec((None, 2, chunks, HEAD), lambda b, h, i: (b, h, i, 0))],
        out_shape=[big, big, big, big,
                   jax.ShapeDtypeStruct((nb, seq, n_kheads * HEAD), BF16),
                   jax.ShapeDtypeStruct((nb, n_vheads, seq // CHUNK, HEAD), F32)],
        compiler_params=_cparams("parallel", "parallel", "parallel"),
        name="gdn_chunks",
    )(qk3, qk3, v3, gb3, grow)


def _gdn_scan_kernel(u_ref, w_ref, qd_ref, kd_ref, qk_ref, egl_ref, z_ref, gn_ref, o_ref, s_ref,
                     *, heads, chunks):
    @pl.when(pl.program_id(2) == 0)
    def _():
        s_ref[...] = jnp.zeros_like(s_ref)

    zeros = jnp.zeros((CHUNK, HEAD), F32)
    gn = gn_ref[...]
    cols = [slice(g * HEAD, (g + 1) * HEAD) for g in range(heads)]
    for c in range(chunks):
        sl = slice(c * CHUNK, (c + 1) * CHUNK)
        rs = [_dot(jnp.concatenate([w_ref[sl, cols[g]], qd_ref[sl, cols[g]]], axis=0), s_ref[g].astype(BF16))
              for g in range(heads)]
        vnew = [u_ref[sl, cols[g]].astype(F32) - rs[g][:CHUNK, :] for g in range(heads)]
        intra = []
        for pr in range(heads // 2):
            v2 = jnp.concatenate([jnp.concatenate([vnew[2 * pr], zeros], axis=1),
                                  jnp.concatenate([zeros, vnew[2 * pr + 1]], axis=1)], axis=0).astype(BF16)
            intra.append(_dot(qk_ref[sl, pr * HEAD:(pr + 1) * HEAD], v2))
        for g in range(heads):
            s_ref[g] = s_ref[g] * egl_ref[g, c:c + 1, :] + _dot_tn(kd_ref[sl, cols[g]], vnew[g].astype(BF16))
        for g in range(heads):
            o = rs[g][CHUNK:, :] + intra[g // 2][:, (g % 2) * HEAD:(g % 2 + 1) * HEAD]
            ms = jnp.mean(o * o, axis=-1, keepdims=True)
            z = z_ref[sl, cols[g]].astype(F32)
            o_ref[sl, cols[g]] = (o * lax.rsqrt(ms + EPS) * gn * (z * jax.nn.sigmoid(z))).astype(o_ref.dtype)


def _gdn_scan(u, w, qd, kd, qkm, egl, z3, gn_row, n_vheads, heads=8, tm=512):
    nb, seq, vd = u.shape
    chunks = tm // CHUNK
    wide = pl.BlockSpec((None, tm, heads * HEAD), lambda b, g, i: (b, i, g))
    return pl.pallas_call(
        functools.partial(_gdn_scan_kernel, heads=heads, chunks=chunks),
        grid=(nb, n_vheads // heads, seq // tm),
        in_specs=[wide, wide, wide, wide,
                  pl.BlockSpec((None, tm, heads // 2 * HEAD), lambda b, g, i: (b, i, g)),
                  pl.BlockSpec((None, heads, chunks, HEAD), lambda b, g, i: (b, g, i, 0)),
                  wide,
                  pl.BlockSpec((1, HEAD), lambda b, g, i: (0, 0))],
        out_specs=wide,
        out_shape=jax.ShapeDtypeStruct((nb, seq, vd), BF16),
        scratch_shapes=[pltpu.VMEM((heads, HEAD, HEAD), F32)],
        compiler_params=_cparams("parallel", "parallel", "arbitrary"),
        name="gdn_scan",
    )(u, w, qd, kd, qkm, egl, z3, gn_row)


def _pad_cols(w, n):
    return jnp.pad(w, ((0, 0), (0, n - w.shape[1])))


def _row(v, n=HEAD):
    return jnp.pad(v.astype(F32), (0, n - v.shape[0])).reshape(1, n)


def _fox_layer(x2, nb, seq, g_pre, g_post, mod, w_in, f_bias, q_norm, k_norm, w_o):
    d = x2.shape[1]
    nh = d // HEAD
    qd = nh * HEAD
    w_main = jnp.concatenate([w_in[:, :3 * qd], w_in[:, 3 * qd + nh:]], axis=1).astype(BF16)
    w_small = _pad_cols(w_in[:, 3 * qd:3 * qd + nh], HEAD).astype(BF16)
    gn = jnp.concatenate([jnp.tile(q_norm.astype(F32) * (HEAD ** -0.5 * LOG2E), nh),
                          jnp.tile(k_norm.astype(F32), nh)]).reshape(1, 2 * qd)
    qk, f_logit, h = _fox_qk_proj(x2, g_pre, mod, w_main, gn, w_small, seq)
    vo = _plain_proj(h, w_main, 2 * qd, 2 * qd)
    fcum = _fox_gate_cumsum(f_logit, _row(f_bias), nb, seq)
    attn = _fox_attention(qk.reshape(nb, seq, 2 * qd), vo.reshape(nb, seq, 2 * qd), fcum, gn, nh)
    return _out_proj(attn.reshape(nb * seq, qd), w_o.astype(BF16), x2, g_post, mod, seq, tm=512)


def _gdn_layer(x2, nb, seq, g_pre, g_post, mod, w_in, conv_w, a_log, dt_bias, out_norm, w_o):
    d = x2.shape[1]
    nk = d // HEAD
    nv = 2 * nk
    kd, vd = nk * HEAD, nv * HEAD
    n_main = 2 * kd + 2 * vd
    w_main = w_in[:, :n_main].astype(BF16)
    w_small = _pad_cols(w_in[:, n_main:], HEAD).astype(BF16)
    conv_w = conv_w.astype(F32)
    qscale = jnp.concatenate([jnp.full((kd,), HEAD ** -0.5, F32), jnp.ones((kd,), F32)]).reshape(1, 2 * kd)
    qk, ab, h = _gdn_qk_proj(x2, g_pre, mod, w_main, conv_w, qscale, w_small, seq)
    v = _gdn_v_proj(h, w_main, conv_w, 2 * kd, vd, seq)
    z = _plain_proj(h, w_main, 2 * kd + vd, vd)
    gb = _gdn_gates(ab, _row(a_log), _row(dt_bias), nv)
    n_chunks = seq // CHUNK
    grow = gb[:, :nv].reshape(nb, n_chunks, CHUNK, nk, 2)
    grow = jnp.transpose(grow, (0, 3, 1, 4, 2)).reshape(nb, nk, n_chunks, 2 * CHUNK)
    u, w, qdec, kdec, qkm, egl = _gdn_chunks(qk.reshape(nb, seq, 2 * kd), v.reshape(nb, seq, vd),
                                             gb.reshape(nb, seq, HEAD), grow, nk, nv)
    a = _gdn_scan(u, w, qdec, kdec, qkm, egl, z.reshape(nb, seq, vd), _row(out_norm), nv)
    return _out_proj(a.reshape(nb * seq, vd), w_o.astype(BF16), x2, g_post, mod, seq, tm=512)


def _ffn_layer(x2, seq, g_pre, g_post, mod, w_up, conv_w, conv_b, w_down):
    u = _ffn_up(x2, g_pre, mod, w_up.astype(BF16), conv_w.astype(F32), conv_b.reshape(1, -1).astype(F32), seq)
    return _out_proj(u, w_down.astype(BF16), x2, g_post, mod, seq, tm=256)


def kernel(x, c, ada_w, ada_b, norm_g, fox_w_in, fox_f_bias, fox_q_norm, fox_k_norm, fox_w_o,
           gdn_w_in, gdn_conv_w, gdn_a_log, gdn_dt_bias, gdn_out_norm, gdn_w_o,
           ffn_w_up, ffn_conv_w, ffn_conv_b, ffn_w_down):
    nb, seq, d = x.shape
    depth = ada_w.shape[0]
    mods = _ada_mods(c, ada_w, ada_b)
    x2 = x.reshape(nb * seq, d)
    for i in range(depth):
        g = [norm_g[i, r].reshape(1, d).astype(F32) for r in range(4)]
        j = i // 2
        if i % 2 == 0:
            x2 = _fox_layer(x2, nb, seq, g[0], g[1], mods[2 * i], fox_w_in[j], fox_f_bias[j],
                            fox_q_norm[j], fox_k_norm[j], fox_w_o[j])
        else:
            x2 = _gdn_layer(x2, nb, seq, g[0], g[1], mods[2 * i], gdn_w_in[j], gdn_conv_w[j],
                            gdn_a_log[j], gdn_dt_bias[j], gdn_out_norm[j], gdn_w_o[j])
        x2 = _ffn_layer(x2, seq, g[2], g[3], mods[2 * i + 1], ffn_w_up[i], ffn_conv_w[i],
                        ffn_conv_b[i], ffn_w_down[i])
    return x2.reshape(nb, seq, d)
```

```python
import functools

import jax
import jax.numpy as jnp
from jax import lax
from jax.experimental import pallas as pl
from jax.experimental.pallas import tpu as pltpu

F32 = jnp.float32
BF16 = jnp.bfloat16
EPS = 1e-6
HEAD = 128
CHUNK = 64
SUBLANES = 8
MXU_N = 256
VMEM_LIMIT = 56 * 1024 * 1024
INV_SQRT2 = 0.7071067811865476
LOG2E = 1.4426950408889634


def _cparams(*sem):
    return pltpu.CompilerParams(dimension_semantics=sem, vmem_limit_bytes=VMEM_LIMIT)


def _dot(a, b):
    return jnp.dot(a, b, preferred_element_type=F32)


def _dot_nt(a, b):
    return lax.dot_general(a, b, (((1,), (1,)), ((), ())), preferred_element_type=F32)


def _dot_tn(a, b):
    return lax.dot_general(a, b, (((0,), (0,)), ((), ())), preferred_element_type=F32)


def _dot_f32(a, b):
    return jnp.dot(a, b, preferred_element_type=F32, precision=lax.Precision.HIGHEST)


def _softplus(x):
    return jnp.maximum(x, 0.0) + jnp.log1p(jnp.exp(-jnp.abs(x)))


def _ada_kernel(c_ref, w_ref, b_ref, o_ref):
    c = c_ref[...]
    o_ref[0] = _dot_f32(c * jax.nn.sigmoid(c), w_ref[0]) + b_ref[0]


def _ada_mods(c, ada_w, ada_b):
    nb, d = c.shape
    n = ada_w.shape[0] * ada_w.shape[1]
    w = ada_w.reshape(n, d, 3 * d)
    b = ada_b.reshape(n, 1, 3 * d)
    cp = jnp.zeros((SUBLANES, d), F32).at[:nb].set(c)
    tn = 1024
    out = pl.pallas_call(
        _ada_kernel,
        grid=(n, 3 * d // tn),
        in_specs=[pl.BlockSpec((SUBLANES, d), lambda s, j: (0, 0)),
                  pl.BlockSpec((1, d, tn), lambda s, j: (s, 0, j)),
                  pl.BlockSpec((1, 1, tn), lambda s, j: (s, 0, j))],
        out_specs=pl.BlockSpec((1, SUBLANES, tn), lambda s, j: (s, 0, j)),
        out_shape=jax.ShapeDtypeStruct((n, SUBLANES, 3 * d), F32),
        compiler_params=_cparams("parallel", "parallel"),
        name="ada_mods",
    )(cp, w, b)
    return out[:, :nb].reshape(n, nb, 1, 3 * d)


def _norm_mod(h_ref, x_ref, g_ref, sh_ref, sc_ref):
    x = x_ref[...]
    ms = jnp.mean(x * x, axis=-1, keepdims=True)
    y = x * lax.rsqrt(ms + EPS) * g_ref[...]
    h_ref[...] = (y * (1.0 + sc_ref[...]) + sh_ref[...]).astype(BF16)


def _sub_dots(h_ref, w_ref):
    return [_dot(h_ref[...], w_ref[:, c * MXU_N:(c + 1) * MXU_N]) for c in range(w_ref.shape[1] // MXU_N)]


def _head_norm_store(o_ref, c, y, gain_ref, mean):
    for g in range(MXU_N // HEAD):
        sl = slice(c * MXU_N + g * HEAD, c * MXU_N + (g + 1) * HEAD)
        t = y[:, g * HEAD:(g + 1) * HEAD]
        ss = jnp.sum(t * t, axis=-1, keepdims=True)
        if mean:
            ss = ss * (1.0 / HEAD)
        o_ref[:, sl] = (t * lax.rsqrt(ss + EPS) * gain_ref[:, sl]).astype(o_ref.dtype)


def _causal_conv(acc, cw, buf_ref, carry_ref, slot, first):
    tm = acc.shape[0]
    taps = cw.shape[0]
    prev = jnp.where(first, 0.0, carry_ref[slot])
    carry_ref[slot] = acc[tm - SUBLANES:tm, :]
    row = lax.broadcasted_iota(jnp.int32, prev.shape, 0)
    y = cw[taps - 1:taps, :] * acc
    for k in range(taps - 1):
        d = taps - 1 - k
        r = pltpu.roll(acc, d, axis=0)
        top = jnp.where(row < d, pltpu.roll(prev, d, axis=0), r[0:SUBLANES, :])
        y = y + cw[k:k + 1, :] * jnp.concatenate([top, r[SUBLANES:, :]], axis=0)
    return y


def _zero_carry_at_start(carry_ref):
    @pl.when((pl.program_id(0) == 0) & (pl.program_id(1) == 0))
    def _():
        carry_ref[...] = jnp.zeros_like(carry_ref)


def _fox_qk_kernel(x_ref, g_ref, sh_ref, sc_ref, w_ref, gn_ref, ws_ref, o_ref, os_ref, h_ref):
    @pl.when(pl.program_id(1) == 0)
    def _():
        _norm_mod(h_ref, x_ref, g_ref, sh_ref, sc_ref)
        os_ref[...] = _dot(h_ref[...], ws_ref[...])

    for c, acc in enumerate(_sub_dots(h_ref, w_ref)):
        _head_norm_store(o_ref, c, acc, gn_ref, mean=True)


def _plain_proj_kernel(h_ref, w_ref, o_ref):
    o_ref[...] = _dot(h_ref[...], w_ref[...]).astype(o_ref.dtype)


def _gdn_qk_kernel(x_ref, g_ref, sh_ref, sc_ref, w_ref, cw_ref, qs_ref, ws_ref, o_ref, os_ref, h_ref,
                   bufs_ref, carry_ref, *, blocks_per_seq):
    i = pl.program_id(0)
    j = pl.program_id(1)
    _zero_carry_at_start(carry_ref)

    @pl.when(j == 0)
    def _():
        _norm_mod(h_ref, x_ref, g_ref, sh_ref, sc_ref)
        os_ref[...] = _dot(h_ref[...], ws_ref[...])

    first = (i % blocks_per_seq) == 0
    accs = _sub_dots(h_ref, w_ref)
    for c, acc in enumerate(accs):
        y = _causal_conv(acc, cw_ref[:, c * MXU_N:(c + 1) * MXU_N], bufs_ref.at[c], carry_ref,
                         j * len(accs) + c, first)
        _head_norm_store(o_ref, c, y * jax.nn.sigmoid(y), qs_ref, mean=False)


def _gdn_v_kernel(h_ref, w_ref, cw_ref, o_ref, bufs_ref, carry_ref, *, blocks_per_seq):
    i = pl.program_id(0)
    j = pl.program_id(1)
    _zero_carry_at_start(carry_ref)
    first = (i % blocks_per_seq) == 0
    n_sub = w_ref.shape[1] // MXU_N

    def epilogue(c, acc):
        cs = slice(c * MXU_N, (c + 1) * MXU_N)
        y = _causal_conv(acc, cw_ref[:, cs], bufs_ref.at[c], carry_ref, j * n_sub + c, first)
        o_ref[:, cs] = (y * jax.nn.sigmoid(y)).astype(o_ref.dtype)

    prev = None
    for c in range(n_sub):
        acc = _dot(h_ref[...], w_ref[:, c * MXU_N:(c + 1) * MXU_N])
        if prev is not None:
            epilogue(c - 1, prev)
        prev = acc
    epilogue(n_sub - 1, prev)


def _ffn_up_kernel(x_ref, g_ref, sh_ref, sc_ref, wg_ref, wv_ref, cw_ref, cb_ref, o_ref,
                   h_ref, bufs_ref, carry_ref, *, blocks_per_seq):
    i = pl.program_id(0)
    j = pl.program_id(1)
    _zero_carry_at_start(carry_ref)

    @pl.when(j == 0)
    def _():
        _norm_mod(h_ref, x_ref, g_ref, sh_ref, sc_ref)

    first = (i % blocks_per_seq) == 0
    n_sub = wg_ref.shape[1] // MXU_N
    pairs = []
    for c in range(n_sub):
        cs = slice(c * MXU_N, (c + 1) * MXU_N)
        pairs.append((_dot(h_ref[...], wg_ref[:, cs]), _dot(h_ref[...], wv_ref[:, cs])))
    for c, (gate, val) in enumerate(pairs):
        cs = slice(c * MXU_N, (c + 1) * MXU_N)
        gate = _causal_conv(gate, cw_ref[:, cs], bufs_ref.at[c], carry_ref, j * n_sub + c, first) + cb_ref[:, cs]
        act = 0.5 * gate * (1.0 + lax.erf(gate * INV_SQRT2))
        o_ref[:, cs] = (act * val).astype(o_ref.dtype)


def _x_mod_specs(tm, d, bps):
    return [pl.BlockSpec((tm, d), lambda i, j: (i, 0)),
            pl.BlockSpec((1, d), lambda i, j: (0, 0)),
            pl.BlockSpec((None, 1, d), lambda i, j: (i // bps, 0, 0)),
            pl.BlockSpec((None, 1, d), lambda i, j: (i // bps, 0, 1))]


def _conv_scratch(tm, tn, nj):
    n_sub = tn // MXU_N
    return [pltpu.VMEM((n_sub, tm + SUBLANES, MXU_N), F32), pltpu.VMEM((nj * n_sub, SUBLANES, MXU_N), F32)]


def _fox_qk_proj(x2, g_pre, mod, w_main, gn, w_small, seq, tm=1024, tn=1024):
    t, d = x2.shape
    n = gn.shape[1]
    return pl.pallas_call(
        _fox_qk_kernel,
        grid=(t // tm, n // tn),
        in_specs=_x_mod_specs(tm, d, seq // tm) + [
            pl.BlockSpec((d, tn), lambda i, j: (0, j)),
            pl.BlockSpec((1, tn), lambda i, j: (0, j)),
            pl.BlockSpec((d, HEAD), lambda i, j: (0, 0))],
        out_specs=[pl.BlockSpec((tm, tn), lambda i, j: (i, j)),
                   pl.BlockSpec((tm, HEAD), lambda i, j: (i, 0)),
                   pl.BlockSpec((tm, d), lambda i, j: (i, 0))],
        out_shape=[jax.ShapeDtypeStruct((t, n), BF16), jax.ShapeDtypeStruct((t, HEAD), F32),
                   jax.ShapeDtypeStruct((t, d), BF16)],
        compiler_params=_cparams("parallel", "arbitrary"),
        name="fox_qk_proj",
    )(x2, g_pre, mod, mod, w_main, gn, w_small)


def _plain_proj(h, w_main, col0, n, tm=1024, tn=1024):
    t, d = h.shape
    off = col0 // tn
    return pl.pallas_call(
        _plain_proj_kernel,
        grid=(t // tm, n // tn),
        in_specs=[pl.BlockSpec((tm, d), lambda i, j: (i, 0)),
                  pl.BlockSpec((d, tn), lambda i, j: (0, off + j))],
        out_specs=pl.BlockSpec((tm, tn), lambda i, j: (i, j)),
        out_shape=jax.ShapeDtypeStruct((t, n), BF16),
        compiler_params=_cparams("parallel", "arbitrary"),
        name="plain_proj",
    )(h, w_main)


def _gdn_qk_proj(x2, g_pre, mod, w_main, conv_w, qscale, w_small, seq, tm=1024, tn=1024):
    t, d = x2.shape
    n = qscale.shape[1]
    taps = conv_w.shape[0]
    return pl.pallas_call(
        functools.partial(_gdn_qk_kernel, blocks_per_seq=seq // tm),
        grid=(t // tm, n // tn),
        in_specs=_x_mod_specs(tm, d, seq // tm) + [
            pl.BlockSpec((d, tn), lambda i, j: (0, j)),
            pl.BlockSpec((taps, tn), lambda i, j: (0, j)),
            pl.BlockSpec((1, tn), lambda i, j: (0, j)),
            pl.BlockSpec((d, HEAD), lambda i, j: (0, 0))],
        out_specs=[pl.BlockSpec((tm, tn), lambda i, j: (i, j)),
                   pl.BlockSpec((tm, HEAD), lambda i, j: (i, 0)),
                   pl.BlockSpec((tm, d), lambda i, j: (i, 0))],
        out_shape=[jax.ShapeDtypeStruct((t, n), BF16), jax.ShapeDtypeStruct((t, HEAD), F32),
                   jax.ShapeDtypeStruct((t, d), BF16)],
        scratch_shapes=_conv_scratch(tm, tn, n // tn),
        compiler_params=_cparams("arbitrary", "arbitrary"),
        name="gdn_qk_proj",
    )(x2, g_pre, mod, mod, w_main, conv_w, qscale, w_small)


def _gdn_v_proj(h, w_main, conv_w, col0, n, seq, tm=1024, tn=1024):
    t, d = h.shape
    taps = conv_w.shape[0]
    off = col0 // tn
    return pl.pallas_call(
        functools.partial(_gdn_v_kernel, blocks_per_seq=seq // tm),
        grid=(t // tm, n // tn),
        in_specs=[pl.BlockSpec((tm, d), lambda i, j: (i, 0)),
                  pl.BlockSpec((d, tn), lambda i, j: (0, off + j)),
                  pl.BlockSpec((taps, tn), lambda i, j: (0, off + j))],
        out_specs=pl.BlockSpec((tm, tn), lambda i, j: (i, j)),
        out_shape=jax.ShapeDtypeStruct((t, n), BF16),
        scratch_shapes=_conv_scratch(tm, tn, n // tn),
        compiler_params=_cparams("arbitrary", "arbitrary"),
        name="gdn_v_proj",
    )(h, w_main, conv_w)


def _ffn_up(x2, g_pre, mod, w_up, conv_w, conv_b, seq, tm=1024, tn=512):
    t, d = x2.shape
    dff = w_up.shape[1] // 2
    nj = dff // tn
    taps = conv_w.shape[0]
    return pl.pallas_call(
        functools.partial(_ffn_up_kernel, blocks_per_seq=seq // tm),
        grid=(t // tm, nj),
        in_specs=_x_mod_specs(tm, d, seq // tm) + [
            pl.BlockSpec((d, tn), lambda i, j: (0, j)),
            pl.BlockSpec((d, tn), lambda i, j: (0, j + nj)),
            pl.BlockSpec((taps, tn), lambda i, j: (0, j)),
            pl.BlockSpec((1, tn), lambda i, j: (0, j))],
        out_specs=pl.BlockSpec((tm, tn), lambda i, j: (i, j)),
        out_shape=jax.ShapeDtypeStruct((t, dff), BF16),
        scratch_shapes=[pltpu.VMEM((tm, d), BF16)] + _conv_scratch(tm, tn, nj),
        compiler_params=_cparams("arbitrary", "arbitrary"),
        name="ffn_up",
    )(x2, g_pre, mod, mod, w_up, w_up, conv_w, conv_b)


def _out_proj_kernel(a_ref, w_ref, x_ref, g_ref, gate_ref, o_ref):
    y = _dot(a_ref[...], w_ref[...])
    ms = jnp.mean(y * y, axis=-1, keepdims=True)
    o_ref[...] = x_ref[...] + gate_ref[...] * (y * lax.rsqrt(ms + EPS) * g_ref[...])


def _out_proj(a, w, x2, g_post, mod, seq, tm):
    t, kd = a.shape
    d = w.shape[1]
    bps = seq // tm
    return pl.pallas_call(
        _out_proj_kernel,
        grid=(t // tm,),
        in_specs=[pl.BlockSpec((tm, kd), lambda i: (i, 0)),
                  pl.BlockSpec((kd, d), lambda i: (0, 0), pipeline_mode=pl.Buffered(1)),
                  pl.BlockSpec((tm, d), lambda i: (i, 0)),
                  pl.BlockSpec((1, d), lambda i: (0, 0)),
                  pl.BlockSpec((None, 1, d), lambda i: (i // bps, 0, 2))],
        out_specs=pl.BlockSpec((tm, d), lambda i: (i, 0)),
        out_shape=jax.ShapeDtypeStruct((t, d), F32),
        compiler_params=_cparams("parallel"),
        name="out_proj",
    )(a, w, x2, g_post, mod)


def _fox_gate_kernel(f_ref, fb_ref, o_ref, carry_ref):
    tm = f_ref.shape[0]

    @pl.when(pl.program_id(1) == 0)
    def _():
        carry_ref[...] = jnp.zeros_like(carry_ref)

    x = f_ref[...] + fb_ref[...]
    logf = (jnp.minimum(x, 0.0) - jnp.log1p(jnp.exp(-jnp.abs(x)))) * LOG2E
    r = lax.broadcasted_iota(jnp.int32, (tm, tm), 0)
    c = lax.broadcasted_iota(jnp.int32, (tm, tm), 1)
    tri = (r >= c).astype(F32)
    cum = _dot_f32(tri, logf) + carry_ref[0:1, :]
    o_ref[...] = cum
    carry_ref[0:1, :] = cum[tm - 1:tm, :]


def _fox_gate_cumsum(f_logit, f_bias_row, nb, seq, tm=256):
    return pl.pallas_call(
        _fox_gate_kernel,
        grid=(nb, seq // tm),
        in_specs=[pl.BlockSpec((None, tm, HEAD), lambda b, i: (b, i, 0)),
                  pl.BlockSpec((1, HEAD), lambda b, i: (0, 0))],
        out_specs=pl.BlockSpec((None, tm, HEAD), lambda b, i: (b, i, 0)),
        out_shape=jax.ShapeDtypeStruct((nb, seq, HEAD), F32),
        scratch_shapes=[pltpu.VMEM((SUBLANES, HEAD), F32)],
        compiler_params=_cparams("parallel", "arbitrary"),
        name="fox_gate_cumsum",
    )(f_logit.reshape(nb, seq, HEAD), f_bias_row)


def _attn_kernel(q_ref, k_ref, v_ref, og_ref, fk_ref, o_ref,
                 s0_ref, s1_ref, m_ref, l_ref, acc_ref, *, tq):
    h = pl.program_id(1)
    qi = pl.program_id(2)
    m_ref[...] = jnp.full(m_ref.shape, -jnp.inf, F32)
    l_ref[...] = jnp.zeros(l_ref.shape, F32)
    acc_ref[...] = jnp.zeros(acc_ref.shape, F32)

    def scores(s_ref, ki):
        ks = pl.multiple_of(ki * tq, tq)
        s_ref[...] = _dot_nt(q_ref[...], k_ref[pl.ds(ks, tq), :]) - fk_ref[pl.ds(h, 1), pl.ds(ks, tq)]

    def update(s_ref, ki, masked):
        ks = pl.multiple_of(ki * tq, tq)
        s = s_ref[...]
        if masked:
            row = lax.broadcasted_iota(jnp.int32, (tq, tq), 0)
            col = lax.broadcasted_iota(jnp.int32, (tq, tq), 1)
            s = jnp.where(col <= row, s, -jnp.inf)
        m_old = m_ref[...]
        m_new = jnp.maximum(m_old, jnp.max(s, axis=-1, keepdims=True))
        p = jnp.exp2(s - m_new)
        alpha = jnp.exp2(m_old - m_new)
        l_ref[...] = alpha * l_ref[...] + jnp.sum(p, axis=-1, keepdims=True)
        m_ref[...] = m_new
        acc_ref[...] = alpha * acc_ref[...] + _dot(p.astype(BF16), v_ref[pl.ds(ks, tq), :])

    scores(s0_ref, 0)

    def pair(pi, carry):
        scores(s1_ref, 2 * pi + 1)
        update(s0_ref, 2 * pi, False)
        scores(s0_ref, 2 * pi + 2)
        update(s1_ref, 2 * pi + 1, False)
        return carry

    lax.fori_loop(0, qi // 2, pair, 0)

    @pl.when(qi % 2 == 1)
    def _():
        scores(s1_ref, qi)
        update(s0_ref, qi - 1, False)
        update(s1_ref, qi, True)

    @pl.when(qi % 2 == 0)
    def _():
        update(s0_ref, qi, True)

    o_ref[...] = (acc_ref[...] / l_ref[...] * jax.nn.sigmoid(og_ref[...].astype(F32))).astype(o_ref.dtype)


def _split3(x):
    hi = x.astype(BF16).astype(F32)
    mid = (x - hi).astype(BF16).astype(F32)
    return hi, mid, x - hi - mid


def _attn_fixed_shift_kernel(q_ref, k_ref, v_ref, og_ref, fc_ref, c_ref, o_ref,
                             ka_ref, va_ref, qa_ref, p0_ref, p1_ref, acc_ref, *, tq):
    h = pl.program_id(1)
    step = pl.program_id(2)
    seq = k_ref.shape[0]
    nq = seq // tq
    q_blocks = (step, nq - 1 - step)

    def block_rows(i):
        return pl.ds(pl.multiple_of(i * tq, tq), tq)

    def head_col(rows):
        lane = lax.broadcasted_iota(jnp.int32, (tq, HEAD), 1)
        return jnp.sum(jnp.where(lane == h, fc_ref[rows, :], 0.0), axis=-1, keepdims=True)

    def extra_cols(first, second):
        lane = lax.broadcasted_iota(jnp.int32, (tq, HEAD), 1)
        cols = jnp.zeros((tq, HEAD), F32)
        for n, term in enumerate(first + second):
            cols = jnp.where(lane == n, term, cols)
        return cols.astype(BF16)

    ones3 = (1.0, 1.0, 1.0)

    @pl.when(step == 0)
    def _():
        def build(i, carry):
            rows = block_rows(i)
            hi, mid, lo = _split3(-head_col(rows))
            ka_ref[rows, 0:HEAD] = k_ref[rows, :]
            ka_ref[rows, HEAD:2 * HEAD] = extra_cols((hi, mid, lo), ones3)
            va_ref[rows, 0:HEAD] = v_ref[rows, :]
            va_ref[rows, HEAD:2 * HEAD] = jnp.ones((tq, HEAD), BF16)
            return carry
        lax.fori_loop(0, nq, build, 0)

    for n, qb in enumerate(q_blocks):
        rows = block_rows(qb)
        qa_ref[n] = jnp.concatenate(
            [q_ref[rows, :], extra_cols(ones3, _split3(head_col(rows) - c_ref[0:1, 0:1]))], axis=1)
    acc_ref[...] = jnp.zeros(acc_ref.shape, F32)

    def probs(p_ref, which, kb, masked):
        s = _dot_nt(qa_ref[which], ka_ref[block_rows(kb), :])
        if masked:
            row = lax.broadcasted_iota(jnp.int32, (tq, tq), 0)
            col = lax.broadcasted_iota(jnp.int32, (tq, tq), 1)
            s = jnp.where(col <= row, s, -jnp.inf)
        p_ref[...] = jnp.exp2(s).astype(BF16)

    def accumulate(p_ref, which, kb):
        acc_ref[which] += _dot(p_ref[...], va_ref[block_rows(kb), :])

    items = [(0, q_blocks[0], True), (1, q_blocks[1], True)]
    for n in range(nq - 1):
        which = (n >= q_blocks[0]).astype(jnp.int32)
        items.append((which, n - which * q_blocks[0], False))
    p_refs = (p0_ref, p1_ref)
    for n, (which, kb, masked) in enumerate(items):
        probs(p_refs[n % 2], which, kb, masked)
        if n > 0:
            accumulate(p_refs[(n - 1) % 2], *items[n - 1][:2])
    accumulate(p_refs[(len(items) - 1) % 2], *items[-1][:2])

    for n, qb in enumerate(q_blocks):
        rows = block_rows(qb)
        acc = acc_ref[n]
        o_ref[rows, :] = (acc[:, 0:HEAD] / acc[:, HEAD:HEAD + 1]
                          * jax.nn.sigmoid(og_ref[rows, :].astype(F32))).astype(o_ref.dtype)


def _fox_attention_fixed_shift(qk3, vo3, fcum, c_row, n_heads, tq=512):
    nb, seq, _ = qk3.shape
    nq = seq // tq
    assert nq % 2 == 0
    full = lambda off: pl.BlockSpec((None, seq, HEAD), lambda b, h, i: (b, 0, off + h))
    return pl.pallas_call(
        functools.partial(_attn_fixed_shift_kernel, tq=tq),
        grid=(nb, n_heads, nq // 2),
        in_specs=[full(0), full(n_heads), full(0), full(n_heads),
                  pl.BlockSpec((None, seq, HEAD), lambda b, h, i: (b, 0, 0)),
                  pl.BlockSpec((1, HEAD), lambda b, h, i: (0, 0))],
        out_specs=full(0),
        out_shape=jax.ShapeDtypeStruct((nb, seq, n_heads * HEAD), BF16),
        scratch_shapes=[pltpu.VMEM((seq, 2 * HEAD), BF16), pltpu.VMEM((seq, 2 * HEAD), BF16),
                        pltpu.VMEM((2, tq, 2 * HEAD), BF16),
                        pltpu.VMEM((tq, tq), BF16), pltpu.VMEM((tq, tq), BF16),
                        pltpu.VMEM((2, tq, 2 * HEAD), F32)],
        compiler_params=_cparams("parallel", "parallel", "arbitrary"),
        name="fox_attention_fixed_shift",
    )(qk3, qk3, vo3, vo3, fcum, c_row)


MAX_SHIFT_GAP = 100.0


def _fox_attention(qk3, vo3, fcum, gn, n_heads):
    qd = n_heads * HEAD
    c = (HEAD * 1.01) * jnp.max(jnp.abs(gn[:, :qd])) * jnp.max(jnp.abs(gn[:, qd:]))
    c_row = jnp.full((1, HEAD), c, F32)

    def running_max(qk3, vo3, fcum, c_row):
        fk = jnp.transpose(fcum[:, :, :n_heads], (0, 2, 1))
        return _fox_attention_running_max(qk3, vo3, fk, n_heads)

    def fixed_shift(qk3, vo3, fcum, c_row):
        return _fox_attention_fixed_shift(qk3, vo3, fcum, c_row, n_heads)

    return lax.cond(2.0 * c <= MAX_SHIFT_GAP, fixed_shift, running_max, qk3, vo3, fcum, c_row)


def _fox_attention_running_max(qk3, vo3, fk, n_heads, tq=512):
    nb, seq, _ = qk3.shape
    return pl.pallas_call(
        functools.partial(_attn_kernel, tq=tq),
        grid=(nb, n_heads, seq // tq),
        in_specs=[pl.BlockSpec((None, tq, HEAD), lambda b, h, i: (b, i, h)),
                  pl.BlockSpec((None, seq, HEAD), lambda b, h, i: (b, 0, n_heads + h)),
                  pl.BlockSpec((None, seq, HEAD), lambda b, h, i: (b, 0, h)),
                  pl.BlockSpec((None, tq, HEAD), lambda b, h, i: (b, i, n_heads + h)),
                  pl.BlockSpec((None, n_heads, seq), lambda b, h, i: (b, 0, 0))],
        out_specs=pl.BlockSpec((None, tq, HEAD), lambda b, h, i: (b, i, h)),
        out_shape=jax.ShapeDtypeStruct((nb, seq, n_heads * HEAD), BF16),
        scratch_shapes=[pltpu.VMEM((tq, tq), F32), pltpu.VMEM((tq, tq), F32),
                        pltpu.VMEM((tq, 1), F32), pltpu.VMEM((tq, 1), F32), pltpu.VMEM((tq, HEAD), F32)],
        compiler_params=_cparams("parallel", "parallel", "arbitrary"),
        name="fox_attention",
    )(qk3, qk3, vo3, vo3, fk)


def _gdn_gate_kernel(ab_ref, alog_ref, dt_ref, o_ref, *, n_heads):
    tm = ab_ref.shape[0]
    x = ab_ref[...]
    lane = lax.broadcasted_iota(jnp.int32, x.shape, 1)
    g = -jnp.exp(alog_ref[...]) * _softplus(x + dt_ref[...])
    r = lax.broadcasted_iota(jnp.int32, (tm, tm), 0)
    c = lax.broadcasted_iota(jnp.int32, (tm, tm), 1)
    tri = ((r >= c) & ((r // CHUNK) == (c // CHUNK))).astype(F32)
    gc = _dot_f32(tri, jnp.where(lane < n_heads, g, 0.0))
    o_ref[...] = jnp.where(lane < n_heads, gc, jax.nn.sigmoid(x))


def _gdn_gates(ab, alog_row, dt_row, n_heads, tm=256):
    t = ab.shape[0]
    return pl.pallas_call(
        functools.partial(_gdn_gate_kernel, n_heads=n_heads),
        grid=(t // tm,),
        in_specs=[pl.BlockSpec((tm, HEAD), lambda i: (i, 0)),
                  pl.BlockSpec((1, HEAD), lambda i: (0, 0)),
                  pl.BlockSpec((1, HEAD), lambda i: (0, 0))],
        out_specs=pl.BlockSpec((tm, HEAD), lambda i: (i, 0)),
        out_shape=jax.ShapeDtypeStruct((t, HEAD), F32),
        compiler_params=_cparams("parallel"),
        name="gdn_gates",
    )(ab, alog_row, dt_row)


def _block_diag2(p, half):
    return jnp.concatenate([jnp.where(half, 0.0, p), jnp.where(half, p, 0.0)], axis=0)


def _gdn_chunk_kernel(q_ref, k_ref, v_ref, gb_ref, grow_ref, u_ref, w_ref, qd_ref, kd_ref, qk_ref, egl_ref,
                      *, n_vheads, chunks):
    hk = pl.program_id(1)
    lane = lax.broadcasted_iota(jnp.int32, (CHUNK, HEAD), 1)
    row = lax.broadcasted_iota(jnp.int32, (CHUNK, HEAD), 0)
    col = lane & (CHUNK - 1)
    half = lane >= CHUNK
    lower = row >= col
    strict = row > col
    eye = (row == col).astype(F32)
    level_masks = []
    for lvl in range(6):
        level_masks.append((((row >> lvl) & 1) == 1) & (((col >> lvl) & 1) == 0)
                           & ((row >> (lvl + 1)) == (col >> (lvl + 1))))
    zeros = jnp.zeros((CHUNK, HEAD), F32)

    def pick(gb, idx):
        return jnp.sum(jnp.where(lane == idx, gb, 0.0), axis=-1, keepdims=True)

    sls = [slice(c * CHUNK, (c + 1) * CHUNK) for c in range(chunks)]
    gcs, bes, a_s, xs = [], [], [], []
    for c, sl in enumerate(sls):
        q = q_ref[sl, :]
        k = k_ref[sl, :]
        gb = gb_ref[sl, :]
        gc0, gc1 = pick(gb, 2 * hk), pick(gb, 2 * hk + 1)
        be0, be1 = pick(gb, n_vheads + 2 * hk), pick(gb, n_vheads + 2 * hk + 1)
        gcol = jnp.where(half, gc1, gc0)
        bcol = jnp.where(half, be1, be0)
        decay = jnp.exp(jnp.where(lower, gcol - grow_ref[c:c + 1, :], -jnp.inf))
        qkk = _dot_nt(jnp.concatenate([q, k], axis=0), jnp.concatenate([k, k], axis=0))
        a = jnp.where(strict, qkk[CHUNK:, :] * decay * bcol, 0.0)
        qk_ref[sl, :] = (qkk[:CHUNK, :] * decay).astype(qk_ref.dtype)
        gcs.append((gc0, gc1))
        bes.append((be0, be1))
        a_s.append(a)
        xs.append(eye - jnp.where(level_masks[0], a, 0.0))

    for lvl in range(1, 6):
        ys = [_dot(xs[c].astype(BF16), _block_diag2(jnp.where(level_masks[lvl], a_s[c], 0.0), half).astype(BF16))
              for c in range(chunks)]
        xs = [xs[c] - _dot(ys[c].astype(BF16), _block_diag2(xs[c], half).astype(BF16)) for c in range(chunks)]

    for c, sl in enumerate(sls):
        (gc0, gc1), (be0, be1) = gcs[c], bes[c]
        kf = k_ref[sl, :].astype(F32)
        qf = q_ref[sl, :].astype(F32)
        v = v_ref[sl, :].astype(F32)
        eg0, eg1 = jnp.exp(gc0), jnp.exp(gc1)
        rhs = jnp.concatenate([
            jnp.concatenate([v[:, :HEAD] * be0, kf * (be0 * eg0), zeros, zeros], axis=1),
            jnp.concatenate([zeros, zeros, v[:, HEAD:] * be1, kf * (be1 * eg1)], axis=1)], axis=0)
        sol = _dot(xs[c].astype(BF16), rhs.astype(BF16))
        u_ref[sl, :] = jnp.concatenate([sol[:, 0:HEAD], sol[:, 2 * HEAD:3 * HEAD]], axis=1).astype(u_ref.dtype)
        w_ref[sl, :] = jnp.concatenate([sol[:, HEAD:2 * HEAD], sol[:, 3 * HEAD:]], axis=1).astype(w_ref.dtype)
        qd_ref[sl, :] = jnp.concatenate([qf * eg0, qf * eg1], axis=1).astype(qd_ref.dtype)
        gl0, gl1 = gc0[CHUNK - 1:CHUNK, :], gc1[CHUNK - 1:CHUNK, :]
        kd_ref[sl, :] = jnp.concatenate([kf * jnp.exp(gl0 - gc0), kf * jnp.exp(gl1 - gc1)],
                                        axis=1).astype(kd_ref.dtype)
        egl_ref[0, c:c + 1, :] = jnp.broadcast_to(jnp.exp(gl0), (1, HEAD))
        egl_ref[1, c:c + 1, :] = jnp.broadcast_to(jnp.exp(gl1), (1, HEAD))


def _gdn_chunks(qk3, v3, gb3, grow, n_kheads, n_vheads, tm=1024):
    nb, seq, _ = qk3.shape
    chunks = tm // CHUNK
    vd = n_vheads * HEAD
    big = jax.ShapeDtypeStruct((nb, seq, vd), BF16)
    pair_spec = pl.BlockSpec((None, tm, 2 * HEAD), lambda b, h, i: (b, i, h))
    return pl.pallas_call(
        functools.partial(_gdn_chunk_kernel, n_vheads=n_vheads, chunks=chunks),
        grid=(nb, n_kheads, seq // tm),
        in_specs=[pl.BlockSpec((None, tm, HEAD), lambda b, h, i: (b, i, h)),
                  pl.BlockSpec((None, tm, HEAD), lambda b, h, i: (b, i, n_kheads + h)),
                  pair_spec,
                  pl.BlockSpec((None, tm, HEAD), lambda b, h, i: (b, i, 0)),
                  pl.BlockSpec((None, None, chunks, HEAD), lambda b, h, i: (b, h, i, 0))],
        out_specs=[pair_spec, pair_spec, pair_spec, pair_spec,
                   pl.BlockSpec((None, tm, HEAD), lambda b, h, i: (b, i, h)),
                   pl.BlockSpec((None, 2, chunks, HEAD), lambda b, h, i: (b, h, i, 0))],
        out_shape=[big, big, big, big,
                   jax.ShapeDtypeStruct((nb, seq, n_kheads * HEAD), BF16),
                   jax.ShapeDtypeStruct((nb, n_vheads, seq // CHUNK, HEAD), F32)],
        compiler_params=_cparams("parallel", "parallel", "parallel"),
        name="gdn_chunks",
    )(qk3, qk3, v3, gb3, grow)


def _gdn_scan_kernel(u_ref, w_ref, qd_ref, kd_ref, qk_ref, egl_ref, z_ref, gn_ref, o_ref, s_ref,
                     *, heads, chunks):
    @pl.when(pl.program_id(2) == 0)
    def _():
        s_ref[...] = jnp.zeros_like(s_ref)

    zeros = jnp.zeros((CHUNK, HEAD), F32)
    gn = gn_ref[...]
    cols = [slice(g * HEAD, (g + 1) * HEAD) for g in range(heads)]
    for c in range(chunks):
        sl = slice(c * CHUNK, (c + 1) * CHUNK)
        rs = [_dot(jnp.concatenate([w_ref[sl, cols[g]], qd_ref[sl, cols[g]]], axis=0), s_ref[g].astype(BF16))
              for g in range(heads)]
        vnew = [u_ref[sl, cols[g]].astype(F32) - rs[g][:CHUNK, :] for g in range(heads)]
        intra = []
        for pr in range(heads // 2):
            v2 = jnp.concatenate([jnp.concatenate([vnew[2 * pr], zeros], axis=1),
                                  jnp.concatenate([zeros, vnew[2 * pr + 1]], axis=1)], axis=0).astype(BF16)
            intra.append(_dot(qk_ref[sl, pr * HEAD:(pr + 1) * HEAD], v2))
        for g in range(heads):
            s_ref[g] = s_ref[g] * egl_ref[g, c:c + 1, :] + _dot_tn(kd_ref[sl, cols[g]], vnew[g].astype(BF16))
        for g in range(heads):
            o = rs[g][CHUNK:, :] + intra[g // 2][:, (g % 2) * HEAD:(g % 2 + 1) * HEAD]
            ms = jnp.mean(o * o, axis=-1, keepdims=True)
            z = z_ref[sl, cols[g]].astype(F32)
            o_ref[sl, cols[g]] = (o * lax.rsqrt(ms + EPS) * gn * (z * jax.nn.sigmoid(z))).astype(o_ref.dtype)


def _gdn_scan(u, w, qd, kd, qkm, egl, z3, gn_row, n_vheads, heads=16, tm=512):
    nb, seq, vd = u.shape
    chunks = tm // CHUNK
    wide = pl.BlockSpec((None, tm, heads * HEAD), lambda b, g, i: (b, i, g))
    return pl.pallas_call(
        functools.partial(_gdn_scan_kernel, heads=heads, chunks=chunks),
        grid=(nb, n_vheads // heads, seq // tm),
        in_specs=[wide, wide, wide, wide,
                  pl.BlockSpec((None, tm, heads // 2 * HEAD), lambda b, g, i: (b, i, g)),
                  pl.BlockSpec((None, heads, chunks, HEAD), lambda b, g, i: (b, g, i, 0)),
                  wide,
                  pl.BlockSpec((1, HEAD), lambda b, g, i: (0, 0))],
        out_specs=wide,
        out_shape=jax.ShapeDtypeStruct((nb, seq, vd), BF16),
        scratch_shapes=[pltpu.VMEM((heads, HEAD, HEAD), F32)],
        compiler_params=_cparams("parallel", "parallel", "arbitrary"),
        name="gdn_scan",
    )(u, w, qd, kd, qkm, egl, z3, gn_row)


def _pad_cols(w, n):
    return jnp.pad(w, ((0, 0), (0, n - w.shape[1])))


def _row(v, n=HEAD):
    return jnp.pad(v.astype(F32), (0, n - v.shape[0])).reshape(1, n)


def _fox_layer(x2, nb, seq, g_pre, g_post, mod, w_in, f_bias, q_norm, k_norm, w_o):
    d = x2.shape[1]
    nh = d // HEAD
    qd = nh * HEAD
    w_main = jnp.concatenate([w_in[:, :3 * qd], w_in[:, 3 * qd + nh:]], axis=1).astype(BF16)
    w_small = _pad_cols(w_in[:, 3 * qd:3 * qd + nh], HEAD).astype(BF16)
    gn = jnp.concatenate([jnp.tile(q_norm.astype(F32) * (HEAD ** -0.5 * LOG2E), nh),
                          jnp.tile(k_norm.astype(F32), nh)]).reshape(1, 2 * qd)
    qk, f_logit, h = _fox_qk_proj(x2, g_pre, mod, w_main, gn, w_small, seq)
    vo = _plain_proj(h, w_main, 2 * qd, 2 * qd)
    fcum = _fox_gate_cumsum(f_logit, _row(f_bias), nb, seq)
    attn = _fox_attention(qk.reshape(nb, seq, 2 * qd), vo.reshape(nb, seq, 2 * qd), fcum, gn, nh)
    return _out_proj(attn.reshape(nb * seq, qd), w_o.astype(BF16), x2, g_post, mod, seq, tm=512)


def _gdn_layer(x2, nb, seq, g_pre, g_post, mod, w_in, conv_w, a_log, dt_bias, out_norm, w_o):
    d = x2.shape[1]
    nk = d // HEAD
    nv = 2 * nk
    kd, vd = nk * HEAD, nv * HEAD
    n_main = 2 * kd + 2 * vd
    w_main = w_in[:, :n_main].astype(BF16)
    w_small = _pad_cols(w_in[:, n_main:], HEAD).astype(BF16)
    conv_w = conv_w.astype(F32)
    qscale = jnp.concatenate([jnp.full((kd,), HEAD ** -0.5, F32), jnp.ones((kd,), F32)]).reshape(1, 2 * kd)
    qk, ab, h = _gdn_qk_proj(x2, g_pre, mod, w_main, conv_w, qscale, w_small, seq)
    v = _gdn_v_proj(h, w_main, conv_w, 2 * kd, vd, seq)
    z = _plain_proj(h, w_main, 2 * kd + vd, vd)
    gb = _gdn_gates(ab, _row(a_log), _row(dt_bias), nv)
    n_chunks = seq // CHUNK
    grow = gb[:, :nv].reshape(nb, n_chunks, CHUNK, nk, 2)
    grow = jnp.transpose(grow, (0, 3, 1, 4, 2)).reshape(nb, nk, n_chunks, 2 * CHUNK)
    u, w, qdec, kdec, qkm, egl = _gdn_chunks(qk.reshape(nb, seq, 2 * kd), v.reshape(nb, seq, vd),
                                             gb.reshape(nb, seq, HEAD), grow, nk, nv)
    a = _gdn_scan(u, w, qdec, kdec, qkm, egl, z.reshape(nb, seq, vd), _row(out_norm), nv)
    return _out_proj(a.reshape(nb * seq, vd), w_o.astype(BF16), x2, g_post, mod, seq, tm=512)


def _ffn_layer(x2, seq, g_pre, g_post, mod, w_up, conv_w, conv_b, w_down):
    u = _ffn_up(x2, g_pre, mod, w_up.astype(BF16), conv_w.astype(F32), conv_b.reshape(1, -1).astype(F32), seq)
    return _out_proj(u, w_down.astype(BF16), x2, g_post, mod, seq, tm=256)


def kernel(x, c, ada_w, ada_b, norm_g, fox_w_in, fox_f_bias, fox_q_norm, fox_k_norm, fox_w_o,
           gdn_w_in, gdn_conv_w, gdn_a_log, gdn_dt_bias, gdn_out_norm, gdn_w_o,
           ffn_w_up, ffn_conv_w, ffn_conv_b, ffn_w_down):
    nb, seq, d = x.shape
    depth = ada_w.shape[0]
    mods = _ada_mods(c, ada_w, ada_b)
    x2 = x.reshape(nb * seq, d)
    for i in range(depth):
        g = [norm_g[i, r].reshape(1, d).astype(F32) for r in range(4)]
        j = i // 2
        if i % 2 == 0:
            x2 = _fox_layer(x2, nb, seq, g[0], g[1], mods[2 * i], fox_w_in[j], fox_f_bias[j],
                            fox_q_norm[j], fox_k_norm[j], fox_w_o[j])
        else:
            x2 = _gdn_layer(x2, nb, seq, g[0], g[1], mods[2 * i], gdn_w_in[j], gdn_conv_w[j],
                            gdn_a_log[j], gdn_dt_bias[j], gdn_out_norm[j], gdn_w_o[j])
        x2 = _ffn_layer(x2, seq, g[2], g[3], mods[2 * i + 1], ffn_w_up[i], ffn_conv_w[i],
                        ffn_conv_b[i], ffn_w_down[i])
    return x2.reshape(nb, seq, d)
```

```python
import functools

import jax
import jax.numpy as jnp
from jax import lax
from jax.experimental import pallas as pl
from jax.experimental.pallas import tpu as pltpu

F32 = jnp.float32
BF16 = jnp.bfloat16
EPS = 1e-6
HEAD = 128
CHUNK = 64
SUBLANES = 8
MXU_N = 256
VMEM_LIMIT = 56 * 1024 * 1024
INV_SQRT2 = 0.7071067811865476
LOG2E = 1.4426950408889634


def _cparams(*sem):
    return pltpu.CompilerParams(dimension_semantics=sem, vmem_limit_bytes=VMEM_LIMIT)


def _dot(a, b):
    return jnp.dot(a, b, preferred_element_type=F32)


def _dot_nt(a, b):
    return lax.dot_general(a, b, (((1,), (1,)), ((), ())), preferred_element_type=F32)


def _dot_tn(a, b):
    return lax.dot_general(a, b, (((0,), (0,)), ((), ())), preferred_element_type=F32)


def _dot_f32(a, b):
    return jnp.dot(a, b, preferred_element_type=F32, precision=lax.Precision.HIGHEST)


def _softplus(x):
    return jnp.maximum(x, 0.0) + jnp.log1p(jnp.exp(-jnp.abs(x)))


def _ada_kernel(c_ref, w_ref, b_ref, o_ref):
    c = c_ref[...]
    o_ref[0] = _dot_f32(c * jax.nn.sigmoid(c), w_ref[0]) + b_ref[0]


def _ada_mods(c, ada_w, ada_b):
    nb, d = c.shape
    n = ada_w.shape[0] * ada_w.shape[1]
    w = ada_w.reshape(n, d, 3 * d)
    b = ada_b.reshape(n, 1, 3 * d)
    cp = jnp.zeros((SUBLANES, d), F32).at[:nb].set(c)
    tn = 1024
    out = pl.pallas_call(
        _ada_kernel,
        grid=(n, 3 * d // tn),
        in_specs=[pl.BlockSpec((SUBLANES, d), lambda s, j: (0, 0)),
                  pl.BlockSpec((1, d, tn), lambda s, j: (s, 0, j)),
                  pl.BlockSpec((1, 1, tn), lambda s, j: (s, 0, j))],
        out_specs=pl.BlockSpec((1, SUBLANES, tn), lambda s, j: (s, 0, j)),
        out_shape=jax.ShapeDtypeStruct((n, SUBLANES, 3 * d), F32),
        compiler_params=_cparams("parallel", "parallel"),
        name="ada_mods",
    )(cp, w, b)
    return out[:, :nb].reshape(n, nb, 1, 3 * d)


def _norm_mod(h_ref, x_ref, g_ref, sh_ref, sc_ref):
    x = x_ref[...]
    ms = jnp.mean(x * x, axis=-1, keepdims=True)
    y = x * lax.rsqrt(ms + EPS) * g_ref[...]
    h_ref[...] = (y * (1.0 + sc_ref[...]) + sh_ref[...]).astype(BF16)


def _sub_dots(h_ref, w_ref):
    return [_dot(h_ref[...], w_ref[:, c * MXU_N:(c + 1) * MXU_N]) for c in range(w_ref.shape[1] // MXU_N)]


def _head_norm_store(o_ref, c, y, gain_ref, mean):
    for g in range(MXU_N // HEAD):
        sl = slice(c * MXU_N + g * HEAD, c * MXU_N + (g + 1) * HEAD)
        t = y[:, g * HEAD:(g + 1) * HEAD]
        ss = jnp.sum(t * t, axis=-1, keepdims=True)
        if mean:
            ss = ss * (1.0 / HEAD)
        o_ref[:, sl] = (t * lax.rsqrt(ss + EPS) * gain_ref[:, sl]).astype(o_ref.dtype)


def _causal_conv(acc, cw, carry_ref, slot, first):
    tm = acc.shape[0]
    taps = cw.shape[0]
    prev = jnp.where(first, 0.0, carry_ref[slot])
    carry_ref[slot] = acc[tm - SUBLANES:tm, :]
    xe = jnp.concatenate([prev, acc], axis=0)
    ye = xe * cw[0:1, :]
    for k in range(1, taps):
        ye = xe * cw[k:k + 1, :] + pltpu.roll(ye, 1, axis=0)
    return ye[SUBLANES:, :]


def _zero_carry_at_start(carry_ref):
    @pl.when((pl.program_id(0) == 0) & (pl.program_id(1) == 0))
    def _():
        carry_ref[...] = jnp.zeros_like(carry_ref)


def _fox_qk_kernel(x_ref, g_ref, sh_ref, sc_ref, w_ref, gn_ref, ws_ref, o_ref, os_ref, h_ref):
    @pl.when(pl.program_id(1) == 0)
    def _():
        _norm_mod(h_ref, x_ref, g_ref, sh_ref, sc_ref)
        os_ref[...] = _dot(h_ref[...], ws_ref[...])

    for c, acc in enumerate(_sub_dots(h_ref, w_ref)):
        _head_norm_store(o_ref, c, acc, gn_ref, mean=True)


def _plain_proj_kernel(h_ref, w_ref, o_ref):
    o_ref[...] = _dot(h_ref[...], w_ref[...]).astype(o_ref.dtype)


def _gdn_qk_kernel(x_ref, g_ref, sh_ref, sc_ref, w_ref, cw_ref, qs_ref, ws_ref, o_ref, os_ref, h_ref,
                   carry_ref, *, blocks_per_seq):
    i = pl.program_id(0)
    j = pl.program_id(1)
    _zero_carry_at_start(carry_ref)

    @pl.when(j == 0)
    def _():
        _norm_mod(h_ref, x_ref, g_ref, sh_ref, sc_ref)
        os_ref[...] = _dot(h_ref[...], ws_ref[...])

    first = (i % blocks_per_seq) == 0
    accs = _sub_dots(h_ref, w_ref)
    for c, acc in enumerate(accs):
        y = _causal_conv(acc, cw_ref[:, c * MXU_N:(c + 1) * MXU_N], carry_ref, j * len(accs) + c, first)
        _head_norm_store(o_ref, c, y * jax.nn.sigmoid(y), qs_ref, mean=False)


def _gdn_v_kernel(h_ref, w_ref, cw_ref, o_ref, carry_ref, *, blocks_per_seq):
    i = pl.program_id(0)
    j = pl.program_id(1)
    _zero_carry_at_start(carry_ref)
    first = (i % blocks_per_seq) == 0
    accs = _sub_dots(h_ref, w_ref)
    for c, acc in enumerate(accs):
        cs = slice(c * MXU_N, (c + 1) * MXU_N)
        y = _causal_conv(acc, cw_ref[:, cs], carry_ref, j * len(accs) + c, first)
        o_ref[:, cs] = (y * jax.nn.sigmoid(y)).astype(o_ref.dtype)


def _ffn_up_kernel(x_ref, g_ref, sh_ref, sc_ref, wg_ref, wv_ref, cw_ref, cb_ref, o_ref,
                   h_ref, carry_ref, *, blocks_per_seq):
    i = pl.program_id(0)
    j = pl.program_id(1)
    _zero_carry_at_start(carry_ref)

    @pl.when(j == 0)
    def _():
        _norm_mod(h_ref, x_ref, g_ref, sh_ref, sc_ref)

    first = (i % blocks_per_seq) == 0
    n_sub = wg_ref.shape[1] // MXU_N
    pairs = []
    for c in range(n_sub):
        cs = slice(c * MXU_N, (c + 1) * MXU_N)
        pairs.append((_dot(h_ref[...], wg_ref[:, cs]), _dot(h_ref[...], wv_ref[:, cs])))
    for c, (gate, val) in enumerate(pairs):
        cs = slice(c * MXU_N, (c + 1) * MXU_N)
        gate = _causal_conv(gate, cw_ref[:, cs], carry_ref, j * n_sub + c, first) + cb_ref[:, cs]
        act = 0.5 * gate * (1.0 + lax.erf(gate * INV_SQRT2))
        o_ref[:, cs] = (act * val).astype(o_ref.dtype)


def _x_mod_specs(tm, d, bps):
    return [pl.BlockSpec((tm, d), lambda i, j: (i, 0)),
            pl.BlockSpec((1, d), lambda i, j: (0, 0)),
            pl.BlockSpec((None, 1, d), lambda i, j: (i // bps, 0, 0)),
            pl.BlockSpec((None, 1, d), lambda i, j: (i // bps, 0, 1))]


def _carry_scratch(tn, nj):
    return pltpu.VMEM((nj * (tn // MXU_N), SUBLANES, MXU_N), F32)


def _fox_qk_proj(x2, g_pre, mod, w_main, gn, w_small, seq, tm=1024, tn=1024):
    t, d = x2.shape
    n = gn.shape[1]
    return pl.pallas_call(
        _fox_qk_kernel,
        grid=(t // tm, n // tn),
        in_specs=_x_mod_specs(tm, d, seq // tm) + [
            pl.BlockSpec((d, tn), lambda i, j: (0, j)),
            pl.BlockSpec((1, tn), lambda i, j: (0, j)),
            pl.BlockSpec((d, HEAD), lambda i, j: (0, 0))],
        out_specs=[pl.BlockSpec((tm, tn), lambda i, j: (i, j)),
                   pl.BlockSpec((tm, HEAD), lambda i, j: (i, 0)),
                   pl.BlockSpec((tm, d), lambda i, j: (i, 0))],
        out_shape=[jax.ShapeDtypeStruct((t, n), BF16), jax.ShapeDtypeStruct((t, HEAD), F32),
                   jax.ShapeDtypeStruct((t, d), BF16)],
        compiler_params=_cparams("parallel", "arbitrary"),
        name="fox_qk_proj",
    )(x2, g_pre, mod, mod, w_main, gn, w_small)


def _plain_proj(h, w_main, col0, n, tm=1024, tn=1024):
    t, d = h.shape
    off = col0 // tn
    return pl.pallas_call(
        _plain_proj_kernel,
        grid=(t // tm, n // tn),
        in_specs=[pl.BlockSpec((tm, d), lambda i, j: (i, 0)),
                  pl.BlockSpec((d, tn), lambda i, j: (0, off + j))],
        out_specs=pl.BlockSpec((tm, tn), lambda i, j: (i, j)),
        out_shape=jax.ShapeDtypeStruct((t, n), BF16),
        compiler_params=_cparams("parallel", "arbitrary"),
        name="plain_proj",
    )(h, w_main)


def _gdn_qk_proj(x2, g_pre, mod, w_main, conv_w, qscale, w_small, seq, tm=1024, tn=1024):
    t, d = x2.shape
    n = qscale.shape[1]
    taps = conv_w.shape[0]
    return pl.pallas_call(
        functools.partial(_gdn_qk_kernel, blocks_per_seq=seq // tm),
        grid=(t // tm, n // tn),
        in_specs=_x_mod_specs(tm, d, seq // tm) + [
            pl.BlockSpec((d, tn), lambda i, j: (0, j)),
            pl.BlockSpec((taps, tn), lambda i, j: (0, j)),
            pl.BlockSpec((1, tn), lambda i, j: (0, j)),
            pl.BlockSpec((d, HEAD), lambda i, j: (0, 0))],
        out_specs=[pl.BlockSpec((tm, tn), lambda i, j: (i, j)),
                   pl.BlockSpec((tm, HEAD), lambda i, j: (i, 0)),
                   pl.BlockSpec((tm, d), lambda i, j: (i, 0))],
        out_shape=[jax.ShapeDtypeStruct((t, n), BF16), jax.ShapeDtypeStruct((t, HEAD), F32),
                   jax.ShapeDtypeStruct((t, d), BF16)],
        scratch_shapes=[_carry_scratch(tn, n // tn)],
        compiler_params=_cparams("arbitrary", "arbitrary"),
        name="gdn_qk_proj",
    )(x2, g_pre, mod, mod, w_main, conv_w, qscale, w_small)


def _gdn_v_proj(h, w_main, conv_w, col0, n, seq, tm=1024, tn=1024):
    t, d = h.shape
    taps = conv_w.shape[0]
    off = col0 // tn
    return pl.pallas_call(
        functools.partial(_gdn_v_kernel, blocks_per_seq=seq // tm),
        grid=(t // tm, n // tn),
        in_specs=[pl.BlockSpec((tm, d), lambda i, j: (i, 0)),
                  pl.BlockSpec((d, tn), lambda i, j: (0, off + j)),
                  pl.BlockSpec((taps, tn), lambda i, j: (0, off + j))],
        out_specs=pl.BlockSpec((tm, tn), lambda i, j: (i, j)),
        out_shape=jax.ShapeDtypeStruct((t, n), BF16),
        scratch_shapes=[_carry_scratch(tn, n // tn)],
        compiler_params=_cparams("arbitrary", "arbitrary"),
        name="gdn_v_proj",
    )(h, w_main, conv_w)


def _ffn_up(x2, g_pre, mod, w_up, layer, conv_w, conv_b, seq, tm=1024, tn=512):
    t, d = x2.shape
    dff = w_up.shape[2] // 2
    nj = dff // tn
    taps = conv_w.shape[0]
    return pl.pallas_call(
        functools.partial(_ffn_up_kernel, blocks_per_seq=seq // tm),
        grid=(t // tm, nj),
        in_specs=_x_mod_specs(tm, d, seq // tm) + [
            pl.BlockSpec((None, d, tn), lambda i, j: (layer, 0, j)),
            pl.BlockSpec((None, d, tn), lambda i, j: (layer, 0, j + nj)),
            pl.BlockSpec((taps, tn), lambda i, j: (0, j)),
            pl.BlockSpec((1, tn), lambda i, j: (0, j))],
        out_specs=pl.BlockSpec((tm, tn), lambda i, j: (i, j)),
        out_shape=jax.ShapeDtypeStruct((t, dff), BF16),
        scratch_shapes=[pltpu.VMEM((tm, d), BF16), _carry_scratch(tn, nj)],
        compiler_params=_cparams("arbitrary", "arbitrary"),
        name="ffn_up",
    )(x2, g_pre, mod, mod, w_up, w_up, conv_w, conv_b)


def _out_proj_kernel(a_ref, w_ref, x_ref, g_ref, gate_ref, o_ref):
    y = _dot(a_ref[...], w_ref[...])
    ms = jnp.mean(y * y, axis=-1, keepdims=True)
    o_ref[...] = x_ref[...] + gate_ref[...] * (y * lax.rsqrt(ms + EPS) * g_ref[...])


def _out_proj(a, w, layer, x2, g_post, mod, seq, tm):
    t, kd = a.shape
    d = w.shape[2]
    bps = seq // tm
    return pl.pallas_call(
        _out_proj_kernel,
        grid=(t // tm,),
        in_specs=[pl.BlockSpec((tm, kd), lambda i: (i, 0)),
                  pl.BlockSpec((None, kd, d), lambda i: (layer, 0, 0), pipeline_mode=pl.Buffered(1)),
                  pl.BlockSpec((tm, d), lambda i: (i, 0)),
                  pl.BlockSpec((1, d), lambda i: (0, 0)),
                  pl.BlockSpec((None, 1, d), lambda i: (i // bps, 0, 2))],
        out_specs=pl.BlockSpec((tm, d), lambda i: (i, 0)),
        out_shape=jax.ShapeDtypeStruct((t, d), F32),
        compiler_params=_cparams("parallel"),
        name="out_proj",
    )(a, w, x2, g_post, mod)


def _fox_gate_kernel(f_ref, fb_ref, o_ref, carry_ref):
    tm = f_ref.shape[0]

    @pl.when(pl.program_id(1) == 0)
    def _():
        carry_ref[...] = jnp.zeros_like(carry_ref)

    x = f_ref[...] + fb_ref[...]
    logf = (jnp.minimum(x, 0.0) - jnp.log1p(jnp.exp(-jnp.abs(x)))) * LOG2E
    r = lax.broadcasted_iota(jnp.int32, (tm, tm), 0)
    c = lax.broadcasted_iota(jnp.int32, (tm, tm), 1)
    tri = (r >= c).astype(F32)
    cum = _dot_f32(tri, logf) + carry_ref[0:1, :]
    o_ref[...] = cum
    carry_ref[0:1, :] = cum[tm - 1:tm, :]


def _fox_gate_cumsum(f_logit, f_bias_row, nb, seq, tm=256):
    return pl.pallas_call(
        _fox_gate_kernel,
        grid=(nb, seq // tm),
        in_specs=[pl.BlockSpec((None, tm, HEAD), lambda b, i: (b, i, 0)),
                  pl.BlockSpec((1, HEAD), lambda b, i: (0, 0))],
        out_specs=pl.BlockSpec((None, tm, HEAD), lambda b, i: (b, i, 0)),
        out_shape=jax.ShapeDtypeStruct((nb, seq, HEAD), F32),
        scratch_shapes=[pltpu.VMEM((SUBLANES, HEAD), F32)],
        compiler_params=_cparams("parallel", "arbitrary"),
        name="fox_gate_cumsum",
    )(f_logit.reshape(nb, seq, HEAD), f_bias_row)


def _attn_kernel(q_ref, k_ref, v_ref, og_ref, fk_ref, o_ref,
                 s0_ref, s1_ref, m_ref, l_ref, acc_ref, *, tq):
    h = pl.program_id(1)
    qi = pl.program_id(2)
    m_ref[...] = jnp.full(m_ref.shape, -jnp.inf, F32)
    l_ref[...] = jnp.zeros(l_ref.shape, F32)
    acc_ref[...] = jnp.zeros(acc_ref.shape, F32)

    def scores(s_ref, ki):
        ks = pl.multiple_of(ki * tq, tq)
        s_ref[...] = _dot_nt(q_ref[...], k_ref[pl.ds(ks, tq), :]) - fk_ref[pl.ds(h, 1), pl.ds(ks, tq)]

    def update(s_ref, ki, masked):
        ks = pl.multiple_of(ki * tq, tq)
        s = s_ref[...]
        if masked:
            row = lax.broadcasted_iota(jnp.int32, (tq, tq), 0)
            col = lax.broadcasted_iota(jnp.int32, (tq, tq), 1)
            s = jnp.where(col <= row, s, -jnp.inf)
        m_old = m_ref[...]
        m_new = jnp.maximum(m_old, jnp.max(s, axis=-1, keepdims=True))
        p = jnp.exp2(s - m_new)
        alpha = jnp.exp2(m_old - m_new)
        l_ref[...] = alpha * l_ref[...] + jnp.sum(p, axis=-1, keepdims=True)
        m_ref[...] = m_new
        acc_ref[...] = alpha * acc_ref[...] + _dot(p.astype(BF16), v_ref[pl.ds(ks, tq), :])

    scores(s0_ref, 0)

    def pair(pi, carry):
        scores(s1_ref, 2 * pi + 1)
        update(s0_ref, 2 * pi, False)
        scores(s0_ref, 2 * pi + 2)
        update(s1_ref, 2 * pi + 1, False)
        return carry

    lax.fori_loop(0, qi // 2, pair, 0)

    @pl.when(qi % 2 == 1)
    def _():
        scores(s1_ref, qi)
        update(s0_ref, qi - 1, False)
        update(s1_ref, qi, True)

    @pl.when(qi % 2 == 0)
    def _():
        update(s0_ref, qi, True)

    o_ref[...] = (acc_ref[...] / l_ref[...] * jax.nn.sigmoid(og_ref[...].astype(F32))).astype(o_ref.dtype)


def _split3(x):
    hi = x.astype(BF16).astype(F32)
    mid = (x - hi).astype(BF16).astype(F32)
    return hi, mid, x - hi - mid


def _attn_fixed_shift_kernel(q_ref, k_ref, v_ref, og_ref, fc_ref, c_ref, o_ref,
                             ka_ref, va_ref, qa_ref, p0_ref, p1_ref, acc_ref, *, tq):
    h = pl.program_id(1)
    step = pl.program_id(2)
    seq = k_ref.shape[0]
    nq = seq // tq
    q_blocks = (step, nq - 1 - step)

    def block_rows(i):
        return pl.ds(pl.multiple_of(i * tq, tq), tq)

    def head_col(rows):
        lane = lax.broadcasted_iota(jnp.int32, (tq, HEAD), 1)
        return jnp.sum(jnp.where(lane == h, fc_ref[rows, :], 0.0), axis=-1, keepdims=True)

    def extra_cols(first, second):
        lane = lax.broadcasted_iota(jnp.int32, (tq, HEAD), 1)
        cols = jnp.zeros((tq, HEAD), F32)
        for n, term in enumerate(first + second):
            cols = jnp.where(lane == n, term, cols)
        return cols.astype(BF16)

    ones3 = (1.0, 1.0, 1.0)

    @pl.when(step == 0)
    def _():
        def build(i, carry):
            rows = block_rows(i)
            hi, mid, lo = _split3(-head_col(rows))
            ka_ref[rows, 0:HEAD] = k_ref[rows, :]
            ka_ref[rows, HEAD:2 * HEAD] = extra_cols((hi, mid, lo), ones3)
            va_ref[rows, 0:HEAD] = v_ref[rows, :]
            va_ref[rows, HEAD:2 * HEAD] = jnp.ones((tq, HEAD), BF16)
            return carry
        lax.fori_loop(0, nq, build, 0)

    for n, qb in enumerate(q_blocks):
        rows = block_rows(qb)
        qa_ref[n] = jnp.concatenate(
            [q_ref[rows, :], extra_cols(ones3, _split3(head_col(rows) - c_ref[0:1, 0:1]))], axis=1)
    acc_ref[...] = jnp.zeros(acc_ref.shape, F32)

    def probs(p_ref, which, kb, masked):
        s = _dot_nt(qa_ref[which], ka_ref[block_rows(kb), :])
        if masked:
            row = lax.broadcasted_iota(jnp.int32, (tq, tq), 0)
            col = lax.broadcasted_iota(jnp.int32, (tq, tq), 1)
            s = jnp.where(col <= row, s, -jnp.inf)
        p_ref[...] = jnp.exp2(s).astype(BF16)

    def accumulate(p_ref, which, kb):
        acc_ref[which] += _dot(p_ref[...], va_ref[block_rows(kb), :])

    items = [(0, q_blocks[0], True), (1, q_blocks[1], True)]
    for n in range(nq - 1):
        which = (n >= q_blocks[0]).astype(jnp.int32)
        items.append((which, n - which * q_blocks[0], False))
    p_refs = (p0_ref, p1_ref)
    for n, (which, kb, masked) in enumerate(items):
        probs(p_refs[n % 2], which, kb, masked)
        if n > 0:
            accumulate(p_refs[(n - 1) % 2], *items[n - 1][:2])
    accumulate(p_refs[(len(items) - 1) % 2], *items[-1][:2])

    for n, qb in enumerate(q_blocks):
        rows = block_rows(qb)
        acc = acc_ref[n]
        o_ref[rows, :] = (acc[:, 0:HEAD] / acc[:, HEAD:HEAD + 1]
                          * jax.nn.sigmoid(og_ref[rows, :].astype(F32))).astype(o_ref.dtype)


def _fox_attention_fixed_shift(qk3, v3, og3, fcum, c_row, n_heads, tq=512):
    nb, seq, _ = qk3.shape
    nq = seq // tq
    assert nq % 2 == 0
    full = lambda off: pl.BlockSpec((None, seq, HEAD), lambda b, h, i: (b, 0, off + h))
    return pl.pallas_call(
        functools.partial(_attn_fixed_shift_kernel, tq=tq),
        grid=(nb, n_heads, nq // 2),
        in_specs=[full(0), full(n_heads), full(0), full(0),
                  pl.BlockSpec((None, seq, HEAD), lambda b, h, i: (b, 0, 0)),
                  pl.BlockSpec((1, HEAD), lambda b, h, i: (0, 0))],
        out_specs=full(0),
        out_shape=jax.ShapeDtypeStruct((nb, seq, n_heads * HEAD), BF16),
        scratch_shapes=[pltpu.VMEM((seq, 2 * HEAD), BF16), pltpu.VMEM((seq, 2 * HEAD), BF16),
                        pltpu.VMEM((2, tq, 2 * HEAD), BF16),
                        pltpu.VMEM((tq, tq), BF16), pltpu.VMEM((tq, tq), BF16),
                        pltpu.VMEM((2, tq, 2 * HEAD), F32)],
        compiler_params=_cparams("parallel", "parallel", "arbitrary"),
        name="fox_attention_fixed_shift",
    )(qk3, qk3, v3, og3, fcum, c_row)


MAX_SHIFT_GAP = 100.0


def _fox_attention(qk3, v3, og3, fcum, gn, n_heads):
    qd = n_heads * HEAD
    c = (HEAD * 1.01) * jnp.max(jnp.abs(gn[:, :qd])) * jnp.max(jnp.abs(gn[:, qd:]))
    c_row = jnp.full((1, HEAD), c, F32)

    def running_max(qk3, v3, og3, fcum, c_row):
        fk = jnp.transpose(fcum[:, :, :n_heads], (0, 2, 1))
        return _fox_attention_running_max(qk3, v3, og3, fk, n_heads)

    def fixed_shift(qk3, v3, og3, fcum, c_row):
        return _fox_attention_fixed_shift(qk3, v3, og3, fcum, c_row, n_heads)

    return lax.cond(2.0 * c <= MAX_SHIFT_GAP, fixed_shift, running_max, qk3, v3, og3, fcum, c_row)


def _fox_attention_running_max(qk3, v3, og3, fk, n_heads, tq=512):
    nb, seq, _ = qk3.shape
    return pl.pallas_call(
        functools.partial(_attn_kernel, tq=tq),
        grid=(nb, n_heads, seq // tq),
        in_specs=[pl.BlockSpec((None, tq, HEAD), lambda b, h, i: (b, i, h)),
                  pl.BlockSpec((None, seq, HEAD), lambda b, h, i: (b, 0, n_heads + h)),
                  pl.BlockSpec((None, seq, HEAD), lambda b, h, i: (b, 0, h)),
                  pl.BlockSpec((None, tq, HEAD), lambda b, h, i: (b, i, h)),
                  pl.BlockSpec((None, n_heads, seq), lambda b, h, i: (b, 0, 0))],
        out_specs=pl.BlockSpec((None, tq, HEAD), lambda b, h, i: (b, i, h)),
        out_shape=jax.ShapeDtypeStruct((nb, seq, n_heads * HEAD), BF16),
        scratch_shapes=[pltpu.VMEM((tq, tq), F32), pltpu.VMEM((tq, tq), F32),
                        pltpu.VMEM((tq, 1), F32), pltpu.VMEM((tq, 1), F32), pltpu.VMEM((tq, HEAD), F32)],
        compiler_params=_cparams("parallel", "parallel", "arbitrary"),
        name="fox_attention",
    )(qk3, qk3, v3, og3, fk)


def _gdn_gate_kernel(ab_ref, alog_ref, dt_ref, o_ref, *, n_heads):
    tm = ab_ref.shape[0]
    x = ab_ref[...]
    lane = lax.broadcasted_iota(jnp.int32, x.shape, 1)
    g = -jnp.exp(alog_ref[...]) * _softplus(x + dt_ref[...])
    r = lax.broadcasted_iota(jnp.int32, (tm, tm), 0)
    c = lax.broadcasted_iota(jnp.int32, (tm, tm), 1)
    tri = ((r >= c) & ((r // CHUNK) == (c // CHUNK))).astype(F32)
    gc = _dot_f32(tri, jnp.where(lane < n_heads, g, 0.0))
    o_ref[...] = jnp.where(lane < n_heads, gc, jax.nn.sigmoid(x))


def _gdn_gates(ab, alog_row, dt_row, n_heads, tm=256):
    t = ab.shape[0]
    return pl.pallas_call(
        functools.partial(_gdn_gate_kernel, n_heads=n_heads),
        grid=(t // tm,),
        in_specs=[pl.BlockSpec((tm, HEAD), lambda i: (i, 0)),
                  pl.BlockSpec((1, HEAD), lambda i: (0, 0)),
                  pl.BlockSpec((1, HEAD), lambda i: (0, 0))],
        out_specs=pl.BlockSpec((tm, HEAD), lambda i: (i, 0)),
        out_shape=jax.ShapeDtypeStruct((t, HEAD), F32),
        compiler_params=_cparams("parallel"),
        name="gdn_gates",
    )(ab, alog_row, dt_row)


def _block_diag2(p, half):
    return jnp.concatenate([jnp.where(half, 0.0, p), jnp.where(half, p, 0.0)], axis=0)


def _gdn_chunk_kernel(q_ref, k_ref, v_ref, gb_ref, grow_ref, u_ref, w_ref, qd_ref, kd_ref, qk_ref, egl_ref,
                      *, n_vheads, chunks):
    hk = pl.program_id(1)
    lane = lax.broadcasted_iota(jnp.int32, (CHUNK, HEAD), 1)
    row = lax.broadcasted_iota(jnp.int32, (CHUNK, HEAD), 0)
    col = lane & (CHUNK - 1)
    half = lane >= CHUNK
    lower = row >= col
    strict = row > col
    eye = (row == col).astype(F32)
    level_masks = []
    for lvl in range(6):
        level_masks.append((((row >> lvl) & 1) == 1) & (((col >> lvl) & 1) == 0)
                           & ((row >> (lvl + 1)) == (col >> (lvl + 1))))
    zeros = jnp.zeros((CHUNK, HEAD), F32)

    def pick(gb, idx):
        return jnp.sum(jnp.where(lane == idx, gb, 0.0), axis=-1, keepdims=True)

    sls = [slice(c * CHUNK, (c + 1) * CHUNK) for c in range(chunks)]
    gcs, bes, a_s, xs = [], [], [], []
    for c, sl in enumerate(sls):
        q = q_ref[sl, :]
        k = k_ref[sl, :]
        gb = gb_ref[sl, :]
        gc0, gc1 = pick(gb, 2 * hk), pick(gb, 2 * hk + 1)
        be0, be1 = pick(gb, n_vheads + 2 * hk), pick(gb, n_vheads + 2 * hk + 1)
        gcol = jnp.where(half, gc1, gc0)
        bcol = jnp.where(half, be1, be0)
        decay = jnp.exp(jnp.where(lower, gcol - grow_ref[c:c + 1, :], -jnp.inf))
        qkk = _dot_nt(jnp.concatenate([q, k], axis=0), jnp.concatenate([k, k], axis=0))
        a = jnp.where(strict, qkk[CHUNK:, :] * decay * bcol, 0.0)
        qk_ref[sl, :] = (qkk[:CHUNK, :] * decay).astype(qk_ref.dtype)
        gcs.append((gc0, gc1))
        bes.append((be0, be1))
        a_s.append(a)
        xs.append(eye - jnp.where(level_masks[0], a, 0.0))

    for lvl in range(1, 6):
        ys = [_dot(xs[c].astype(BF16), _block_diag2(jnp.where(level_masks[lvl], a_s[c], 0.0), half).astype(BF16))
              for c in range(chunks)]
        xs = [xs[c] - _dot(ys[c].astype(BF16), _block_diag2(xs[c], half).astype(BF16)) for c in range(chunks)]

    for c, sl in enumerate(sls):
        (gc0, gc1), (be0, be1) = gcs[c], bes[c]
        kf = k_ref[sl, :].astype(F32)
        qf = q_ref[sl, :].astype(F32)
        v = v_ref[sl, :].astype(F32)
        eg0, eg1 = jnp.exp(gc0), jnp.exp(gc1)
        rhs = jnp.concatenate([
            jnp.concatenate([v[:, :HEAD] * be0, kf * (be0 * eg0), zeros, zeros], axis=1),
            jnp.concatenate([zeros, zeros, v[:, HEAD:] * be1, kf * (be1 * eg1)], axis=1)], axis=0)
        sol = _dot(xs[c].astype(BF16), rhs.astype(BF16))
        u_ref[sl, :] = jnp.concatenate([sol[:, 0:HEAD], sol[:, 2 * HEAD:3 * HEAD]], axis=1).astype(u_ref.dtype)
        w_ref[sl, :] = jnp.concatenate([sol[:, HEAD:2 * HEAD], sol[:, 3 * HEAD:]], axis=1).astype(w_ref.dtype)
        qd_ref[sl, :] = jnp.concatenate([qf * eg0, qf * eg1], axis=1).astype(qd_ref.dtype)
        gl0, gl1 = gc0[CHUNK - 1:CHUNK, :], gc1[CHUNK - 1:CHUNK, :]
        kd_ref[sl, :] = jnp.concatenate([kf * jnp.exp(gl0 - gc0), kf * jnp.exp(gl1 - gc1)],
                                        axis=1).astype(kd_ref.dtype)
        egl_ref[0, c:c + 1, :] = jnp.broadcast_to(jnp.exp(gl0), (1, HEAD))
        egl_ref[1, c:c + 1, :] = jnp.broadcast_to(jnp.exp(gl1), (1, HEAD))


def _gdn_chunks(qk3, v3, gb3, grow, n_kheads, n_vheads, tm=1024):
    nb, seq, _ = qk3.shape
    chunks = tm // CHUNK
    vd = n_vheads * HEAD
    big = jax.ShapeDtypeStruct((nb, seq, vd), BF16)
    pair_spec = pl.BlockSpec((None, tm, 2 * HEAD), lambda b, h, i: (b, i, h))
    return pl.pallas_call(
        functools.partial(_gdn_chunk_kernel, n_vheads=n_vheads, chunks=chunks),
        grid=(nb, n_kheads, seq // tm),
        in_specs=[pl.BlockSpec((None, tm, HEAD), lambda b, h, i: (b, i, h)),
                  pl.BlockSpec((None, tm, HEAD), lambda b, h, i: (b, i, n_kheads + h)),
                  pair_spec,
                  pl.BlockSpec((None, tm, HEAD), lambda b, h, i: (b, i, 0)),
                  pl.BlockSpec((None, None, chunks, HEAD), lambda b, h, i: (b, h, i, 0))],
        out_specs=[pair_spec, pair_spec, pair_spec, pair_spec,
                   pl.BlockSpec((None, tm, HEAD), lambda b, h, i: (b, i, h)),
                   pl.BlockSpec((None, 2, chunks, HEAD), lambda b, h, i: (b, h, i, 0))],
        out_shape=[big, big, big, big,
                   jax.ShapeDtypeStruct((nb, seq, n_kheads * HEAD), BF16),
                   jax.ShapeDtypeStruct((nb, n_vheads, seq // CHUNK, HEAD), F32)],
        compiler_params=_cparams("parallel", "parallel", "parallel"),
        name="gdn_chunks",
    )(qk3, qk3, v3, gb3, grow)


def _gdn_scan_kernel(u_ref, w_ref, qd_ref, kd_ref, qk_ref, egl_ref, z_ref, gn_ref, o_ref, s_ref,
                     *, heads, chunks):
    @pl.when(pl.program_id(2) == 0)
    def _():
        s_ref[...] = jnp.zeros_like(s_ref)

    zeros = jnp.zeros((CHUNK, HEAD), F32)
    gn = gn_ref[...]
    cols = [slice(g * HEAD, (g + 1) * HEAD) for g in range(heads)]
    for c in range(chunks):
        sl = slice(c * CHUNK, (c + 1) * CHUNK)
        rs = [_dot(jnp.concatenate([w_ref[sl, cols[g]], qd_ref[sl, cols[g]]], axis=0), s_ref[g].astype(BF16))
              for g in range(heads)]
        vnew = [u_ref[sl, cols[g]].astype(F32) - rs[g][:CHUNK, :] for g in range(heads)]
        intra = []
        for pr in range(heads // 2):
            v2 = jnp.concatenate([jnp.concatenate([vnew[2 * pr], zeros], axis=1),
                                  jnp.concatenate([zeros, vnew[2 * pr + 1]], axis=1)], axis=0).astype(BF16)
            intra.append(_dot(qk_ref[sl, pr * HEAD:(pr + 1) * HEAD], v2))
        for g in range(heads):
            s_ref[g] = s_ref[g] * egl_ref[g, c:c + 1, :] + _dot_tn(kd_ref[sl, cols[g]], vnew[g].astype(BF16))
        for g in range(heads):
            o = rs[g][CHUNK:, :] + intra[g // 2][:, (g % 2) * HEAD:(g % 2 + 1) * HEAD]
            ms = jnp.mean(o * o, axis=-1, keepdims=True)
            z = z_ref[sl, cols[g]].astype(F32)
            o_ref[sl, cols[g]] = (o * lax.rsqrt(ms + EPS) * gn * (z * jax.nn.sigmoid(z))).astype(o_ref.dtype)


def _gdn_scan(u, w, qd, kd, qkm, egl, z3, gn_row, n_vheads, heads=16, tm=512):
    nb, seq, vd = u.shape
    chunks = tm // CHUNK
    wide = pl.BlockSpec((None, tm, heads * HEAD), lambda b, g, i: (b, i, g))
    return pl.pallas_call(
        functools.partial(_gdn_scan_kernel, heads=heads, chunks=chunks),
        grid=(nb, n_vheads // heads, seq // tm),
        in_specs=[wide, wide, wide, wide,
                  pl.BlockSpec((None, tm, heads // 2 * HEAD), lambda b, g, i: (b, i, g)),
                  pl.BlockSpec((None, heads, chunks, HEAD), lambda b, g, i: (b, g, i, 0)),
                  wide,
                  pl.BlockSpec((1, HEAD), lambda b, g, i: (0, 0))],
        out_specs=wide,
        out_shape=jax.ShapeDtypeStruct((nb, seq, vd), BF16),
        scratch_shapes=[pltpu.VMEM((heads, HEAD, HEAD), F32)],
        compiler_params=_cparams("parallel", "parallel", "arbitrary"),
        name="gdn_scan",
    )(u, w, qd, kd, qkm, egl, z3, gn_row)


def _pad_cols(w, n):
    return jnp.pad(w, ((0, 0), (0, n - w.shape[1])))


def _row(v, n=HEAD):
    return jnp.pad(v.astype(F32), (0, n - v.shape[0])).reshape(1, n)


def _fox_layer(x2, nb, seq, g_pre, g_post, mod, w_in, f_bias, q_norm, k_norm, w_o, layer):
    d = x2.shape[1]
    nh = d // HEAD
    qd = nh * HEAD
    w_small = _pad_cols(w_in[:, 3 * qd:3 * qd + nh], HEAD)
    w_gate = w_in[:, 3 * qd + nh:]
    gn = jnp.concatenate([jnp.tile(q_norm.astype(F32) * (HEAD ** -0.5 * LOG2E), nh),
                          jnp.tile(k_norm.astype(F32), nh)]).reshape(1, 2 * qd)
    qk, f_logit, h = _fox_qk_proj(x2, g_pre, mod, w_in, gn, w_small, seq)
    v = _plain_proj(h, w_in, 2 * qd, qd)
    og = _plain_proj(h, w_gate, 0, qd)
    fcum = _fox_gate_cumsum(f_logit, _row(f_bias), nb, seq)
    attn = _fox_attention(qk.reshape(nb, seq, 2 * qd), v.reshape(nb, seq, qd), og.reshape(nb, seq, qd),
                          fcum, gn, nh)
    return _out_proj(attn.reshape(nb * seq, qd), w_o, layer, x2, g_post, mod, seq, tm=512)


def _gdn_layer(x2, nb, seq, g_pre, g_post, mod, w_in, conv_w, a_log, dt_bias, out_norm, w_o, layer):
    d = x2.shape[1]
    nk = d // HEAD
    nv = 2 * nk
    kd, vd = nk * HEAD, nv * HEAD
    n_main = 2 * kd + 2 * vd
    w_small = _pad_cols(w_in[:, n_main:], HEAD)
    conv_w = conv_w.astype(F32)
    qscale = jnp.concatenate([jnp.full((kd,), HEAD ** -0.5, F32), jnp.ones((kd,), F32)]).reshape(1, 2 * kd)
    qk, ab, h = _gdn_qk_proj(x2, g_pre, mod, w_in, conv_w, qscale, w_small, seq)
    v = _gdn_v_proj(h, w_in, conv_w, 2 * kd, vd, seq)
    z = _plain_proj(h, w_in, 2 * kd + vd, vd)
    gb = _gdn_gates(ab, _row(a_log), _row(dt_bias), nv)
    n_chunks = seq // CHUNK
    grow = gb[:, :nv].reshape(nb, n_chunks, CHUNK, nk, 2)
    grow = jnp.transpose(grow, (0, 3, 1, 4, 2)).reshape(nb, nk, n_chunks, 2 * CHUNK)
    u, w, qdec, kdec, qkm, egl = _gdn_chunks(qk.reshape(nb, seq, 2 * kd), v.reshape(nb, seq, vd),
                                             gb.reshape(nb, seq, HEAD), grow, nk, nv)
    a = _gdn_scan(u, w, qdec, kdec, qkm, egl, z.reshape(nb, seq, vd), _row(out_norm), nv)
    return _out_proj(a.reshape(nb * seq, vd), w_o, layer, x2, g_post, mod, seq, tm=512)


def _ffn_layer(x2, seq, g_pre, g_post, mod, w_up, conv_w, conv_b, w_down, layer):
    u = _ffn_up(x2, g_pre, mod, w_up, layer, conv_w.astype(F32), conv_b.reshape(1, -1).astype(F32), seq)
    return _out_proj(u, w_down, layer, x2, g_post, mod, seq, tm=256)


def kernel(x, c, ada_w, ada_b, norm_g, fox_w_in, fox_f_bias, fox_q_norm, fox_k_norm, fox_w_o,
           gdn_w_in, gdn_conv_w, gdn_a_log, gdn_dt_bias, gdn_out_norm, gdn_w_o,
           ffn_w_up, ffn_conv_w, ffn_conv_b, ffn_w_down):
    nb, seq, d = x.shape
    depth = ada_w.shape[0]
    mods = _ada_mods(c, ada_w, ada_b)
    fox_w_in, fox_w_o, gdn_w_in, gdn_w_o, ffn_w_up, ffn_w_down = (
        w.astype(BF16) for w in (fox_w_in, fox_w_o, gdn_w_in, gdn_w_o, ffn_w_up, ffn_w_down))
    x2 = x.reshape(nb * seq, d)
    for i in range(depth):
        g = [norm_g[i, r].reshape(1, d).astype(F32) for r in range(4)]
        j = i // 2
        if i % 2 == 0:
            x2 = _fox_layer(x2, nb, seq, g[0], g[1], mods[2 * i], fox_w_in[j], fox_f_bias[j],
                            fox_q_norm[j], fox_k_norm[j], fox_w_o, j)
        else:
            x2 = _gdn_layer(x2, nb, seq, g[0], g[1], mods[2 * i], gdn_w_in[j], gdn_conv_w[j],
                            gdn_a_log[j], gdn_dt_bias[j], gdn_out_norm[j], gdn_w_o, j)
        x2 = _ffn_layer(x2, seq, g[2], g[3], mods[2 * i + 1], ffn_w_up, ffn_conv_w[i],
                        ffn_conv_b[i], ffn_w_down, i)
    return x2.reshape(nb, seq, d)
```

```python
import functools

import jax
import jax.numpy as jnp
from jax import lax
from jax.experimental import pallas as pl
from jax.experimental.pallas import tpu as pltpu

F32 = jnp.float32
BF16 = jnp.bfloat16
EPS = 1e-6
HEAD = 128
CHUNK = 64
SUBLANES = 8
MXU_N = 256
VMEM_LIMIT = 56 * 1024 * 1024
INV_SQRT2 = 0.7071067811865476
LOG2E = 1.4426950408889634


def _cparams(*sem):
    return pltpu.CompilerParams(dimension_semantics=sem, vmem_limit_bytes=VMEM_LIMIT)


def _dot(a, b):
    return jnp.dot(a, b, preferred_element_type=F32)


def _dot_nt(a, b):
    return lax.dot_general(a, b, (((1,), (1,)), ((), ())), preferred_element_type=F32)


def _dot_tn(a, b):
    return lax.dot_general(a, b, (((0,), (0,)), ((), ())), preferred_element_type=F32)


def _dot_f32(a, b):
    return jnp.dot(a, b, preferred_element_type=F32, precision=lax.Precision.HIGHEST)


def _softplus(x):
    return jnp.maximum(x, 0.0) + jnp.log1p(jnp.exp(-jnp.abs(x)))


def _ada_kernel(c_ref, w_ref, b_ref, o_ref):
    c = c_ref[...]
    o_ref[0] = _dot_f32(c * jax.nn.sigmoid(c), w_ref[0]) + b_ref[0]


def _ada_mods(c, ada_w, ada_b):
    nb, d = c.shape
    n = ada_w.shape[0] * ada_w.shape[1]
    w = ada_w.reshape(n, d, 3 * d)
    b = ada_b.reshape(n, 1, 3 * d)
    cp = jnp.zeros((SUBLANES, d), F32).at[:nb].set(c)
    tn = 1024
    out = pl.pallas_call(
        _ada_kernel,
        grid=(n, 3 * d // tn),
        in_specs=[pl.BlockSpec((SUBLANES, d), lambda s, j: (0, 0)),
                  pl.BlockSpec((1, d, tn), lambda s, j: (s, 0, j)),
                  pl.BlockSpec((1, 1, tn), lambda s, j: (s, 0, j))],
        out_specs=pl.BlockSpec((1, SUBLANES, tn), lambda s, j: (s, 0, j)),
        out_shape=jax.ShapeDtypeStruct((n, SUBLANES, 3 * d), F32),
        compiler_params=_cparams("parallel", "parallel"),
        name="ada_mods",
    )(cp, w, b)
    return out[:, :nb].reshape(n, nb, 1, 3 * d)


def _norm_mod(h_ref, x_ref, g_ref, sh_ref, sc_ref):
    x = x_ref[...]
    ms = jnp.mean(x * x, axis=-1, keepdims=True)
    y = x * lax.rsqrt(ms + EPS) * g_ref[...]
    h_ref[...] = (y * (1.0 + sc_ref[...]) + sh_ref[...]).astype(BF16)


def _sub_dots(h_ref, w_ref):
    return [_dot(h_ref[...], w_ref[:, c * MXU_N:(c + 1) * MXU_N]) for c in range(w_ref.shape[1] // MXU_N)]


def _head_norm_store(o_ref, c, y, gain_ref, mean):
    for g in range(MXU_N // HEAD):
        sl = slice(c * MXU_N + g * HEAD, c * MXU_N + (g + 1) * HEAD)
        t = y[:, g * HEAD:(g + 1) * HEAD]
        ss = jnp.sum(t * t, axis=-1, keepdims=True)
        if mean:
            ss = ss * (1.0 / HEAD)
        o_ref[:, sl] = (t * lax.rsqrt(ss + EPS) * gain_ref[:, sl]).astype(o_ref.dtype)


def _causal_conv(acc, cw, carry_ref, slot, first):
    tm = acc.shape[0]
    taps = cw.shape[0]
    prev = jnp.where(first, 0.0, carry_ref[slot])
    carry_ref[slot] = acc[tm - SUBLANES:tm, :]
    xe = jnp.concatenate([prev, acc], axis=0)
    ye = xe * cw[0:1, :]
    for k in range(1, taps):
        ye = xe * cw[k:k + 1, :] + pltpu.roll(ye, 1, axis=0)
    return ye[SUBLANES:, :]


def _zero_carry_at_start(carry_ref):
    @pl.when((pl.program_id(0) == 0) & (pl.program_id(1) == 0))
    def _():
        carry_ref[...] = jnp.zeros_like(carry_ref)


def _fox_qk_kernel(x_ref, g_ref, sh_ref, sc_ref, w_ref, gn_ref, ws_ref, o_ref, os_ref, h_ref):
    @pl.when(pl.program_id(1) == 0)
    def _():
        _norm_mod(h_ref, x_ref, g_ref, sh_ref, sc_ref)
        os_ref[...] = _dot(h_ref[...], ws_ref[...])

    for c, acc in enumerate(_sub_dots(h_ref, w_ref)):
        _head_norm_store(o_ref, c, acc, gn_ref, mean=True)


def _plain_proj_kernel(h_ref, w_ref, o_ref):
    o_ref[...] = _dot(h_ref[...], w_ref[...]).astype(o_ref.dtype)


def _gdn_qk_kernel(x_ref, g_ref, sh_ref, sc_ref, w_ref, cw_ref, qs_ref, ws_ref, o_ref, os_ref, h_ref,
                   carry_ref, *, blocks_per_seq):
    i = pl.program_id(0)
    j = pl.program_id(1)
    _zero_carry_at_start(carry_ref)

    @pl.when(j == 0)
    def _():
        _norm_mod(h_ref, x_ref, g_ref, sh_ref, sc_ref)
        os_ref[...] = _dot(h_ref[...], ws_ref[...])

    first = (i % blocks_per_seq) == 0
    accs = _sub_dots(h_ref, w_ref)
    for c, acc in enumerate(accs):
        y = _causal_conv(acc, cw_ref[:, c * MXU_N:(c + 1) * MXU_N], carry_ref, j * len(accs) + c, first)
        _head_norm_store(o_ref, c, y * jax.nn.sigmoid(y), qs_ref, mean=False)


def _gdn_v_kernel(h_ref, w_ref, cw_ref, o_ref, carry_ref, *, blocks_per_seq):
    i = pl.program_id(0)
    j = pl.program_id(1)
    _zero_carry_at_start(carry_ref)
    first = (i % blocks_per_seq) == 0
    accs = _sub_dots(h_ref, w_ref)
    for c, acc in enumerate(accs):
        cs = slice(c * MXU_N, (c + 1) * MXU_N)
        y = _causal_conv(acc, cw_ref[:, cs], carry_ref, j * len(accs) + c, first)
        o_ref[:, cs] = (y * jax.nn.sigmoid(y)).astype(o_ref.dtype)


def _ffn_up_kernel(x_ref, g_ref, sh_ref, sc_ref, wg_ref, wv_ref, cw_ref, cb_ref, o_ref,
                   h_ref, carry_ref, *, blocks_per_seq):
    i = pl.program_id(0)
    j = pl.program_id(1)
    _zero_carry_at_start(carry_ref)

    @pl.when(j == 0)
    def _():
        _norm_mod(h_ref, x_ref, g_ref, sh_ref, sc_ref)

    first = (i % blocks_per_seq) == 0
    n_sub = wg_ref.shape[1] // MXU_N
    pairs = []
    for c in range(n_sub):
        cs = slice(c * MXU_N, (c + 1) * MXU_N)
        pairs.append((_dot(h_ref[...], wg_ref[:, cs]), _dot(h_ref[...], wv_ref[:, cs])))
    for c, (gate, val) in enumerate(pairs):
        cs = slice(c * MXU_N, (c + 1) * MXU_N)
        gate = _causal_conv(gate, cw_ref[:, cs], carry_ref, j * n_sub + c, first) + cb_ref[:, cs]
        act = 0.5 * gate * (1.0 + lax.erf(gate * INV_SQRT2))
        o_ref[:, cs] = (act * val).astype(o_ref.dtype)


def _x_mod_specs(tm, d, bps):
    return [pl.BlockSpec((tm, d), lambda i, j: (i, 0)),
            pl.BlockSpec((1, d), lambda i, j: (0, 0)),
            pl.BlockSpec((None, 1, d), lambda i, j: (i // bps, 0, 0)),
            pl.BlockSpec((None, 1, d), lambda i, j: (i // bps, 0, 1))]


def _carry_scratch(tn, nj):
    return pltpu.VMEM((nj * (tn // MXU_N), SUBLANES, MXU_N), F32)


def _fox_qk_proj(x2, g_pre, mod, w_main, gn, w_small, seq, tm=1024, tn=1024):
    t, d = x2.shape
    n = gn.shape[1]
    return pl.pallas_call(
        _fox_qk_kernel,
        grid=(t // tm, n // tn),
        in_specs=_x_mod_specs(tm, d, seq // tm) + [
            pl.BlockSpec((d, tn), lambda i, j: (0, j)),
            pl.BlockSpec((1, tn), lambda i, j: (0, j)),
            pl.BlockSpec((d, HEAD), lambda i, j: (0, 0))],
        out_specs=[pl.BlockSpec((tm, tn), lambda i, j: (i, j)),
                   pl.BlockSpec((tm, HEAD), lambda i, j: (i, 0)),
                   pl.BlockSpec((tm, d), lambda i, j: (i, 0))],
        out_shape=[jax.ShapeDtypeStruct((t, n), BF16), jax.ShapeDtypeStruct((t, HEAD), F32),
                   jax.ShapeDtypeStruct((t, d), BF16)],
        compiler_params=_cparams("parallel", "arbitrary"),
        name="fox_qk_proj",
    )(x2, g_pre, mod, mod, w_main, gn, w_small)


def _plain_proj(h, w_main, col0, n, tm=1024, tn=1024):
    t, d = h.shape
    off = col0 // tn
    return pl.pallas_call(
        _plain_proj_kernel,
        grid=(t // tm, n // tn),
        in_specs=[pl.BlockSpec((tm, d), lambda i, j: (i, 0)),
                  pl.BlockSpec((d, tn), lambda i, j: (0, off + j))],
        out_specs=pl.BlockSpec((tm, tn), lambda i, j: (i, j)),
        out_shape=jax.ShapeDtypeStruct((t, n), BF16),
        compiler_params=_cparams("parallel", "arbitrary"),
        name="plain_proj",
    )(h, w_main)


def _gdn_qk_proj(x2, g_pre, mod, w_main, conv_w, qscale, w_small, seq, tm=1024, tn=1024):
    t, d = x2.shape
    n = qscale.shape[1]
    taps = conv_w.shape[0]
    return pl.pallas_call(
        functools.partial(_gdn_qk_kernel, blocks_per_seq=seq // tm),
        grid=(t // tm, n // tn),
        in_specs=_x_mod_specs(tm, d, seq // tm) + [
            pl.BlockSpec((d, tn), lambda i, j: (0, j)),
            pl.BlockSpec((taps, tn), lambda i, j: (0, j)),
            pl.BlockSpec((1, tn), lambda i, j: (0, j)),
            pl.BlockSpec((d, HEAD), lambda i, j: (0, 0))],
        out_specs=[pl.BlockSpec((tm, tn), lambda i, j: (i, j)),
                   pl.BlockSpec((tm, HEAD), lambda i, j: (i, 0)),
                   pl.BlockSpec((tm, d), lambda i, j: (i, 0))],
        out_shape=[jax.ShapeDtypeStruct((t, n), BF16), jax.ShapeDtypeStruct((t, HEAD), F32),
                   jax.ShapeDtypeStruct((t, d), BF16)],
        scratch_shapes=[_carry_scratch(tn, n // tn)],
        compiler_params=_cparams("arbitrary", "arbitrary"),
        name="gdn_qk_proj",
    )(x2, g_pre, mod, mod, w_main, conv_w, qscale, w_small)


def _gdn_v_proj(h, w_main, conv_w, col0, n, seq, tm=1024, tn=1024):
    t, d = h.shape
    taps = conv_w.shape[0]
    off = col0 // tn
    return pl.pallas_call(
        functools.partial(_gdn_v_kernel, blocks_per_seq=seq // tm),
        grid=(t // tm, n // tn),
        in_specs=[pl.BlockSpec((tm, d), lambda i, j: (i, 0)),
                  pl.BlockSpec((d, tn), lambda i, j: (0, off + j)),
                  pl.BlockSpec((taps, tn), lambda i, j: (0, off + j))],
        out_specs=pl.BlockSpec((tm, tn), lambda i, j: (i, j)),
        out_shape=jax.ShapeDtypeStruct((t, n), BF16),
        scratch_shapes=[_carry_scratch(tn, n // tn)],
        compiler_params=_cparams("arbitrary", "arbitrary"),
        name="gdn_v_proj",
    )(h, w_main, conv_w)


def _ffn_up(x2, g_pre, mod, w_up, layer, conv_w, conv_b, seq, tm=1024, tn=512):
    t, d = x2.shape
    dff = w_up.shape[2] // 2
    nj = dff // tn
    taps = conv_w.shape[0]
    return pl.pallas_call(
        functools.partial(_ffn_up_kernel, blocks_per_seq=seq // tm),
        grid=(t // tm, nj),
        in_specs=_x_mod_specs(tm, d, seq // tm) + [
            pl.BlockSpec((None, d, tn), lambda i, j: (layer, 0, j)),
            pl.BlockSpec((None, d, tn), lambda i, j: (layer, 0, j + nj)),
            pl.BlockSpec((taps, tn), lambda i, j: (0, j)),
            pl.BlockSpec((1, tn), lambda i, j: (0, j))],
        out_specs=pl.BlockSpec((tm, tn), lambda i, j: (i, j)),
        out_shape=jax.ShapeDtypeStruct((t, dff), BF16),
        scratch_shapes=[pltpu.VMEM((tm, d), BF16), _carry_scratch(tn, nj)],
        compiler_params=_cparams("arbitrary", "arbitrary"),
        name="ffn_up",
    )(x2, g_pre, mod, mod, w_up, w_up, conv_w, conv_b)


def _out_proj_kernel(a_ref, w_ref, x_ref, g_ref, gate_ref, o_ref):
    y = _dot(a_ref[...], w_ref[...])
    ms = jnp.mean(y * y, axis=-1, keepdims=True)
    o_ref[...] = x_ref[...] + gate_ref[...] * (y * lax.rsqrt(ms + EPS) * g_ref[...])


def _out_proj(a, w, layer, x2, g_post, mod, seq, tm):
    t, kd = a.shape
    d = w.shape[2]
    bps = seq // tm
    return pl.pallas_call(
        _out_proj_kernel,
        grid=(t // tm,),
        in_specs=[pl.BlockSpec((tm, kd), lambda i: (i, 0)),
                  pl.BlockSpec((None, kd, d), lambda i: (layer, 0, 0), pipeline_mode=pl.Buffered(1)),
                  pl.BlockSpec((tm, d), lambda i: (i, 0)),
                  pl.BlockSpec((1, d), lambda i: (0, 0)),
                  pl.BlockSpec((None, 1, d), lambda i: (i // bps, 0, 2))],
        out_specs=pl.BlockSpec((tm, d), lambda i: (i, 0)),
        out_shape=jax.ShapeDtypeStruct((t, d), F32),
        compiler_params=_cparams("parallel"),
        name="out_proj",
    )(a, w, x2, g_post, mod)


def _fox_gate_kernel(f_ref, fb_ref, o_ref, carry_ref):
    tm = f_ref.shape[0]

    @pl.when(pl.program_id(1) == 0)
    def _():
        carry_ref[...] = jnp.zeros_like(carry_ref)

    x = f_ref[...] + fb_ref[...]
    logf = (jnp.minimum(x, 0.0) - jnp.log1p(jnp.exp(-jnp.abs(x)))) * LOG2E
    r = lax.broadcasted_iota(jnp.int32, (tm, tm), 0)
    c = lax.broadcasted_iota(jnp.int32, (tm, tm), 1)
    tri = (r >= c).astype(F32)
    cum = _dot_f32(tri, logf) + carry_ref[0:1, :]
    o_ref[...] = cum
    carry_ref[0:1, :] = cum[tm - 1:tm, :]


def _fox_gate_cumsum(f_logit, f_bias_row, nb, seq, tm=256):
    return pl.pallas_call(
        _fox_gate_kernel,
        grid=(nb, seq // tm),
        in_specs=[pl.BlockSpec((None, tm, HEAD), lambda b, i: (b, i, 0)),
                  pl.BlockSpec((1, HEAD), lambda b, i: (0, 0))],
        out_specs=pl.BlockSpec((None, tm, HEAD), lambda b, i: (b, i, 0)),
        out_shape=jax.ShapeDtypeStruct((nb, seq, HEAD), F32),
        scratch_shapes=[pltpu.VMEM((SUBLANES, HEAD), F32)],
        compiler_params=_cparams("parallel", "arbitrary"),
        name="fox_gate_cumsum",
    )(f_logit.reshape(nb, seq, HEAD), f_bias_row)


def _attn_kernel(q_ref, k_ref, v_ref, og_ref, fk_ref, o_ref,
                 s0_ref, s1_ref, m_ref, l_ref, acc_ref, *, tq):
    h = pl.program_id(1)
    qi = pl.program_id(2)
    m_ref[...] = jnp.full(m_ref.shape, -jnp.inf, F32)
    l_ref[...] = jnp.zeros(l_ref.shape, F32)
    acc_ref[...] = jnp.zeros(acc_ref.shape, F32)

    def scores(s_ref, ki):
        ks = pl.multiple_of(ki * tq, tq)
        s_ref[...] = _dot_nt(q_ref[...], k_ref[pl.ds(ks, tq), :]) - fk_ref[pl.ds(h, 1), pl.ds(ks, tq)]

    def update(s_ref, ki, masked):
        ks = pl.multiple_of(ki * tq, tq)
        s = s_ref[...]
        if masked:
            row = lax.broadcasted_iota(jnp.int32, (tq, tq), 0)
            col = lax.broadcasted_iota(jnp.int32, (tq, tq), 1)
            s = jnp.where(col <= row, s, -jnp.inf)
        m_old = m_ref[...]
        m_new = jnp.maximum(m_old, jnp.max(s, axis=-1, keepdims=True))
        p = jnp.exp2(s - m_new)
        alpha = jnp.exp2(m_old - m_new)
        l_ref[...] = alpha * l_ref[...] + jnp.sum(p, axis=-1, keepdims=True)
        m_ref[...] = m_new
        acc_ref[...] = alpha * acc_ref[...] + _dot(p.astype(BF16), v_ref[pl.ds(ks, tq), :])

    scores(s0_ref, 0)

    def pair(pi, carry):
        scores(s1_ref, 2 * pi + 1)
        update(s0_ref, 2 * pi, False)
        scores(s0_ref, 2 * pi + 2)
        update(s1_ref, 2 * pi + 1, False)
        return carry

    lax.fori_loop(0, qi // 2, pair, 0)

    @pl.when(qi % 2 == 1)
    def _():
        scores(s1_ref, qi)
        update(s0_ref, qi - 1, False)
        update(s1_ref, qi, True)

    @pl.when(qi % 2 == 0)
    def _():
        update(s0_ref, qi, True)

    o_ref[...] = (acc_ref[...] / l_ref[...] * jax.nn.sigmoid(og_ref[...].astype(F32))).astype(o_ref.dtype)


def _split3(x):
    hi = x.astype(BF16).astype(F32)
    mid = (x - hi).astype(BF16).astype(F32)
    return hi, mid, x - hi - mid


def _attn_fixed_shift_kernel(q_ref, k_ref, v_ref, og_ref, fc_ref, c_ref, o_ref,
                             ka_ref, va_ref, qa_ref, p0_ref, p1_ref, acc_ref, *, tq):
    h = pl.program_id(1)
    step = pl.program_id(2)
    seq = k_ref.shape[0]
    nq = seq // tq
    q_blocks = (step, nq - 1 - step)

    def block_rows(i):
        return pl.ds(pl.multiple_of(i * tq, tq), tq)

    def head_col(rows):
        lane = lax.broadcasted_iota(jnp.int32, (tq, HEAD), 1)
        return jnp.sum(jnp.where(lane == h, fc_ref[rows, :], 0.0), axis=-1, keepdims=True)

    def extra_cols(first, second):
        lane = lax.broadcasted_iota(jnp.int32, (tq, HEAD), 1)
        cols = jnp.zeros((tq, HEAD), F32)
        for n, term in enumerate(first + second):
            cols = jnp.where(lane == n, term, cols)
        return cols.astype(BF16)

    ones3 = (1.0, 1.0, 1.0)

    @pl.when(step == 0)
    def _():
        def build(i, carry):
            rows = block_rows(i)
            hi, mid, lo = _split3(-head_col(rows))
            ka_ref[rows, 0:HEAD] = k_ref[rows, :]
            ka_ref[rows, HEAD:2 * HEAD] = extra_cols((hi, mid, lo), ones3)
            va_ref[rows, 0:HEAD] = v_ref[rows, :]
            va_ref[rows, HEAD:2 * HEAD] = jnp.ones((tq, HEAD), BF16)
            return carry
        lax.fori_loop(0, nq, build, 0)

    for n, qb in enumerate(q_blocks):
        rows = block_rows(qb)
        qa_ref[n] = jnp.concatenate(
            [q_ref[rows, :], extra_cols(ones3, _split3(head_col(rows) - c_ref[0:1, 0:1]))], axis=1)
    acc_ref[...] = jnp.zeros(acc_ref.shape, F32)

    def probs(p_ref, which, kb, masked):
        s = _dot_nt(qa_ref[which], ka_ref[block_rows(kb), :])
        if masked:
            row = lax.broadcasted_iota(jnp.int32, (tq, tq), 0)
            col = lax.broadcasted_iota(jnp.int32, (tq, tq), 1)
            s = jnp.where(col <= row, s, -jnp.inf)
        p_ref[...] = jnp.exp2(s).astype(BF16)

    def accumulate(p_ref, which, kb):
        acc_ref[which] += _dot(p_ref[...], va_ref[block_rows(kb), :])

    items = [(0, q_blocks[0], True), (1, q_blocks[1], True)]
    for n in range(nq - 1):
        which = (n >= q_blocks[0]).astype(jnp.int32)
        items.append((which, n - which * q_blocks[0], False))
    p_refs = (p0_ref, p1_ref)
    for n, (which, kb, masked) in enumerate(items):
        probs(p_refs[n % 2], which, kb, masked)
        if n > 0:
            accumulate(p_refs[(n - 1) % 2], *items[n - 1][:2])
    accumulate(p_refs[(len(items) - 1) % 2], *items[-1][:2])

    for n, qb in enumerate(q_blocks):
        rows = block_rows(qb)
        acc = acc_ref[n]
        o_ref[rows, :] = (acc[:, 0:HEAD] / acc[:, HEAD:HEAD + 1]
                          * jax.nn.sigmoid(og_ref[rows, :].astype(F32))).astype(o_ref.dtype)


def _fox_attention_fixed_shift(qk3, v3, og3, fcum, c_row, n_heads, tq=512):
    nb, seq, _ = qk3.shape
    nq = seq // tq
    assert nq % 2 == 0
    full = lambda off: pl.BlockSpec((None, seq, HEAD), lambda b, h, i: (b, 0, off + h))
    return pl.pallas_call(
        functools.partial(_attn_fixed_shift_kernel, tq=tq),
        grid=(nb, n_heads, nq // 2),
        in_specs=[full(0), full(n_heads), full(0), full(0),
                  pl.BlockSpec((None, seq, HEAD), lambda b, h, i: (b, 0, 0)),
                  pl.BlockSpec((1, HEAD), lambda b, h, i: (0, 0))],
        out_specs=full(0),
        out_shape=jax.ShapeDtypeStruct((nb, seq, n_heads * HEAD), BF16),
        scratch_shapes=[pltpu.VMEM((seq, 2 * HEAD), BF16), pltpu.VMEM((seq, 2 * HEAD), BF16),
                        pltpu.VMEM((2, tq, 2 * HEAD), BF16),
                        pltpu.VMEM((tq, tq), BF16), pltpu.VMEM((tq, tq), BF16),
                        pltpu.VMEM((2, tq, 2 * HEAD), F32)],
        compiler_params=_cparams("parallel", "parallel", "arbitrary"),
        name="fox_attention_fixed_shift",
    )(qk3, qk3, v3, og3, fcum, c_row)


ITEM_STRIDE = 32


def _attn_block_list_kernel(items_ref, count_ref, q_ref, k_ref, v_ref, og_ref, fc_ref, c_ref, o_ref,
                            ka_ref, va_ref, qa_ref, p0_ref, p1_ref, acc_ref, *, tq):
    h = pl.program_id(1)
    bh = pl.program_id(0) * pl.num_programs(1) + h
    seq = k_ref.shape[0]
    nq = seq // tq

    def block_rows(i):
        return pl.ds(pl.multiple_of(i * tq, tq), tq)

    def extra_cols(terms):
        lane = lax.broadcasted_iota(jnp.int32, (tq, HEAD), 1)
        cols = jnp.zeros((tq, HEAD), F32)
        for n, term in enumerate(terms):
            cols = jnp.where(lane == n, term, cols)
        return cols.astype(BF16)

    ones3 = (1.0, 1.0, 1.0)

    def build(i, carry):
        rows = block_rows(i)
        lane = lax.broadcasted_iota(jnp.int32, (tq, HEAD), 1)
        f = jnp.sum(jnp.where(lane == h, fc_ref[rows, :], 0.0), axis=-1, keepdims=True)
        ka_ref[rows, 0:HEAD] = k_ref[rows, :]
        ka_ref[rows, HEAD:2 * HEAD] = extra_cols(_split3(-f) + ones3)
        va_ref[rows, 0:HEAD] = v_ref[rows, :]
        va_ref[rows, HEAD:2 * HEAD] = jnp.ones((tq, HEAD), BF16)
        qa_ref[rows, 0:HEAD] = q_ref[rows, :]
        qa_ref[rows, HEAD:2 * HEAD] = extra_cols(ones3 + _split3(f - c_ref[0:1, 0:1]))
        return carry

    lax.fori_loop(0, nq, build, 0)
    acc_ref[nq] = jnp.zeros(acc_ref.shape[1:], F32)

    def probs(p_ref, qb, kb, masked):
        qb = jnp.minimum(qb, nq - 1)
        s = _dot_nt(qa_ref[block_rows(qb), :], ka_ref[block_rows(kb), :])
        if masked:
            row = lax.broadcasted_iota(jnp.int32, (tq, tq), 0)
            col = lax.broadcasted_iota(jnp.int32, (tq, tq), 1)
            s = jnp.where(col <= row, s, -jnp.inf)
        p_ref[...] = jnp.exp2(s).astype(BF16)

    def product(p_ref, kb):
        return _dot(p_ref[...], va_ref[block_rows(kb), :])

    p_refs = (p0_ref, p1_ref)
    for i in range(nq):
        probs(p_refs[i % 2], i, i, True)
        if i > 0:
            acc_ref[i - 1] = product(p_refs[(i - 1) % 2], i - 1)
    acc_ref[nq - 1] = product(p_refs[(nq - 1) % 2], nq - 1)

    def item(n):
        code = items_ref[bh, n]
        return code // ITEM_STRIDE, code % ITEM_STRIDE

    def accumulate(p_ref, n):
        qb, kb = item(n)
        acc_ref[qb] += product(p_ref, kb)

    probs(p0_ref, *item(0), False)

    def pair(m, carry):
        probs(p1_ref, *item(2 * m + 1), False)
        accumulate(p0_ref, 2 * m)
        probs(p0_ref, *item(2 * m + 2), False)
        accumulate(p1_ref, 2 * m + 1)
        return carry

    lax.fori_loop(0, count_ref[bh] // 2, pair, 0)

    def finish(i, carry):
        rows = block_rows(i)
        acc = acc_ref[i]
        o_ref[rows, :] = (acc[:, 0:HEAD] / acc[:, HEAD:HEAD + 1]
                          * jax.nn.sigmoid(og_ref[rows, :].astype(F32))).astype(o_ref.dtype)
        return carry

    lax.fori_loop(0, nq, finish, 0)


def _fox_attention_block_list(qk3, v3, og3, fcum, c_row, items, counts, n_heads, tq=512):
    nb, seq, _ = qk3.shape
    nq = seq // tq
    full = lambda off: pl.BlockSpec((None, seq, HEAD), lambda b, h, *_: (b, 0, off + h))
    grid_spec = pltpu.PrefetchScalarGridSpec(
        num_scalar_prefetch=2,
        grid=(nb, n_heads),
        in_specs=[full(0), full(n_heads), full(0), full(0),
                  pl.BlockSpec((None, seq, HEAD), lambda b, h, *_: (b, 0, 0), pipeline_mode=pl.Buffered(1)),
                  pl.BlockSpec((1, HEAD), lambda b, h, *_: (0, 0))],
        out_specs=full(0),
        scratch_shapes=[pltpu.VMEM((seq, 2 * HEAD), BF16), pltpu.VMEM((seq, 2 * HEAD), BF16),
                        pltpu.VMEM((seq, 2 * HEAD), BF16),
                        pltpu.VMEM((tq, tq), BF16), pltpu.VMEM((tq, tq), BF16),
                        pltpu.VMEM((nq + 1, tq, 2 * HEAD), F32)])
    return pl.pallas_call(
        functools.partial(_attn_block_list_kernel, tq=tq),
        grid_spec=grid_spec,
        out_shape=jax.ShapeDtypeStruct((nb, seq, n_heads * HEAD), BF16),
        compiler_params=_cparams("parallel", "arbitrary"),
        name="fox_attention_block_list",
    )(items, counts, qk3, qk3, v3, og3, fcum, c_row)


MAX_SHIFT_GAP = 100.0
SKIP_LOG2 = 152.0
LIST_MAX_FRACTION = 0.75


def _needed_blocks(fcum, n_heads, tq):
    nb, seq, _ = fcum.shape
    nq = seq // tq
    f_first = fcum[:, 0::tq, :n_heads]
    f_last = fcum[:, tq - 1::tq, :n_heads]
    bound = f_first[:, :, None, :] - f_last[:, None, :, :]
    i = jnp.arange(nq, dtype=jnp.int32)[:, None]
    j = jnp.arange(nq, dtype=jnp.int32)[None, :]
    need = (bound > -SKIP_LOG2) & (j < i)[None, :, :, None]
    need = jnp.transpose(need, (0, 3, 1, 2)).reshape(nb * n_heads, nq * nq)
    pad_code = nq * ITEM_STRIDE
    codes = jnp.where(need, (i * ITEM_STRIDE + j).reshape(1, nq * nq), pad_code)
    codes = jnp.sort(codes, axis=-1)
    codes = jnp.concatenate([codes, jnp.full((nb * n_heads, 2), pad_code, jnp.int32)], axis=-1)
    counts = jnp.sum(need, axis=-1).astype(jnp.int32)
    return codes.astype(jnp.int32), counts + counts % 2, jnp.sum(counts)


def _fox_attention(qk3, v3, og3, fcum, gn, n_heads, tq=512):
    nb, seq, _ = qk3.shape
    nq = seq // tq
    qd = n_heads * HEAD
    c = (HEAD * 1.01) * jnp.max(jnp.abs(gn[:, :qd])) * jnp.max(jnp.abs(gn[:, qd:]))
    c_row = jnp.full((1, HEAD), c, F32)
    items, counts, total = _needed_blocks(fcum, n_heads, tq)
    all_pairs = nb * n_heads * (nq * (nq - 1) // 2)

    def block_list(qk3, v3, og3, fcum, c_row, items, counts):
        return _fox_attention_block_list(qk3, v3, og3, fcum, c_row, items, counts, n_heads, tq)

    def full_stream(qk3, v3, og3, fcum, c_row, items, counts):
        return _fox_attention_fixed_shift(qk3, v3, og3, fcum, c_row, n_heads, tq)

    def running_max(qk3, v3, og3, fcum, c_row, items, counts):
        fk = jnp.transpose(fcum[:, :, :n_heads], (0, 2, 1))
        return _fox_attention_running_max(qk3, v3, og3, fk, n_heads, tq)

    branch = jnp.where(2.0 * c > MAX_SHIFT_GAP, 2,
                       jnp.where(total <= LIST_MAX_FRACTION * all_pairs, 0, 1)).astype(jnp.int32)
    return lax.switch(branch, (block_list, full_stream, running_max), qk3, v3, og3, fcum, c_row, items, counts)


def _fox_attention_running_max(qk3, v3, og3, fk, n_heads, tq=512):
    nb, seq, _ = qk3.shape
    return pl.pallas_call(
        functools.partial(_attn_kernel, tq=tq),
        grid=(nb, n_heads, seq // tq),
        in_specs=[pl.BlockSpec((None, tq, HEAD), lambda b, h, i: (b, i, h)),
                  pl.BlockSpec((None, seq, HEAD), lambda b, h, i: (b, 0, n_heads + h)),
                  pl.BlockSpec((None, seq, HEAD), lambda b, h, i: (b, 0, h)),
                  pl.BlockSpec((None, tq, HEAD), lambda b, h, i: (b, i, h)),
                  pl.BlockSpec((None, n_heads, seq), lambda b, h, i: (b, 0, 0))],
        out_specs=pl.BlockSpec((None, tq, HEAD), lambda b, h, i: (b, i, h)),
        out_shape=jax.ShapeDtypeStruct((nb, seq, n_heads * HEAD), BF16),
        scratch_shapes=[pltpu.VMEM((tq, tq), F32), pltpu.VMEM((tq, tq), F32),
                        pltpu.VMEM((tq, 1), F32), pltpu.VMEM((tq, 1), F32), pltpu.VMEM((tq, HEAD), F32)],
        compiler_params=_cparams("parallel", "parallel", "arbitrary"),
        name="fox_attention",
    )(qk3, qk3, v3, og3, fk)


def _gdn_gate_kernel(ab_ref, alog_ref, dt_ref, o_ref, *, n_heads):
    tm = ab_ref.shape[0]
    x = ab_ref[...]
    lane = lax.broadcasted_iota(jnp.int32, x.shape, 1)
    g = -jnp.exp(alog_ref[...]) * _softplus(x + dt_ref[...])
    r = lax.broadcasted_iota(jnp.int32, (tm, tm), 0)
    c = lax.broadcasted_iota(jnp.int32, (tm, tm), 1)
    tri = ((r >= c) & ((r // CHUNK) == (c // CHUNK))).astype(F32)
    gc = _dot_f32(tri, jnp.where(lane < n_heads, g, 0.0))
    o_ref[...] = jnp.where(lane < n_heads, gc, jax.nn.sigmoid(x))


def _gdn_gates(ab, alog_row, dt_row, n_heads, tm=256):
    t = ab.shape[0]
    return pl.pallas_call(
        functools.partial(_gdn_gate_kernel, n_heads=n_heads),
        grid=(t // tm,),
        in_specs=[pl.BlockSpec((tm, HEAD), lambda i: (i, 0)),
                  pl.BlockSpec((1, HEAD), lambda i: (0, 0)),
                  pl.BlockSpec((1, HEAD), lambda i: (0, 0))],
        out_specs=pl.BlockSpec((tm, HEAD), lambda i: (i, 0)),
        out_shape=jax.ShapeDtypeStruct((t, HEAD), F32),
        compiler_params=_cparams("parallel"),
        name="gdn_gates",
    )(ab, alog_row, dt_row)


def _block_diag2(p, half):
    return jnp.concatenate([jnp.where(half, 0.0, p), jnp.where(half, p, 0.0)], axis=0)


def _gdn_chunk_kernel(q_ref, k_ref, v_ref, gb_ref, grow_ref, u_ref, w_ref, qd_ref, kd_ref, qk_ref, egl_ref,
                      *, n_vheads, chunks):
    hk = pl.program_id(1)
    lane = lax.broadcasted_iota(jnp.int32, (CHUNK, HEAD), 1)
    row = lax.broadcasted_iota(jnp.int32, (CHUNK, HEAD), 0)
    col = lane & (CHUNK - 1)
    half = lane >= CHUNK
    lower = row >= col
    strict = row > col
    eye = (row == col).astype(F32)
    level_masks = []
    for lvl in range(6):
        level_masks.append((((row >> lvl) & 1) == 1) & (((col >> lvl) & 1) == 0)
                           & ((row >> (lvl + 1)) == (col >> (lvl + 1))))
    zeros = jnp.zeros((CHUNK, HEAD), F32)

    def pick(gb, idx):
        return jnp.sum(jnp.where(lane == idx, gb, 0.0), axis=-1, keepdims=True)

    sls = [slice(c * CHUNK, (c + 1) * CHUNK) for c in range(chunks)]
    gcs, bes, a_s, xs = [], [], [], []
    for c, sl in enumerate(sls):
        q = q_ref[sl, :]
        k = k_ref[sl, :]
        gb = gb_ref[sl, :]
        gc0, gc1 = pick(gb, 2 * hk), pick(gb, 2 * hk + 1)
        be0, be1 = pick(gb, n_vheads + 2 * hk), pick(gb, n_vheads + 2 * hk + 1)
        gcol = jnp.where(half, gc1, gc0)
        bcol = jnp.where(half, be1, be0)
        decay = jnp.exp(jnp.where(lower, gcol - grow_ref[c:c + 1, :], -jnp.inf))
        qkk = _dot_nt(jnp.concatenate([q, k], axis=0), jnp.concatenate([k, k], axis=0))
        a = jnp.where(strict, qkk[CHUNK:, :] * decay * bcol, 0.0)
        qk_ref[sl, :] = (qkk[:CHUNK, :] * decay).astype(qk_ref.dtype)
        gcs.append((gc0, gc1))
        bes.append((be0, be1))
        a_s.append(a)
        xs.append(eye - jnp.where(level_masks[0], a, 0.0))

    for lvl in range(1, 6):
        ys = [_dot(xs[c].astype(BF16), _block_diag2(jnp.where(level_masks[lvl], a_s[c], 0.0), half).astype(BF16))
              for c in range(chunks)]
        xs = [xs[c] - _dot(ys[c].astype(BF16), _block_diag2(xs[c], half).astype(BF16)) for c in range(chunks)]

    for c, sl in enumerate(sls):
        (gc0, gc1), (be0, be1) = gcs[c], bes[c]
        kf = k_ref[sl, :].astype(F32)
        qf = q_ref[sl, :].astype(F32)
        v = v_ref[sl, :].astype(F32)
        eg0, eg1 = jnp.exp(gc0), jnp.exp(gc1)
        rhs = jnp.concatenate([
            jnp.concatenate([v[:, :HEAD] * be0, kf * (be0 * eg0), zeros, zeros], axis=1),
            jnp.concatenate([zeros, zeros, v[:, HEAD:] * be1, kf * (be1 * eg1)], axis=1)], axis=0)
        sol = _dot(xs[c].astype(BF16), rhs.astype(BF16))
        u_ref[sl, :] = jnp.concatenate([sol[:, 0:HEAD], sol[:, 2 * HEAD:3 * HEAD]], axis=1).astype(u_ref.dtype)
        w_ref[sl, :] = jnp.concatenate([sol[:, HEAD:2 * HEAD], sol[:, 3 * HEAD:]], axis=1).astype(w_ref.dtype)
        qd_ref[sl, :] = jnp.concatenate([qf * eg0, qf * eg1], axis=1).astype(qd_ref.dtype)
        gl0, gl1 = gc0[CHUNK - 1:CHUNK, :], gc1[CHUNK - 1:CHUNK, :]
        kd_ref[sl, :] = jnp.concatenate([kf * jnp.exp(gl0 - gc0), kf * jnp.exp(gl1 - gc1)],
                                        axis=1).astype(kd_ref.dtype)
        egl_ref[0, c:c + 1, :] = jnp.broadcast_to(jnp.exp(gl0), (1, HEAD))
        egl_ref[1, c:c + 1, :] = jnp.broadcast_to(jnp.exp(gl1), (1, HEAD))


def _gdn_chunks(qk3, v3, gb3, grow, n_kheads, n_vheads, tm=1024):
    nb, seq, _ = qk3.shape
    chunks = tm // CHUNK
    vd = n_vheads * HEAD
    big = jax.ShapeDtypeStruct((nb, seq, vd), BF16)
    pair_spec = pl.BlockSpec((None, tm, 2 * HEAD), lambda b, h, i: (b, i, h))
    return pl.pallas_call(
        functools.partial(_gdn_chunk_kernel, n_vheads=n_vheads, chunks=chunks),
        grid=(nb, n_kheads, seq // tm),
        in_specs=[pl.BlockSpec((None, tm, HEAD), lambda b, h, i: (b, i, h)),
                  pl.BlockSpec((None, tm, HEAD), lambda b, h, i: (b, i, n_kheads + h)),
                  pair_spec,
                  pl.BlockSpec((None, tm, HEAD), lambda b, h, i: (b, i, 0)),
                  pl.BlockSpec((None, None, chunks, HEAD), lambda b, h, i: (b, h, i, 0))],
        out_specs=[pair_spec, pair_spec, pair_spec, pair_spec,
                   pl.BlockSpec((None, tm, HEAD), lambda b, h, i: (b, i, h)),
                   pl.BlockSpec((None, 2, chunks, HEAD), lambda b, h, i: (b, h, i, 0))],
        out_shape=[big, big, big, big,
                   jax.ShapeDtypeStruct((nb, seq, n_kheads * HEAD), BF16),
                   jax.ShapeDtypeStruct((nb, n_vheads, seq // CHUNK, HEAD), F32)],
        compiler_params=_cparams("parallel", "parallel", "parallel"),
        name="gdn_chunks",
    )(qk3, qk3, v3, gb3, grow)


def _gdn_scan_kernel(u_ref, w_ref, qd_ref, kd_ref, qk_ref, egl_ref, z_ref, gn_ref, o_ref, s_ref,
                     *, heads, chunks):
    @pl.when(pl.program_id(2) == 0)
    def _():
        s_ref[...] = jnp.zeros_like(s_ref)

    zeros = jnp.zeros((CHUNK, HEAD), F32)
    gn = gn_ref[...]
    cols = [slice(g * HEAD, (g + 1) * HEAD) for g in range(heads)]
    for c in range(chunks):
        sl = slice(c * CHUNK, (c + 1) * CHUNK)
        rs = [_dot(jnp.concatenate([w_ref[sl, cols[g]], qd_ref[sl, cols[g]]], axis=0), s_ref[g].astype(BF16))
              for g in range(heads)]
        vnew = [u_ref[sl, cols[g]].astype(F32) - rs[g][:CHUNK, :] for g in range(heads)]
        intra = []
        for pr in range(heads // 2):
            v2 = jnp.concatenate([jnp.concatenate([vnew[2 * pr], zeros], axis=1),
                                  jnp.concatenate([zeros, vnew[2 * pr + 1]], axis=1)], axis=0).astype(BF16)
            intra.append(_dot(qk_ref[sl, pr * HEAD:(pr + 1) * HEAD], v2))
        for g in range(heads):
            s_ref[g] = s_ref[g] * egl_ref[g, c:c + 1, :] + _dot_tn(kd_ref[sl, cols[g]], vnew[g].astype(BF16))
        for g in range(heads):
            o = rs[g][CHUNK:, :] + intra[g // 2][:, (g % 2) * HEAD:(g % 2 + 1) * HEAD]
            ms = jnp.mean(o * o, axis=-1, keepdims=True)
            z = z_ref[sl, cols[g]].astype(F32)
            o_ref[sl, cols[g]] = (o * lax.rsqrt(ms + EPS) * gn * (z * jax.nn.sigmoid(z))).astype(o_ref.dtype)


def _gdn_scan(u, w, qd, kd, qkm, egl, z3, gn_row, n_vheads, heads=16, tm=512):
    nb, seq, vd = u.shape
    chunks = tm // CHUNK
    wide = pl.BlockSpec((None, tm, heads * HEAD), lambda b, g, i: (b, i, g))
    return pl.pallas_call(
        functools.partial(_gdn_scan_kernel, heads=heads, chunks=chunks),
        grid=(nb, n_vheads // heads, seq // tm),
        in_specs=[wide, wide, wide, wide,
                  pl.BlockSpec((None, tm, heads // 2 * HEAD), lambda b, g, i: (b, i, g)),
                  pl.BlockSpec((None, heads, chunks, HEAD), lambda b, g, i: (b, g, i, 0)),
                  wide,
                  pl.BlockSpec((1, HEAD), lambda b, g, i: (0, 0))],
        out_specs=wide,
        out_shape=jax.ShapeDtypeStruct((nb, seq, vd), BF16),
        scratch_shapes=[pltpu.VMEM((heads, HEAD, HEAD), F32)],
        compiler_params=_cparams("parallel", "parallel", "arbitrary"),
        name="gdn_scan",
    )(u, w, qd, kd, qkm, egl, z3, gn_row)


def _pad_cols(w, n):
    return jnp.pad(w, ((0, 0), (0, n - w.shape[1])))


def _row(v, n=HEAD):
    return jnp.pad(v.astype(F32), (0, n - v.shape[0])).reshape(1, n)


def _fox_layer(x2, nb, seq, g_pre, g_post, mod, w_in, f_bias, q_norm, k_norm, w_o, layer):
    d = x2.shape[1]
    nh = d // HEAD
    qd = nh * HEAD
    w_small = _pad_cols(w_in[:, 3 * qd:3 * qd + nh], HEAD)
    w_gate = w_in[:, 3 * qd + nh:]
    gn = jnp.concatenate([jnp.tile(q_norm.astype(F32) * (HEAD ** -0.5 * LOG2E), nh),
                          jnp.tile(k_norm.astype(F32), nh)]).reshape(1, 2 * qd)
    qk, f_logit, h = _fox_qk_proj(x2, g_pre, mod, w_in, gn, w_small, seq)
    v = _plain_proj(h, w_in, 2 * qd, qd)
    og = _plain_proj(h, w_gate, 0, qd)
    fcum = _fox_gate_cumsum(f_logit, _row(f_bias), nb, seq)
    attn = _fox_attention(qk.reshape(nb, seq, 2 * qd), v.reshape(nb, seq, qd), og.reshape(nb, seq, qd),
                          fcum, gn, nh)
    return _out_proj(attn.reshape(nb * seq, qd), w_o, layer, x2, g_post, mod, seq, tm=512)


def _gdn_layer(x2, nb, seq, g_pre, g_post, mod, w_in, conv_w, a_log, dt_bias, out_norm, w_o, layer):
    d = x2.shape[1]
    nk = d // HEAD
    nv = 2 * nk
    kd, vd = nk * HEAD, nv * HEAD
    n_main = 2 * kd + 2 * vd
    w_small = _pad_cols(w_in[:, n_main:], HEAD)
    conv_w = conv_w.astype(F32)
    qscale = jnp.concatenate([jnp.full((kd,), HEAD ** -0.5, F32), jnp.ones((kd,), F32)]).reshape(1, 2 * kd)
    qk, ab, h = _gdn_qk_proj(x2, g_pre, mod, w_in, conv_w, qscale, w_small, seq)
    v = _gdn_v_proj(h, w_in, conv_w, 2 * kd, vd, seq)
    z = _plain_proj(h, w_in, 2 * kd + vd, vd)
    gb = _gdn_gates(ab, _row(a_log), _row(dt_bias), nv)
    n_chunks = seq // CHUNK
    grow = gb[:, :nv].reshape(nb, n_chunks, CHUNK, nk, 2)
    grow = jnp.transpose(grow, (0, 3, 1, 4, 2)).reshape(nb, nk, n_chunks, 2 * CHUNK)
    u, w, qdec, kdec, qkm, egl = _gdn_chunks(qk.reshape(nb, seq, 2 * kd), v.reshape(nb, seq, vd),
                                             gb.reshape(nb, seq, HEAD), grow, nk, nv)
    a = _gdn_scan(u, w, qdec, kdec, qkm, egl, z.reshape(nb, seq, vd), _row(out_norm), nv)
    return _out_proj(a.reshape(nb * seq, vd), w_o, layer, x2, g_post, mod, seq, tm=512)


def _ffn_layer(x2, seq, g_pre, g_post, mod, w_up, conv_w, conv_b, w_down, layer):
    u = _ffn_up(x2, g_pre, mod, w_up, layer, conv_w.astype(F32), conv_b.reshape(1, -1).astype(F32), seq)
    return _out_proj(u, w_down, layer, x2, g_post, mod, seq, tm=256)


def kernel(x, c, ada_w, ada_b, norm_g, fox_w_in, fox_f_bias, fox_q_norm, fox_k_norm, fox_w_o,
           gdn_w_in, gdn_conv_w, gdn_a_log, gdn_dt_bias, gdn_out_norm, gdn_w_o,
           ffn_w_up, ffn_conv_w, ffn_conv_b, ffn_w_down):
    nb, seq, d = x.shape
    depth = ada_w.shape[0]
    mods = _ada_mods(c, ada_w, ada_b)
    fox_w_in, fox_w_o, gdn_w_in, gdn_w_o, ffn_w_up, ffn_w_down = (
        w.astype(BF16) for w in (fox_w_in, fox_w_o, gdn_w_in, gdn_w_o, ffn_w_up, ffn_w_down))
    x2 = x.reshape(nb * seq, d)
    for i in range(depth):
        g = [norm_g[i, r].reshape(1, d).astype(F32) for r in range(4)]
        j = i // 2
        if i % 2 == 0:
            x2 = _fox_layer(x2, nb, seq, g[0], g[1], mods[2 * i], fox_w_in[j], fox_f_bias[j],
                            fox_q_norm[j], fox_k_norm[j], fox_w_o, j)
        else:
            x2 = _gdn_layer(x2, nb, seq, g[0], g[1], mods[2 * i], gdn_w_in[j], gdn_conv_w[j],
                            gdn_a_log[j], gdn_dt_bias[j], gdn_out_norm[j], gdn_w_o, j)
        x2 = _ffn_layer(x2, seq, g[2], g[3], mods[2 * i + 1], ffn_w_up, ffn_conv_w[i],
                        ffn_conv_b[i], ffn_w_down, i)
    return x2.reshape(nb, seq, d)
```

```python
import functools

import jax
import jax.numpy as jnp
from jax import lax
from jax.experimental import pallas as pl
from jax.experimental.pallas import tpu as pltpu

F32 = jnp.float32
BF16 = jnp.bfloat16
EPS = 1e-6
HEAD = 128
CHUNK = 64
SUBLANES = 8
MXU_N = 256
VMEM_LIMIT = 56 * 1024 * 1024
INV_SQRT2 = 0.7071067811865476
LOG2E = 1.4426950408889634


def _cparams(*sem):
    return pltpu.CompilerParams(dimension_semantics=sem, vmem_limit_bytes=VMEM_LIMIT)


def _dot(a, b):
    return jnp.dot(a, b, preferred_element_type=F32)


def _dot_nt(a, b):
    return lax.dot_general(a, b, (((1,), (1,)), ((), ())), preferred_element_type=F32)


def _dot_tn(a, b):
    return lax.dot_general(a, b, (((0,), (0,)), ((), ())), preferred_element_type=F32)


def _dot_f32(a, b):
    return jnp.dot(a, b, preferred_element_type=F32, precision=lax.Precision.HIGHEST)


def _softplus(x):
    return jnp.maximum(x, 0.0) + jnp.log1p(jnp.exp(-jnp.abs(x)))


def _ada_kernel(c_ref, w_ref, b_ref, o_ref):
    c = c_ref[...]
    o_ref[0] = _dot_f32(c * jax.nn.sigmoid(c), w_ref[0]) + b_ref[0]


def _ada_mods(c, ada_w, ada_b):
    nb, d = c.shape
    n = ada_w.shape[0] * ada_w.shape[1]
    w = ada_w.reshape(n, d, 3 * d)
    b = ada_b.reshape(n, 1, 3 * d)
    cp = jnp.zeros((SUBLANES, d), F32).at[:nb].set(c)
    tn = 1024
    out = pl.pallas_call(
        _ada_kernel,
        grid=(n, 3 * d // tn),
        in_specs=[pl.BlockSpec((SUBLANES, d), lambda s, j: (0, 0)),
                  pl.BlockSpec((1, d, tn), lambda s, j: (s, 0, j)),
                  pl.BlockSpec((1, 1, tn), lambda s, j: (s, 0, j))],
        out_specs=pl.BlockSpec((1, SUBLANES, tn), lambda s, j: (s, 0, j)),
        out_shape=jax.ShapeDtypeStruct((n, SUBLANES, 3 * d), F32),
        compiler_params=_cparams("parallel", "parallel"),
        name="ada_mods",
    )(cp, w, b)
    return out[:, :nb].reshape(n, nb, 1, 3 * d)


def _norm_mod(h_ref, x_ref, g_ref, sh_ref, sc_ref):
    x = x_ref[...]
    ms = jnp.mean(x * x, axis=-1, keepdims=True)
    y = x * lax.rsqrt(ms + EPS) * g_ref[...]
    h_ref[...] = (y * (1.0 + sc_ref[...]) + sh_ref[...]).astype(BF16)


def _sub_dots(h_ref, w_ref):
    return [_dot(h_ref[...], w_ref[:, c * MXU_N:(c + 1) * MXU_N]) for c in range(w_ref.shape[1] // MXU_N)]


def _head_norm_store(o_ref, c, y, gain_ref, mean):
    for g in range(MXU_N // HEAD):
        sl = slice(c * MXU_N + g * HEAD, c * MXU_N + (g + 1) * HEAD)
        t = y[:, g * HEAD:(g + 1) * HEAD]
        ss = jnp.sum(t * t, axis=-1, keepdims=True)
        if mean:
            ss = ss * (1.0 / HEAD)
        o_ref[:, sl] = (t * lax.rsqrt(ss + EPS) * gain_ref[:, sl]).astype(o_ref.dtype)


def _causal_conv(acc, cw, carry_ref, slot, first):
    tm = acc.shape[0]
    taps = cw.shape[0]
    prev = jnp.where(first, 0.0, carry_ref[slot])
    carry_ref[slot] = acc[tm - SUBLANES:tm, :]
    xe = jnp.concatenate([prev, acc], axis=0)
    ye = xe * cw[0:1, :]
    for k in range(1, taps):
        ye = xe * cw[k:k + 1, :] + pltpu.roll(ye, 1, axis=0)
    return ye[SUBLANES:, :]


def _zero_carry_at_start(carry_ref):
    @pl.when((pl.program_id(0) == 0) & (pl.program_id(1) == 0))
    def _():
        carry_ref[...] = jnp.zeros_like(carry_ref)


def _fox_qk_kernel(x_ref, g_ref, sh_ref, sc_ref, w_ref, gn_ref, ws_ref, o_ref, os_ref, h_ref):
    @pl.when(pl.program_id(1) == 0)
    def _():
        _norm_mod(h_ref, x_ref, g_ref, sh_ref, sc_ref)
        os_ref[...] = _dot(h_ref[...], ws_ref[...])

    for c, acc in enumerate(_sub_dots(h_ref, w_ref)):
        _head_norm_store(o_ref, c, acc, gn_ref, mean=True)


def _plain_proj_kernel(h_ref, w_ref, o_ref):
    o_ref[...] = _dot(h_ref[...], w_ref[...]).astype(o_ref.dtype)


def _gdn_qk_kernel(x_ref, g_ref, sh_ref, sc_ref, w_ref, cw_ref, qs_ref, ws_ref, o_ref, os_ref, h_ref,
                   carry_ref, *, blocks_per_seq):
    i = pl.program_id(0)
    j = pl.program_id(1)
    _zero_carry_at_start(carry_ref)

    @pl.when(j == 0)
    def _():
        _norm_mod(h_ref, x_ref, g_ref, sh_ref, sc_ref)
        os_ref[...] = _dot(h_ref[...], ws_ref[...])

    first = (i % blocks_per_seq) == 0
    accs = _sub_dots(h_ref, w_ref)
    for c, acc in enumerate(accs):
        y = _causal_conv(acc, cw_ref[:, c * MXU_N:(c + 1) * MXU_N], carry_ref, j * len(accs) + c, first)
        _head_norm_store(o_ref, c, y * jax.nn.sigmoid(y), qs_ref, mean=False)


def _gdn_v_kernel(h_ref, w_ref, cw_ref, o_ref, carry_ref, *, blocks_per_seq):
    i = pl.program_id(0)
    j = pl.program_id(1)
    _zero_carry_at_start(carry_ref)
    first = (i % blocks_per_seq) == 0
    accs = _sub_dots(h_ref, w_ref)
    for c, acc in enumerate(accs):
        cs = slice(c * MXU_N, (c + 1) * MXU_N)
        y = _causal_conv(acc, cw_ref[:, cs], carry_ref, j * len(accs) + c, first)
        o_ref[:, cs] = (y * jax.nn.sigmoid(y)).astype(o_ref.dtype)


def _ffn_up_kernel(x_ref, g_ref, sh_ref, sc_ref, wg_ref, wv_ref, cw_ref, cb_ref, o_ref,
                   h_ref, carry_ref, *, blocks_per_seq):
    i = pl.program_id(0)
    j = pl.program_id(1)
    _zero_carry_at_start(carry_ref)

    @pl.when(j == 0)
    def _():
        _norm_mod(h_ref, x_ref, g_ref, sh_ref, sc_ref)

    first = (i % blocks_per_seq) == 0
    n_sub = wg_ref.shape[1] // MXU_N
    pairs = []
    for c in range(n_sub):
        cs = slice(c * MXU_N, (c + 1) * MXU_N)
        pairs.append((_dot(h_ref[...], wg_ref[:, cs]), _dot(h_ref[...], wv_ref[:, cs])))
    for c, (gate, val) in enumerate(pairs):
        cs = slice(c * MXU_N, (c + 1) * MXU_N)
        gate = _causal_conv(gate, cw_ref[:, cs], carry_ref, j * n_sub + c, first) + cb_ref[:, cs]
        act = 0.5 * gate * (1.0 + lax.erf(gate * INV_SQRT2))
        o_ref[:, cs] = (act * val).astype(o_ref.dtype)


def _x_mod_specs(tm, d, bps):
    return [pl.BlockSpec((tm, d), lambda i, j: (i, 0)),
            pl.BlockSpec((1, d), lambda i, j: (0, 0)),
            pl.BlockSpec((None, 1, d), lambda i, j: (i // bps, 0, 0)),
            pl.BlockSpec((None, 1, d), lambda i, j: (i // bps, 0, 1))]


def _carry_scratch(tn, nj):
    return pltpu.VMEM((nj * (tn // MXU_N), SUBLANES, MXU_N), F32)


def _fox_qk_proj(x2, g_pre, mod, w_main, gn, w_small, seq, tm=1024, tn=1024):
    t, d = x2.shape
    n = gn.shape[1]
    return pl.pallas_call(
        _fox_qk_kernel,
        grid=(t // tm, n // tn),
        in_specs=_x_mod_specs(tm, d, seq // tm) + [
            pl.BlockSpec((d, tn), lambda i, j: (0, j)),
            pl.BlockSpec((1, tn), lambda i, j: (0, j)),
            pl.BlockSpec((d, HEAD), lambda i, j: (0, 0))],
        out_specs=[pl.BlockSpec((tm, tn), lambda i, j: (i, j)),
                   pl.BlockSpec((tm, HEAD), lambda i, j: (i, 0)),
                   pl.BlockSpec((tm, d), lambda i, j: (i, 0))],
        out_shape=[jax.ShapeDtypeStruct((t, n), BF16), jax.ShapeDtypeStruct((t, HEAD), F32),
                   jax.ShapeDtypeStruct((t, d), BF16)],
        compiler_params=_cparams("parallel", "arbitrary"),
        name="fox_qk_proj",
    )(x2, g_pre, mod, mod, w_main, gn, w_small)


def _plain_proj(h, w_main, col0, n, tm=1024, tn=1024):
    t, d = h.shape
    off = col0 // tn
    return pl.pallas_call(
        _plain_proj_kernel,
        grid=(t // tm, n // tn),
        in_specs=[pl.BlockSpec((tm, d), lambda i, j: (i, 0)),
                  pl.BlockSpec((d, tn), lambda i, j: (0, off + j))],
        out_specs=pl.BlockSpec((tm, tn), lambda i, j: (i, j)),
        out_shape=jax.ShapeDtypeStruct((t, n), BF16),
        compiler_params=_cparams("parallel", "arbitrary"),
        name="plain_proj",
    )(h, w_main)


def _gdn_qk_proj(x2, g_pre, mod, w_main, conv_w, qscale, w_small, seq, tm=1024, tn=1024):
    t, d = x2.shape
    n = qscale.shape[1]
    taps = conv_w.shape[0]
    return pl.pallas_call(
        functools.partial(_gdn_qk_kernel, blocks_per_seq=seq // tm),
        grid=(t // tm, n // tn),
        in_specs=_x_mod_specs(tm, d, seq // tm) + [
            pl.BlockSpec((d, tn), lambda i, j: (0, j)),
            pl.BlockSpec((taps, tn), lambda i, j: (0, j)),
            pl.BlockSpec((1, tn), lambda i, j: (0, j)),
            pl.BlockSpec((d, HEAD), lambda i, j: (0, 0))],
        out_specs=[pl.BlockSpec((tm, tn), lambda i, j: (i, j)),
                   pl.BlockSpec((tm, HEAD), lambda i, j: (i, 0)),
                   pl.BlockSpec((tm, d), lambda i, j: (i, 0))],
        out_shape=[jax.ShapeDtypeStruct((t, n), BF16), jax.ShapeDtypeStruct((t, HEAD), F32),
                   jax.ShapeDtypeStruct((t, d), BF16)],
        scratch_shapes=[_carry_scratch(tn, n // tn)],
        compiler_params=_cparams("arbitrary", "arbitrary"),
        name="gdn_qk_proj",
    )(x2, g_pre, mod, mod, w_main, conv_w, qscale, w_small)


def _gdn_v_proj(h, w_main, conv_w, col0, n, seq, tm=1024, tn=1024):
    t, d = h.shape
    taps = conv_w.shape[0]
    off = col0 // tn
    return pl.pallas_call(
        functools.partial(_gdn_v_kernel, blocks_per_seq=seq // tm),
        grid=(t // tm, n // tn),
        in_specs=[pl.BlockSpec((tm, d), lambda i, j: (i, 0)),
                  pl.BlockSpec((d, tn), lambda i, j: (0, off + j)),
                  pl.BlockSpec((taps, tn), lambda i, j: (0, off + j))],
        out_specs=pl.BlockSpec((tm, tn), lambda i, j: (i, j)),
        out_shape=jax.ShapeDtypeStruct((t, n), BF16),
        scratch_shapes=[_carry_scratch(tn, n // tn)],
        compiler_params=_cparams("arbitrary", "arbitrary"),
        name="gdn_v_proj",
    )(h, w_main, conv_w)


def _ffn_up(x2, g_pre, mod, w_up, layer, conv_w, conv_b, seq, tm=1024, tn=512):
    t, d = x2.shape
    dff = w_up.shape[2] // 2
    nj = dff // tn
    taps = conv_w.shape[0]
    return pl.pallas_call(
        functools.partial(_ffn_up_kernel, blocks_per_seq=seq // tm),
        grid=(t // tm, nj),
        in_specs=_x_mod_specs(tm, d, seq // tm) + [
            pl.BlockSpec((None, d, tn), lambda i, j: (layer, 0, j)),
            pl.BlockSpec((None, d, tn), lambda i, j: (layer, 0, j + nj)),
            pl.BlockSpec((taps, tn), lambda i, j: (0, j)),
            pl.BlockSpec((1, tn), lambda i, j: (0, j))],
        out_specs=pl.BlockSpec((tm, tn), lambda i, j: (i, j)),
        out_shape=jax.ShapeDtypeStruct((t, dff), BF16),
        scratch_shapes=[pltpu.VMEM((tm, d), BF16), _carry_scratch(tn, nj)],
        compiler_params=_cparams("arbitrary", "arbitrary"),
        name="ffn_up",
    )(x2, g_pre, mod, mod, w_up, w_up, conv_w, conv_b)


def _out_proj_kernel(a_ref, w_ref, x_ref, g_ref, gate_ref, o_ref):
    y = _dot(a_ref[...], w_ref[...])
    ms = jnp.mean(y * y, axis=-1, keepdims=True)
    o_ref[...] = x_ref[...] + gate_ref[...] * (y * lax.rsqrt(ms + EPS) * g_ref[...])


def _out_proj(a, w, layer, x2, g_post, mod, seq, tm):
    t, kd = a.shape
    d = w.shape[2]
    bps = seq // tm
    return pl.pallas_call(
        _out_proj_kernel,
        grid=(t // tm,),
        in_specs=[pl.BlockSpec((tm, kd), lambda i: (i, 0)),
                  pl.BlockSpec((None, kd, d), lambda i: (layer, 0, 0), pipeline_mode=pl.Buffered(1)),
                  pl.BlockSpec((tm, d), lambda i: (i, 0)),
                  pl.BlockSpec((1, d), lambda i: (0, 0)),
                  pl.BlockSpec((None, 1, d), lambda i: (i // bps, 0, 2))],
        out_specs=pl.BlockSpec((tm, d), lambda i: (i, 0)),
        out_shape=jax.ShapeDtypeStruct((t, d), F32),
        compiler_params=_cparams("parallel"),
        name="out_proj",
    )(a, w, x2, g_post, mod)


def _fox_gate_kernel(f_ref, fb_ref, o_ref, carry_ref):
    tm = f_ref.shape[0]

    @pl.when(pl.program_id(1) == 0)
    def _():
        carry_ref[...] = jnp.zeros_like(carry_ref)

    x = f_ref[...] + fb_ref[...]
    logf = (jnp.minimum(x, 0.0) - jnp.log1p(jnp.exp(-jnp.abs(x)))) * LOG2E
    r = lax.broadcasted_iota(jnp.int32, (tm, tm), 0)
    c = lax.broadcasted_iota(jnp.int32, (tm, tm), 1)
    tri = (r >= c).astype(F32)
    cum = _dot_f32(tri, logf) + carry_ref[0:1, :]
    o_ref[...] = cum
    carry_ref[0:1, :] = cum[tm - 1:tm, :]


def _fox_gate_cumsum(f_logit, f_bias_row, nb, seq, tm=256):
    return pl.pallas_call(
        _fox_gate_kernel,
        grid=(nb, seq // tm),
        in_specs=[pl.BlockSpec((None, tm, HEAD), lambda b, i: (b, i, 0)),
                  pl.BlockSpec((1, HEAD), lambda b, i: (0, 0))],
        out_specs=pl.BlockSpec((None, tm, HEAD), lambda b, i: (b, i, 0)),
        out_shape=jax.ShapeDtypeStruct((nb, seq, HEAD), F32),
        scratch_shapes=[pltpu.VMEM((SUBLANES, HEAD), F32)],
        compiler_params=_cparams("parallel", "arbitrary"),
        name="fox_gate_cumsum",
    )(f_logit.reshape(nb, seq, HEAD), f_bias_row)


def _attn_kernel(q_ref, k_ref, v_ref, og_ref, fk_ref, o_ref,
                 s0_ref, s1_ref, m_ref, l_ref, acc_ref, *, tq):
    h = pl.program_id(1)
    qi = pl.program_id(2)
    m_ref[...] = jnp.full(m_ref.shape, -jnp.inf, F32)
    l_ref[...] = jnp.zeros(l_ref.shape, F32)
    acc_ref[...] = jnp.zeros(acc_ref.shape, F32)

    def scores(s_ref, ki):
        ks = pl.multiple_of(ki * tq, tq)
        s_ref[...] = _dot_nt(q_ref[...], k_ref[pl.ds(ks, tq), :]) - fk_ref[pl.ds(h, 1), pl.ds(ks, tq)]

    def update(s_ref, ki, masked):
        ks = pl.multiple_of(ki * tq, tq)
        s = s_ref[...]
        if masked:
            row = lax.broadcasted_iota(jnp.int32, (tq, tq), 0)
            col = lax.broadcasted_iota(jnp.int32, (tq, tq), 1)
            s = jnp.where(col <= row, s, -jnp.inf)
        m_old = m_ref[...]
        m_new = jnp.maximum(m_old, jnp.max(s, axis=-1, keepdims=True))
        p = jnp.exp2(s - m_new)
        alpha = jnp.exp2(m_old - m_new)
        l_ref[...] = alpha * l_ref[...] + jnp.sum(p, axis=-1, keepdims=True)
        m_ref[...] = m_new
        acc_ref[...] = alpha * acc_ref[...] + _dot(p.astype(BF16), v_ref[pl.ds(ks, tq), :])

    scores(s0_ref, 0)

    def pair(pi, carry):
        scores(s1_ref, 2 * pi + 1)
        update(s0_ref, 2 * pi, False)
        scores(s0_ref, 2 * pi + 2)
        update(s1_ref, 2 * pi + 1, False)
        return carry

    lax.fori_loop(0, qi // 2, pair, 0)

    @pl.when(qi % 2 == 1)
    def _():
        scores(s1_ref, qi)
        update(s0_ref, qi - 1, False)
        update(s1_ref, qi, True)

    @pl.when(qi % 2 == 0)
    def _():
        update(s0_ref, qi, True)

    o_ref[...] = (acc_ref[...] / l_ref[...] * jax.nn.sigmoid(og_ref[...].astype(F32))).astype(o_ref.dtype)


def _split3(x):
    hi = x.astype(BF16).astype(F32)
    mid = (x - hi).astype(BF16).astype(F32)
    return hi, mid, x - hi - mid


def _attn_fixed_shift_kernel(q_ref, k_ref, v_ref, og_ref, fc_ref, c_ref, o_ref,
                             ka_ref, va_ref, qa_ref, p0_ref, p1_ref, acc_ref, *, tq):
    h = pl.program_id(1)
    step = pl.program_id(2)
    seq = k_ref.shape[0]
    nq = seq // tq
    q_blocks = (step, nq - 1 - step)

    def block_rows(i):
        return pl.ds(pl.multiple_of(i * tq, tq), tq)

    def head_col(rows):
        lane = lax.broadcasted_iota(jnp.int32, (tq, HEAD), 1)
        return jnp.sum(jnp.where(lane == h, fc_ref[rows, :], 0.0), axis=-1, keepdims=True)

    def extra_cols(first, second):
        lane = lax.broadcasted_iota(jnp.int32, (tq, HEAD), 1)
        cols = jnp.zeros((tq, HEAD), F32)
        for n, term in enumerate(first + second):
            cols = jnp.where(lane == n, term, cols)
        return cols.astype(BF16)

    ones3 = (1.0, 1.0, 1.0)

    @pl.when(step == 0)
    def _():
        def build(i, carry):
            rows = block_rows(i)
            hi, mid, lo = _split3(-head_col(rows))
            ka_ref[rows, 0:HEAD] = k_ref[rows, :]
            ka_ref[rows, HEAD:2 * HEAD] = extra_cols((hi, mid, lo), ones3)
            va_ref[rows, 0:HEAD] = v_ref[rows, :]
            va_ref[rows, HEAD:2 * HEAD] = jnp.ones((tq, HEAD), BF16)
            return carry
        lax.fori_loop(0, nq, build, 0)

    for n, qb in enumerate(q_blocks):
        rows = block_rows(qb)
        qa_ref[n] = jnp.concatenate(
            [q_ref[rows, :], extra_cols(ones3, _split3(head_col(rows) - c_ref[0:1, 0:1]))], axis=1)
    acc_ref[...] = jnp.zeros(acc_ref.shape, F32)

    def probs(p_ref, which, kb, masked):
        s = _dot_nt(qa_ref[which], ka_ref[block_rows(kb), :])
        if masked:
            row = lax.broadcasted_iota(jnp.int32, (tq, tq), 0)
            col = lax.broadcasted_iota(jnp.int32, (tq, tq), 1)
            s = jnp.where(col <= row, s, -jnp.inf)
        p_ref[...] = jnp.exp2(s).astype(BF16)

    def accumulate(p_ref, which, kb):
        acc_ref[which] += _dot(p_ref[...], va_ref[block_rows(kb), :])

    items = [(0, q_blocks[0], True), (1, q_blocks[1], True)]
    for n in range(nq - 1):
        which = (n >= q_blocks[0]).astype(jnp.int32)
        items.append((which, n - which * q_blocks[0], False))
    p_refs = (p0_ref, p1_ref)
    for n, (which, kb, masked) in enumerate(items):
        probs(p_refs[n % 2], which, kb, masked)
        if n > 0:
            accumulate(p_refs[(n - 1) % 2], *items[n - 1][:2])
    accumulate(p_refs[(len(items) - 1) % 2], *items[-1][:2])

    for n, qb in enumerate(q_blocks):
        rows = block_rows(qb)
        acc = acc_ref[n]
        o_ref[rows, :] = (acc[:, 0:HEAD] / acc[:, HEAD:HEAD + 1]
                          * jax.nn.sigmoid(og_ref[rows, :].astype(F32))).astype(o_ref.dtype)


def _fox_attention_fixed_shift(qk3, v3, og3, fcum, c_row, n_heads, tq=512):
    nb, seq, _ = qk3.shape
    nq = seq // tq
    assert nq % 2 == 0
    full = lambda off: pl.BlockSpec((None, seq, HEAD), lambda b, h, i: (b, 0, off + h))
    return pl.pallas_call(
        functools.partial(_attn_fixed_shift_kernel, tq=tq),
        grid=(nb, n_heads, nq // 2),
        in_specs=[full(0), full(n_heads), full(0), full(0),
                  pl.BlockSpec((None, seq, HEAD), lambda b, h, i: (b, 0, 0)),
                  pl.BlockSpec((1, HEAD), lambda b, h, i: (0, 0))],
        out_specs=full(0),
        out_shape=jax.ShapeDtypeStruct((nb, seq, n_heads * HEAD), BF16),
        scratch_shapes=[pltpu.VMEM((seq, 2 * HEAD), BF16), pltpu.VMEM((seq, 2 * HEAD), BF16),
                        pltpu.VMEM((2, tq, 2 * HEAD), BF16),
                        pltpu.VMEM((tq, tq), BF16), pltpu.VMEM((tq, tq), BF16),
                        pltpu.VMEM((2, tq, 2 * HEAD), F32)],
        compiler_params=_cparams("parallel", "parallel", "arbitrary"),
        name="fox_attention_fixed_shift",
    )(qk3, qk3, v3, og3, fcum, c_row)


ITEM_STRIDE = 32
LIST_UNROLL = 4


def _attn_block_list_kernel(items_ref, count_ref, q_ref, k_ref, v_ref, og_ref, fc_ref, c_ref, o_ref,
                            ka_ref, va_ref, qa_ref, p0_ref, p1_ref, acc_ref, *, tq):
    h = pl.program_id(1)
    bh = pl.program_id(0) * pl.num_programs(1) + h
    seq = k_ref.shape[0]
    nq = seq // tq

    def block_rows(i):
        return pl.ds(pl.multiple_of(i * tq, tq), tq)

    lane1 = lax.broadcasted_iota(jnp.int32, (1, HEAD), 1)
    c_hi, c_mid, c_lo = _split3(c_ref[...])
    k_fixed = jnp.where((lane1 >= 3) & (lane1 < 6), 1.0,
                        jnp.where(lane1 == 6, -c_hi, jnp.where(lane1 == 7, -c_mid,
                                                               jnp.where(lane1 == 8, -c_lo, 0.0))))
    q_fixed = jnp.where((lane1 < 3) | ((lane1 >= 6) & (lane1 < 9)), 1.0, 0.0)

    def build(i, carry):
        rows = block_rows(i)
        lane = lax.broadcasted_iota(jnp.int32, (tq, HEAD), 1)
        f = jnp.sum(jnp.where(lane == h, fc_ref[rows, :], 0.0), axis=-1, keepdims=True)
        hi, mid, lo = _split3(f)
        k_cols = jnp.where(lane == 0, -hi, jnp.where(lane == 1, -mid, jnp.where(lane == 2, -lo, k_fixed)))
        q_cols = jnp.where(lane == 3, hi, jnp.where(lane == 4, mid, jnp.where(lane == 5, lo, q_fixed)))
        ka_ref[rows, 0:HEAD] = k_ref[rows, :]
        ka_ref[rows, HEAD:2 * HEAD] = k_cols.astype(BF16)
        va_ref[rows, 0:HEAD] = v_ref[rows, :]
        va_ref[rows, HEAD:2 * HEAD] = jnp.ones((tq, HEAD), BF16)
        qa_ref[rows, 0:HEAD] = q_ref[rows, :]
        qa_ref[rows, HEAD:2 * HEAD] = q_cols.astype(BF16)
        return carry

    lax.fori_loop(0, nq, build, 0)
    acc_ref[nq] = jnp.zeros(acc_ref.shape[1:], F32)

    def probs(p_ref, qb, kb, masked):
        qb = jnp.minimum(qb, nq - 1)
        s = _dot_nt(qa_ref[block_rows(qb), :], ka_ref[block_rows(kb), :])
        if masked:
            row = lax.broadcasted_iota(jnp.int32, (tq, tq), 0)
            col = lax.broadcasted_iota(jnp.int32, (tq, tq), 1)
            s = jnp.where(col <= row, s, -jnp.inf)
        p_ref[...] = jnp.exp2(s).astype(BF16)

    def product(p_ref, kb):
        return _dot(p_ref[...], va_ref[block_rows(kb), :])

    p_refs = (p0_ref, p1_ref)
    for i in range(nq):
        probs(p_refs[i % 2], i, i, True)
        if i > 0:
            acc_ref[i - 1] = product(p_refs[(i - 1) % 2], i - 1)
    acc_ref[nq - 1] = product(p_refs[(nq - 1) % 2], nq - 1)

    def item(n):
        code = items_ref[bh, n]
        return code // ITEM_STRIDE, code % ITEM_STRIDE

    def accumulate(p_ref, n):
        qb, kb = item(n)
        acc_ref[qb] += product(p_ref, kb)

    probs(p0_ref, *item(0), False)

    def group(m, carry):
        for u in range(LIST_UNROLL):
            n = LIST_UNROLL * m + u
            probs(p_refs[(u + 1) % 2], *item(n + 1), False)
            accumulate(p_refs[u % 2], n)
        return carry

    lax.fori_loop(0, count_ref[bh] // LIST_UNROLL, group, 0)

    def finish(i, carry):
        rows = block_rows(i)
        acc = acc_ref[i]
        o_ref[rows, :] = (acc[:, 0:HEAD] / acc[:, HEAD:HEAD + 1]
                          * jax.nn.sigmoid(og_ref[rows, :].astype(F32))).astype(o_ref.dtype)
        return carry

    lax.fori_loop(0, nq, finish, 0)


def _fox_attention_block_list(qk3, v3, og3, fcum, c_row, items, counts, n_heads, tq=512):
    nb, seq, _ = qk3.shape
    nq = seq // tq
    full = lambda off: pl.BlockSpec((None, seq, HEAD), lambda b, h, *_: (b, 0, off + h))
    grid_spec = pltpu.PrefetchScalarGridSpec(
        num_scalar_prefetch=2,
        grid=(nb, n_heads),
        in_specs=[full(0), full(n_heads), full(0), full(0),
                  pl.BlockSpec((None, seq, HEAD), lambda b, h, *_: (b, 0, 0), pipeline_mode=pl.Buffered(1)),
                  pl.BlockSpec((1, HEAD), lambda b, h, *_: (0, 0))],
        out_specs=full(0),
        scratch_shapes=[pltpu.VMEM((seq, 2 * HEAD), BF16), pltpu.VMEM((seq, 2 * HEAD), BF16),
                        pltpu.VMEM((seq, 2 * HEAD), BF16),
                        pltpu.VMEM((tq, tq), BF16), pltpu.VMEM((tq, tq), BF16),
                        pltpu.VMEM((nq + 1, tq, 2 * HEAD), F32)])
    return pl.pallas_call(
        functools.partial(_attn_block_list_kernel, tq=tq),
        grid_spec=grid_spec,
        out_shape=jax.ShapeDtypeStruct((nb, seq, n_heads * HEAD), BF16),
        compiler_params=_cparams("parallel", "arbitrary"),
        name="fox_attention_block_list",
    )(items, counts, qk3, qk3, v3, og3, fcum, c_row)


MAX_SHIFT_GAP = 100.0
SKIP_LOG2 = 136.0
LIST_MAX_FRACTION = 0.75


def _needed_blocks(fcum, n_heads, tq):
    nb, seq, _ = fcum.shape
    nq = seq // tq
    f_first = fcum[:, 0::tq, :n_heads]
    f_last = fcum[:, tq - 1::tq, :n_heads]
    bound = f_first[:, :, None, :] - f_last[:, None, :, :]
    i = jnp.arange(nq, dtype=jnp.int32)[:, None]
    j = jnp.arange(nq, dtype=jnp.int32)[None, :]
    need = (bound > -SKIP_LOG2) & (j < i)[None, :, :, None]
    need = jnp.transpose(need, (0, 3, 1, 2)).reshape(nb * n_heads, nq * nq)
    pad_code = nq * ITEM_STRIDE
    codes = jnp.where(need, (i * ITEM_STRIDE + j).reshape(1, nq * nq), pad_code)
    codes = jnp.sort(codes, axis=-1)
    codes = jnp.concatenate([codes, jnp.full((nb * n_heads, LIST_UNROLL), pad_code, jnp.int32)], axis=-1)
    counts = jnp.sum(need, axis=-1).astype(jnp.int32)
    return codes.astype(jnp.int32), counts + (-counts) % LIST_UNROLL, jnp.sum(counts)


def _fox_attention(qk3, v3, og3, fcum, gn, n_heads, tq=512):
    nb, seq, _ = qk3.shape
    nq = seq // tq
    qd = n_heads * HEAD
    c = (HEAD * 1.01) * jnp.max(jnp.abs(gn[:, :qd])) * jnp.max(jnp.abs(gn[:, qd:]))
    c_row = jnp.full((1, HEAD), c, F32)
    items, counts, total = _needed_blocks(fcum, n_heads, tq)
    all_pairs = nb * n_heads * (nq * (nq - 1) // 2)

    def block_list(qk3, v3, og3, fcum, c_row, items, counts):
        return _fox_attention_block_list(qk3, v3, og3, fcum, c_row, items, counts, n_heads, tq)

    def full_stream(qk3, v3, og3, fcum, c_row, items, counts):
        return _fox_attention_fixed_shift(qk3, v3, og3, fcum, c_row, n_heads, tq)

    def running_max(qk3, v3, og3, fcum, c_row, items, counts):
        fk = jnp.transpose(fcum[:, :, :n_heads], (0, 2, 1))
        return _fox_attention_running_max(qk3, v3, og3, fk, n_heads, tq)

    branch = jnp.where(2.0 * c > MAX_SHIFT_GAP, 2,
                       jnp.where(total <= LIST_MAX_FRACTION * all_pairs, 0, 1)).astype(jnp.int32)
    return lax.switch(branch, (block_list, full_stream, running_max), qk3, v3, og3, fcum, c_row, items, counts)


def _fox_attention_running_max(qk3, v3, og3, fk, n_heads, tq=512):
    nb, seq, _ = qk3.shape
    return pl.pallas_call(
        functools.partial(_attn_kernel, tq=tq),
        grid=(nb, n_heads, seq // tq),
        in_specs=[pl.BlockSpec((None, tq, HEAD), lambda b, h, i: (b, i, h)),
                  pl.BlockSpec((None, seq, HEAD), lambda b, h, i: (b, 0, n_heads + h)),
                  pl.BlockSpec((None, seq, HEAD), lambda b, h, i: (b, 0, h)),
                  pl.BlockSpec((None, tq, HEAD), lambda b, h, i: (b, i, h)),
                  pl.BlockSpec((None, n_heads, seq), lambda b, h, i: (b, 0, 0))],
        out_specs=pl.BlockSpec((None, tq, HEAD), lambda b, h, i: (b, i, h)),
        out_shape=jax.ShapeDtypeStruct((nb, seq, n_heads * HEAD), BF16),
        scratch_shapes=[pltpu.VMEM((tq, tq), F32), pltpu.VMEM((tq, tq), F32),
                        pltpu.VMEM((tq, 1), F32), pltpu.VMEM((tq, 1), F32), pltpu.VMEM((tq, HEAD), F32)],
        compiler_params=_cparams("parallel", "parallel", "arbitrary"),
        name="fox_attention",
    )(qk3, qk3, v3, og3, fk)


def _gdn_gate_kernel(ab_ref, alog_ref, dt_ref, o_ref, *, n_heads):
    tm = ab_ref.shape[0]
    x = ab_ref[...]
    lane = lax.broadcasted_iota(jnp.int32, x.shape, 1)
    g = -jnp.exp(alog_ref[...]) * _softplus(x + dt_ref[...])
    r = lax.broadcasted_iota(jnp.int32, (tm, tm), 0)
    c = lax.broadcasted_iota(jnp.int32, (tm, tm), 1)
    tri = ((r >= c) & ((r // CHUNK) == (c // CHUNK))).astype(F32)
    gc = _dot_f32(tri, jnp.where(lane < n_heads, g, 0.0))
    o_ref[...] = jnp.where(lane < n_heads, gc, jax.nn.sigmoid(x))


def _gdn_gates(ab, alog_row, dt_row, n_heads, tm=256):
    t = ab.shape[0]
    return pl.pallas_call(
        functools.partial(_gdn_gate_kernel, n_heads=n_heads),
        grid=(t // tm,),
        in_specs=[pl.BlockSpec((tm, HEAD), lambda i: (i, 0)),
                  pl.BlockSpec((1, HEAD), lambda i: (0, 0)),
                  pl.BlockSpec((1, HEAD), lambda i: (0, 0))],
        out_specs=pl.BlockSpec((tm, HEAD), lambda i: (i, 0)),
        out_shape=jax.ShapeDtypeStruct((t, HEAD), F32),
        compiler_params=_cparams("parallel"),
        name="gdn_gates",
    )(ab, alog_row, dt_row)


def _block_diag2(p, half):
    return jnp.concatenate([jnp.where(half, 0.0, p), jnp.where(half, p, 0.0)], axis=0)


def _gdn_chunk_kernel(q_ref, k_ref, v_ref, gb_ref, grow_ref, u_ref, w_ref, qd_ref, kd_ref, qk_ref, egl_ref,
                      *, n_vheads, chunks):
    hk = pl.program_id(1)
    lane = lax.broadcasted_iota(jnp.int32, (CHUNK, HEAD), 1)
    row = lax.broadcasted_iota(jnp.int32, (CHUNK, HEAD), 0)
    col = lane & (CHUNK - 1)
    half = lane >= CHUNK
    lower = row >= col
    strict = row > col
    eye = (row == col).astype(F32)
    level_masks = []
    for lvl in range(6):
        level_masks.append((((row >> lvl) & 1) == 1) & (((col >> lvl) & 1) == 0)
                           & ((row >> (lvl + 1)) == (col >> (lvl + 1))))
    zeros = jnp.zeros((CHUNK, HEAD), F32)

    def pick(gb, idx):
        return jnp.sum(jnp.where(lane == idx, gb, 0.0), axis=-1, keepdims=True)

    sls = [slice(c * CHUNK, (c + 1) * CHUNK) for c in range(chunks)]
    gcs, bes, a_s, xs = [], [], [], []
    for c, sl in enumerate(sls):
        q = q_ref[sl, :]
        k = k_ref[sl, :]
        gb = gb_ref[sl, :]
        gc0, gc1 = pick(gb, 2 * hk), pick(gb, 2 * hk + 1)
        be0, be1 = pick(gb, n_vheads + 2 * hk), pick(gb, n_vheads + 2 * hk + 1)
        gcol = jnp.where(half, gc1, gc0)
        bcol = jnp.where(half, be1, be0)
        decay = jnp.exp(jnp.where(lower, gcol - grow_ref[c:c + 1, :], -jnp.inf))
        qkk = _dot_nt(jnp.concatenate([q, k], axis=0), jnp.concatenate([k, k], axis=0))
        a = jnp.where(strict, qkk[CHUNK:, :] * decay * bcol, 0.0)
        qk_ref[sl, :] = (qkk[:CHUNK, :] * decay).astype(qk_ref.dtype)
        gcs.append((gc0, gc1))
        bes.append((be0, be1))
        a_s.append(a)
        xs.append(eye - jnp.where(level_masks[0], a, 0.0))

    for lvl in range(1, 6):
        ys = [_dot(xs[c].astype(BF16), _block_diag2(jnp.where(level_masks[lvl], a_s[c], 0.0), half).astype(BF16))
              for c in range(chunks)]
        xs = [xs[c] - _dot(ys[c].astype(BF16), _block_diag2(xs[c], half).astype(BF16)) for c in range(chunks)]

    for c, sl in enumerate(sls):
        (gc0, gc1), (be0, be1) = gcs[c], bes[c]
        kf = k_ref[sl, :].astype(F32)
        qf = q_ref[sl, :].astype(F32)
        v = v_ref[sl, :].astype(F32)
        eg0, eg1 = jnp.exp(gc0), jnp.exp(gc1)
        rhs = jnp.concatenate([
            jnp.concatenate([v[:, :HEAD] * be0, kf * (be0 * eg0), zeros, zeros], axis=1),
            jnp.concatenate([zeros, zeros, v[:, HEAD:] * be1, kf * (be1 * eg1)], axis=1)], axis=0)
        sol = _dot(xs[c].astype(BF16), rhs.astype(BF16))
        u_ref[sl, :] = jnp.concatenate([sol[:, 0:HEAD], sol[:, 2 * HEAD:3 * HEAD]], axis=1).astype(u_ref.dtype)
        w_ref[sl, :] = jnp.concatenate([sol[:, HEAD:2 * HEAD], sol[:, 3 * HEAD:]], axis=1).astype(w_ref.dtype)
        qd_ref[sl, :] = jnp.concatenate([qf * eg0, qf * eg1], axis=1).astype(qd_ref.dtype)
        gl0, gl1 = gc0[CHUNK - 1:CHUNK, :], gc1[CHUNK - 1:CHUNK, :]
        kd_ref[sl, :] = jnp.concatenate([kf * jnp.exp(gl0 - gc0), kf * jnp.exp(gl1 - gc1)],
                                        axis=1).astype(kd_ref.dtype)
        egl_ref[0, c:c + 1, :] = jnp.broadcast_to(jnp.exp(gl0), (1, HEAD))
        egl_ref[1, c:c + 1, :] = jnp.broadcast_to(jnp.exp(gl1), (1, HEAD))


def _gdn_chunks(qk3, v3, gb3, grow, n_kheads, n_vheads, tm=1024):
    nb, seq, _ = qk3.shape
    chunks = tm // CHUNK
    vd = n_vheads * HEAD
    big = jax.ShapeDtypeStruct((nb, seq, vd), BF16)
    pair_spec = pl.BlockSpec((None, tm, 2 * HEAD), lambda b, h, i: (b, i, h))
    return pl.pallas_call(
        functools.partial(_gdn_chunk_kernel, n_vheads=n_vheads, chunks=chunks),
        grid=(nb, n_kheads, seq // tm),
        in_specs=[pl.BlockSpec((None, tm, HEAD), lambda b, h, i: (b, i, h)),
                  pl.BlockSpec((None, tm, HEAD), lambda b, h, i: (b, i, n_kheads + h)),
                  pair_spec,
                  pl.BlockSpec((None, tm, HEAD), lambda b, h, i: (b, i, 0)),
                  pl.BlockSpec((None, None, chunks, HEAD), lambda b, h, i: (b, h, i, 0))],
        out_specs=[pair_spec, pair_spec, pair_spec, pair_spec,
                   pl.BlockSpec((None, tm, HEAD), lambda b, h, i: (b, i, h)),
                   pl.BlockSpec((None, 2, chunks, HEAD), lambda b, h, i: (b, h, i, 0))],
        out_shape=[big, big, big, big,
                   jax.ShapeDtypeStruct((nb, seq, n_kheads * HEAD), BF16),
                   jax.ShapeDtypeStruct((nb, n_vheads, seq // CHUNK, HEAD), F32)],
        compiler_params=_cparams("parallel", "parallel", "parallel"),
        name="gdn_chunks",
    )(qk3, qk3, v3, gb3, grow)


def _gdn_scan_kernel(u_ref, w_ref, qd_ref, kd_ref, qk_ref, egl_ref, z_ref, gn_ref, o_ref, s_ref,
                     *, heads, chunks):
    @pl.when(pl.program_id(2) == 0)
    def _():
        s_ref[...] = jnp.zeros_like(s_ref)

    zeros = jnp.zeros((CHUNK, HEAD), F32)
    gn = gn_ref[...]
    cols = [slice(g * HEAD, (g + 1) * HEAD) for g in range(heads)]
    for c in range(chunks):
        sl = slice(c * CHUNK, (c + 1) * CHUNK)
        rs = [_dot(jnp.concatenate([w_ref[sl, cols[g]], qd_ref[sl, cols[g]]], axis=0), s_ref[g].astype(BF16))
              for g in range(heads)]
        vnew = [u_ref[sl, cols[g]].astype(F32) - rs[g][:CHUNK, :] for g in range(heads)]
        intra = []
        for pr in range(heads // 2):
            v2 = jnp.concatenate([jnp.concatenate([vnew[2 * pr], zeros], axis=1),
                                  jnp.concatenate([zeros, vnew[2 * pr + 1]], axis=1)], axis=0).astype(BF16)
            intra.append(_dot(qk_ref[sl, pr * HEAD:(pr + 1) * HEAD], v2))
        for g in range(heads):
            s_ref[g] = s_ref[g] * egl_ref[g, c:c + 1, :] + _dot_tn(kd_ref[sl, cols[g]], vnew[g].astype(BF16))
        for g in range(heads):
            o = rs[g][CHUNK:, :] + intra[g // 2][:, (g % 2) * HEAD:(g % 2 + 1) * HEAD]
            ms = jnp.mean(o * o, axis=-1, keepdims=True)
            z = z_ref[sl, cols[g]].astype(F32)
            o_ref[sl, cols[g]] = (o * lax.rsqrt(ms + EPS) * gn * (z * jax.nn.sigmoid(z))).astype(o_ref.dtype)


def _gdn_scan(u, w, qd, kd, qkm, egl, z3, gn_row, n_vheads, heads=16, tm=512):
    nb, seq, vd = u.shape
    chunks = tm // CHUNK
    wide = pl.BlockSpec((None, tm, heads * HEAD), lambda b, g, i: (b, i, g))
    return pl.pallas_call(
        functools.partial(_gdn_scan_kernel, heads=heads, chunks=chunks),
        grid=(nb, n_vheads // heads, seq // tm),
        in_specs=[wide, wide, wide, wide,
                  pl.BlockSpec((None, tm, heads // 2 * HEAD), lambda b, g, i: (b, i, g)),
                  pl.BlockSpec((None, heads, chunks, HEAD), lambda b, g, i: (b, g, i, 0)),
                  wide,
                  pl.BlockSpec((1, HEAD), lambda b, g, i: (0, 0))],
        out_specs=wide,
        out_shape=jax.ShapeDtypeStruct((nb, seq, vd), BF16),
        scratch_shapes=[pltpu.VMEM((heads, HEAD, HEAD), F32)],
        compiler_params=_cparams("parallel", "parallel", "arbitrary"),
        name="gdn_scan",
    )(u, w, qd, kd, qkm, egl, z3, gn_row)


def _pad_cols(w, n):
    return jnp.pad(w, ((0, 0), (0, n - w.shape[1])))


def _row(v, n=HEAD):
    return jnp.pad(v.astype(F32), (0, n - v.shape[0])).reshape(1, n)


def _fox_layer(x2, nb, seq, g_pre, g_post, mod, w_in, f_bias, q_norm, k_norm, w_o, layer):
    d = x2.shape[1]
    nh = d // HEAD
    qd = nh * HEAD
    w_small = _pad_cols(w_in[:, 3 * qd:3 * qd + nh], HEAD)
    w_gate = w_in[:, 3 * qd + nh:]
    gn = jnp.concatenate([jnp.tile(q_norm.astype(F32) * (HEAD ** -0.5 * LOG2E), nh),
                          jnp.tile(k_norm.astype(F32), nh)]).reshape(1, 2 * qd)
    qk, f_logit, h = _fox_qk_proj(x2, g_pre, mod, w_in, gn, w_small, seq)
    v = _plain_proj(h, w_in, 2 * qd, qd)
    og = _plain_proj(h, w_gate, 0, qd)
    fcum = _fox_gate_cumsum(f_logit, _row(f_bias), nb, seq)
    attn = _fox_attention(qk.reshape(nb, seq, 2 * qd), v.reshape(nb, seq, qd), og.reshape(nb, seq, qd),
                          fcum, gn, nh)
    return _out_proj(attn.reshape(nb * seq, qd), w_o, layer, x2, g_post, mod, seq, tm=512)


def _gdn_layer(x2, nb, seq, g_pre, g_post, mod, w_in, conv_w, a_log, dt_bias, out_norm, w_o, layer):
    d = x2.shape[1]
    nk = d // HEAD
    nv = 2 * nk
    kd, vd = nk * HEAD, nv * HEAD
    n_main = 2 * kd + 2 * vd
    w_small = _pad_cols(w_in[:, n_main:], HEAD)
    conv_w = conv_w.astype(F32)
    qscale = jnp.concatenate([jnp.full((kd,), HEAD ** -0.5, F32), jnp.ones((kd,), F32)]).reshape(1, 2 * kd)
    qk, ab, h = _gdn_qk_proj(x2, g_pre, mod, w_in, conv_w, qscale, w_small, seq)
    v = _gdn_v_proj(h, w_in, conv_w, 2 * kd, vd, seq)
    z = _plain_proj(h, w_in, 2 * kd + vd, vd)
    gb = _gdn_gates(ab, _row(a_log), _row(dt_bias), nv)
    n_chunks = seq // CHUNK
    grow = gb[:, :nv].reshape(nb, n_chunks, CHUNK, nk, 2)
    grow = jnp.transpose(grow, (0, 3, 1, 4, 2)).reshape(nb, nk, n_chunks, 2 * CHUNK)
    u, w, qdec, kdec, qkm, egl = _gdn_chunks(qk.reshape(nb, seq, 2 * kd), v.reshape(nb, seq, vd),
                                             gb.reshape(nb, seq, HEAD), grow, nk, nv)
    a = _gdn_scan(u, w, qdec, kdec, qkm, egl, z.reshape(nb, seq, vd), _row(out_norm), nv)
    return _out_proj(a.reshape(nb * seq, vd), w_o, layer, x2, g_post, mod, seq, tm=512)


def _ffn_layer(x2, seq, g_pre, g_post, mod, w_up, conv_w, conv_b, w_down, layer):
    u = _ffn_up(x2, g_pre, mod, w_up, layer, conv_w.astype(F32), conv_b.reshape(1, -1).astype(F32), seq)
    return _out_proj(u, w_down, layer, x2, g_post, mod, seq, tm=256)


def kernel(x, c, ada_w, ada_b, norm_g, fox_w_in, fox_f_bias, fox_q_norm, fox_k_norm, fox_w_o,
           gdn_w_in, gdn_conv_w, gdn_a_log, gdn_dt_bias, gdn_out_norm, gdn_w_o,
           ffn_w_up, ffn_conv_w, ffn_conv_b, ffn_w_down):
    nb, seq, d = x.shape
    depth = ada_w.shape[0]
    mods = _ada_mods(c, ada_w, ada_b)
    fox_w_in, fox_w_o, gdn_w_in, gdn_w_o, ffn_w_up, ffn_w_down = (
        w.astype(BF16) for w in (fox_w_in, fox_w_o, gdn_w_in, gdn_w_o, ffn_w_up, ffn_w_down))
    x2 = x.reshape(nb * seq, d)
    for i in range(depth):
        g = [norm_g[i, r].reshape(1, d).astype(F32) for r in range(4)]
        j = i // 2
        if i % 2 == 0:
            x2 = _fox_layer(x2, nb, seq, g[0], g[1], mods[2 * i], fox_w_in[j], fox_f_bias[j],
                            fox_q_norm[j], fox_k_norm[j], fox_w_o, j)
        else:
            x2 = _gdn_layer(x2, nb, seq, g[0], g[1], mods[2 * i], gdn_w_in[j], gdn_conv_w[j],
                            gdn_a_log[j], gdn_dt_bias[j], gdn_out_norm[j], gdn_w_o, j)
        x2 = _ffn_layer(x2, seq, g[2], g[3], mods[2 * i + 1], ffn_w_up, ffn_conv_w[i],
                        ffn_conv_b[i], ffn_w_down, i)
    return x2.reshape(nb, seq, d)
```

```python
import functools

import jax
import jax.numpy as jnp
from jax import lax
from jax.experimental import pallas as pl
from jax.experimental.pallas import tpu as pltpu

F32 = jnp.float32
BF16 = jnp.bfloat16
EPS = 1e-6
HEAD = 128
CHUNK = 64
SUBLANES = 8
MXU_N = 256
VMEM_LIMIT = 56 * 1024 * 1024
INV_SQRT2 = 0.7071067811865476
LOG2E = 1.4426950408889634


def _cparams(*sem):
    return pltpu.CompilerParams(dimension_semantics=sem, vmem_limit_bytes=VMEM_LIMIT)


def _dot(a, b):
    return jnp.dot(a, b, preferred_element_type=F32)


def _dot_nt(a, b):
    return lax.dot_general(a, b, (((1,), (1,)), ((), ())), preferred_element_type=F32)


def _dot_tn(a, b):
    return lax.dot_general(a, b, (((0,), (0,)), ((), ())), preferred_element_type=F32)


def _dot_f32(a, b):
    return jnp.dot(a, b, preferred_element_type=F32, precision=lax.Precision.HIGHEST)


def _softplus(x):
    return jnp.maximum(x, 0.0) + jnp.log1p(jnp.exp(-jnp.abs(x)))


def _ada_kernel(c_ref, w_ref, b_ref, o_ref):
    c = c_ref[...]
    o_ref[0] = _dot_f32(c * jax.nn.sigmoid(c), w_ref[0]) + b_ref[0]


def _ada_mods(c, ada_w, ada_b):
    nb, d = c.shape
    n = ada_w.shape[0] * ada_w.shape[1]
    w = ada_w.reshape(n, d, 3 * d)
    b = ada_b.reshape(n, 1, 3 * d)
    cp = jnp.zeros((SUBLANES, d), F32).at[:nb].set(c)
    tn = 1024
    out = pl.pallas_call(
        _ada_kernel,
        grid=(n, 3 * d // tn),
        in_specs=[pl.BlockSpec((SUBLANES, d), lambda s, j: (0, 0)),
                  pl.BlockSpec((1, d, tn), lambda s, j: (s, 0, j)),
                  pl.BlockSpec((1, 1, tn), lambda s, j: (s, 0, j))],
        out_specs=pl.BlockSpec((1, SUBLANES, tn), lambda s, j: (s, 0, j)),
        out_shape=jax.ShapeDtypeStruct((n, SUBLANES, 3 * d), F32),
        compiler_params=_cparams("parallel", "parallel"),
        name="ada_mods",
    )(cp, w, b)
    return out[:, :nb].reshape(n, nb, 1, 3 * d)


NORM_ROWS = 32


def _norm_mod(h_ref, x_ref, g_ref, sh_ref, sc_ref):
    gain = g_ref[...] * (1.0 + sc_ref[...])
    shift = sh_ref[...]

    def rows_pass(r, carry):
        rows = pl.ds(pl.multiple_of(r * NORM_ROWS, NORM_ROWS), NORM_ROWS)
        x = x_ref[rows, :]
        ms = jnp.mean(x * x, axis=-1, keepdims=True)
        h_ref[rows, :] = (x * lax.rsqrt(ms + EPS) * gain + shift).astype(BF16)
        return carry

    lax.fori_loop(0, x_ref.shape[0] // NORM_ROWS, rows_pass, 0, unroll=8)


def _sub_dots(h_ref, w_ref):
    return [_dot(h_ref[...], w_ref[:, c * MXU_N:(c + 1) * MXU_N]) for c in range(w_ref.shape[1] // MXU_N)]


def _head_norm_store(o_ref, c, y, gain_ref, mean):
    for g in range(MXU_N // HEAD):
        sl = slice(c * MXU_N + g * HEAD, c * MXU_N + (g + 1) * HEAD)
        t = y[:, g * HEAD:(g + 1) * HEAD]
        ss = jnp.sum(t * t, axis=-1, keepdims=True)
        if mean:
            ss = ss * (1.0 / HEAD)
        o_ref[:, sl] = (t * lax.rsqrt(ss + EPS) * gain_ref[:, sl]).astype(o_ref.dtype)


def _causal_conv(acc, cw, carry_ref, slot, first):
    tm = acc.shape[0]
    taps = cw.shape[0]
    prev = jnp.where(first, 0.0, carry_ref[slot])
    carry_ref[slot] = acc[tm - SUBLANES:tm, :]
    xe = jnp.concatenate([prev, acc], axis=0)
    ye = xe * cw[0:1, :]
    for k in range(1, taps):
        ye = xe * cw[k:k + 1, :] + pltpu.roll(ye, 1, axis=0)
    return ye[SUBLANES:, :]


def _zero_carry_at_start(carry_ref):
    @pl.when((pl.program_id(0) == 0) & (pl.program_id(1) == 0))
    def _():
        carry_ref[...] = jnp.zeros_like(carry_ref)


def _fox_qk_kernel(x_ref, g_ref, sh_ref, sc_ref, w_ref, gn_ref, ws_ref, o_ref, os_ref, h_ref):
    @pl.when(pl.program_id(1) == 0)
    def _():
        _norm_mod(h_ref, x_ref, g_ref, sh_ref, sc_ref)
        os_ref[...] = _dot(h_ref[...], ws_ref[...])

    for c, acc in enumerate(_sub_dots(h_ref, w_ref)):
        _head_norm_store(o_ref, c, acc, gn_ref, mean=True)


def _plain_proj_kernel(h_ref, w_ref, o_ref):
    o_ref[...] = _dot(h_ref[...], w_ref[...]).astype(o_ref.dtype)


def _gdn_qk_kernel(x_ref, g_ref, sh_ref, sc_ref, w_ref, cw_ref, qs_ref, ws_ref, o_ref, os_ref, h_ref,
                   carry_ref, *, blocks_per_seq):
    i = pl.program_id(0)
    j = pl.program_id(1)
    _zero_carry_at_start(carry_ref)

    @pl.when(j == 0)
    def _():
        _norm_mod(h_ref, x_ref, g_ref, sh_ref, sc_ref)
        os_ref[...] = _dot(h_ref[...], ws_ref[...])

    first = (i % blocks_per_seq) == 0
    accs = _sub_dots(h_ref, w_ref)
    for c, acc in enumerate(accs):
        y = _causal_conv(acc, cw_ref[:, c * MXU_N:(c + 1) * MXU_N], carry_ref, j * len(accs) + c, first)
        _head_norm_store(o_ref, c, y * jax.nn.sigmoid(y), qs_ref, mean=False)


def _gdn_v_kernel(h_ref, w_ref, cw_ref, o_ref, carry_ref, *, blocks_per_seq):
    i = pl.program_id(0)
    j = pl.program_id(1)
    _zero_carry_at_start(carry_ref)
    first = (i % blocks_per_seq) == 0
    accs = _sub_dots(h_ref, w_ref)
    for c, acc in enumerate(accs):
        cs = slice(c * MXU_N, (c + 1) * MXU_N)
        y = _causal_conv(acc, cw_ref[:, cs], carry_ref, j * len(accs) + c, first)
        o_ref[:, cs] = (y * jax.nn.sigmoid(y)).astype(o_ref.dtype)


def _ffn_up_kernel(x_ref, g_ref, sh_ref, sc_ref, wg_ref, wv_ref, cw_ref, cb_ref, o_ref,
                   h_ref, carry_ref, *, blocks_per_seq):
    i = pl.program_id(0)
    j = pl.program_id(1)
    _zero_carry_at_start(carry_ref)

    @pl.when(j == 0)
    def _():
        _norm_mod(h_ref, x_ref, g_ref, sh_ref, sc_ref)

    first = (i % blocks_per_seq) == 0
    n_sub = wg_ref.shape[1] // MXU_N
    pairs = []
    for c in range(n_sub):
        cs = slice(c * MXU_N, (c + 1) * MXU_N)
        pairs.append((_dot(h_ref[...], wg_ref[:, cs]), _dot(h_ref[...], wv_ref[:, cs])))
    for c, (gate, val) in enumerate(pairs):
        cs = slice(c * MXU_N, (c + 1) * MXU_N)
        gate = _causal_conv(gate, cw_ref[:, cs], carry_ref, j * n_sub + c, first) + cb_ref[:, cs]
        act = 0.5 * gate * (1.0 + lax.erf(gate * INV_SQRT2))
        o_ref[:, cs] = (act * val).astype(o_ref.dtype)


def _x_mod_specs(tm, d, bps):
    return [pl.BlockSpec((tm, d), lambda i, j: (i, 0)),
            pl.BlockSpec((1, d), lambda i, j: (0, 0)),
            pl.BlockSpec((None, 1, d), lambda i, j: (i // bps, 0, 0)),
            pl.BlockSpec((None, 1, d), lambda i, j: (i // bps, 0, 1))]


def _carry_scratch(tn, nj):
    return pltpu.VMEM((nj * (tn // MXU_N), SUBLANES, MXU_N), F32)


def _fox_qk_proj(x2, g_pre, mod, w_main, gn, w_small, seq, tm=1024, tn=1024):
    t, d = x2.shape
    n = gn.shape[1]
    return pl.pallas_call(
        _fox_qk_kernel,
        grid=(t // tm, n // tn),
        in_specs=_x_mod_specs(tm, d, seq // tm) + [
            pl.BlockSpec((d, tn), lambda i, j: (0, j)),
            pl.BlockSpec((1, tn), lambda i, j: (0, j)),
            pl.BlockSpec((d, HEAD), lambda i, j: (0, 0))],
        out_specs=[pl.BlockSpec((tm, tn), lambda i, j: (i, j)),
                   pl.BlockSpec((tm, HEAD), lambda i, j: (i, 0)),
                   pl.BlockSpec((tm, d), lambda i, j: (i, 0))],
        out_shape=[jax.ShapeDtypeStruct((t, n), BF16), jax.ShapeDtypeStruct((t, HEAD), F32),
                   jax.ShapeDtypeStruct((t, d), BF16)],
        compiler_params=_cparams("parallel", "arbitrary"),
        name="fox_qk_proj",
    )(x2, g_pre, mod, mod, w_main, gn, w_small)


def _plain_proj(h, w_main, col0, n, tm=1024, tn=1024):
    t, d = h.shape
    off = col0 // tn
    return pl.pallas_call(
        _plain_proj_kernel,
        grid=(t // tm, n // tn),
        in_specs=[pl.BlockSpec((tm, d), lambda i, j: (i, 0)),
                  pl.BlockSpec((d, tn), lambda i, j: (0, off + j))],
        out_specs=pl.BlockSpec((tm, tn), lambda i, j: (i, j)),
        out_shape=jax.ShapeDtypeStruct((t, n), BF16),
        compiler_params=_cparams("parallel", "arbitrary"),
        name="plain_proj",
    )(h, w_main)


def _gdn_qk_proj(x2, g_pre, mod, w_main, conv_w, qscale, w_small, seq, tm=1024, tn=1024):
    t, d = x2.shape
    n = qscale.shape[1]
    taps = conv_w.shape[0]
    return pl.pallas_call(
        functools.partial(_gdn_qk_kernel, blocks_per_seq=seq // tm),
        grid=(t // tm, n // tn),
        in_specs=_x_mod_specs(tm, d, seq // tm) + [
            pl.BlockSpec((d, tn), lambda i, j: (0, j)),
            pl.BlockSpec((taps, tn), lambda i, j: (0, j)),
            pl.BlockSpec((1, tn), lambda i, j: (0, j)),
            pl.BlockSpec((d, HEAD), lambda i, j: (0, 0))],
        out_specs=[pl.BlockSpec((tm, tn), lambda i, j: (i, j)),
                   pl.BlockSpec((tm, HEAD), lambda i, j: (i, 0)),
                   pl.BlockSpec((tm, d), lambda i, j: (i, 0))],
        out_shape=[jax.ShapeDtypeStruct((t, n), BF16), jax.ShapeDtypeStruct((t, HEAD), F32),
                   jax.ShapeDtypeStruct((t, d), BF16)],
        scratch_shapes=[_carry_scratch(tn, n // tn)],
        compiler_params=_cparams("arbitrary", "arbitrary"),
        name="gdn_qk_proj",
    )(x2, g_pre, mod, mod, w_main, conv_w, qscale, w_small)


def _gdn_v_proj(h, w_main, conv_w, col0, n, seq, tm=1024, tn=1024):
    t, d = h.shape
    taps = conv_w.shape[0]
    off = col0 // tn
    return pl.pallas_call(
        functools.partial(_gdn_v_kernel, blocks_per_seq=seq // tm),
        grid=(t // tm, n // tn),
        in_specs=[pl.BlockSpec((tm, d), lambda i, j: (i, 0)),
                  pl.BlockSpec((d, tn), lambda i, j: (0, off + j)),
                  pl.BlockSpec((taps, tn), lambda i, j: (0, off + j))],
        out_specs=pl.BlockSpec((tm, tn), lambda i, j: (i, j)),
        out_shape=jax.ShapeDtypeStruct((t, n), BF16),
        scratch_shapes=[_carry_scratch(tn, n // tn)],
        compiler_params=_cparams("arbitrary", "arbitrary"),
        name="gdn_v_proj",
    )(h, w_main, conv_w)


def _ffn_up(x2, g_pre, mod, w_up, layer, conv_w, conv_b, seq, tm=1024, tn=512):
    t, d = x2.shape
    dff = w_up.shape[2] // 2
    nj = dff // tn
    taps = conv_w.shape[0]
    return pl.pallas_call(
        functools.partial(_ffn_up_kernel, blocks_per_seq=seq // tm),
        grid=(t // tm, nj),
        in_specs=_x_mod_specs(tm, d, seq // tm) + [
            pl.BlockSpec((None, d, tn), lambda i, j: (layer, 0, j)),
            pl.BlockSpec((None, d, tn), lambda i, j: (layer, 0, j + nj)),
            pl.BlockSpec((taps, tn), lambda i, j: (0, j)),
            pl.BlockSpec((1, tn), lambda i, j: (0, j))],
        out_specs=pl.BlockSpec((tm, tn), lambda i, j: (i, j)),
        out_shape=jax.ShapeDtypeStruct((t, dff), BF16),
        scratch_shapes=[pltpu.VMEM((tm, d), BF16), _carry_scratch(tn, nj)],
        compiler_params=_cparams("arbitrary", "arbitrary"),
        name="ffn_up",
    )(x2, g_pre, mod, mod, w_up, w_up, conv_w, conv_b)


def _out_proj_kernel(a_ref, w_ref, x_ref, g_ref, gate_ref, o_ref):
    y = _dot(a_ref[...], w_ref[...])
    ms = jnp.mean(y * y, axis=-1, keepdims=True)
    o_ref[...] = x_ref[...] + (y * lax.rsqrt(ms + EPS)) * (gate_ref[...] * g_ref[...])


def _out_proj(a, w, layer, x2, g_post, mod, seq, tm):
    t, kd = a.shape
    d = w.shape[2]
    bps = seq // tm
    return pl.pallas_call(
        _out_proj_kernel,
        grid=(t // tm,),
        in_specs=[pl.BlockSpec((tm, kd), lambda i: (i, 0)),
                  pl.BlockSpec((None, kd, d), lambda i: (layer, 0, 0), pipeline_mode=pl.Buffered(1)),
                  pl.BlockSpec((tm, d), lambda i: (i, 0)),
                  pl.BlockSpec((1, d), lambda i: (0, 0)),
                  pl.BlockSpec((None, 1, d), lambda i: (i // bps, 0, 2))],
        out_specs=pl.BlockSpec((tm, d), lambda i: (i, 0)),
        out_shape=jax.ShapeDtypeStruct((t, d), F32),
        compiler_params=_cparams("parallel"),
        name="out_proj",
    )(a, w, x2, g_post, mod)


def _fox_gate_kernel(f_ref, fb_ref, o_ref, carry_ref):
    tm = f_ref.shape[0]

    @pl.when(pl.program_id(1) == 0)
    def _():
        carry_ref[...] = jnp.zeros_like(carry_ref)

    x = f_ref[...] + fb_ref[...]
    logf = (jnp.minimum(x, 0.0) - jnp.log1p(jnp.exp(-jnp.abs(x)))) * LOG2E
    r = lax.broadcasted_iota(jnp.int32, (tm, tm), 0)
    c = lax.broadcasted_iota(jnp.int32, (tm, tm), 1)
    tri = (r >= c).astype(F32)
    cum = _dot_f32(tri, logf) + carry_ref[0:1, :]
    o_ref[...] = cum
    carry_ref[0:1, :] = cum[tm - 1:tm, :]


def _fox_gate_cumsum(f_logit, f_bias_row, nb, seq, tm=256):
    return pl.pallas_call(
        _fox_gate_kernel,
        grid=(nb, seq // tm),
        in_specs=[pl.BlockSpec((None, tm, HEAD), lambda b, i: (b, i, 0)),
                  pl.BlockSpec((1, HEAD), lambda b, i: (0, 0))],
        out_specs=pl.BlockSpec((None, tm, HEAD), lambda b, i: (b, i, 0)),
        out_shape=jax.ShapeDtypeStruct((nb, seq, HEAD), F32),
        scratch_shapes=[pltpu.VMEM((SUBLANES, HEAD), F32)],
        compiler_params=_cparams("parallel", "arbitrary"),
        name="fox_gate_cumsum",
    )(f_logit.reshape(nb, seq, HEAD), f_bias_row)


def _attn_kernel(q_ref, k_ref, v_ref, og_ref, fk_ref, o_ref,
                 s0_ref, s1_ref, m_ref, l_ref, acc_ref, *, tq):
    h = pl.program_id(1)
    qi = pl.program_id(2)
    m_ref[...] = jnp.full(m_ref.shape, -jnp.inf, F32)
    l_ref[...] = jnp.zeros(l_ref.shape, F32)
    acc_ref[...] = jnp.zeros(acc_ref.shape, F32)

    def scores(s_ref, ki):
        ks = pl.multiple_of(ki * tq, tq)
        s_ref[...] = _dot_nt(q_ref[...], k_ref[pl.ds(ks, tq), :]) - fk_ref[pl.ds(h, 1), pl.ds(ks, tq)]

    def update(s_ref, ki, masked):
        ks = pl.multiple_of(ki * tq, tq)
        s = s_ref[...]
        if masked:
            row = lax.broadcasted_iota(jnp.int32, (tq, tq), 0)
            col = lax.broadcasted_iota(jnp.int32, (tq, tq), 1)
            s = jnp.where(col <= row, s, -jnp.inf)
        m_old = m_ref[...]
        m_new = jnp.maximum(m_old, jnp.max(s, axis=-1, keepdims=True))
        p = jnp.exp2(s - m_new)
        alpha = jnp.exp2(m_old - m_new)
        l_ref[...] = alpha * l_ref[...] + jnp.sum(p, axis=-1, keepdims=True)
        m_ref[...] = m_new
        acc_ref[...] = alpha * acc_ref[...] + _dot(p.astype(BF16), v_ref[pl.ds(ks, tq), :])

    scores(s0_ref, 0)

    def pair(pi, carry):
        scores(s1_ref, 2 * pi + 1)
        update(s0_ref, 2 * pi, False)
        scores(s0_ref, 2 * pi + 2)
        update(s1_ref, 2 * pi + 1, False)
        return carry

    lax.fori_loop(0, qi // 2, pair, 0)

    @pl.when(qi % 2 == 1)
    def _():
        scores(s1_ref, qi)
        update(s0_ref, qi - 1, False)
        update(s1_ref, qi, True)

    @pl.when(qi % 2 == 0)
    def _():
        update(s0_ref, qi, True)

    o_ref[...] = (acc_ref[...] / l_ref[...] * jax.nn.sigmoid(og_ref[...].astype(F32))).astype(o_ref.dtype)


def _split3(x):
    hi = x.astype(BF16).astype(F32)
    mid = (x - hi).astype(BF16).astype(F32)
    return hi, mid, x - hi - mid


def _attn_fixed_shift_kernel(q_ref, k_ref, v_ref, og_ref, fc_ref, c_ref, o_ref,
                             ka_ref, va_ref, qa_ref, p0_ref, p1_ref, acc_ref, *, tq):
    h = pl.program_id(1)
    step = pl.program_id(2)
    seq = k_ref.shape[0]
    nq = seq // tq
    q_blocks = (step, nq - 1 - step)

    def block_rows(i):
        return pl.ds(pl.multiple_of(i * tq, tq), tq)

    def head_col(rows):
        lane = lax.broadcasted_iota(jnp.int32, (tq, HEAD), 1)
        return jnp.sum(jnp.where(lane == h, fc_ref[rows, :], 0.0), axis=-1, keepdims=True)

    def extra_cols(first, second):
        lane = lax.broadcasted_iota(jnp.int32, (tq, HEAD), 1)
        cols = jnp.zeros((tq, HEAD), F32)
        for n, term in enumerate(first + second):
            cols = jnp.where(lane == n, term, cols)
        return cols.astype(BF16)

    ones3 = (1.0, 1.0, 1.0)

    @pl.when(step == 0)
    def _():
        def build(i, carry):
            rows = block_rows(i)
            hi, mid, lo = _split3(-head_col(rows))
            ka_ref[rows, 0:HEAD] = k_ref[rows, :]
            ka_ref[rows, HEAD:2 * HEAD] = extra_cols((hi, mid, lo), ones3)
            va_ref[rows, 0:HEAD] = v_ref[rows, :]
            va_ref[rows, HEAD:2 * HEAD] = jnp.ones((tq, HEAD), BF16)
            return carry
        lax.fori_loop(0, nq, build, 0)

    for n, qb in enumerate(q_blocks):
        rows = block_rows(qb)
        qa_ref[n] = jnp.concatenate(
            [q_ref[rows, :], extra_cols(ones3, _split3(head_col(rows) - c_ref[0:1, 0:1]))], axis=1)
    acc_ref[...] = jnp.zeros(acc_ref.shape, F32)

    def probs(p_ref, which, kb, masked):
        s = _dot_nt(qa_ref[which], ka_ref[block_rows(kb), :])
        if masked:
            row = lax.broadcasted_iota(jnp.int32, (tq, tq), 0)
            col = lax.broadcasted_iota(jnp.int32, (tq, tq), 1)
            s = jnp.where(col <= row, s, -jnp.inf)
        p_ref[...] = jnp.exp2(s).astype(BF16)

    def accumulate(p_ref, which, kb):
        acc_ref[which] += _dot(p_ref[...], va_ref[block_rows(kb), :])

    items = [(0, q_blocks[0], True), (1, q_blocks[1], True)]
    for n in range(nq - 1):
        which = (n >= q_blocks[0]).astype(jnp.int32)
        items.append((which, n - which * q_blocks[0], False))
    p_refs = (p0_ref, p1_ref)
    for n, (which, kb, masked) in enumerate(items):
        probs(p_refs[n % 2], which, kb, masked)
        if n > 0:
            accumulate(p_refs[(n - 1) % 2], *items[n - 1][:2])
    accumulate(p_refs[(len(items) - 1) % 2], *items[-1][:2])

    for n, qb in enumerate(q_blocks):
        rows = block_rows(qb)
        acc = acc_ref[n]
        o_ref[rows, :] = (acc[:, 0:HEAD] / acc[:, HEAD:HEAD + 1]
                          * jax.nn.sigmoid(og_ref[rows, :].astype(F32))).astype(o_ref.dtype)


def _fox_attention_fixed_shift(qk3, v3, og3, fcum, c_row, n_heads, tq=512):
    nb, seq, _ = qk3.shape
    nq = seq // tq
    assert nq % 2 == 0
    full = lambda off: pl.BlockSpec((None, seq, HEAD), lambda b, h, i: (b, 0, off + h))
    return pl.pallas_call(
        functools.partial(_attn_fixed_shift_kernel, tq=tq),
        grid=(nb, n_heads, nq // 2),
        in_specs=[full(0), full(n_heads), full(0), full(0),
                  pl.BlockSpec((None, seq, HEAD), lambda b, h, i: (b, 0, 0)),
                  pl.BlockSpec((1, HEAD), lambda b, h, i: (0, 0))],
        out_specs=full(0),
        out_shape=jax.ShapeDtypeStruct((nb, seq, n_heads * HEAD), BF16),
        scratch_shapes=[pltpu.VMEM((seq, 2 * HEAD), BF16), pltpu.VMEM((seq, 2 * HEAD), BF16),
                        pltpu.VMEM((2, tq, 2 * HEAD), BF16),
                        pltpu.VMEM((tq, tq), BF16), pltpu.VMEM((tq, tq), BF16),
                        pltpu.VMEM((2, tq, 2 * HEAD), F32)],
        compiler_params=_cparams("parallel", "parallel", "arbitrary"),
        name="fox_attention_fixed_shift",
    )(qk3, qk3, v3, og3, fcum, c_row)


ITEM_STRIDE = 32
LIST_UNROLL = 4


def _attn_block_list_kernel(items_ref, count_ref, q_ref, k_ref, v_ref, og_ref, fc_ref, c_ref, o_ref,
                            ka_ref, va_ref, qa_ref, p0_ref, p1_ref, acc_ref, *, tq):
    h = pl.program_id(1)
    bh = pl.program_id(0) * pl.num_programs(1) + h
    seq = k_ref.shape[0]
    nq = seq // tq

    def block_rows(i):
        return pl.ds(pl.multiple_of(i * tq, tq), tq)

    lane1 = lax.broadcasted_iota(jnp.int32, (1, HEAD), 1)
    c_hi, c_mid, c_lo = _split3(c_ref[...])
    k_fixed = jnp.where((lane1 >= 3) & (lane1 < 6), 1.0,
                        jnp.where(lane1 == 6, -c_hi, jnp.where(lane1 == 7, -c_mid,
                                                               jnp.where(lane1 == 8, -c_lo, 0.0))))
    q_fixed = jnp.where((lane1 < 3) | ((lane1 >= 6) & (lane1 < 9)), 1.0, 0.0)

    def build(i, carry):
        rows = block_rows(i)
        lane = lax.broadcasted_iota(jnp.int32, (tq, HEAD), 1)
        f = jnp.sum(jnp.where(lane == h, fc_ref[rows, :], 0.0), axis=-1, keepdims=True)
        hi, mid, lo = _split3(f)
        k_cols = jnp.where(lane == 0, -hi, jnp.where(lane == 1, -mid, jnp.where(lane == 2, -lo, k_fixed)))
        q_cols = jnp.where(lane == 3, hi, jnp.where(lane == 4, mid, jnp.where(lane == 5, lo, q_fixed)))
        ka_ref[rows, 0:HEAD] = k_ref[rows, :]
        ka_ref[rows, HEAD:2 * HEAD] = k_cols.astype(BF16)
        va_ref[rows, 0:HEAD] = v_ref[rows, :]
        va_ref[rows, HEAD:2 * HEAD] = jnp.ones((tq, HEAD), BF16)
        qa_ref[rows, 0:HEAD] = q_ref[rows, :]
        qa_ref[rows, HEAD:2 * HEAD] = q_cols.astype(BF16)
        return carry

    lax.fori_loop(0, nq, build, 0)
    acc_ref[nq] = jnp.zeros(acc_ref.shape[1:], F32)

    def probs(p_ref, qb, kb, masked):
        qb = jnp.minimum(qb, nq - 1)
        s = _dot_nt(qa_ref[block_rows(qb), :], ka_ref[block_rows(kb), :])
        if masked:
            row = lax.broadcasted_iota(jnp.int32, (tq, tq), 0)
            col = lax.broadcasted_iota(jnp.int32, (tq, tq), 1)
            s = jnp.where(col <= row, s, -jnp.inf)
        p_ref[...] = jnp.exp2(s).astype(BF16)

    def product(p_ref, kb):
        return _dot(p_ref[...], va_ref[block_rows(kb), :])

    p_refs = (p0_ref, p1_ref)
    for i in range(nq):
        probs(p_refs[i % 2], i, i, True)
        if i > 0:
            acc_ref[i - 1] = product(p_refs[(i - 1) % 2], i - 1)
    acc_ref[nq - 1] = product(p_refs[(nq - 1) % 2], nq - 1)

    def item(n):
        code = items_ref[bh, n]
        return code // ITEM_STRIDE, code % ITEM_STRIDE

    def accumulate(p_ref, n):
        qb, kb = item(n)
        acc_ref[qb] += product(p_ref, kb)

    probs(p0_ref, *item(0), False)

    def group(m, carry):
        for u in range(LIST_UNROLL):
            n = LIST_UNROLL * m + u
            probs(p_refs[(u + 1) % 2], *item(n + 1), False)
            accumulate(p_refs[u % 2], n)
        return carry

    lax.fori_loop(0, count_ref[bh] // LIST_UNROLL, group, 0)

    def finish(i, carry):
        rows = block_rows(i)
        acc = acc_ref[i]
        o_ref[rows, :] = (acc[:, 0:HEAD] / acc[:, HEAD:HEAD + 1]
                          * jax.nn.sigmoid(og_ref[rows, :].astype(F32))).astype(o_ref.dtype)
        return carry

    lax.fori_loop(0, nq, finish, 0)


def _fox_attention_block_list(qk3, v3, og3, fcum, c_row, items, counts, n_heads, tq=512):
    nb, seq, _ = qk3.shape
    nq = seq // tq
    full = lambda off: pl.BlockSpec((None, seq, HEAD), lambda b, h, *_: (b, 0, off + h))
    grid_spec = pltpu.PrefetchScalarGridSpec(
        num_scalar_prefetch=2,
        grid=(nb, n_heads),
        in_specs=[full(0), full(n_heads), full(0), full(0),
                  pl.BlockSpec((None, seq, HEAD), lambda b, h, *_: (b, 0, 0), pipeline_mode=pl.Buffered(1)),
                  pl.BlockSpec((1, HEAD), lambda b, h, *_: (0, 0))],
        out_specs=full(0),
        scratch_shapes=[pltpu.VMEM((seq, 2 * HEAD), BF16), pltpu.VMEM((seq, 2 * HEAD), BF16),
                        pltpu.VMEM((seq, 2 * HEAD), BF16),
                        pltpu.VMEM((tq, tq), BF16), pltpu.VMEM((tq, tq), BF16),
                        pltpu.VMEM((nq + 1, tq, 2 * HEAD), F32)])
    return pl.pallas_call(
        functools.partial(_attn_block_list_kernel, tq=tq),
        grid_spec=grid_spec,
        out_shape=jax.ShapeDtypeStruct((nb, seq, n_heads * HEAD), BF16),
        compiler_params=_cparams("parallel", "arbitrary"),
        name="fox_attention_block_list",
    )(items, counts, qk3, qk3, v3, og3, fcum, c_row)


MAX_SHIFT_GAP = 100.0
SKIP_LOG2 = 136.0
LIST_MAX_FRACTION = 0.75


def _needed_blocks(fcum, n_heads, tq):
    nb, seq, _ = fcum.shape
    nq = seq // tq
    f_first = fcum[:, 0::tq, :n_heads]
    f_last = fcum[:, tq - 1::tq, :n_heads]
    bound = f_first[:, :, None, :] - f_last[:, None, :, :]
    i = jnp.arange(nq, dtype=jnp.int32)[:, None]
    j = jnp.arange(nq, dtype=jnp.int32)[None, :]
    need = (bound > -SKIP_LOG2) & (j < i)[None, :, :, None]
    need = jnp.transpose(need, (0, 3, 1, 2)).reshape(nb * n_heads, nq * nq)
    pad_code = nq * ITEM_STRIDE
    codes = jnp.where(need, (i * ITEM_STRIDE + j).reshape(1, nq * nq), pad_code)
    codes = jnp.sort(codes, axis=-1)
    codes = jnp.concatenate([codes, jnp.full((nb * n_heads, LIST_UNROLL), pad_code, jnp.int32)], axis=-1)
    counts = jnp.sum(need, axis=-1).astype(jnp.int32)
    return codes.astype(jnp.int32), counts + (-counts) % LIST_UNROLL, jnp.sum(counts)


def _fox_attention(qk3, v3, og3, fcum, gn, n_heads, tq=512):
    nb, seq, _ = qk3.shape
    nq = seq // tq
    qd = n_heads * HEAD
    c = (HEAD * 1.01) * jnp.max(jnp.abs(gn[:, :qd])) * jnp.max(jnp.abs(gn[:, qd:]))
    c_row = jnp.full((1, HEAD), c, F32)
    items, counts, total = _needed_blocks(fcum, n_heads, tq)
    all_pairs = nb * n_heads * (nq * (nq - 1) // 2)

    def block_list(qk3, v3, og3, fcum, c_row, items, counts):
        return _fox_attention_block_list(qk3, v3, og3, fcum, c_row, items, counts, n_heads, tq)

    def full_stream(qk3, v3, og3, fcum, c_row, items, counts):
        return _fox_attention_fixed_shift(qk3, v3, og3, fcum, c_row, n_heads, tq)

    def running_max(qk3, v3, og3, fcum, c_row, items, counts):
        fk = jnp.transpose(fcum[:, :, :n_heads], (0, 2, 1))
        return _fox_attention_running_max(qk3, v3, og3, fk, n_heads, tq)

    branch = jnp.where(2.0 * c > MAX_SHIFT_GAP, 2,
                       jnp.where(total <= LIST_MAX_FRACTION * all_pairs, 0, 1)).astype(jnp.int32)
    return lax.switch(branch, (block_list, full_stream, running_max), qk3, v3, og3, fcum, c_row, items, counts)


def _fox_attention_running_max(qk3, v3, og3, fk, n_heads, tq=512):
    nb, seq, _ = qk3.shape
    return pl.pallas_call(
        functools.partial(_attn_kernel, tq=tq),
        grid=(nb, n_heads, seq // tq),
        in_specs=[pl.BlockSpec((None, tq, HEAD), lambda b, h, i: (b, i, h)),
                  pl.BlockSpec((None, seq, HEAD), lambda b, h, i: (b, 0, n_heads + h)),
                  pl.BlockSpec((None, seq, HEAD), lambda b, h, i: (b, 0, h)),
                  pl.BlockSpec((None, tq, HEAD), lambda b, h, i: (b, i, h)),
                  pl.BlockSpec((None, n_heads, seq), lambda b, h, i: (b, 0, 0))],
        out_specs=pl.BlockSpec((None, tq, HEAD), lambda b, h, i: (b, i, h)),
        out_shape=jax.ShapeDtypeStruct((nb, seq, n_heads * HEAD), BF16),
        scratch_shapes=[pltpu.VMEM((tq, tq), F32), pltpu.VMEM((tq, tq), F32),
                        pltpu.VMEM((tq, 1), F32), pltpu.VMEM((tq, 1), F32), pltpu.VMEM((tq, HEAD), F32)],
        compiler_params=_cparams("parallel", "parallel", "arbitrary"),
        name="fox_attention",
    )(qk3, qk3, v3, og3, fk)


def _gdn_gate_kernel(ab_ref, alog_ref, dt_ref, o_ref, *, n_heads):
    tm = ab_ref.shape[0]
    x = ab_ref[...]
    lane = lax.broadcasted_iota(jnp.int32, x.shape, 1)
    g = -jnp.exp(alog_ref[...]) * _softplus(x + dt_ref[...])
    r = lax.broadcasted_iota(jnp.int32, (tm, tm), 0)
    c = lax.broadcasted_iota(jnp.int32, (tm, tm), 1)
    tri = ((r >= c) & ((r // CHUNK) == (c // CHUNK))).astype(F32)
    gc = _dot_f32(tri, jnp.where(lane < n_heads, g, 0.0))
    o_ref[...] = jnp.where(lane < n_heads, gc, jax.nn.sigmoid(x))


def _gdn_gates(ab, alog_row, dt_row, n_heads, tm=256):
    t = ab.shape[0]
    return pl.pallas_call(
        functools.partial(_gdn_gate_kernel, n_heads=n_heads),
        grid=(t // tm,),
        in_specs=[pl.BlockSpec((tm, HEAD), lambda i: (i, 0)),
                  pl.BlockSpec((1, HEAD), lambda i: (0, 0)),
                  pl.BlockSpec((1, HEAD), lambda i: (0, 0))],
        out_specs=pl.BlockSpec((tm, HEAD), lambda i: (i, 0)),
        out_shape=jax.ShapeDtypeStruct((t, HEAD), F32),
        compiler_params=_cparams("parallel"),
        name="gdn_gates",
    )(ab, alog_row, dt_row)


def _block_diag2(p, half):
    return jnp.concatenate([jnp.where(half, 0.0, p), jnp.where(half, p, 0.0)], axis=0)


def _gdn_chunk_kernel(q_ref, k_ref, v_ref, gb_ref, grow_ref, u_ref, w_ref, qd_ref, kd_ref, qk_ref, egl_ref,
                      *, n_vheads, chunks):
    hk = pl.program_id(1)
    lane = lax.broadcasted_iota(jnp.int32, (CHUNK, HEAD), 1)
    row = lax.broadcasted_iota(jnp.int32, (CHUNK, HEAD), 0)
    col = lane & (CHUNK - 1)
    half = lane >= CHUNK
    lower = row >= col
    strict = row > col
    eye = (row == col).astype(F32)
    level_masks = []
    for lvl in range(6):
        level_masks.append((((row >> lvl) & 1) == 1) & (((col >> lvl) & 1) == 0)
                           & ((row >> (lvl + 1)) == (col >> (lvl + 1))))
    zeros = jnp.zeros((CHUNK, HEAD), F32)

    def pick(gb, idx):
        return jnp.sum(jnp.where(lane == idx, gb, 0.0), axis=-1, keepdims=True)

    sls = [slice(c * CHUNK, (c + 1) * CHUNK) for c in range(chunks)]
    gcs, bes, a_s, xs = [], [], [], []
    for c, sl in enumerate(sls):
        q = q_ref[sl, :]
        k = k_ref[sl, :]
        gb = gb_ref[sl, :]
        gc0, gc1 = pick(gb, 2 * hk), pick(gb, 2 * hk + 1)
        be0, be1 = pick(gb, n_vheads + 2 * hk), pick(gb, n_vheads + 2 * hk + 1)
        gcol = jnp.where(half, gc1, gc0)
        bcol = jnp.where(half, be1, be0)
        decay = jnp.exp(jnp.where(lower, gcol - grow_ref[c:c + 1, :], -jnp.inf))
        qkk = _dot_nt(jnp.concatenate([q, k], axis=0), jnp.concatenate([k, k], axis=0))
        a = jnp.where(strict, qkk[CHUNK:, :] * decay * bcol, 0.0)
        qk_ref[sl, :] = (qkk[:CHUNK, :] * decay).astype(qk_ref.dtype)
        gcs.append((gc0, gc1))
        bes.append((be0, be1))
        a_s.append(a)
        xs.append(eye - jnp.where(level_masks[0], a, 0.0))

    for lvl in range(1, 6):
        ys = [_dot(xs[c].astype(BF16), _block_diag2(jnp.where(level_masks[lvl], a_s[c], 0.0), half).astype(BF16))
              for c in range(chunks)]
        xs = [xs[c] - _dot(ys[c].astype(BF16), _block_diag2(xs[c], half).astype(BF16)) for c in range(chunks)]

    for c, sl in enumerate(sls):
        (gc0, gc1), (be0, be1) = gcs[c], bes[c]
        kf = k_ref[sl, :].astype(F32)
        qf = q_ref[sl, :].astype(F32)
        v = v_ref[sl, :].astype(F32)
        eg0, eg1 = jnp.exp(gc0), jnp.exp(gc1)
        rhs = jnp.concatenate([
            jnp.concatenate([v[:, :HEAD] * be0, kf * (be0 * eg0), zeros, zeros], axis=1),
            jnp.concatenate([zeros, zeros, v[:, HEAD:] * be1, kf * (be1 * eg1)], axis=1)], axis=0)
        sol = _dot(xs[c].astype(BF16), rhs.astype(BF16))
        u_ref[sl, :] = jnp.concatenate([sol[:, 0:HEAD], sol[:, 2 * HEAD:3 * HEAD]], axis=1).astype(u_ref.dtype)
        w_ref[sl, :] = jnp.concatenate([sol[:, HEAD:2 * HEAD], sol[:, 3 * HEAD:]], axis=1).astype(w_ref.dtype)
        qd_ref[sl, :] = jnp.concatenate([qf * eg0, qf * eg1], axis=1).astype(qd_ref.dtype)
        gl0, gl1 = gc0[CHUNK - 1:CHUNK, :], gc1[CHUNK - 1:CHUNK, :]
        kd_ref[sl, :] = jnp.concatenate([kf * jnp.exp(gl0 - gc0), kf * jnp.exp(gl1 - gc1)],
                                        axis=1).astype(kd_ref.dtype)
        egl_ref[0, c:c + 1, :] = jnp.broadcast_to(jnp.exp(gl0), (1, HEAD))
        egl_ref[1, c:c + 1, :] = jnp.broadcast_to(jnp.exp(gl1), (1, HEAD))


def _gdn_chunks(qk3, v3, gb3, grow, n_kheads, n_vheads, tm=1024):
    nb, seq, _ = qk3.shape
    chunks = tm // CHUNK
    vd = n_vheads * HEAD
    big = jax.ShapeDtypeStruct((nb, seq, vd), BF16)
    pair_spec = pl.BlockSpec((None, tm, 2 * HEAD), lambda b, h, i: (b, i, h))
    return pl.pallas_call(
        functools.partial(_gdn_chunk_kernel, n_vheads=n_vheads, chunks=chunks),
        grid=(nb, n_kheads, seq // tm),
        in_specs=[pl.BlockSpec((None, tm, HEAD), lambda b, h, i: (b, i, h)),
                  pl.BlockSpec((None, tm, HEAD), lambda b, h, i: (b, i, n_kheads + h)),
                  pair_spec,
                  pl.BlockSpec((None, tm, HEAD), lambda b, h, i: (b, i, 0)),
                  pl.BlockSpec((None, None, chunks, HEAD), lambda b, h, i: (b, h, i, 0))],
        out_specs=[pair_spec, pair_spec, pair_spec, pair_spec,
                   pl.BlockSpec((None, tm, HEAD), lambda b, h, i: (b, i, h)),
                   pl.BlockSpec((None, 2, chunks, HEAD), lambda b, h, i: (b, h, i, 0))],
        out_shape=[big, big, big, big,
                   jax.ShapeDtypeStruct((nb, seq, n_kheads * HEAD), BF16),
                   jax.ShapeDtypeStruct((nb, n_vheads, seq // CHUNK, HEAD), F32)],
        compiler_params=_cparams("parallel", "parallel", "parallel"),
        name="gdn_chunks",
    )(qk3, qk3, v3, gb3, grow)


def _gdn_scan_kernel(u_ref, w_ref, qd_ref, kd_ref, qk_ref, egl_ref, z_ref, gn_ref, o_ref, s_ref,
                     *, heads, chunks):
    @pl.when(pl.program_id(2) == 0)
    def _():
        s_ref[...] = jnp.zeros_like(s_ref)

    zeros = jnp.zeros((CHUNK, HEAD), F32)
    gn = gn_ref[...]
    cols = [slice(g * HEAD, (g + 1) * HEAD) for g in range(heads)]
    for c in range(chunks):
        sl = slice(c * CHUNK, (c + 1) * CHUNK)
        rs = [_dot(jnp.concatenate([w_ref[sl, cols[g]], qd_ref[sl, cols[g]]], axis=0), s_ref[g].astype(BF16))
              for g in range(heads)]
        vnew = [u_ref[sl, cols[g]].astype(F32) - rs[g][:CHUNK, :] for g in range(heads)]
        intra = []
        for pr in range(heads // 2):
            v2 = jnp.concatenate([jnp.concatenate([vnew[2 * pr], zeros], axis=1),
                                  jnp.concatenate([zeros, vnew[2 * pr + 1]], axis=1)], axis=0).astype(BF16)
            intra.append(_dot(qk_ref[sl, pr * HEAD:(pr + 1) * HEAD], v2))
        for g in range(heads):
            s_ref[g] = s_ref[g] * egl_ref[g, c:c + 1, :] + _dot_tn(kd_ref[sl, cols[g]], vnew[g].astype(BF16))
        for g in range(heads):
            o = rs[g][CHUNK:, :] + intra[g // 2][:, (g % 2) * HEAD:(g % 2 + 1) * HEAD]
            ms = jnp.mean(o * o, axis=-1, keepdims=True)
            z = z_ref[sl, cols[g]].astype(F32)
            o_ref[sl, cols[g]] = (o * lax.rsqrt(ms + EPS) * gn * (z * jax.nn.sigmoid(z))).astype(o_ref.dtype)


def _gdn_scan(u, w, qd, kd, qkm, egl, z3, gn_row, n_vheads, heads=16, tm=512):
    nb, seq, vd = u.shape
    chunks = tm // CHUNK
    wide = pl.BlockSpec((None, tm, heads * HEAD), lambda b, g, i: (b, i, g))
    return pl.pallas_call(
        functools.partial(_gdn_scan_kernel, heads=heads, chunks=chunks),
        grid=(nb, n_vheads // heads, seq // tm),
        in_specs=[wide, wide, wide, wide,
                  pl.BlockSpec((None, tm, heads // 2 * HEAD), lambda b, g, i: (b, i, g)),
                  pl.BlockSpec((None, heads, chunks, HEAD), lambda b, g, i: (b, g, i, 0)),
                  wide,
                  pl.BlockSpec((1, HEAD), lambda b, g, i: (0, 0))],
        out_specs=wide,
        out_shape=jax.ShapeDtypeStruct((nb, seq, vd), BF16),
        scratch_shapes=[pltpu.VMEM((heads, HEAD, HEAD), F32)],
        compiler_params=_cparams("parallel", "parallel", "arbitrary"),
        name="gdn_scan",
    )(u, w, qd, kd, qkm, egl, z3, gn_row)


def _pad_cols(w, n):
    return jnp.pad(w, ((0, 0), (0, n - w.shape[1])))


def _row(v, n=HEAD):
    return jnp.pad(v.astype(F32), (0, n - v.shape[0])).reshape(1, n)


def _fox_layer(x2, nb, seq, g_pre, g_post, mod, w_in, f_bias, q_norm, k_norm, w_o, layer):
    d = x2.shape[1]
    nh = d // HEAD
    qd = nh * HEAD
    w_small = _pad_cols(w_in[:, 3 * qd:3 * qd + nh], HEAD)
    w_gate = w_in[:, 3 * qd + nh:]
    gn = jnp.concatenate([jnp.tile(q_norm.astype(F32) * (HEAD ** -0.5 * LOG2E), nh),
                          jnp.tile(k_norm.astype(F32), nh)]).reshape(1, 2 * qd)
    qk, f_logit, h = _fox_qk_proj(x2, g_pre, mod, w_in, gn, w_small, seq)
    v = _plain_proj(h, w_in, 2 * qd, qd)
    og = _plain_proj(h, w_gate, 0, qd)
    fcum = _fox_gate_cumsum(f_logit, _row(f_bias), nb, seq)
    attn = _fox_attention(qk.reshape(nb, seq, 2 * qd), v.reshape(nb, seq, qd), og.reshape(nb, seq, qd),
                          fcum, gn, nh)
    return _out_proj(attn.reshape(nb * seq, qd), w_o, layer, x2, g_post, mod, seq, tm=512)


def _gdn_layer(x2, nb, seq, g_pre, g_post, mod, w_in, conv_w, a_log, dt_bias, out_norm, w_o, layer):
    d = x2.shape[1]
    nk = d // HEAD
    nv = 2 * nk
    kd, vd = nk * HEAD, nv * HEAD
    n_main = 2 * kd + 2 * vd
    w_small = _pad_cols(w_in[:, n_main:], HEAD)
    conv_w = conv_w.astype(F32)
    qscale = jnp.concatenate([jnp.full((kd,), HEAD ** -0.5, F32), jnp.ones((kd,), F32)]).reshape(1, 2 * kd)
    qk, ab, h = _gdn_qk_proj(x2, g_pre, mod, w_in, conv_w, qscale, w_small, seq)
    v = _gdn_v_proj(h, w_in, conv_w, 2 * kd, vd, seq)
    z = _plain_proj(h, w_in, 2 * kd + vd, vd)
    gb = _gdn_gates(ab, _row(a_log), _row(dt_bias), nv)
    n_chunks = seq // CHUNK
    grow = gb[:, :nv].reshape(nb, n_chunks, CHUNK, nk, 2)
    grow = jnp.transpose(grow, (0, 3, 1, 4, 2)).reshape(nb, nk, n_chunks, 2 * CHUNK)
    u, w, qdec, kdec, qkm, egl = _gdn_chunks(qk.reshape(nb, seq, 2 * kd), v.reshape(nb, seq, vd),
                                             gb.reshape(nb, seq, HEAD), grow, nk, nv)
    a = _gdn_scan(u, w, qdec, kdec, qkm, egl, z.reshape(nb, seq, vd), _row(out_norm), nv)
    return _out_proj(a.reshape(nb * seq, vd), w_o, layer, x2, g_post, mod, seq, tm=512)


def _ffn_layer(x2, seq, g_pre, g_post, mod, w_up, conv_w, conv_b, w_down, layer):
    u = _ffn_up(x2, g_pre, mod, w_up, layer, conv_w.astype(F32), conv_b.reshape(1, -1).astype(F32), seq)
    return _out_proj(u, w_down, layer, x2, g_post, mod, seq, tm=256)


def kernel(x, c, ada_w, ada_b, norm_g, fox_w_in, fox_f_bias, fox_q_norm, fox_k_norm, fox_w_o,
           gdn_w_in, gdn_conv_w, gdn_a_log, gdn_dt_bias, gdn_out_norm, gdn_w_o,
           ffn_w_up, ffn_conv_w, ffn_conv_b, ffn_w_down):
    nb, seq, d = x.shape
    depth = ada_w.shape[0]
    mods = _ada_mods(c, ada_w, ada_b)
    fox_w_in, fox_w_o, gdn_w_in, gdn_w_o, ffn_w_up, ffn_w_down = (
        w.astype(BF16) for w in (fox_w_in, fox_w_o, gdn_w_in, gdn_w_o, ffn_w_up, ffn_w_down))
    x2 = x.reshape(nb * seq, d)
    for i in range(depth):
        g = [norm_g[i, r].reshape(1, d).astype(F32) for r in range(4)]
        j = i // 2
        if i % 2 == 0:
            x2 = _fox_layer(x2, nb, seq, g[0], g[1], mods[2 * i], fox_w_in[j], fox_f_bias[j],
                            fox_q_norm[j], fox_k_norm[j], fox_w_o, j)
        else:
            x2 = _gdn_layer(x2, nb, seq, g[0], g[1], mods[2 * i], gdn_w_in[j], gdn_conv_w[j],
                            gdn_a_log[j], gdn_dt_bias[j], gdn_out_norm[j], gdn_w_o, j)
        x2 = _ffn_layer(x2, seq, g[2], g[3], mods[2 * i + 1], ffn_w_up, ffn_conv_w[i],
                        ffn_conv_b[i], ffn_w_down, i)
    return x2.reshape(nb, seq, d)
```

```python
import functools

import jax
import jax.numpy as jnp
from jax import lax
from jax.experimental import pallas as pl
from jax.experimental.pallas import tpu as pltpu

F32 = jnp.float32
BF16 = jnp.bfloat16
EPS = 1e-6
HEAD = 128
CHUNK = 64
SUBLANES = 8
MXU_N = 256
VMEM_LIMIT = 56 * 1024 * 1024
INV_SQRT2 = 0.7071067811865476
LOG2E = 1.4426950408889634


def _cparams(*sem):
    return pltpu.CompilerParams(dimension_semantics=sem, vmem_limit_bytes=VMEM_LIMIT)


def _dot(a, b):
    return jnp.dot(a, b, preferred_element_type=F32)


def _dot_nt(a, b):
    return lax.dot_general(a, b, (((1,), (1,)), ((), ())), preferred_element_type=F32)


def _dot_tn(a, b):
    return lax.dot_general(a, b, (((0,), (0,)), ((), ())), preferred_element_type=F32)


def _split3(x):
    hi = x.astype(BF16).astype(F32)
    mid = (x - hi).astype(BF16).astype(F32)
    return hi, mid, x - hi - mid


def _mask_dot(mask, x):
    m = jnp.where(mask, 1.0, 0.0).astype(BF16)
    hi, mid, lo = _split3(x)
    return _dot(m, hi.astype(BF16)) + _dot(m, mid.astype(BF16)) + _dot(m, lo.astype(BF16))


def _softplus(x):
    return jnp.maximum(x, 0.0) + jnp.log1p(jnp.exp(-jnp.abs(x)))


def _ada_kernel(c_ref, w_ref, b_ref, o_ref):
    o_ref[0] = jnp.zeros(o_ref.shape[1:], F32)
    for b in range(c_ref.shape[0]):
        c = c_ref[b]
        row = jnp.sum(w_ref[0] * (c * jax.nn.sigmoid(c)), axis=0, keepdims=True)
        o_ref[0, b:b + 1, :] = row + b_ref[0]


def _ada_mods(c, ada_w, ada_b):
    nb, d = c.shape
    assert nb <= SUBLANES
    n = ada_w.shape[0] * ada_w.shape[1]
    w = ada_w.reshape(n, d, 3 * d)
    b = ada_b.reshape(n, 1, 3 * d)
    tn = 1024
    out = pl.pallas_call(
        _ada_kernel,
        grid=(n, 3 * d // tn),
        in_specs=[pl.BlockSpec((nb, d, 1), lambda s, j: (0, 0, 0)),
                  pl.BlockSpec((1, d, tn), lambda s, j: (s, 0, j)),
                  pl.BlockSpec((1, 1, tn), lambda s, j: (s, 0, j))],
        out_specs=pl.BlockSpec((1, SUBLANES, tn), lambda s, j: (s, 0, j)),
        out_shape=jax.ShapeDtypeStruct((n, SUBLANES, 3 * d), F32),
        compiler_params=_cparams("parallel", "parallel"),
        name="ada_mods",
    )(c.reshape(nb, d, 1), w, b)
    return out[:, :nb].reshape(n, nb, 1, 3 * d)


NORM_ROWS = 32


def _norm_mod(h_ref, x_ref, g_ref, sh_ref, sc_ref):
    gain = g_ref[...] * (1.0 + sc_ref[...])
    shift = sh_ref[...]

    def rows_pass(r, carry):
        rows = pl.ds(pl.multiple_of(r * NORM_ROWS, NORM_ROWS), NORM_ROWS)
        x = x_ref[rows, :]
        ms = jnp.mean(x * x, axis=-1, keepdims=True)
        h_ref[rows, :] = (x * lax.rsqrt(ms + EPS) * gain + shift).astype(BF16)
        return carry

    lax.fori_loop(0, x_ref.shape[0] // NORM_ROWS, rows_pass, 0, unroll=8)


def _sub_dots(h_ref, w_ref):
    return [_dot(h_ref[...], w_ref[:, c * MXU_N:(c + 1) * MXU_N]) for c in range(w_ref.shape[1] // MXU_N)]


def _head_norm_store(o_ref, c, y, gain_ref, mean):
    for g in range(MXU_N // HEAD):
        sl = slice(c * MXU_N + g * HEAD, c * MXU_N + (g + 1) * HEAD)
        t = y[:, g * HEAD:(g + 1) * HEAD]
        ss = jnp.sum(t * t, axis=-1, keepdims=True)
        if mean:
            ss = ss * (1.0 / HEAD)
        o_ref[:, sl] = (t * lax.rsqrt(ss + EPS) * gain_ref[:, sl]).astype(o_ref.dtype)


def _causal_conv(acc, cw, carry_ref, slot, first):
    tm = acc.shape[0]
    taps = cw.shape[0]
    prev = jnp.where(first, 0.0, carry_ref[slot])
    carry_ref[slot] = acc[tm - SUBLANES:tm, :]
    xe = jnp.concatenate([prev, acc], axis=0)
    ye = xe * cw[0:1, :]
    for k in range(1, taps):
        ye = xe * cw[k:k + 1, :] + pltpu.roll(ye, 1, axis=0)
    return ye[SUBLANES:, :]


def _zero_carry_at_start(carry_ref):
    @pl.when((pl.program_id(0) == 0) & (pl.program_id(1) == 0))
    def _():
        carry_ref[...] = jnp.zeros_like(carry_ref)


def _fox_qk_kernel(x_ref, g_ref, sh_ref, sc_ref, w_ref, gn_ref, ws_ref, o_ref, os_ref, h_ref):
    @pl.when(pl.program_id(1) == 0)
    def _():
        _norm_mod(h_ref, x_ref, g_ref, sh_ref, sc_ref)
        os_ref[...] = _dot(h_ref[...], ws_ref[...])

    for c, acc in enumerate(_sub_dots(h_ref, w_ref)):
        _head_norm_store(o_ref, c, acc, gn_ref, mean=True)


def _plain_proj_kernel(h_ref, w_ref, o_ref):
    o_ref[...] = _dot(h_ref[...], w_ref[...]).astype(o_ref.dtype)


def _gdn_qk_kernel(x_ref, g_ref, sh_ref, sc_ref, w_ref, cw_ref, qs_ref, ws_ref, o_ref, os_ref, h_ref,
                   carry_ref, *, blocks_per_seq):
    i = pl.program_id(0)
    j = pl.program_id(1)
    _zero_carry_at_start(carry_ref)

    @pl.when(j == 0)
    def _():
        _norm_mod(h_ref, x_ref, g_ref, sh_ref, sc_ref)
        os_ref[...] = _dot(h_ref[...], ws_ref[...])

    first = (i % blocks_per_seq) == 0
    accs = _sub_dots(h_ref, w_ref)
    for c, acc in enumerate(accs):
        y = _causal_conv(acc, cw_ref[:, c * MXU_N:(c + 1) * MXU_N], carry_ref, j * len(accs) + c, first)
        _head_norm_store(o_ref, c, y * jax.nn.sigmoid(y), qs_ref, mean=False)


def _gdn_v_kernel(h_ref, w_ref, cw_ref, o_ref, carry_ref, *, blocks_per_seq):
    i = pl.program_id(0)
    j = pl.program_id(1)
    _zero_carry_at_start(carry_ref)
    first = (i % blocks_per_seq) == 0
    accs = _sub_dots(h_ref, w_ref)
    for c, acc in enumerate(accs):
        cs = slice(c * MXU_N, (c + 1) * MXU_N)
        y = _causal_conv(acc, cw_ref[:, cs], carry_ref, j * len(accs) + c, first)
        o_ref[:, cs] = (y * jax.nn.sigmoid(y)).astype(o_ref.dtype)


def _ffn_up_kernel(x_ref, g_ref, sh_ref, sc_ref, wg_ref, wv_ref, cw_ref, cb_ref, o_ref,
                   h_ref, carry_ref, *, blocks_per_seq):
    i = pl.program_id(0)
    j = pl.program_id(1)
    _zero_carry_at_start(carry_ref)

    @pl.when(j == 0)
    def _():
        _norm_mod(h_ref, x_ref, g_ref, sh_ref, sc_ref)

    first = (i % blocks_per_seq) == 0
    n_sub = wg_ref.shape[1] // MXU_N
    pairs = []
    for c in range(n_sub):
        cs = slice(c * MXU_N, (c + 1) * MXU_N)
        pairs.append((_dot(h_ref[...], wg_ref[:, cs]), _dot(h_ref[...], wv_ref[:, cs])))
    for c, (gate, val) in enumerate(pairs):
        cs = slice(c * MXU_N, (c + 1) * MXU_N)
        gate = _causal_conv(gate, cw_ref[:, cs], carry_ref, j * n_sub + c, first) + cb_ref[:, cs]
        act = 0.5 * gate * (1.0 + lax.erf(gate * INV_SQRT2))
        o_ref[:, cs] = (act * val).astype(o_ref.dtype)


def _x_mod_specs(tm, d, bps):
    return [pl.BlockSpec((tm, d), lambda i, j: (i, 0)),
            pl.BlockSpec((1, d), lambda i, j: (0, 0)),
            pl.BlockSpec((None, 1, d), lambda i, j: (i // bps, 0, 0)),
            pl.BlockSpec((None, 1, d), lambda i, j: (i // bps, 0, 1))]


def _carry_scratch(tn, nj):
    return pltpu.VMEM((nj * (tn // MXU_N), SUBLANES, MXU_N), F32)


def _fox_qk_proj(x2, g_pre, mod, w_main, gn, w_small, seq, tm=1024, tn=1024):
    t, d = x2.shape
    n = gn.shape[1]
    return pl.pallas_call(
        _fox_qk_kernel,
        grid=(t // tm, n // tn),
        in_specs=_x_mod_specs(tm, d, seq // tm) + [
            pl.BlockSpec((d, tn), lambda i, j: (0, j)),
            pl.BlockSpec((1, tn), lambda i, j: (0, j)),
            pl.BlockSpec((d, HEAD), lambda i, j: (0, 0))],
        out_specs=[pl.BlockSpec((tm, tn), lambda i, j: (i, j)),
                   pl.BlockSpec((tm, HEAD), lambda i, j: (i, 0)),
                   pl.BlockSpec((tm, d), lambda i, j: (i, 0))],
        out_shape=[jax.ShapeDtypeStruct((t, n), BF16), jax.ShapeDtypeStruct((t, HEAD), F32),
                   jax.ShapeDtypeStruct((t, d), BF16)],
        compiler_params=_cparams("parallel", "arbitrary"),
        name="fox_qk_proj",
    )(x2, g_pre, mod, mod, w_main, gn, w_small)


def _plain_proj(h, w_main, col0, n, tm=1024, tn=1024):
    t, d = h.shape
    off = col0 // tn
    return pl.pallas_call(
        _plain_proj_kernel,
        grid=(t // tm, n // tn),
        in_specs=[pl.BlockSpec((tm, d), lambda i, j: (i, 0)),
                  pl.BlockSpec((d, tn), lambda i, j: (0, off + j))],
        out_specs=pl.BlockSpec((tm, tn), lambda i, j: (i, j)),
        out_shape=jax.ShapeDtypeStruct((t, n), BF16),
        compiler_params=_cparams("parallel", "arbitrary"),
        name="plain_proj",
    )(h, w_main)


def _gdn_qk_proj(x2, g_pre, mod, w_main, conv_w, qscale, w_small, seq, tm=1024, tn=1024):
    t, d = x2.shape
    n = qscale.shape[1]
    taps = conv_w.shape[0]
    return pl.pallas_call(
        functools.partial(_gdn_qk_kernel, blocks_per_seq=seq // tm),
        grid=(t // tm, n // tn),
        in_specs=_x_mod_specs(tm, d, seq // tm) + [
            pl.BlockSpec((d, tn), lambda i, j: (0, j)),
            pl.BlockSpec((taps, tn), lambda i, j: (0, j)),
            pl.BlockSpec((1, tn), lambda i, j: (0, j)),
            pl.BlockSpec((d, HEAD), lambda i, j: (0, 0))],
        out_specs=[pl.BlockSpec((tm, tn), lambda i, j: (i, j)),
                   pl.BlockSpec((tm, HEAD), lambda i, j: (i, 0)),
                   pl.BlockSpec((tm, d), lambda i, j: (i, 0))],
        out_shape=[jax.ShapeDtypeStruct((t, n), BF16), jax.ShapeDtypeStruct((t, HEAD), F32),
                   jax.ShapeDtypeStruct((t, d), BF16)],
        scratch_shapes=[_carry_scratch(tn, n // tn)],
        compiler_params=_cparams("arbitrary", "arbitrary"),
        name="gdn_qk_proj",
    )(x2, g_pre, mod, mod, w_main, conv_w, qscale, w_small)


def _gdn_v_proj(h, w_main, conv_w, col0, n, seq, tm=1024, tn=1024):
    t, d = h.shape
    taps = conv_w.shape[0]
    off = col0 // tn
    return pl.pallas_call(
        functools.partial(_gdn_v_kernel, blocks_per_seq=seq // tm),
        grid=(t // tm, n // tn),
        in_specs=[pl.BlockSpec((tm, d), lambda i, j: (i, 0)),
                  pl.BlockSpec((d, tn), lambda i, j: (0, off + j)),
                  pl.BlockSpec((taps, tn), lambda i, j: (0, off + j))],
        out_specs=pl.BlockSpec((tm, tn), lambda i, j: (i, j)),
        out_shape=jax.ShapeDtypeStruct((t, n), BF16),
        scratch_shapes=[_carry_scratch(tn, n // tn)],
        compiler_params=_cparams("arbitrary", "arbitrary"),
        name="gdn_v_proj",
    )(h, w_main, conv_w)


def _ffn_up(x2, g_pre, mod, w_up, layer, conv_w, conv_b, seq, tm=1024, tn=512):
    t, d = x2.shape
    dff = w_up.shape[2] // 2
    nj = dff // tn
    taps = conv_w.shape[0]
    return pl.pallas_call(
        functools.partial(_ffn_up_kernel, blocks_per_seq=seq // tm),
        grid=(t // tm, nj),
        in_specs=_x_mod_specs(tm, d, seq // tm) + [
            pl.BlockSpec((None, d, tn), lambda i, j: (layer, 0, j)),
            pl.BlockSpec((None, d, tn), lambda i, j: (layer, 0, j + nj)),
            pl.BlockSpec((taps, tn), lambda i, j: (0, j)),
            pl.BlockSpec((1, tn), lambda i, j: (0, j))],
        out_specs=pl.BlockSpec((tm, tn), lambda i, j: (i, j)),
        out_shape=jax.ShapeDtypeStruct((t, dff), BF16),
        scratch_shapes=[pltpu.VMEM((tm, d), BF16), _carry_scratch(tn, nj)],
        compiler_params=_cparams("arbitrary", "arbitrary"),
        name="ffn_up",
    )(x2, g_pre, mod, mod, w_up, w_up, conv_w, conv_b)


def _out_proj_kernel(a_ref, w_ref, x_ref, g_ref, gate_ref, o_ref):
    y = _dot(a_ref[...], w_ref[...])
    ms = jnp.mean(y * y, axis=-1, keepdims=True)
    o_ref[...] = x_ref[...] + (y * lax.rsqrt(ms + EPS)) * (gate_ref[...] * g_ref[...])


def _out_proj(a, w, layer, x2, g_post, mod, seq, tm):
    t, kd = a.shape
    d = w.shape[2]
    bps = seq // tm
    return pl.pallas_call(
        _out_proj_kernel,
        grid=(t // tm,),
        in_specs=[pl.BlockSpec((tm, kd), lambda i: (i, 0)),
                  pl.BlockSpec((None, kd, d), lambda i: (layer, 0, 0), pipeline_mode=pl.Buffered(1)),
                  pl.BlockSpec((tm, d), lambda i: (i, 0)),
                  pl.BlockSpec((1, d), lambda i: (0, 0)),
                  pl.BlockSpec((None, 1, d), lambda i: (i // bps, 0, 2))],
        out_specs=pl.BlockSpec((tm, d), lambda i: (i, 0)),
        out_shape=jax.ShapeDtypeStruct((t, d), F32),
        compiler_params=_cparams("parallel"),
        name="out_proj",
    )(a, w, x2, g_post, mod)


def _fox_gate_kernel(f_ref, fb_ref, o_ref, carry_ref):
    tm = f_ref.shape[0]

    @pl.when(pl.program_id(1) == 0)
    def _():
        carry_ref[...] = jnp.zeros_like(carry_ref)

    x = f_ref[...] + fb_ref[...]
    logf = (jnp.minimum(x, 0.0) - jnp.log1p(jnp.exp(-jnp.abs(x)))) * LOG2E
    r = lax.broadcasted_iota(jnp.int32, (tm, tm), 0)
    c = lax.broadcasted_iota(jnp.int32, (tm, tm), 1)
    cum = _mask_dot(r >= c, logf) + carry_ref[0:1, :]
    o_ref[...] = cum
    carry_ref[0:1, :] = cum[tm - 1:tm, :]


def _fox_gate_cumsum(f_logit, f_bias_row, nb, seq, tm=256):
    return pl.pallas_call(
        _fox_gate_kernel,
        grid=(nb, seq // tm),
        in_specs=[pl.BlockSpec((None, tm, HEAD), lambda b, i: (b, i, 0)),
                  pl.BlockSpec((1, HEAD), lambda b, i: (0, 0))],
        out_specs=pl.BlockSpec((None, tm, HEAD), lambda b, i: (b, i, 0)),
        out_shape=jax.ShapeDtypeStruct((nb, seq, HEAD), F32),
        scratch_shapes=[pltpu.VMEM((SUBLANES, HEAD), F32)],
        compiler_params=_cparams("parallel", "arbitrary"),
        name="fox_gate_cumsum",
    )(f_logit.reshape(nb, seq, HEAD), f_bias_row)


def _attn_kernel(q_ref, k_ref, v_ref, og_ref, fk_ref, o_ref,
                 s0_ref, s1_ref, m_ref, l_ref, acc_ref, *, tq):
    h = pl.program_id(1)
    qi = pl.program_id(2)
    m_ref[...] = jnp.full(m_ref.shape, -jnp.inf, F32)
    l_ref[...] = jnp.zeros(l_ref.shape, F32)
    acc_ref[...] = jnp.zeros(acc_ref.shape, F32)

    def scores(s_ref, ki):
        ks = pl.multiple_of(ki * tq, tq)
        s_ref[...] = _dot_nt(q_ref[...], k_ref[pl.ds(ks, tq), :]) - fk_ref[pl.ds(h, 1), pl.ds(ks, tq)]

    def update(s_ref, ki, masked):
        ks = pl.multiple_of(ki * tq, tq)
        s = s_ref[...]
        if masked:
            row = lax.broadcasted_iota(jnp.int32, (tq, tq), 0)
            col = lax.broadcasted_iota(jnp.int32, (tq, tq), 1)
            s = jnp.where(col <= row, s, -jnp.inf)
        m_old = m_ref[...]
        m_new = jnp.maximum(m_old, jnp.max(s, axis=-1, keepdims=True))
        p = jnp.exp2(s - m_new)
        alpha = jnp.exp2(m_old - m_new)
        l_ref[...] = alpha * l_ref[...] + jnp.sum(p, axis=-1, keepdims=True)
        m_ref[...] = m_new
        acc_ref[...] = alpha * acc_ref[...] + _dot(p.astype(BF16), v_ref[pl.ds(ks, tq), :])

    scores(s0_ref, 0)

    def pair(pi, carry):
        scores(s1_ref, 2 * pi + 1)
        update(s0_ref, 2 * pi, False)
        scores(s0_ref, 2 * pi + 2)
        update(s1_ref, 2 * pi + 1, False)
        return carry

    lax.fori_loop(0, qi // 2, pair, 0)

    @pl.when(qi % 2 == 1)
    def _():
        scores(s1_ref, qi)
        update(s0_ref, qi - 1, False)
        update(s1_ref, qi, True)

    @pl.when(qi % 2 == 0)
    def _():
        update(s0_ref, qi, True)

    o_ref[...] = (acc_ref[...] / l_ref[...] * jax.nn.sigmoid(og_ref[...].astype(F32))).astype(o_ref.dtype)


def _attn_fixed_shift_kernel(q_ref, k_ref, v_ref, og_ref, fc_ref, c_ref, o_ref,
                             ka_ref, va_ref, qa_ref, p0_ref, p1_ref, acc_ref, *, tq):
    h = pl.program_id(1)
    step = pl.program_id(2)
    seq = k_ref.shape[0]
    nq = seq // tq
    q_blocks = (step, nq - 1 - step)

    def block_rows(i):
        return pl.ds(pl.multiple_of(i * tq, tq), tq)

    def head_col(rows):
        lane = lax.broadcasted_iota(jnp.int32, (tq, HEAD), 1)
        return jnp.sum(jnp.where(lane == h, fc_ref[rows, :], 0.0), axis=-1, keepdims=True)

    def extra_cols(first, second):
        lane = lax.broadcasted_iota(jnp.int32, (tq, HEAD), 1)
        cols = jnp.zeros((tq, HEAD), F32)
        for n, term in enumerate(first + second):
            cols = jnp.where(lane == n, term, cols)
        return cols.astype(BF16)

    ones3 = (1.0, 1.0, 1.0)

    @pl.when(step == 0)
    def _():
        def build(i, carry):
            rows = block_rows(i)
            hi, mid, lo = _split3(-head_col(rows))
            ka_ref[rows, 0:HEAD] = k_ref[rows, :]
            ka_ref[rows, HEAD:2 * HEAD] = extra_cols((hi, mid, lo), ones3)
            va_ref[rows, 0:HEAD] = v_ref[rows, :]
            va_ref[rows, HEAD:2 * HEAD] = jnp.ones((tq, HEAD), BF16)
            return carry
        lax.fori_loop(0, nq, build, 0)

    for n, qb in enumerate(q_blocks):
        rows = block_rows(qb)
        qa_ref[n] = jnp.concatenate(
            [q_ref[rows, :], extra_cols(ones3, _split3(head_col(rows) - c_ref[0:1, 0:1]))], axis=1)
    acc_ref[...] = jnp.zeros(acc_ref.shape, F32)

    def probs(p_ref, which, kb, masked):
        s = _dot_nt(qa_ref[which], ka_ref[block_rows(kb), :])
        if masked:
            row = lax.broadcasted_iota(jnp.int32, (tq, tq), 0)
            col = lax.broadcasted_iota(jnp.int32, (tq, tq), 1)
            s = jnp.where(col <= row, s, -jnp.inf)
        p_ref[...] = jnp.exp2(s).astype(BF16)

    def accumulate(p_ref, which, kb):
        acc_ref[which] += _dot(p_ref[...], va_ref[block_rows(kb), :])

    items = [(0, q_blocks[0], True), (1, q_blocks[1], True)]
    for n in range(nq - 1):
        which = (n >= q_blocks[0]).astype(jnp.int32)
        items.append((which, n - which * q_blocks[0], False))
    p_refs = (p0_ref, p1_ref)
    for n, (which, kb, masked) in enumerate(items):
        probs(p_refs[n % 2], which, kb, masked)
        if n > 0:
            accumulate(p_refs[(n - 1) % 2], *items[n - 1][:2])
    accumulate(p_refs[(len(items) - 1) % 2], *items[-1][:2])

    for n, qb in enumerate(q_blocks):
        rows = block_rows(qb)
        acc = acc_ref[n]
        o_ref[rows, :] = (acc[:, 0:HEAD] / acc[:, HEAD:HEAD + 1]
                          * jax.nn.sigmoid(og_ref[rows, :].astype(F32))).astype(o_ref.dtype)


def _fox_attention_fixed_shift(qk3, v3, og3, fcum, c_row, n_heads, tq=512):
    nb, seq, _ = qk3.shape
    nq = seq // tq
    assert nq % 2 == 0
    full = lambda off: pl.BlockSpec((None, seq, HEAD), lambda b, h, i: (b, 0, off + h))
    return pl.pallas_call(
        functools.partial(_attn_fixed_shift_kernel, tq=tq),
        grid=(nb, n_heads, nq // 2),
        in_specs=[full(0), full(n_heads), full(0), full(0),
                  pl.BlockSpec((None, seq, HEAD), lambda b, h, i: (b, 0, 0)),
                  pl.BlockSpec((1, HEAD), lambda b, h, i: (0, 0))],
        out_specs=full(0),
        out_shape=jax.ShapeDtypeStruct((nb, seq, n_heads * HEAD), BF16),
        scratch_shapes=[pltpu.VMEM((seq, 2 * HEAD), BF16), pltpu.VMEM((seq, 2 * HEAD), BF16),
                        pltpu.VMEM((2, tq, 2 * HEAD), BF16),
                        pltpu.VMEM((tq, tq), BF16), pltpu.VMEM((tq, tq), BF16),
                        pltpu.VMEM((2, tq, 2 * HEAD), F32)],
        compiler_params=_cparams("parallel", "parallel", "arbitrary"),
        name="fox_attention_fixed_shift",
    )(qk3, qk3, v3, og3, fcum, c_row)


ITEM_STRIDE = 32
LIST_UNROLL = 4


def _attn_block_list_kernel(items_ref, count_ref, q_ref, k_ref, v_ref, og_ref, fc_ref, c_ref, o_ref,
                            ka_ref, va_ref, qa_ref, p0_ref, p1_ref, acc_ref, *, tq):
    h = pl.program_id(1)
    bh = pl.program_id(0) * pl.num_programs(1) + h
    seq = k_ref.shape[0]
    nq = seq // tq

    def block_rows(i):
        return pl.ds(pl.multiple_of(i * tq, tq), tq)

    lane1 = lax.broadcasted_iota(jnp.int32, (1, HEAD), 1)
    c_hi, c_mid, c_lo = _split3(c_ref[...])
    k_fixed = jnp.where((lane1 >= 3) & (lane1 < 6), 1.0,
                        jnp.where(lane1 == 6, -c_hi, jnp.where(lane1 == 7, -c_mid,
                                                               jnp.where(lane1 == 8, -c_lo, 0.0))))
    q_fixed = jnp.where((lane1 < 3) | ((lane1 >= 6) & (lane1 < 9)), 1.0, 0.0)

    def build(i, carry):
        rows = block_rows(i)
        lane = lax.broadcasted_iota(jnp.int32, (tq, HEAD), 1)
        f = jnp.sum(jnp.where(lane == h, fc_ref[rows, :], 0.0), axis=-1, keepdims=True)
        hi, mid, lo = _split3(f)
        k_cols = jnp.where(lane == 0, -hi, jnp.where(lane == 1, -mid, jnp.where(lane == 2, -lo, k_fixed)))
        q_cols = jnp.where(lane == 3, hi, jnp.where(lane == 4, mid, jnp.where(lane == 5, lo, q_fixed)))
        ka_ref[rows, 0:HEAD] = k_ref[rows, :]
        ka_ref[rows, HEAD:2 * HEAD] = k_cols.astype(BF16)
        va_ref[rows, 0:HEAD] = v_ref[rows, :]
        va_ref[rows, HEAD:2 * HEAD] = jnp.ones((tq, HEAD), BF16)
        qa_ref[rows, 0:HEAD] = q_ref[rows, :]
        qa_ref[rows, HEAD:2 * HEAD] = q_cols.astype(BF16)
        return carry

    lax.fori_loop(0, nq, build, 0)
    acc_ref[nq] = jnp.zeros(acc_ref.shape[1:], F32)

    def probs(p_ref, qb, kb, masked):
        qb = jnp.minimum(qb, nq - 1)
        s = _dot_nt(qa_ref[block_rows(qb), :], ka_ref[block_rows(kb), :])
        if masked:
            row = lax.broadcasted_iota(jnp.int32, (tq, tq), 0)
            col = lax.broadcasted_iota(jnp.int32, (tq, tq), 1)
            s = jnp.where(col <= row, s, -jnp.inf)
        p_ref[...] = jnp.exp2(s).astype(BF16)

    def product(p_ref, kb):
        return _dot(p_ref[...], va_ref[block_rows(kb), :])

    p_refs = (p0_ref, p1_ref)
    for i in range(nq):
        probs(p_refs[i % 2], i, i, True)
        if i > 0:
            acc_ref[i - 1] = product(p_refs[(i - 1) % 2], i - 1)
    acc_ref[nq - 1] = product(p_refs[(nq - 1) % 2], nq - 1)

    def item(n):
        code = items_ref[bh, n]
        return code // ITEM_STRIDE, code % ITEM_STRIDE

    def accumulate(p_ref, n):
        qb, kb = item(n)
        acc_ref[qb] += product(p_ref, kb)

    probs(p0_ref, *item(0), False)

    def group(m, carry):
        for u in range(LIST_UNROLL):
            n = LIST_UNROLL * m + u
            probs(p_refs[(u + 1) % 2], *item(n + 1), False)
            accumulate(p_refs[u % 2], n)
        return carry

    lax.fori_loop(0, count_ref[bh] // LIST_UNROLL, group, 0)

    def finish(i, carry):
        rows = block_rows(i)
        acc = acc_ref[i]
        o_ref[rows, :] = (acc[:, 0:HEAD] / acc[:, HEAD:HEAD + 1]
                          * jax.nn.sigmoid(og_ref[rows, :].astype(F32))).astype(o_ref.dtype)
        return carry

    lax.fori_loop(0, nq, finish, 0)


def _fox_attention_block_list(qk3, v3, og3, fcum, c_row, items, counts, n_heads, tq=512):
    nb, seq, _ = qk3.shape
    nq = seq // tq
    full = lambda off: pl.BlockSpec((None, seq, HEAD), lambda b, h, *_: (b, 0, off + h))
    grid_spec = pltpu.PrefetchScalarGridSpec(
        num_scalar_prefetch=2,
        grid=(nb, n_heads),
        in_specs=[full(0), full(n_heads), full(0), full(0),
                  pl.BlockSpec((None, seq, HEAD), lambda b, h, *_: (b, 0, 0), pipeline_mode=pl.Buffered(1)),
                  pl.BlockSpec((1, HEAD), lambda b, h, *_: (0, 0))],
        out_specs=full(0),
        scratch_shapes=[pltpu.VMEM((seq, 2 * HEAD), BF16), pltpu.VMEM((seq, 2 * HEAD), BF16),
                        pltpu.VMEM((seq, 2 * HEAD), BF16),
                        pltpu.VMEM((tq, tq), BF16), pltpu.VMEM((tq, tq), BF16),
                        pltpu.VMEM((nq + 1, tq, 2 * HEAD), F32)])
    return pl.pallas_call(
        functools.partial(_attn_block_list_kernel, tq=tq),
        grid_spec=grid_spec,
        out_shape=jax.ShapeDtypeStruct((nb, seq, n_heads * HEAD), BF16),
        compiler_params=_cparams("parallel", "arbitrary"),
        name="fox_attention_block_list",
    )(items, counts, qk3, qk3, v3, og3, fcum, c_row)


MAX_SHIFT_GAP = 100.0
SKIP_LOG2 = 136.0
LIST_MAX_FRACTION = 0.75


def _needed_blocks(fcum, n_heads, tq):
    nb, seq, _ = fcum.shape
    nq = seq // tq
    f_first = fcum[:, 0::tq, :n_heads]
    f_last = fcum[:, tq - 1::tq, :n_heads]
    bound = f_first[:, :, None, :] - f_last[:, None, :, :]
    i = jnp.arange(nq, dtype=jnp.int32)[:, None]
    j = jnp.arange(nq, dtype=jnp.int32)[None, :]
    need = (bound > -SKIP_LOG2) & (j < i)[None, :, :, None]
    need = jnp.transpose(need, (0, 3, 1, 2)).reshape(nb * n_heads, nq * nq)
    pad_code = nq * ITEM_STRIDE
    codes = jnp.where(need, (i * ITEM_STRIDE + j).reshape(1, nq * nq), pad_code)
    codes = jnp.sort(codes, axis=-1)
    codes = jnp.concatenate([codes, jnp.full((nb * n_heads, LIST_UNROLL), pad_code, jnp.int32)], axis=-1)
    counts = jnp.sum(need, axis=-1).astype(jnp.int32)
    return codes.astype(jnp.int32), counts + (-counts) % LIST_UNROLL, jnp.sum(counts)


def _fox_attention(qk3, v3, og3, fcum, gn, n_heads, tq=512):
    nb, seq, _ = qk3.shape
    nq = seq // tq
    qd = n_heads * HEAD
    c = (HEAD * 1.01) * jnp.max(jnp.abs(gn[:, :qd])) * jnp.max(jnp.abs(gn[:, qd:]))
    c_row = jnp.full((1, HEAD), c, F32)
    items, counts, total = _needed_blocks(fcum, n_heads, tq)
    all_pairs = nb * n_heads * (nq * (nq - 1) // 2)

    def block_list(qk3, v3, og3, fcum, c_row, items, counts):
        return _fox_attention_block_list(qk3, v3, og3, fcum, c_row, items, counts, n_heads, tq)

    def full_stream(qk3, v3, og3, fcum, c_row, items, counts):
        return _fox_attention_fixed_shift(qk3, v3, og3, fcum, c_row, n_heads, tq)

    def running_max(qk3, v3, og3, fcum, c_row, items, counts):
        fk = jnp.transpose(fcum[:, :, :n_heads], (0, 2, 1))
        return _fox_attention_running_max(qk3, v3, og3, fk, n_heads, tq)

    branch = jnp.where(2.0 * c > MAX_SHIFT_GAP, 2,
                       jnp.where(total <= LIST_MAX_FRACTION * all_pairs, 0, 1)).astype(jnp.int32)
    return lax.switch(branch, (block_list, full_stream, running_max), qk3, v3, og3, fcum, c_row, items, counts)


def _fox_attention_running_max(qk3, v3, og3, fk, n_heads, tq=512):
    nb, seq, _ = qk3.shape
    return pl.pallas_call(
        functools.partial(_attn_kernel, tq=tq),
        grid=(nb, n_heads, seq // tq),
        in_specs=[pl.BlockSpec((None, tq, HEAD), lambda b, h, i: (b, i, h)),
                  pl.BlockSpec((None, seq, HEAD), lambda b, h, i: (b, 0, n_heads + h)),
                  pl.BlockSpec((None, seq, HEAD), lambda b, h, i: (b, 0, h)),
                  pl.BlockSpec((None, tq, HEAD), lambda b, h, i: (b, i, h)),
                  pl.BlockSpec((None, n_heads, seq), lambda b, h, i: (b, 0, 0))],
        out_specs=pl.BlockSpec((None, tq, HEAD), lambda b, h, i: (b, i, h)),
        out_shape=jax.ShapeDtypeStruct((nb, seq, n_heads * HEAD), BF16),
        scratch_shapes=[pltpu.VMEM((tq, tq), F32), pltpu.VMEM((tq, tq), F32),
                        pltpu.VMEM((tq, 1), F32), pltpu.VMEM((tq, 1), F32), pltpu.VMEM((tq, HEAD), F32)],
        compiler_params=_cparams("parallel", "parallel", "arbitrary"),
        name="fox_attention",
    )(qk3, qk3, v3, og3, fk)


def _gdn_gate_kernel(ab_ref, alog_ref, dt_ref, o_ref, *, n_heads):
    tm = ab_ref.shape[0]
    x = ab_ref[...]
    lane = lax.broadcasted_iota(jnp.int32, x.shape, 1)
    g = -jnp.exp(alog_ref[...]) * _softplus(x + dt_ref[...])
    r = lax.broadcasted_iota(jnp.int32, (tm, tm), 0)
    c = lax.broadcasted_iota(jnp.int32, (tm, tm), 1)
    gc = _mask_dot((r >= c) & ((r // CHUNK) == (c // CHUNK)), jnp.where(lane < n_heads, g, 0.0))
    o_ref[...] = jnp.where(lane < n_heads, gc, jax.nn.sigmoid(x))


def _gdn_gates(ab, alog_row, dt_row, n_heads, tm=256):
    t = ab.shape[0]
    return pl.pallas_call(
        functools.partial(_gdn_gate_kernel, n_heads=n_heads),
        grid=(t // tm,),
        in_specs=[pl.BlockSpec((tm, HEAD), lambda i: (i, 0)),
                  pl.BlockSpec((1, HEAD), lambda i: (0, 0)),
                  pl.BlockSpec((1, HEAD), lambda i: (0, 0))],
        out_specs=pl.BlockSpec((tm, HEAD), lambda i: (i, 0)),
        out_shape=jax.ShapeDtypeStruct((t, HEAD), F32),
        compiler_params=_cparams("parallel"),
        name="gdn_gates",
    )(ab, alog_row, dt_row)


def _block_diag2(p, half):
    return jnp.concatenate([jnp.where(half, 0.0, p), jnp.where(half, p, 0.0)], axis=0)


def _gdn_chunk_kernel(q_ref, k_ref, v_ref, gb_ref, grow_ref, u_ref, w_ref, qd_ref, kd_ref, qk_ref, egl_ref,
                      *, n_vheads, chunks):
    hk = pl.program_id(1)
    lane = lax.broadcasted_iota(jnp.int32, (CHUNK, HEAD), 1)
    row = lax.broadcasted_iota(jnp.int32, (CHUNK, HEAD), 0)
    col = lane & (CHUNK - 1)
    half = lane >= CHUNK
    lower = row >= col
    strict = row > col
    eye = (row == col).astype(F32)
    level_masks = []
    for lvl in range(6):
        level_masks.append((((row >> lvl) & 1) == 1) & (((col >> lvl) & 1) == 0)
                           & ((row >> (lvl + 1)) == (col >> (lvl + 1))))
    zeros = jnp.zeros((CHUNK, HEAD), F32)

    def pick(gb, idx):
        return jnp.sum(jnp.where(lane == idx, gb, 0.0), axis=-1, keepdims=True)

    sls = [slice(c * CHUNK, (c + 1) * CHUNK) for c in range(chunks)]
    gcs, bes, a_s, xs = [], [], [], []
    for c, sl in enumerate(sls):
        q = q_ref[sl, :]
        k = k_ref[sl, :]
        gb = gb_ref[sl, :]
        gc0, gc1 = pick(gb, 2 * hk), pick(gb, 2 * hk + 1)
        be0, be1 = pick(gb, n_vheads + 2 * hk), pick(gb, n_vheads + 2 * hk + 1)
        gcol = jnp.where(half, gc1, gc0)
        bcol = jnp.where(half, be1, be0)
        decay = jnp.exp(jnp.where(lower, gcol - grow_ref[c:c + 1, :], -jnp.inf))
        qkk = _dot_nt(jnp.concatenate([q, k], axis=0), jnp.concatenate([k, k], axis=0))
        a = jnp.where(strict, qkk[CHUNK:, :] * decay * bcol, 0.0)
        qk_ref[sl, :] = (qkk[:CHUNK, :] * decay).astype(qk_ref.dtype)
        gcs.append((gc0, gc1))
        bes.append((be0, be1))
        a_s.append(a)
        xs.append(eye - jnp.where(level_masks[0], a, 0.0))

    for lvl in range(1, 6):
        ys = [_dot(xs[c].astype(BF16), _block_diag2(jnp.where(level_masks[lvl], a_s[c], 0.0), half).astype(BF16))
              for c in range(chunks)]
        xs = [xs[c] - _dot(ys[c].astype(BF16), _block_diag2(xs[c], half).astype(BF16)) for c in range(chunks)]

    for c, sl in enumerate(sls):
        (gc0, gc1), (be0, be1) = gcs[c], bes[c]
        kf = k_ref[sl, :].astype(F32)
        qf = q_ref[sl, :].astype(F32)
        v = v_ref[sl, :].astype(F32)
        eg0, eg1 = jnp.exp(gc0), jnp.exp(gc1)
        rhs = jnp.concatenate([
            jnp.concatenate([v[:, :HEAD] * be0, kf * (be0 * eg0), zeros, zeros], axis=1),
            jnp.concatenate([zeros, zeros, v[:, HEAD:] * be1, kf * (be1 * eg1)], axis=1)], axis=0)
        sol = _dot(xs[c].astype(BF16), rhs.astype(BF16))
        u_ref[sl, :] = jnp.concatenate([sol[:, 0:HEAD], sol[:, 2 * HEAD:3 * HEAD]], axis=1).astype(u_ref.dtype)
        w_ref[sl, :] = jnp.concatenate([sol[:, HEAD:2 * HEAD], sol[:, 3 * HEAD:]], axis=1).astype(w_ref.dtype)
        qd_ref[sl, :] = jnp.concatenate([qf * eg0, qf * eg1], axis=1).astype(qd_ref.dtype)
        gl0, gl1 = gc0[CHUNK - 1:CHUNK, :], gc1[CHUNK - 1:CHUNK, :]
        kd_ref[sl, :] = jnp.concatenate([kf * jnp.exp(gl0 - gc0), kf * jnp.exp(gl1 - gc1)],
                                        axis=1).astype(kd_ref.dtype)
        egl_ref[0, c:c + 1, :] = jnp.broadcast_to(jnp.exp(gl0), (1, HEAD))
        egl_ref[1, c:c + 1, :] = jnp.broadcast_to(jnp.exp(gl1), (1, HEAD))


def _gdn_chunks(qk3, v3, gb3, grow, n_kheads, n_vheads, tm=1024):
    nb, seq, _ = qk3.shape
    chunks = tm // CHUNK
    vd = n_vheads * HEAD
    big = jax.ShapeDtypeStruct((nb, seq, vd), BF16)
    pair_spec = pl.BlockSpec((None, tm, 2 * HEAD), lambda b, h, i: (b, i, h))
    return pl.pallas_call(
        functools.partial(_gdn_chunk_kernel, n_vheads=n_vheads, chunks=chunks),
        grid=(nb, n_kheads, seq // tm),
        in_specs=[pl.BlockSpec((None, tm, HEAD), lambda b, h, i: (b, i, h)),
                  pl.BlockSpec((None, tm, HEAD), lambda b, h, i: (b, i, n_kheads + h)),
                  pair_spec,
                  pl.BlockSpec((None, tm, HEAD), lambda b, h, i: (b, i, 0)),
                  pl.BlockSpec((None, None, chunks, HEAD), lambda b, h, i: (b, h, i, 0))],
        out_specs=[pair_spec, pair_spec, pair_spec, pair_spec,
                   pl.BlockSpec((None, tm, HEAD), lambda b, h, i: (b, i, h)),
                   pl.BlockSpec((None, 2, chunks, HEAD), lambda b, h, i: (b, h, i, 0))],
        out_shape=[big, big, big, big,
                   jax.ShapeDtypeStruct((nb, seq, n_kheads * HEAD), BF16),
                   jax.ShapeDtypeStruct((nb, n_vheads, seq // CHUNK, HEAD), F32)],
        compiler_params=_cparams("parallel", "parallel", "parallel"),
        name="gdn_chunks",
    )(qk3, qk3, v3, gb3, grow)


def _gdn_scan_kernel(u_ref, w_ref, qd_ref, kd_ref, qk_ref, egl_ref, z_ref, gn_ref, o_ref, s_ref,
                     *, heads, chunks):
    @pl.when(pl.program_id(2) == 0)
    def _():
        s_ref[...] = jnp.zeros_like(s_ref)

    zeros = jnp.zeros((CHUNK, HEAD), F32)
    gn = gn_ref[...]
    cols = [slice(g * HEAD, (g + 1) * HEAD) for g in range(heads)]
    for c in range(chunks):
        sl = slice(c * CHUNK, (c + 1) * CHUNK)
        rs = [_dot(jnp.concatenate([w_ref[sl, cols[g]], qd_ref[sl, cols[g]]], axis=0), s_ref[g].astype(BF16))
              for g in range(heads)]
        vnew = [u_ref[sl, cols[g]].astype(F32) - rs[g][:CHUNK, :] for g in range(heads)]
        intra = []
        for pr in range(heads // 2):
            v2 = jnp.concatenate([jnp.concatenate([vnew[2 * pr], zeros], axis=1),
                                  jnp.concatenate([zeros, vnew[2 * pr + 1]], axis=1)], axis=0).astype(BF16)
            intra.append(_dot(qk_ref[sl, pr * HEAD:(pr + 1) * HEAD], v2))
        for g in range(heads):
            s_ref[g] = s_ref[g] * egl_ref[g, c:c + 1, :] + _dot_tn(kd_ref[sl, cols[g]], vnew[g].astype(BF16))
        for g in range(heads):
            o = rs[g][CHUNK:, :] + intra[g // 2][:, (g % 2) * HEAD:(g % 2 + 1) * HEAD]
            ms = jnp.mean(o * o, axis=-1, keepdims=True)
            z = z_ref[sl, cols[g]].astype(F32)
            o_ref[sl, cols[g]] = (o * lax.rsqrt(ms + EPS) * gn * (z * jax.nn.sigmoid(z))).astype(o_ref.dtype)


def _gdn_scan(u, w, qd, kd, qkm, egl, z3, gn_row, n_vheads, heads=16, tm=512):
    nb, seq, vd = u.shape
    chunks = tm // CHUNK
    wide = pl.BlockSpec((None, tm, heads * HEAD), lambda b, g, i: (b, i, g))
    return pl.pallas_call(
        functools.partial(_gdn_scan_kernel, heads=heads, chunks=chunks),
        grid=(nb, n_vheads // heads, seq // tm),
        in_specs=[wide, wide, wide, wide,
                  pl.BlockSpec((None, tm, heads // 2 * HEAD), lambda b, g, i: (b, i, g)),
                  pl.BlockSpec((None, heads, chunks, HEAD), lambda b, g, i: (b, g, i, 0)),
                  wide,
                  pl.BlockSpec((1, HEAD), lambda b, g, i: (0, 0))],
        out_specs=wide,
        out_shape=jax.ShapeDtypeStruct((nb, seq, vd), BF16),
        scratch_shapes=[pltpu.VMEM((heads, HEAD, HEAD), F32)],
        compiler_params=_cparams("parallel", "parallel", "arbitrary"),
        name="gdn_scan",
    )(u, w, qd, kd, qkm, egl, z3, gn_row)


def _pad_cols(w, n):
    return jnp.pad(w, ((0, 0), (0, n - w.shape[1])))


def _row(v, n=HEAD):
    return jnp.pad(v.astype(F32), (0, n - v.shape[0])).reshape(1, n)


def _fox_layer(x2, nb, seq, g_pre, g_post, mod, w_in, f_bias, q_norm, k_norm, w_o, layer):
    d = x2.shape[1]
    nh = d // HEAD
    qd = nh * HEAD
    w_small = _pad_cols(w_in[:, 3 * qd:3 * qd + nh], HEAD)
    w_gate = w_in[:, 3 * qd + nh:]
    gn = jnp.concatenate([jnp.tile(q_norm.astype(F32) * (HEAD ** -0.5 * LOG2E), nh),
                          jnp.tile(k_norm.astype(F32), nh)]).reshape(1, 2 * qd)
    qk, f_logit, h = _fox_qk_proj(x2, g_pre, mod, w_in, gn, w_small, seq)
    v = _plain_proj(h, w_in, 2 * qd, qd)
    og = _plain_proj(h, w_gate, 0, qd)
    fcum = _fox_gate_cumsum(f_logit, _row(f_bias), nb, seq)
    attn = _fox_attention(qk.reshape(nb, seq, 2 * qd), v.reshape(nb, seq, qd), og.reshape(nb, seq, qd),
                          fcum, gn, nh)
    return _out_proj(attn.reshape(nb * seq, qd), w_o, layer, x2, g_post, mod, seq, tm=512)


def _gdn_layer(x2, nb, seq, g_pre, g_post, mod, w_in, conv_w, a_log, dt_bias, out_norm, w_o, layer):
    d = x2.shape[1]
    nk = d // HEAD
    nv = 2 * nk
    kd, vd = nk * HEAD, nv * HEAD
    n_main = 2 * kd + 2 * vd
    w_small = _pad_cols(w_in[:, n_main:], HEAD)
    conv_w = conv_w.astype(F32)
    qscale = jnp.concatenate([jnp.full((kd,), HEAD ** -0.5, F32), jnp.ones((kd,), F32)]).reshape(1, 2 * kd)
    qk, ab, h = _gdn_qk_proj(x2, g_pre, mod, w_in, conv_w, qscale, w_small, seq)
    v = _gdn_v_proj(h, w_in, conv_w, 2 * kd, vd, seq)
    z = _plain_proj(h, w_in, 2 * kd + vd, vd)
    gb = _gdn_gates(ab, _row(a_log), _row(dt_bias), nv)
    n_chunks = seq // CHUNK
    grow = gb[:, :nv].reshape(nb, n_chunks, CHUNK, nk, 2)
    grow = jnp.transpose(grow, (0, 3, 1, 4, 2)).reshape(nb, nk, n_chunks, 2 * CHUNK)
    u, w, qdec, kdec, qkm, egl = _gdn_chunks(qk.reshape(nb, seq, 2 * kd), v.reshape(nb, seq, vd),
                                             gb.reshape(nb, seq, HEAD), grow, nk, nv)
    a = _gdn_scan(u, w, qdec, kdec, qkm, egl, z.reshape(nb, seq, vd), _row(out_norm), nv)
    return _out_proj(a.reshape(nb * seq, vd), w_o, layer, x2, g_post, mod, seq, tm=512)


def _ffn_layer(x2, seq, g_pre, g_post, mod, w_up, conv_w, conv_b, w_down, layer):
    u = _ffn_up(x2, g_pre, mod, w_up, layer, conv_w.astype(F32), conv_b.reshape(1, -1).astype(F32), seq)
    return _out_proj(u, w_down, layer, x2, g_post, mod, seq, tm=256)


def kernel(x, c, ada_w, ada_b, norm_g, fox_w_in, fox_f_bias, fox_q_norm, fox_k_norm, fox_w_o,
           gdn_w_in, gdn_conv_w, gdn_a_log, gdn_dt_bias, gdn_out_norm, gdn_w_o,
           ffn_w_up, ffn_conv_w, ffn_conv_b, ffn_w_down):
    nb, seq, d = x.shape
    depth = ada_w.shape[0]
    mods = _ada_mods(c, ada_w, ada_b)
    fox_w_in, fox_w_o, gdn_w_in, gdn_w_o, ffn_w_up, ffn_w_down = (
        w.astype(BF16) for w in (fox_w_in, fox_w_o, gdn_w_in, gdn_w_o, ffn_w_up, ffn_w_down))
    x2 = x.reshape(nb * seq, d)
    for i in range(depth):
        g = [norm_g[i, r].reshape(1, d).astype(F32) for r in range(4)]
        j = i // 2
        if i % 2 == 0:
            x2 = _fox_layer(x2, nb, seq, g[0], g[1], mods[2 * i], fox_w_in[j], fox_f_bias[j],
                            fox_q_norm[j], fox_k_norm[j], fox_w_o, j)
        else:
            x2 = _gdn_layer(x2, nb, seq, g[0], g[1], mods[2 * i], gdn_w_in[j], gdn_conv_w[j],
                            gdn_a_log[j], gdn_dt_bias[j], gdn_out_norm[j], gdn_w_o, j)
        x2 = _ffn_layer(x2, seq, g[2], g[3], mods[2 * i + 1], ffn_w_up, ffn_conv_w[i],
                        ffn_conv_b[i], ffn_w_down, i)
    return x2.reshape(nb, seq, d)
```

```python
import functools

import jax
import jax.numpy as jnp
from jax import lax
from jax.experimental import pallas as pl
from jax.experimental.pallas import tpu as pltpu

F32 = jnp.float32
BF16 = jnp.bfloat16
EPS = 1e-6
HEAD = 128
CHUNK = 64
SUBLANES = 8
MXU_N = 256
VMEM_LIMIT = 56 * 1024 * 1024
INV_SQRT2 = 0.7071067811865476
LOG2E = 1.4426950408889634


def _cparams(*sem):
    return pltpu.CompilerParams(dimension_semantics=sem, vmem_limit_bytes=VMEM_LIMIT)


def _dot(a, b):
    return jnp.dot(a, b, preferred_element_type=F32)


def _dot_nt(a, b):
    return lax.dot_general(a, b, (((1,), (1,)), ((), ())), preferred_element_type=F32)


def _dot_tn(a, b):
    return lax.dot_general(a, b, (((0,), (0,)), ((), ())), preferred_element_type=F32)


def _split3(x):
    hi = x.astype(BF16).astype(F32)
    mid = (x - hi).astype(BF16).astype(F32)
    return hi, mid, x - hi - mid


def _mask_dot(mask, x):
    m = jnp.where(mask, 1.0, 0.0).astype(BF16)
    hi, mid, lo = _split3(x)
    return _dot(m, hi.astype(BF16)) + _dot(m, mid.astype(BF16)) + _dot(m, lo.astype(BF16))


def _softplus(x):
    return jnp.maximum(x, 0.0) + jnp.log1p(jnp.exp(-jnp.abs(x)))


def _ada_kernel(c_ref, w_ref, b_ref, o_ref):
    o_ref[0] = jnp.zeros(o_ref.shape[1:], F32)
    for b in range(c_ref.shape[0]):
        c = c_ref[b]
        row = jnp.sum(w_ref[0] * (c * jax.nn.sigmoid(c)), axis=0, keepdims=True)
        o_ref[0, b:b + 1, :] = row + b_ref[0]


def _ada_mods(c, ada_w, ada_b):
    nb, d = c.shape
    assert nb <= SUBLANES
    n = ada_w.shape[0] * ada_w.shape[1]
    w = ada_w.reshape(n, d, 3 * d)
    b = ada_b.reshape(n, 1, 3 * d)
    tn = 1024
    out = pl.pallas_call(
        _ada_kernel,
        grid=(n, 3 * d // tn),
        in_specs=[pl.BlockSpec((nb, d, 1), lambda s, j: (0, 0, 0)),
                  pl.BlockSpec((1, d, tn), lambda s, j: (s, 0, j)),
                  pl.BlockSpec((1, 1, tn), lambda s, j: (s, 0, j))],
        out_specs=pl.BlockSpec((1, SUBLANES, tn), lambda s, j: (s, 0, j)),
        out_shape=jax.ShapeDtypeStruct((n, SUBLANES, 3 * d), F32),
        compiler_params=_cparams("parallel", "parallel"),
        name="ada_mods",
    )(c.reshape(nb, d, 1), w, b)
    return out[:, :nb].reshape(n, nb, 1, 3 * d)


NORM_ROWS = 32


def _norm_mod(h_ref, x_ref, g_ref, sh_ref, sc_ref):
    gain = g_ref[...] * (1.0 + sc_ref[...])
    shift = sh_ref[...]

    def rows_pass(r, carry):
        rows = pl.ds(pl.multiple_of(r * NORM_ROWS, NORM_ROWS), NORM_ROWS)
        x = x_ref[rows, :]
        ms = jnp.mean(x * x, axis=-1, keepdims=True)
        h_ref[rows, :] = (x * lax.rsqrt(ms + EPS) * gain + shift).astype(BF16)
        return carry

    lax.fori_loop(0, x_ref.shape[0] // NORM_ROWS, rows_pass, 0, unroll=8)


def _sub_dots(h_ref, w_ref):
    return [_dot(h_ref[...], w_ref[:, c * MXU_N:(c + 1) * MXU_N]) for c in range(w_ref.shape[1] // MXU_N)]


def _head_norm_store(o_ref, c, y, gain_ref, mean):
    for g in range(MXU_N // HEAD):
        sl = slice(c * MXU_N + g * HEAD, c * MXU_N + (g + 1) * HEAD)
        t = y[:, g * HEAD:(g + 1) * HEAD]
        ss = jnp.sum(t * t, axis=-1, keepdims=True)
        if mean:
            ss = ss * (1.0 / HEAD)
        o_ref[:, sl] = (t * lax.rsqrt(ss + EPS) * gain_ref[:, sl]).astype(o_ref.dtype)


def _causal_conv(acc, cw, carry_ref, slot, first):
    tm = acc.shape[0]
    taps = cw.shape[0]
    prev = jnp.where(first, 0.0, carry_ref[slot])
    carry_ref[slot] = acc[tm - SUBLANES:tm, :]
    xe = jnp.concatenate([prev, acc], axis=0)
    ye = xe * cw[0:1, :]
    for k in range(1, taps):
        ye = xe * cw[k:k + 1, :] + pltpu.roll(ye, 1, axis=0)
    return ye[SUBLANES:, :]


def _zero_carry_at_start(carry_ref):
    @pl.when((pl.program_id(0) == 0) & (pl.program_id(1) == 0))
    def _():
        carry_ref[...] = jnp.zeros_like(carry_ref)


def _fox_qk_kernel(x_ref, g_ref, sh_ref, sc_ref, w_ref, gn_ref, ws_ref, o_ref, os_ref, h_ref):
    @pl.when(pl.program_id(1) == 0)
    def _():
        _norm_mod(h_ref, x_ref, g_ref, sh_ref, sc_ref)
        os_ref[...] = _dot(h_ref[...], ws_ref[...])

    for c, acc in enumerate(_sub_dots(h_ref, w_ref)):
        _head_norm_store(o_ref, c, acc, gn_ref, mean=True)


def _plain_proj_kernel(h_ref, w_ref, o_ref):
    o_ref[...] = _dot(h_ref[...], w_ref[...]).astype(o_ref.dtype)


def _gdn_qk_kernel(x_ref, g_ref, sh_ref, sc_ref, w_ref, cw_ref, qs_ref, ws_ref, o_ref, os_ref, h_ref,
                   carry_ref, *, blocks_per_seq):
    i = pl.program_id(0)
    j = pl.program_id(1)
    _zero_carry_at_start(carry_ref)

    @pl.when(j == 0)
    def _():
        _norm_mod(h_ref, x_ref, g_ref, sh_ref, sc_ref)
        os_ref[...] = _dot(h_ref[...], ws_ref[...])

    first = (i % blocks_per_seq) == 0
    accs = _sub_dots(h_ref, w_ref)
    for c, acc in enumerate(accs):
        y = _causal_conv(acc, cw_ref[:, c * MXU_N:(c + 1) * MXU_N], carry_ref, j * len(accs) + c, first)
        _head_norm_store(o_ref, c, y * jax.nn.sigmoid(y), qs_ref, mean=False)


def _gdn_v_kernel(h_ref, w_ref, cw_ref, o_ref, carry_ref, *, blocks_per_seq):
    i = pl.program_id(0)
    j = pl.program_id(1)
    _zero_carry_at_start(carry_ref)
    first = (i % blocks_per_seq) == 0
    accs = _sub_dots(h_ref, w_ref)
    for c, acc in enumerate(accs):
        cs = slice(c * MXU_N, (c + 1) * MXU_N)
        y = _causal_conv(acc, cw_ref[:, cs], carry_ref, j * len(accs) + c, first)
        o_ref[:, cs] = (y * jax.nn.sigmoid(y)).astype(o_ref.dtype)


def _ffn_up_kernel(x_ref, g_ref, sh_ref, sc_ref, wg_ref, wv_ref, cw_ref, cb_ref, o_ref,
                   h_ref, carry_ref, *, blocks_per_seq):
    i = pl.program_id(0)
    j = pl.program_id(1)
    _zero_carry_at_start(carry_ref)

    @pl.when(j == 0)
    def _():
        _norm_mod(h_ref, x_ref, g_ref, sh_ref, sc_ref)

    first = (i % blocks_per_seq) == 0
    n_sub = wg_ref.shape[1] // MXU_N
    pairs = []
    for c in range(n_sub):
        cs = slice(c * MXU_N, (c + 1) * MXU_N)
        pairs.append((_dot(h_ref[...], wg_ref[:, cs]), _dot(h_ref[...], wv_ref[:, cs])))
    for c, (gate, val) in enumerate(pairs):
        cs = slice(c * MXU_N, (c + 1) * MXU_N)
        gate = _causal_conv(gate, cw_ref[:, cs], carry_ref, j * n_sub + c, first) + cb_ref[:, cs]
        act = 0.5 * gate * (1.0 + lax.erf(gate * INV_SQRT2))
        o_ref[:, cs] = (act * val).astype(o_ref.dtype)


def _x_mod_specs(tm, d, bps):
    return [pl.BlockSpec((tm, d), lambda i, j: (i, 0)),
            pl.BlockSpec((1, d), lambda i, j: (0, 0)),
            pl.BlockSpec((None, 1, d), lambda i, j: (i // bps, 0, 0)),
            pl.BlockSpec((None, 1, d), lambda i, j: (i // bps, 0, 1))]


def _carry_scratch(tn, nj):
    return pltpu.VMEM((nj * (tn // MXU_N), SUBLANES, MXU_N), F32)


def _fox_qk_proj(x2, g_pre, mod, w_main, gn, w_small, seq, tm=1024, tn=1024):
    t, d = x2.shape
    n = gn.shape[1]
    return pl.pallas_call(
        _fox_qk_kernel,
        grid=(t // tm, n // tn),
        in_specs=_x_mod_specs(tm, d, seq // tm) + [
            pl.BlockSpec((d, tn), lambda i, j: (0, j)),
            pl.BlockSpec((1, tn), lambda i, j: (0, j)),
            pl.BlockSpec((d, HEAD), lambda i, j: (0, 0))],
        out_specs=[pl.BlockSpec((tm, tn), lambda i, j: (i, j)),
                   pl.BlockSpec((tm, HEAD), lambda i, j: (i, 0)),
                   pl.BlockSpec((tm, d), lambda i, j: (i, 0))],
        out_shape=[jax.ShapeDtypeStruct((t, n), BF16), jax.ShapeDtypeStruct((t, HEAD), F32),
                   jax.ShapeDtypeStruct((t, d), BF16)],
        compiler_params=_cparams("parallel", "arbitrary"),
        name="fox_qk_proj",
    )(x2, g_pre, mod, mod, w_main, gn, w_small)


def _plain_proj(h, w_main, col0, n, tm=1024, tn=1024):
    t, d = h.shape
    off = col0 // tn
    return pl.pallas_call(
        _plain_proj_kernel,
        grid=(t // tm, n // tn),
        in_specs=[pl.BlockSpec((tm, d), lambda i, j: (i, 0)),
                  pl.BlockSpec((d, tn), lambda i, j: (0, off + j))],
        out_specs=pl.BlockSpec((tm, tn), lambda i, j: (i, j)),
        out_shape=jax.ShapeDtypeStruct((t, n), BF16),
        compiler_params=_cparams("parallel", "arbitrary"),
        name="plain_proj",
    )(h, w_main)


def _gdn_qk_proj(x2, g_pre, mod, w_main, conv_w, qscale, w_small, seq, tm=1024, tn=1024):
    t, d = x2.shape
    n = qscale.shape[1]
    taps = conv_w.shape[0]
    return pl.pallas_call(
        functools.partial(_gdn_qk_kernel, blocks_per_seq=seq // tm),
        grid=(t // tm, n // tn),
        in_specs=_x_mod_specs(tm, d, seq // tm) + [
            pl.BlockSpec((d, tn), lambda i, j: (0, j)),
            pl.BlockSpec((taps, tn), lambda i, j: (0, j)),
            pl.BlockSpec((1, tn), lambda i, j: (0, j)),
            pl.BlockSpec((d, HEAD), lambda i, j: (0, 0))],
        out_specs=[pl.BlockSpec((tm, tn), lambda i, j: (i, j)),
                   pl.BlockSpec((tm, HEAD), lambda i, j: (i, 0)),
                   pl.BlockSpec((tm, d), lambda i, j: (i, 0))],
        out_shape=[jax.ShapeDtypeStruct((t, n), BF16), jax.ShapeDtypeStruct((t, HEAD), F32),
                   jax.ShapeDtypeStruct((t, d), BF16)],
        scratch_shapes=[_carry_scratch(tn, n // tn)],
        compiler_params=_cparams("arbitrary", "arbitrary"),
        name="gdn_qk_proj",
    )(x2, g_pre, mod, mod, w_main, conv_w, qscale, w_small)


def _gdn_v_proj(h, w_main, conv_w, col0, n, seq, tm=1024, tn=1024):
    t, d = h.shape
    taps = conv_w.shape[0]
    off = col0 // tn
    return pl.pallas_call(
        functools.partial(_gdn_v_kernel, blocks_per_seq=seq // tm),
        grid=(t // tm, n // tn),
        in_specs=[pl.BlockSpec((tm, d), lambda i, j: (i, 0)),
                  pl.BlockSpec((d, tn), lambda i, j: (0, off + j)),
                  pl.BlockSpec((taps, tn), lambda i, j: (0, off + j))],
        out_specs=pl.BlockSpec((tm, tn), lambda i, j: (i, j)),
        out_shape=jax.ShapeDtypeStruct((t, n), BF16),
        scratch_shapes=[_carry_scratch(tn, n // tn)],
        compiler_params=_cparams("arbitrary", "arbitrary"),
        name="gdn_v_proj",
    )(h, w_main, conv_w)


def _ffn_up(x2, g_pre, mod, w_up, layer, conv_w, conv_b, seq, tm=1024, tn=512):
    t, d = x2.shape
    dff = w_up.shape[2] // 2
    nj = dff // tn
    taps = conv_w.shape[0]
    return pl.pallas_call(
        functools.partial(_ffn_up_kernel, blocks_per_seq=seq // tm),
        grid=(t // tm, nj),
        in_specs=_x_mod_specs(tm, d, seq // tm) + [
            pl.BlockSpec((None, d, tn), lambda i, j: (layer, 0, j)),
            pl.BlockSpec((None, d, tn), lambda i, j: (layer, 0, j + nj)),
            pl.BlockSpec((taps, tn), lambda i, j: (0, j)),
            pl.BlockSpec((1, tn), lambda i, j: (0, j))],
        out_specs=pl.BlockSpec((tm, tn), lambda i, j: (i, j)),
        out_shape=jax.ShapeDtypeStruct((t, dff), BF16),
        scratch_shapes=[pltpu.VMEM((tm, d), BF16), _carry_scratch(tn, nj)],
        compiler_params=_cparams("arbitrary", "arbitrary"),
        name="ffn_up",
    )(x2, g_pre, mod, mod, w_up, w_up, conv_w, conv_b)


def _out_proj_kernel(a_ref, w_ref, x_ref, g_ref, gate_ref, o_ref):
    y = _dot(a_ref[...], w_ref[...])
    ms = jnp.mean(y * y, axis=-1, keepdims=True)
    o_ref[...] = x_ref[...] + (y * lax.rsqrt(ms + EPS)) * (gate_ref[...] * g_ref[...])


def _out_proj(a, w, layer, x2, g_post, mod, seq, tm):
    t, kd = a.shape
    d = w.shape[2]
    bps = seq // tm
    return pl.pallas_call(
        _out_proj_kernel,
        grid=(t // tm,),
        in_specs=[pl.BlockSpec((tm, kd), lambda i: (i, 0)),
                  pl.BlockSpec((None, kd, d), lambda i: (layer, 0, 0), pipeline_mode=pl.Buffered(1)),
                  pl.BlockSpec((tm, d), lambda i: (i, 0)),
                  pl.BlockSpec((1, d), lambda i: (0, 0)),
                  pl.BlockSpec((None, 1, d), lambda i: (i // bps, 0, 2))],
        out_specs=pl.BlockSpec((tm, d), lambda i: (i, 0)),
        out_shape=jax.ShapeDtypeStruct((t, d), F32),
        compiler_params=_cparams("parallel"),
        name="out_proj",
    )(a, w, x2, g_post, mod)


GATE_ROWS = 256


def _fox_gate_kernel(f_ref, fb_ref, o_ref, carry_ref):
    @pl.when(pl.program_id(1) == 0)
    def _():
        carry_ref[...] = jnp.zeros_like(carry_ref)

    r = lax.broadcasted_iota(jnp.int32, (GATE_ROWS, GATE_ROWS), 0)
    c = lax.broadcasted_iota(jnp.int32, (GATE_ROWS, GATE_ROWS), 1)
    carry = carry_ref[0:1, :]
    for r0 in range(0, f_ref.shape[0], GATE_ROWS):
        rows = slice(r0, r0 + GATE_ROWS)
        x = f_ref[rows, :] + fb_ref[...]
        logf = (jnp.minimum(x, 0.0) - jnp.log1p(jnp.exp(-jnp.abs(x)))) * LOG2E
        cum = _mask_dot(r >= c, logf) + carry
        o_ref[rows, :] = cum
        carry = cum[GATE_ROWS - 1:GATE_ROWS, :]
    carry_ref[0:1, :] = carry


def _fox_gate_cumsum(f_logit, f_bias_row, nb, seq, tm=1024):
    return pl.pallas_call(
        _fox_gate_kernel,
        grid=(nb, seq // tm),
        in_specs=[pl.BlockSpec((None, tm, HEAD), lambda b, i: (b, i, 0)),
                  pl.BlockSpec((1, HEAD), lambda b, i: (0, 0))],
        out_specs=pl.BlockSpec((None, tm, HEAD), lambda b, i: (b, i, 0)),
        out_shape=jax.ShapeDtypeStruct((nb, seq, HEAD), F32),
        scratch_shapes=[pltpu.VMEM((SUBLANES, HEAD), F32)],
        compiler_params=_cparams("parallel", "arbitrary"),
        name="fox_gate_cumsum",
    )(f_logit.reshape(nb, seq, HEAD), f_bias_row)


def _attn_kernel(q_ref, k_ref, v_ref, og_ref, fk_ref, o_ref,
                 s0_ref, s1_ref, m_ref, l_ref, acc_ref, *, tq):
    h = pl.program_id(1)
    qi = pl.program_id(2)
    m_ref[...] = jnp.full(m_ref.shape, -jnp.inf, F32)
    l_ref[...] = jnp.zeros(l_ref.shape, F32)
    acc_ref[...] = jnp.zeros(acc_ref.shape, F32)

    def scores(s_ref, ki):
        ks = pl.multiple_of(ki * tq, tq)
        s_ref[...] = _dot_nt(q_ref[...], k_ref[pl.ds(ks, tq), :]) - fk_ref[pl.ds(h, 1), pl.ds(ks, tq)]

    def update(s_ref, ki, masked):
        ks = pl.multiple_of(ki * tq, tq)
        s = s_ref[...]
        if masked:
            row = lax.broadcasted_iota(jnp.int32, (tq, tq), 0)
            col = lax.broadcasted_iota(jnp.int32, (tq, tq), 1)
            s = jnp.where(col <= row, s, -jnp.inf)
        m_old = m_ref[...]
        m_new = jnp.maximum(m_old, jnp.max(s, axis=-1, keepdims=True))
        p = jnp.exp2(s - m_new)
        alpha = jnp.exp2(m_old - m_new)
        l_ref[...] = alpha * l_ref[...] + jnp.sum(p, axis=-1, keepdims=True)
        m_ref[...] = m_new
        acc_ref[...] = alpha * acc_ref[...] + _dot(p.astype(BF16), v_ref[pl.ds(ks, tq), :])

    scores(s0_ref, 0)

    def pair(pi, carry):
        scores(s1_ref, 2 * pi + 1)
        update(s0_ref, 2 * pi, False)
        scores(s0_ref, 2 * pi + 2)
        update(s1_ref, 2 * pi + 1, False)
        return carry

    lax.fori_loop(0, qi // 2, pair, 0)

    @pl.when(qi % 2 == 1)
    def _():
        scores(s1_ref, qi)
        update(s0_ref, qi - 1, False)
        update(s1_ref, qi, True)

    @pl.when(qi % 2 == 0)
    def _():
        update(s0_ref, qi, True)

    o_ref[...] = (acc_ref[...] / l_ref[...] * jax.nn.sigmoid(og_ref[...].astype(F32))).astype(o_ref.dtype)


def _attn_fixed_shift_kernel(q_ref, k_ref, v_ref, og_ref, fc_ref, c_ref, o_ref,
                             ka_ref, va_ref, qa_ref, p0_ref, p1_ref, acc_ref, *, tq):
    h = pl.program_id(1)
    step = pl.program_id(2)
    seq = k_ref.shape[0]
    nq = seq // tq
    q_blocks = (step, nq - 1 - step)

    def block_rows(i):
        return pl.ds(pl.multiple_of(i * tq, tq), tq)

    def head_col(rows):
        lane = lax.broadcasted_iota(jnp.int32, (tq, HEAD), 1)
        return jnp.sum(jnp.where(lane == h, fc_ref[rows, :], 0.0), axis=-1, keepdims=True)

    def extra_cols(first, second):
        lane = lax.broadcasted_iota(jnp.int32, (tq, HEAD), 1)
        cols = jnp.zeros((tq, HEAD), F32)
        for n, term in enumerate(first + second):
            cols = jnp.where(lane == n, term, cols)
        return cols.astype(BF16)

    ones3 = (1.0, 1.0, 1.0)

    @pl.when(step == 0)
    def _():
        def build(i, carry):
            rows = block_rows(i)
            hi, mid, lo = _split3(-head_col(rows))
            ka_ref[rows, 0:HEAD] = k_ref[rows, :]
            ka_ref[rows, HEAD:2 * HEAD] = extra_cols((hi, mid, lo), ones3)
            va_ref[rows, 0:HEAD] = v_ref[rows, :]
            va_ref[rows, HEAD:2 * HEAD] = jnp.ones((tq, HEAD), BF16)
            return carry
        lax.fori_loop(0, nq, build, 0)

    for n, qb in enumerate(q_blocks):
        rows = block_rows(qb)
        qa_ref[n] = jnp.concatenate(
            [q_ref[rows, :], extra_cols(ones3, _split3(head_col(rows) - c_ref[0:1, 0:1]))], axis=1)
    acc_ref[...] = jnp.zeros(acc_ref.shape, F32)

    def probs(p_ref, which, kb, masked):
        s = _dot_nt(qa_ref[which], ka_ref[block_rows(kb), :])
        if masked:
            row = lax.broadcasted_iota(jnp.int32, (tq, tq), 0)
            col = lax.broadcasted_iota(jnp.int32, (tq, tq), 1)
            s = jnp.where(col <= row, s, -jnp.inf)
        p_ref[...] = jnp.exp2(s).astype(BF16)

    def accumulate(p_ref, which, kb):
        acc_ref[which] += _dot(p_ref[...], va_ref[block_rows(kb), :])

    items = [(0, q_blocks[0], True), (1, q_blocks[1], True)]
    for n in range(nq - 1):
        which = (n >= q_blocks[0]).astype(jnp.int32)
        items.append((which, n - which * q_blocks[0], False))
    p_refs = (p0_ref, p1_ref)
    for n, (which, kb, masked) in enumerate(items):
        probs(p_refs[n % 2], which, kb, masked)
        if n > 0:
            accumulate(p_refs[(n - 1) % 2], *items[n - 1][:2])
    accumulate(p_refs[(len(items) - 1) % 2], *items[-1][:2])

    for n, qb in enumerate(q_blocks):
        rows = block_rows(qb)
        acc = acc_ref[n]
        o_ref[rows, :] = (acc[:, 0:HEAD] / acc[:, HEAD:HEAD + 1]
                          * jax.nn.sigmoid(og_ref[rows, :].astype(F32))).astype(o_ref.dtype)


def _fox_attention_fixed_shift(qk3, v3, og3, fcum, c_row, n_heads, tq=512):
    nb, seq, _ = qk3.shape
    nq = seq // tq
    assert nq % 2 == 0
    full = lambda off: pl.BlockSpec((None, seq, HEAD), lambda b, h, i: (b, 0, off + h))
    return pl.pallas_call(
        functools.partial(_attn_fixed_shift_kernel, tq=tq),
        grid=(nb, n_heads, nq // 2),
        in_specs=[full(0), full(n_heads), full(0), full(0),
                  pl.BlockSpec((None, seq, HEAD), lambda b, h, i: (b, 0, 0)),
                  pl.BlockSpec((1, HEAD), lambda b, h, i: (0, 0))],
        out_specs=full(0),
        out_shape=jax.ShapeDtypeStruct((nb, seq, n_heads * HEAD), BF16),
        scratch_shapes=[pltpu.VMEM((seq, 2 * HEAD), BF16), pltpu.VMEM((seq, 2 * HEAD), BF16),
                        pltpu.VMEM((2, tq, 2 * HEAD), BF16),
                        pltpu.VMEM((tq, tq), BF16), pltpu.VMEM((tq, tq), BF16),
                        pltpu.VMEM((2, tq, 2 * HEAD), F32)],
        compiler_params=_cparams("parallel", "parallel", "arbitrary"),
        name="fox_attention_fixed_shift",
    )(qk3, qk3, v3, og3, fcum, c_row)


ITEM_STRIDE = 32
LIST_UNROLL = 4


def _attn_block_list_kernel(items_ref, count_ref, q_ref, k_ref, v_ref, og_ref, fc_ref, c_ref, o_ref,
                            ka_ref, va_ref, qa_ref, p0_ref, p1_ref, acc_ref, *, tq):
    h = pl.program_id(1)
    bh = pl.program_id(0) * pl.num_programs(1) + h
    seq = k_ref.shape[0]
    nq = seq // tq

    def block_rows(i):
        return pl.ds(pl.multiple_of(i * tq, tq), tq)

    lane1 = lax.broadcasted_iota(jnp.int32, (1, HEAD), 1)
    c_hi, c_mid, c_lo = _split3(c_ref[...])
    k_fixed = jnp.where((lane1 >= 3) & (lane1 < 6), 1.0,
                        jnp.where(lane1 == 6, -c_hi, jnp.where(lane1 == 7, -c_mid,
                                                               jnp.where(lane1 == 8, -c_lo, 0.0))))
    q_fixed = jnp.where((lane1 < 3) | ((lane1 >= 6) & (lane1 < 9)), 1.0, 0.0)

    def build(i, carry):
        rows = block_rows(i)
        lane = lax.broadcasted_iota(jnp.int32, (tq, HEAD), 1)
        f = jnp.sum(jnp.where(lane == h, fc_ref[rows, :], 0.0), axis=-1, keepdims=True)
        hi, mid, lo = _split3(f)
        k_cols = jnp.where(lane == 0, -hi, jnp.where(lane == 1, -mid, jnp.where(lane == 2, -lo, k_fixed)))
        q_cols = jnp.where(lane == 3, hi, jnp.where(lane == 4, mid, jnp.where(lane == 5, lo, q_fixed)))
        ka_ref[rows, 0:HEAD] = k_ref[rows, :]
        ka_ref[rows, HEAD:2 * HEAD] = k_cols.astype(BF16)
        va_ref[rows, 0:HEAD] = v_ref[rows, :]
        va_ref[rows, HEAD:2 * HEAD] = jnp.ones((tq, HEAD), BF16)
        qa_ref[rows, 0:HEAD] = q_ref[rows, :]
        qa_ref[rows, HEAD:2 * HEAD] = q_cols.astype(BF16)
        return carry

    lax.fori_loop(0, nq, build, 0)
    acc_ref[nq] = jnp.zeros(acc_ref.shape[1:], F32)

    def probs(p_ref, qb, kb, masked):
        qb = jnp.minimum(qb, nq - 1)
        s = _dot_nt(qa_ref[block_rows(qb), :], ka_ref[block_rows(kb), :])
        if masked:
            row = lax.broadcasted_iota(jnp.int32, (tq, tq), 0)
            col = lax.broadcasted_iota(jnp.int32, (tq, tq), 1)
            s = jnp.where(col <= row, s, -jnp.inf)
        p_ref[...] = jnp.exp2(s).astype(BF16)

    def product(p_ref, kb):
        return _dot(p_ref[...], va_ref[block_rows(kb), :])

    p_refs = (p0_ref, p1_ref)
    for i in range(nq):
        probs(p_refs[i % 2], i, i, True)
        if i > 0:
            acc_ref[i - 1] = product(p_refs[(i - 1) % 2], i - 1)
    acc_ref[nq - 1] = product(p_refs[(nq - 1) % 2], nq - 1)

    def item(n):
        code = items_ref[bh, n]
        return code // ITEM_STRIDE, code % ITEM_STRIDE

    def accumulate(p_ref, n):
        qb, kb = item(n)
        acc_ref[qb] += product(p_ref, kb)

    probs(p0_ref, *item(0), False)

    def group(m, carry):
        for u in range(LIST_UNROLL):
            n = LIST_UNROLL * m + u
            probs(p_refs[(u + 1) % 2], *item(n + 1), False)
            accumulate(p_refs[u % 2], n)
        return carry

    lax.fori_loop(0, count_ref[bh] // LIST_UNROLL, group, 0)

    def finish(i, carry):
        rows = block_rows(i)
        acc = acc_ref[i]
        o_ref[rows, :] = (acc[:, 0:HEAD] / acc[:, HEAD:HEAD + 1]
                          * jax.nn.sigmoid(og_ref[rows, :].astype(F32))).astype(o_ref.dtype)
        return carry

    lax.fori_loop(0, nq, finish, 0)


def _fox_attention_block_list(qk3, v3, og3, fcum, c_row, items, counts, n_heads, tq=512):
    nb, seq, _ = qk3.shape
    nq = seq // tq
    full = lambda off: pl.BlockSpec((None, seq, HEAD), lambda b, h, *_: (b, 0, off + h))
    grid_spec = pltpu.PrefetchScalarGridSpec(
        num_scalar_prefetch=2,
        grid=(nb, n_heads),
        in_specs=[full(0), full(n_heads), full(0), full(0),
                  pl.BlockSpec((None, seq, HEAD), lambda b, h, *_: (b, 0, 0), pipeline_mode=pl.Buffered(1)),
                  pl.BlockSpec((1, HEAD), lambda b, h, *_: (0, 0))],
        out_specs=full(0),
        scratch_shapes=[pltpu.VMEM((seq, 2 * HEAD), BF16), pltpu.VMEM((seq, 2 * HEAD), BF16),
                        pltpu.VMEM((seq, 2 * HEAD), BF16),
                        pltpu.VMEM((tq, tq), BF16), pltpu.VMEM((tq, tq), BF16),
                        pltpu.VMEM((nq + 1, tq, 2 * HEAD), F32)])
    return pl.pallas_call(
        functools.partial(_attn_block_list_kernel, tq=tq),
        grid_spec=grid_spec,
        out_shape=jax.ShapeDtypeStruct((nb, seq, n_heads * HEAD), BF16),
        compiler_params=_cparams("parallel", "arbitrary"),
        name="fox_attention_block_list",
    )(items, counts, qk3, qk3, v3, og3, fcum, c_row)


MAX_SHIFT_GAP = 100.0
SKIP_LOG2 = 136.0
LIST_MAX_FRACTION = 0.75


def _needed_blocks(fcum, n_heads, tq):
    nb, seq, _ = fcum.shape
    nq = seq // tq
    f_first = fcum[:, 0::tq, :n_heads]
    f_last = fcum[:, tq - 1::tq, :n_heads]
    bound = f_first[:, :, None, :] - f_last[:, None, :, :]
    i = jnp.arange(nq, dtype=jnp.int32)[:, None]
    j = jnp.arange(nq, dtype=jnp.int32)[None, :]
    need = (bound > -SKIP_LOG2) & (j < i)[None, :, :, None]
    need = jnp.transpose(need, (0, 3, 1, 2)).reshape(nb * n_heads, nq * nq)
    pad_code = nq * ITEM_STRIDE
    codes = jnp.where(need, (i * ITEM_STRIDE + j).reshape(1, nq * nq), pad_code)
    codes = jnp.sort(codes, axis=-1)
    codes = jnp.concatenate([codes, jnp.full((nb * n_heads, LIST_UNROLL), pad_code, jnp.int32)], axis=-1)
    counts = jnp.sum(need, axis=-1).astype(jnp.int32)
    return codes.astype(jnp.int32), counts + (-counts) % LIST_UNROLL, jnp.sum(counts)


def _fox_attention(qk3, v3, og3, fcum, gn, n_heads, tq=512):
    nb, seq, _ = qk3.shape
    nq = seq // tq
    qd = n_heads * HEAD
    c = (HEAD * 1.01) * jnp.max(jnp.abs(gn[:, :qd])) * jnp.max(jnp.abs(gn[:, qd:]))
    c_row = jnp.full((1, HEAD), c, F32)
    items, counts, total = _needed_blocks(fcum, n_heads, tq)
    all_pairs = nb * n_heads * (nq * (nq - 1) // 2)

    def block_list(qk3, v3, og3, fcum, c_row, items, counts):
        return _fox_attention_block_list(qk3, v3, og3, fcum, c_row, items, counts, n_heads, tq)

    def full_stream(qk3, v3, og3, fcum, c_row, items, counts):
        return _fox_attention_fixed_shift(qk3, v3, og3, fcum, c_row, n_heads, tq)

    def running_max(qk3, v3, og3, fcum, c_row, items, counts):
        fk = jnp.transpose(fcum[:, :, :n_heads], (0, 2, 1))
        return _fox_attention_running_max(qk3, v3, og3, fk, n_heads, tq)

    branch = jnp.where(2.0 * c > MAX_SHIFT_GAP, 2,
                       jnp.where(total <= LIST_MAX_FRACTION * all_pairs, 0, 1)).astype(jnp.int32)
    return lax.switch(branch, (block_list, full_stream, running_max), qk3, v3, og3, fcum, c_row, items, counts)


def _fox_attention_running_max(qk3, v3, og3, fk, n_heads, tq=512):
    nb, seq, _ = qk3.shape
    return pl.pallas_call(
        functools.partial(_attn_kernel, tq=tq),
        grid=(nb, n_heads, seq // tq),
        in_specs=[pl.BlockSpec((None, tq, HEAD), lambda b, h, i: (b, i, h)),
                  pl.BlockSpec((None, seq, HEAD), lambda b, h, i: (b, 0, n_heads + h)),
                  pl.BlockSpec((None, seq, HEAD), lambda b, h, i: (b, 0, h)),
                  pl.BlockSpec((None, tq, HEAD), lambda b, h, i: (b, i, h)),
                  pl.BlockSpec((None, n_heads, seq), lambda b, h, i: (b, 0, 0))],
        out_specs=pl.BlockSpec((None, tq, HEAD), lambda b, h, i: (b, i, h)),
        out_shape=jax.ShapeDtypeStruct((nb, seq, n_heads * HEAD), BF16),
        scratch_shapes=[pltpu.VMEM((tq, tq), F32), pltpu.VMEM((tq, tq), F32),
                        pltpu.VMEM((tq, 1), F32), pltpu.VMEM((tq, 1), F32), pltpu.VMEM((tq, HEAD), F32)],
        compiler_params=_cparams("parallel", "parallel", "arbitrary"),
        name="fox_attention",
    )(qk3, qk3, v3, og3, fk)


def _gdn_gate_kernel(ab_ref, alog_ref, dt_ref, o_ref, *, n_heads):
    lane = lax.broadcasted_iota(jnp.int32, (GATE_ROWS, HEAD), 1)
    r = lax.broadcasted_iota(jnp.int32, (GATE_ROWS, GATE_ROWS), 0)
    c = lax.broadcasted_iota(jnp.int32, (GATE_ROWS, GATE_ROWS), 1)
    mask = (r >= c) & ((r // CHUNK) == (c // CHUNK))
    for r0 in range(0, ab_ref.shape[0], GATE_ROWS):
        rows = slice(r0, r0 + GATE_ROWS)
        x = ab_ref[rows, :]
        g = -jnp.exp(alog_ref[...]) * _softplus(x + dt_ref[...])
        gc = _mask_dot(mask, jnp.where(lane < n_heads, g, 0.0))
        o_ref[rows, :] = jnp.where(lane < n_heads, gc, jax.nn.sigmoid(x))


def _gdn_gates(ab, alog_row, dt_row, n_heads, tm=1024):
    t = ab.shape[0]
    return pl.pallas_call(
        functools.partial(_gdn_gate_kernel, n_heads=n_heads),
        grid=(t // tm,),
        in_specs=[pl.BlockSpec((tm, HEAD), lambda i: (i, 0)),
                  pl.BlockSpec((1, HEAD), lambda i: (0, 0)),
                  pl.BlockSpec((1, HEAD), lambda i: (0, 0))],
        out_specs=pl.BlockSpec((tm, HEAD), lambda i: (i, 0)),
        out_shape=jax.ShapeDtypeStruct((t, HEAD), F32),
        compiler_params=_cparams("parallel"),
        name="gdn_gates",
    )(ab, alog_row, dt_row)


def _block_diag2(p, half):
    return jnp.concatenate([jnp.where(half, 0.0, p), jnp.where(half, p, 0.0)], axis=0)


def _gdn_chunk_kernel(q_ref, k_ref, v_ref, gb_ref, grow_ref, u_ref, w_ref, qd_ref, kd_ref, qk_ref, egl_ref,
                      *, n_vheads, chunks):
    hk = pl.program_id(1)
    lane = lax.broadcasted_iota(jnp.int32, (CHUNK, HEAD), 1)
    row = lax.broadcasted_iota(jnp.int32, (CHUNK, HEAD), 0)
    col = lane & (CHUNK - 1)
    half = lane >= CHUNK
    lower = row >= col
    strict = row > col
    eye = (row == col).astype(F32)
    level_masks = []
    for lvl in range(6):
        level_masks.append((((row >> lvl) & 1) == 1) & (((col >> lvl) & 1) == 0)
                           & ((row >> (lvl + 1)) == (col >> (lvl + 1))))
    zeros = jnp.zeros((CHUNK, HEAD), F32)

    def pick(gb, idx):
        return jnp.sum(jnp.where(lane == idx, gb, 0.0), axis=-1, keepdims=True)

    sls = [slice(c * CHUNK, (c + 1) * CHUNK) for c in range(chunks)]
    gcs, bes, a_s, xs = [], [], [], []
    for c, sl in enumerate(sls):
        q = q_ref[sl, :]
        k = k_ref[sl, :]
        gb = gb_ref[sl, :]
        gc0, gc1 = pick(gb, 2 * hk), pick(gb, 2 * hk + 1)
        be0, be1 = pick(gb, n_vheads + 2 * hk), pick(gb, n_vheads + 2 * hk + 1)
        gcol = jnp.where(half, gc1, gc0)
        bcol = jnp.where(half, be1, be0)
        decay = jnp.exp(jnp.where(lower, gcol - grow_ref[c:c + 1, :], -jnp.inf))
        qkk = _dot_nt(jnp.concatenate([q, k], axis=0), jnp.concatenate([k, k], axis=0))
        a = jnp.where(strict, qkk[CHUNK:, :] * decay * bcol, 0.0)
        qk_ref[sl, :] = (qkk[:CHUNK, :] * decay).astype(qk_ref.dtype)
        gcs.append((gc0, gc1))
        bes.append((be0, be1))
        a_s.append(a)
        xs.append(eye - jnp.where(level_masks[0], a, 0.0))

    for lvl in range(1, 6):
        ys = [_dot(xs[c].astype(BF16), _block_diag2(jnp.where(level_masks[lvl], a_s[c], 0.0), half).astype(BF16))
              for c in range(chunks)]
        xs = [xs[c] - _dot(ys[c].astype(BF16), _block_diag2(xs[c], half).astype(BF16)) for c in range(chunks)]

    for c, sl in enumerate(sls):
        (gc0, gc1), (be0, be1) = gcs[c], bes[c]
        kf = k_ref[sl, :].astype(F32)
        qf = q_ref[sl, :].astype(F32)
        v = v_ref[sl, :].astype(F32)
        eg0, eg1 = jnp.exp(gc0), jnp.exp(gc1)
        rhs = jnp.concatenate([
            jnp.concatenate([v[:, :HEAD] * be0, kf * (be0 * eg0), zeros, zeros], axis=1),
            jnp.concatenate([zeros, zeros, v[:, HEAD:] * be1, kf * (be1 * eg1)], axis=1)], axis=0)
        sol = _dot(xs[c].astype(BF16), rhs.astype(BF16))
        u_ref[sl, :] = jnp.concatenate([sol[:, 0:HEAD], sol[:, 2 * HEAD:3 * HEAD]], axis=1).astype(u_ref.dtype)
        w_ref[sl, :] = jnp.concatenate([sol[:, HEAD:2 * HEAD], sol[:, 3 * HEAD:]], axis=1).astype(w_ref.dtype)
        qd_ref[sl, :] = jnp.concatenate([qf * eg0, qf * eg1], axis=1).astype(qd_ref.dtype)
        gl0, gl1 = gc0[CHUNK - 1:CHUNK, :], gc1[CHUNK - 1:CHUNK, :]
        kd_ref[sl, :] = jnp.concatenate([kf * jnp.exp(gl0 - gc0), kf * jnp.exp(gl1 - gc1)],
                                        axis=1).astype(kd_ref.dtype)
        egl_ref[0, c:c + 1, :] = jnp.broadcast_to(jnp.exp(gl0), (1, HEAD))
        egl_ref[1, c:c + 1, :] = jnp.broadcast_to(jnp.exp(gl1), (1, HEAD))


def _gdn_chunks(qk3, v3, gb3, grow, n_kheads, n_vheads, tm=1024):
    nb, seq, _ = qk3.shape
    chunks = tm // CHUNK
    vd = n_vheads * HEAD
    big = jax.ShapeDtypeStruct((nb, seq, vd), BF16)
    pair_spec = pl.BlockSpec((None, tm, 2 * HEAD), lambda b, h, i: (b, i, h))
    return pl.pallas_call(
        functools.partial(_gdn_chunk_kernel, n_vheads=n_vheads, chunks=chunks),
        grid=(nb, n_kheads, seq // tm),
        in_specs=[pl.BlockSpec((None, tm, HEAD), lambda b, h, i: (b, i, h)),
                  pl.BlockSpec((None, tm, HEAD), lambda b, h, i: (b, i, n_kheads + h)),
                  pair_spec,
                  pl.BlockSpec((None, tm, HEAD), lambda b, h, i: (b, i, 0)),
                  pl.BlockSpec((None, None, chunks, HEAD), lambda b, h, i: (b, h, i, 0))],
        out_specs=[pair_spec, pair_spec, pair_spec, pair_spec,
                   pl.BlockSpec((None, tm, HEAD), lambda b, h, i: (b, i, h)),
                   pl.BlockSpec((None, 2, chunks, HEAD), lambda b, h, i: (b, h, i, 0))],
        out_shape=[big, big, big, big,
                   jax.ShapeDtypeStruct((nb, seq, n_kheads * HEAD), BF16),
                   jax.ShapeDtypeStruct((nb, n_vheads, seq // CHUNK, HEAD), F32)],
        compiler_params=_cparams("parallel", "parallel", "parallel"),
        name="gdn_chunks",
    )(qk3, qk3, v3, gb3, grow)


def _gdn_scan_kernel(u_ref, w_ref, qd_ref, kd_ref, qk_ref, egl_ref, z_ref, gn_ref, o_ref, s_ref,
                     *, heads, chunks):
    @pl.when(pl.program_id(2) == 0)
    def _():
        s_ref[...] = jnp.zeros_like(s_ref)

    zeros = jnp.zeros((CHUNK, HEAD), F32)
    gn = gn_ref[...]
    cols = [slice(g * HEAD, (g + 1) * HEAD) for g in range(heads)]
    for c in range(chunks):
        sl = slice(c * CHUNK, (c + 1) * CHUNK)
        rs = [_dot(jnp.concatenate([w_ref[sl, cols[g]], qd_ref[sl, cols[g]]], axis=0), s_ref[g].astype(BF16))
              for g in range(heads)]
        vnew = [u_ref[sl, cols[g]].astype(F32) - rs[g][:CHUNK, :] for g in range(heads)]
        intra = []
        for pr in range(heads // 2):
            v2 = jnp.concatenate([jnp.concatenate([vnew[2 * pr], zeros], axis=1),
                                  jnp.concatenate([zeros, vnew[2 * pr + 1]], axis=1)], axis=0).astype(BF16)
            intra.append(_dot(qk_ref[sl, pr * HEAD:(pr + 1) * HEAD], v2))
        for g in range(heads):
            s_ref[g] = s_ref[g] * egl_ref[g, c:c + 1, :] + _dot_tn(kd_ref[sl, cols[g]], vnew[g].astype(BF16))
        for g in range(heads):
            o = rs[g][CHUNK:, :] + intra[g // 2][:, (g % 2) * HEAD:(g % 2 + 1) * HEAD]
            ms = jnp.mean(o * o, axis=-1, keepdims=True)
            z = z_ref[sl, cols[g]].astype(F32)
            o_ref[sl, cols[g]] = (o * lax.rsqrt(ms + EPS) * gn * (z * jax.nn.sigmoid(z))).astype(o_ref.dtype)


def _gdn_scan(u, w, qd, kd, qkm, egl, z3, gn_row, n_vheads, heads=16, tm=512):
    nb, seq, vd = u.shape
    chunks = tm // CHUNK
    wide = pl.BlockSpec((None, tm, heads * HEAD), lambda b, g, i: (b, i, g))
    return pl.pallas_call(
        functools.partial(_gdn_scan_kernel, heads=heads, chunks=chunks),
        grid=(nb, n_vheads // heads, seq // tm),
        in_specs=[wide, wide, wide, wide,
                  pl.BlockSpec((None, tm, heads // 2 * HEAD), lambda b, g, i: (b, i, g)),
                  pl.BlockSpec((None, heads, chunks, HEAD), lambda b, g, i: (b, g, i, 0)),
                  wide,
                  pl.BlockSpec((1, HEAD), lambda b, g, i: (0, 0))],
        out_specs=wide,
        out_shape=jax.ShapeDtypeStruct((nb, seq, vd), BF16),
        scratch_shapes=[pltpu.VMEM((heads, HEAD, HEAD), F32)],
        compiler_params=_cparams("parallel", "parallel", "arbitrary"),
        name="gdn_scan",
    )(u, w, qd, kd, qkm, egl, z3, gn_row)


def _pad_cols(w, n):
    return jnp.pad(w, ((0, 0), (0, n - w.shape[1])))


def _row(v, n=HEAD):
    return jnp.pad(v.astype(F32), (0, n - v.shape[0])).reshape(1, n)


def _fox_layer(x2, nb, seq, g_pre, g_post, mod, w_in, f_bias, q_norm, k_norm, w_o, layer):
    d = x2.shape[1]
    nh = d // HEAD
    qd = nh * HEAD
    w_small = _pad_cols(w_in[:, 3 * qd:3 * qd + nh], HEAD)
    w_gate = w_in[:, 3 * qd + nh:]
    gn = jnp.concatenate([jnp.tile(q_norm.astype(F32) * (HEAD ** -0.5 * LOG2E), nh),
                          jnp.tile(k_norm.astype(F32), nh)]).reshape(1, 2 * qd)
    qk, f_logit, h = _fox_qk_proj(x2, g_pre, mod, w_in, gn, w_small, seq)
    v = _plain_proj(h, w_in, 2 * qd, qd)
    og = _plain_proj(h, w_gate, 0, qd)
    fcum = _fox_gate_cumsum(f_logit, _row(f_bias), nb, seq)
    attn = _fox_attention(qk.reshape(nb, seq, 2 * qd), v.reshape(nb, seq, qd), og.reshape(nb, seq, qd),
                          fcum, gn, nh)
    return _out_proj(attn.reshape(nb * seq, qd), w_o, layer, x2, g_post, mod, seq, tm=512)


def _gdn_layer(x2, nb, seq, g_pre, g_post, mod, w_in, conv_w, a_log, dt_bias, out_norm, w_o, layer):
    d = x2.shape[1]
    nk = d // HEAD
    nv = 2 * nk
    kd, vd = nk * HEAD, nv * HEAD
    n_main = 2 * kd + 2 * vd
    w_small = _pad_cols(w_in[:, n_main:], HEAD)
    conv_w = conv_w.astype(F32)
    qscale = jnp.concatenate([jnp.full((kd,), HEAD ** -0.5, F32), jnp.ones((kd,), F32)]).reshape(1, 2 * kd)
    qk, ab, h = _gdn_qk_proj(x2, g_pre, mod, w_in, conv_w, qscale, w_small, seq)
    v = _gdn_v_proj(h, w_in, conv_w, 2 * kd, vd, seq)
    z = _plain_proj(h, w_in, 2 * kd + vd, vd)
    gb = _gdn_gates(ab, _row(a_log), _row(dt_bias), nv)
    n_chunks = seq // CHUNK
    grow = gb[:, :nv].reshape(nb, n_chunks, CHUNK, nk, 2)
    grow = jnp.transpose(grow, (0, 3, 1, 4, 2)).reshape(nb, nk, n_chunks, 2 * CHUNK)
    u, w, qdec, kdec, qkm, egl = _gdn_chunks(qk.reshape(nb, seq, 2 * kd), v.reshape(nb, seq, vd),
                                             gb.reshape(nb, seq, HEAD), grow, nk, nv)
    a = _gdn_scan(u, w, qdec, kdec, qkm, egl, z.reshape(nb, seq, vd), _row(out_norm), nv)
    return _out_proj(a.reshape(nb * seq, vd), w_o, layer, x2, g_post, mod, seq, tm=512)


def _ffn_layer(x2, seq, g_pre, g_post, mod, w_up, conv_w, conv_b, w_down, layer):
    u = _ffn_up(x2, g_pre, mod, w_up, layer, conv_w.astype(F32), conv_b.reshape(1, -1).astype(F32), seq)
    return _out_proj(u, w_down, layer, x2, g_post, mod, seq, tm=256)


def kernel(x, c, ada_w, ada_b, norm_g, fox_w_in, fox_f_bias, fox_q_norm, fox_k_norm, fox_w_o,
           gdn_w_in, gdn_conv_w, gdn_a_log, gdn_dt_bias, gdn_out_norm, gdn_w_o,
           ffn_w_up, ffn_conv_w, ffn_conv_b, ffn_w_down):
    nb, seq, d = x.shape
    depth = ada_w.shape[0]
    mods = _ada_mods(c, ada_w, ada_b)
    fox_w_in, fox_w_o, gdn_w_in, gdn_w_o, ffn_w_up, ffn_w_down = (
        w.astype(BF16) for w in (fox_w_in, fox_w_o, gdn_w_in, gdn_w_o, ffn_w_up, ffn_w_down))
    x2 = x.reshape(nb * seq, d)
    for i in range(depth):
        g = [norm_g[i, r].reshape(1, d).astype(F32) for r in range(4)]
        j = i // 2
        if i % 2 == 0:
            x2 = _fox_layer(x2, nb, seq, g[0], g[1], mods[2 * i], fox_w_in[j], fox_f_bias[j],
                            fox_q_norm[j], fox_k_norm[j], fox_w_o, j)
        else:
            x2 = _gdn_layer(x2, nb, seq, g[0], g[1], mods[2 * i], gdn_w_in[j], gdn_conv_w[j],
                            gdn_a_log[j], gdn_dt_bias[j], gdn_out_norm[j], gdn_w_o, j)
        x2 = _ffn_layer(x2, seq, g[2], g[3], mods[2 * i + 1], ffn_w_up, ffn_conv_w[i],
                        ffn_conv_b[i], ffn_w_down, i)
    return x2.reshape(nb, seq, d)
```

```python
import functools

import jax
import jax.numpy as jnp
from jax import lax
from jax.experimental import pallas as pl
from jax.experimental.pallas import tpu as pltpu

F32 = jnp.float32
BF16 = jnp.bfloat16
EPS = 1e-6
HEAD = 128
CHUNK = 64
SUBLANES = 8
MXU_N = 256
VMEM_LIMIT = 56 * 1024 * 1024
INV_SQRT2 = 0.7071067811865476
LOG2E = 1.4426950408889634


def _cparams(*sem):
    return pltpu.CompilerParams(dimension_semantics=sem, vmem_limit_bytes=VMEM_LIMIT)


def _dot(a, b):
    return jnp.dot(a, b, preferred_element_type=F32)


def _dot_nt(a, b):
    return lax.dot_general(a, b, (((1,), (1,)), ((), ())), preferred_element_type=F32)


def _dot_tn(a, b):
    return lax.dot_general(a, b, (((0,), (0,)), ((), ())), preferred_element_type=F32)


def _split3(x):
    hi = x.astype(BF16).astype(F32)
    mid = (x - hi).astype(BF16).astype(F32)
    return hi, mid, x - hi - mid


def _mask_dot(mask, x):
    m = jnp.where(mask, 1.0, 0.0).astype(BF16)
    hi, mid, lo = _split3(x)
    return _dot(m, hi.astype(BF16)) + _dot(m, mid.astype(BF16)) + _dot(m, lo.astype(BF16))


def _softplus(x):
    return jnp.maximum(x, 0.0) + jnp.log1p(jnp.exp(-jnp.abs(x)))


def _ada_kernel(c_ref, w_ref, b_ref, o_ref):
    o_ref[0] = jnp.zeros(o_ref.shape[1:], F32)
    for b in range(c_ref.shape[0]):
        c = c_ref[b]
        row = jnp.sum(w_ref[0] * (c * jax.nn.sigmoid(c)), axis=0, keepdims=True)
        o_ref[0, b:b + 1, :] = row + b_ref[0]


def _ada_mods(c, ada_w, ada_b):
    nb, d = c.shape
    assert nb <= SUBLANES
    n = ada_w.shape[0] * ada_w.shape[1]
    w = ada_w.reshape(n, d, 3 * d)
    b = ada_b.reshape(n, 1, 3 * d)
    tn = 1024
    out = pl.pallas_call(
        _ada_kernel,
        grid=(n, 3 * d // tn),
        in_specs=[pl.BlockSpec((nb, d, 1), lambda s, j: (0, 0, 0)),
                  pl.BlockSpec((1, d, tn), lambda s, j: (s, 0, j)),
                  pl.BlockSpec((1, 1, tn), lambda s, j: (s, 0, j))],
        out_specs=pl.BlockSpec((1, SUBLANES, tn), lambda s, j: (s, 0, j)),
        out_shape=jax.ShapeDtypeStruct((n, SUBLANES, 3 * d), F32),
        compiler_params=_cparams("parallel", "parallel"),
        name="ada_mods",
    )(c.reshape(nb, d, 1), w, b)
    return out[:, :nb].reshape(n, nb, 1, 3 * d)


NORM_ROWS = 32


def _norm_mod(h_ref, x_ref, g_ref, sh_ref, sc_ref):
    gain = g_ref[...] * (1.0 + sc_ref[...])
    shift = sh_ref[...]

    def rows_pass(r, carry):
        rows = pl.ds(pl.multiple_of(r * NORM_ROWS, NORM_ROWS), NORM_ROWS)
        x = x_ref[rows, :]
        ms = jnp.mean(x * x, axis=-1, keepdims=True)
        h_ref[rows, :] = (x * lax.rsqrt(ms + EPS) * gain + shift).astype(BF16)
        return carry

    lax.fori_loop(0, x_ref.shape[0] // NORM_ROWS, rows_pass, 0, unroll=8)


def _sub_dots(h_ref, w_ref):
    return [_dot(h_ref[...], w_ref[:, c * MXU_N:(c + 1) * MXU_N]) for c in range(w_ref.shape[1] // MXU_N)]


def _head_norm_store(o_ref, c, y, gain_ref, mean):
    for g in range(MXU_N // HEAD):
        sl = slice(c * MXU_N + g * HEAD, c * MXU_N + (g + 1) * HEAD)
        t = y[:, g * HEAD:(g + 1) * HEAD]
        ss = jnp.sum(t * t, axis=-1, keepdims=True)
        if mean:
            ss = ss * (1.0 / HEAD)
        o_ref[:, sl] = (t * lax.rsqrt(ss + EPS) * gain_ref[:, sl]).astype(o_ref.dtype)


def _causal_conv(acc, cw, carry_ref, slot, first):
    tm = acc.shape[0]
    taps = cw.shape[0]
    prev = jnp.where(first, 0.0, carry_ref[slot])
    carry_ref[slot] = acc[tm - SUBLANES:tm, :]
    xe = jnp.concatenate([prev, acc], axis=0)
    ye = xe * cw[0:1, :]
    for k in range(1, taps):
        ye = xe * cw[k:k + 1, :] + pltpu.roll(ye, 1, axis=0)
    return ye[SUBLANES:, :]


def _zero_carry_at_start(carry_ref):
    @pl.when((pl.program_id(0) == 0) & (pl.program_id(1) == 0))
    def _():
        carry_ref[...] = jnp.zeros_like(carry_ref)


def _fox_qk_kernel(x_ref, g_ref, sh_ref, sc_ref, w_ref, gn_ref, ws_ref, o_ref, os_ref, h_ref):
    @pl.when(pl.program_id(1) == 0)
    def _():
        _norm_mod(h_ref, x_ref, g_ref, sh_ref, sc_ref)
        os_ref[...] = _dot(h_ref[...], ws_ref[...])

    for c, acc in enumerate(_sub_dots(h_ref, w_ref)):
        _head_norm_store(o_ref, c, acc, gn_ref, mean=True)


def _plain_proj_kernel(h_ref, w_ref, o_ref):
    o_ref[...] = _dot(h_ref[...], w_ref[...]).astype(o_ref.dtype)


def _gdn_qk_kernel(x_ref, g_ref, sh_ref, sc_ref, w_ref, cw_ref, qs_ref, ws_ref, o_ref, os_ref, h_ref,
                   carry_ref, *, blocks_per_seq):
    i = pl.program_id(0)
    j = pl.program_id(1)
    _zero_carry_at_start(carry_ref)

    @pl.when(j == 0)
    def _():
        _norm_mod(h_ref, x_ref, g_ref, sh_ref, sc_ref)
        os_ref[...] = _dot(h_ref[...], ws_ref[...])

    first = (i % blocks_per_seq) == 0
    accs = _sub_dots(h_ref, w_ref)
    for c, acc in enumerate(accs):
        y = _causal_conv(acc, cw_ref[:, c * MXU_N:(c + 1) * MXU_N], carry_ref, j * len(accs) + c, first)
        _head_norm_store(o_ref, c, y * jax.nn.sigmoid(y), qs_ref, mean=False)


def _gdn_v_kernel(h_ref, w_ref, cw_ref, o_ref, carry_ref, *, blocks_per_seq):
    i = pl.program_id(0)
    j = pl.program_id(1)
    _zero_carry_at_start(carry_ref)
    first = (i % blocks_per_seq) == 0
    accs = _sub_dots(h_ref, w_ref)
    for c, acc in enumerate(accs):
        cs = slice(c * MXU_N, (c + 1) * MXU_N)
        y = _causal_conv(acc, cw_ref[:, cs], carry_ref, j * len(accs) + c, first)
        o_ref[:, cs] = (y * jax.nn.sigmoid(y)).astype(o_ref.dtype)


def _ffn_up_kernel(x_ref, g_ref, sh_ref, sc_ref, wg_ref, wv_ref, cw_ref, cb_ref, o_ref,
                   h_ref, carry_ref, *, blocks_per_seq):
    i = pl.program_id(0)
    j = pl.program_id(1)
    _zero_carry_at_start(carry_ref)

    @pl.when(j == 0)
    def _():
        _norm_mod(h_ref, x_ref, g_ref, sh_ref, sc_ref)

    first = (i % blocks_per_seq) == 0
    n_sub = wg_ref.shape[1] // MXU_N
    pairs = []
    for c in range(n_sub):
        cs = slice(c * MXU_N, (c + 1) * MXU_N)
        pairs.append((_dot(h_ref[...], wg_ref[:, cs]), _dot(h_ref[...], wv_ref[:, cs])))
    for c, (gate, val) in enumerate(pairs):
        cs = slice(c * MXU_N, (c + 1) * MXU_N)
        gate = _causal_conv(gate, cw_ref[:, cs], carry_ref, j * n_sub + c, first) + cb_ref[:, cs]
        act = 0.5 * gate * (1.0 + lax.erf(gate * INV_SQRT2))
        o_ref[:, cs] = (act * val).astype(o_ref.dtype)


def _x_mod_specs(tm, d, bps):
    return [pl.BlockSpec((tm, d), lambda i, j: (i, 0)),
            pl.BlockSpec((1, d), lambda i, j: (0, 0)),
            pl.BlockSpec((None, 1, d), lambda i, j: (i // bps, 0, 0)),
            pl.BlockSpec((None, 1, d), lambda i, j: (i // bps, 0, 1))]


def _carry_scratch(tn, nj):
    return pltpu.VMEM((nj * (tn // MXU_N), SUBLANES, MXU_N), F32)


def _fox_qk_proj(x2, g_pre, mod, w_main, gn, w_small, seq, tm=1024, tn=1024):
    t, d = x2.shape
    n = gn.shape[1]
    return pl.pallas_call(
        _fox_qk_kernel,
        grid=(t // tm, n // tn),
        in_specs=_x_mod_specs(tm, d, seq // tm) + [
            pl.BlockSpec((d, tn), lambda i, j: (0, j)),
            pl.BlockSpec((1, tn), lambda i, j: (0, j)),
            pl.BlockSpec((d, HEAD), lambda i, j: (0, 0))],
        out_specs=[pl.BlockSpec((tm, tn), lambda i, j: (i, j)),
                   pl.BlockSpec((tm, HEAD), lambda i, j: (i, 0)),
                   pl.BlockSpec((tm, d), lambda i, j: (i, 0))],
        out_shape=[jax.ShapeDtypeStruct((t, n), BF16), jax.ShapeDtypeStruct((t, HEAD), F32),
                   jax.ShapeDtypeStruct((t, d), BF16)],
        compiler_params=_cparams("parallel", "arbitrary"),
        name="fox_qk_proj",
    )(x2, g_pre, mod, mod, w_main, gn, w_small)


def _plain_proj(h, w_main, col0, n, tm=1024, tn=1024):
    t, d = h.shape
    off = col0 // tn
    return pl.pallas_call(
        _plain_proj_kernel,
        grid=(t // tm, n // tn),
        in_specs=[pl.BlockSpec((tm, d), lambda i, j: (i, 0)),
                  pl.BlockSpec((d, tn), lambda i, j: (0, off + j))],
        out_specs=pl.BlockSpec((tm, tn), lambda i, j: (i, j)),
        out_shape=jax.ShapeDtypeStruct((t, n), BF16),
        compiler_params=_cparams("parallel", "arbitrary"),
        name="plain_proj",
    )(h, w_main)


def _gdn_qk_proj(x2, g_pre, mod, w_main, conv_w, qscale, w_small, seq, tm=1024, tn=1024):
    t, d = x2.shape
    n = qscale.shape[1]
    taps = conv_w.shape[0]
    return pl.pallas_call(
        functools.partial(_gdn_qk_kernel, blocks_per_seq=seq // tm),
        grid=(t // tm, n // tn),
        in_specs=_x_mod_specs(tm, d, seq // tm) + [
            pl.BlockSpec((d, tn), lambda i, j: (0, j)),
            pl.BlockSpec((taps, tn), lambda i, j: (0, j)),
            pl.BlockSpec((1, tn), lambda i, j: (0, j)),
            pl.BlockSpec((d, HEAD), lambda i, j: (0, 0))],
        out_specs=[pl.BlockSpec((tm, tn), lambda i, j: (i, j)),
                   pl.BlockSpec((tm, HEAD), lambda i, j: (i, 0)),
                   pl.BlockSpec((tm, d), lambda i, j: (i, 0))],
        out_shape=[jax.ShapeDtypeStruct((t, n), BF16), jax.ShapeDtypeStruct((t, HEAD), F32),
                   jax.ShapeDtypeStruct((t, d), BF16)],
        scratch_shapes=[_carry_scratch(tn, n // tn)],
        compiler_params=_cparams("arbitrary", "arbitrary"),
        name="gdn_qk_proj",
    )(x2, g_pre, mod, mod, w_main, conv_w, qscale, w_small)


def _gdn_v_proj(h, w_main, conv_w, col0, n, seq, tm=1024, tn=1024):
    t, d = h.shape
    taps = conv_w.shape[0]
    off = col0 // tn
    return pl.pallas_call(
        functools.partial(_gdn_v_kernel, blocks_per_seq=seq // tm),
        grid=(t // tm, n // tn),
        in_specs=[pl.BlockSpec((tm, d), lambda i, j: (i, 0)),
                  pl.BlockSpec((d, tn), lambda i, j: (0, off + j)),
                  pl.BlockSpec((taps, tn), lambda i, j: (0, off + j))],
        out_specs=pl.BlockSpec((tm, tn), lambda i, j: (i, j)),
        out_shape=jax.ShapeDtypeStruct((t, n), BF16),
        scratch_shapes=[_carry_scratch(tn, n // tn)],
        compiler_params=_cparams("arbitrary", "arbitrary"),
        name="gdn_v_proj",
    )(h, w_main, conv_w)


def _ffn_up(x2, g_pre, mod, w_up, layer, conv_w, conv_b, seq, tm=1024, tn=512):
    t, d = x2.shape
    dff = w_up.shape[2] // 2
    nj = dff // tn
    taps = conv_w.shape[0]
    return pl.pallas_call(
        functools.partial(_ffn_up_kernel, blocks_per_seq=seq // tm),
        grid=(t // tm, nj),
        in_specs=_x_mod_specs(tm, d, seq // tm) + [
            pl.BlockSpec((None, d, tn), lambda i, j: (layer, 0, j)),
            pl.BlockSpec((None, d, tn), lambda i, j: (layer, 0, j + nj)),
            pl.BlockSpec((taps, tn), lambda i, j: (0, j)),
            pl.BlockSpec((1, tn), lambda i, j: (0, j))],
        out_specs=pl.BlockSpec((tm, tn), lambda i, j: (i, j)),
        out_shape=jax.ShapeDtypeStruct((t, dff), BF16),
        scratch_shapes=[pltpu.VMEM((tm, d), BF16), _carry_scratch(tn, nj)],
        compiler_params=_cparams("arbitrary", "arbitrary"),
        name="ffn_up",
    )(x2, g_pre, mod, mod, w_up, w_up, conv_w, conv_b)


def _out_proj_kernel(a_ref, w_ref, x_ref, g_ref, gate_ref, o_ref):
    y = _dot(a_ref[...], w_ref[...])
    ms = jnp.mean(y * y, axis=-1, keepdims=True)
    o_ref[...] = x_ref[...] + (y * lax.rsqrt(ms + EPS)) * (gate_ref[...] * g_ref[...])


def _out_proj(a, w, layer, x2, g_post, mod, seq, tm):
    t, kd = a.shape
    d = w.shape[2]
    bps = seq // tm
    return pl.pallas_call(
        _out_proj_kernel,
        grid=(t // tm,),
        in_specs=[pl.BlockSpec((tm, kd), lambda i: (i, 0)),
                  pl.BlockSpec((None, kd, d), lambda i: (layer, 0, 0), pipeline_mode=pl.Buffered(1)),
                  pl.BlockSpec((tm, d), lambda i: (i, 0)),
                  pl.BlockSpec((1, d), lambda i: (0, 0)),
                  pl.BlockSpec((None, 1, d), lambda i: (i // bps, 0, 2))],
        out_specs=pl.BlockSpec((tm, d), lambda i: (i, 0)),
        out_shape=jax.ShapeDtypeStruct((t, d), F32),
        compiler_params=_cparams("parallel"),
        name="out_proj",
    )(a, w, x2, g_post, mod)


GATE_ROWS = 256


def _fox_gate_kernel(f_ref, fb_ref, o_ref, carry_ref):
    @pl.when(pl.program_id(1) == 0)
    def _():
        carry_ref[...] = jnp.zeros_like(carry_ref)

    r = lax.broadcasted_iota(jnp.int32, (GATE_ROWS, GATE_ROWS), 0)
    c = lax.broadcasted_iota(jnp.int32, (GATE_ROWS, GATE_ROWS), 1)
    carry = carry_ref[0:1, :]
    for r0 in range(0, f_ref.shape[0], GATE_ROWS):
        rows = slice(r0, r0 + GATE_ROWS)
        x = f_ref[rows, :] + fb_ref[...]
        logf = (jnp.minimum(x, 0.0) - jnp.log1p(jnp.exp(-jnp.abs(x)))) * LOG2E
        cum = _mask_dot(r >= c, logf) + carry
        o_ref[rows, :] = cum
        carry = cum[GATE_ROWS - 1:GATE_ROWS, :]
    carry_ref[0:1, :] = carry


def _fox_gate_cumsum(f_logit, f_bias_row, nb, seq, tm=1024):
    return pl.pallas_call(
        _fox_gate_kernel,
        grid=(nb, seq // tm),
        in_specs=[pl.BlockSpec((None, tm, HEAD), lambda b, i: (b, i, 0)),
                  pl.BlockSpec((1, HEAD), lambda b, i: (0, 0))],
        out_specs=pl.BlockSpec((None, tm, HEAD), lambda b, i: (b, i, 0)),
        out_shape=jax.ShapeDtypeStruct((nb, seq, HEAD), F32),
        scratch_shapes=[pltpu.VMEM((SUBLANES, HEAD), F32)],
        compiler_params=_cparams("parallel", "arbitrary"),
        name="fox_gate_cumsum",
    )(f_logit.reshape(nb, seq, HEAD), f_bias_row)


def _attn_kernel(q_ref, k_ref, v_ref, og_ref, fk_ref, o_ref,
                 s0_ref, s1_ref, m_ref, l_ref, acc_ref, *, tq):
    h = pl.program_id(1)
    qi = pl.program_id(2)
    m_ref[...] = jnp.full(m_ref.shape, -jnp.inf, F32)
    l_ref[...] = jnp.zeros(l_ref.shape, F32)
    acc_ref[...] = jnp.zeros(acc_ref.shape, F32)

    def scores(s_ref, ki):
        ks = pl.multiple_of(ki * tq, tq)
        s_ref[...] = _dot_nt(q_ref[...], k_ref[pl.ds(ks, tq), :]) - fk_ref[pl.ds(h, 1), pl.ds(ks, tq)]

    def update(s_ref, ki, masked):
        ks = pl.multiple_of(ki * tq, tq)
        s = s_ref[...]
        if masked:
            row = lax.broadcasted_iota(jnp.int32, (tq, tq), 0)
            col = lax.broadcasted_iota(jnp.int32, (tq, tq), 1)
            s = jnp.where(col <= row, s, -jnp.inf)
        m_old = m_ref[...]
        m_new = jnp.maximum(m_old, jnp.max(s, axis=-1, keepdims=True))
        p = jnp.exp2(s - m_new)
        alpha = jnp.exp2(m_old - m_new)
        l_ref[...] = alpha * l_ref[...] + jnp.sum(p, axis=-1, keepdims=True)
        m_ref[...] = m_new
        acc_ref[...] = alpha * acc_ref[...] + _dot(p.astype(BF16), v_ref[pl.ds(ks, tq), :])

    scores(s0_ref, 0)

    def pair(pi, carry):
        scores(s1_ref, 2 * pi + 1)
        update(s0_ref, 2 * pi, False)
        scores(s0_ref, 2 * pi + 2)
        update(s1_ref, 2 * pi + 1, False)
        return carry

    lax.fori_loop(0, qi // 2, pair, 0)

    @pl.when(qi % 2 == 1)
    def _():
        scores(s1_ref, qi)
        update(s0_ref, qi - 1, False)
        update(s1_ref, qi, True)

    @pl.when(qi % 2 == 0)
    def _():
        update(s0_ref, qi, True)

    o_ref[...] = (acc_ref[...] / l_ref[...] * jax.nn.sigmoid(og_ref[...].astype(F32))).astype(o_ref.dtype)


def _attn_fixed_shift_kernel(q_ref, k_ref, v_ref, og_ref, fc_ref, c_ref, o_ref,
                             ka_ref, va_ref, qa_ref, p0_ref, p1_ref, acc_ref, *, tq):
    h = pl.program_id(1)
    step = pl.program_id(2)
    seq = k_ref.shape[0]
    nq = seq // tq
    q_blocks = (step, nq - 1 - step)

    def block_rows(i):
        return pl.ds(pl.multiple_of(i * tq, tq), tq)

    def head_col(rows):
        lane = lax.broadcasted_iota(jnp.int32, (tq, HEAD), 1)
        return jnp.sum(jnp.where(lane == h, fc_ref[rows, :], 0.0), axis=-1, keepdims=True)

    def extra_cols(first, second):
        lane = lax.broadcasted_iota(jnp.int32, (tq, HEAD), 1)
        cols = jnp.zeros((tq, HEAD), F32)
        for n, term in enumerate(first + second):
            cols = jnp.where(lane == n, term, cols)
        return cols.astype(BF16)

    ones3 = (1.0, 1.0, 1.0)

    @pl.when(step == 0)
    def _():
        def build(i, carry):
            rows = block_rows(i)
            hi, mid, lo = _split3(-head_col(rows))
            ka_ref[rows, 0:HEAD] = k_ref[rows, :]
            ka_ref[rows, HEAD:2 * HEAD] = extra_cols((hi, mid, lo), ones3)
            va_ref[rows, 0:HEAD] = v_ref[rows, :]
            va_ref[rows, HEAD:2 * HEAD] = jnp.ones((tq, HEAD), BF16)
            return carry
        lax.fori_loop(0, nq, build, 0)

    for n, qb in enumerate(q_blocks):
        rows = block_rows(qb)
        qa_ref[n] = jnp.concatenate(
            [q_ref[rows, :], extra_cols(ones3, _split3(head_col(rows) - c_ref[0:1, 0:1]))], axis=1)
    acc_ref[...] = jnp.zeros(acc_ref.shape, F32)

    def probs(p_ref, which, kb, masked):
        s = _dot_nt(qa_ref[which], ka_ref[block_rows(kb), :])
        if masked:
            row = lax.broadcasted_iota(jnp.int32, (tq, tq), 0)
            col = lax.broadcasted_iota(jnp.int32, (tq, tq), 1)
            s = jnp.where(col <= row, s, -jnp.inf)
        p_ref[...] = jnp.exp2(s).astype(BF16)

    def accumulate(p_ref, which, kb):
        acc_ref[which] += _dot(p_ref[...], va_ref[block_rows(kb), :])

    items = [(0, q_blocks[0], True), (1, q_blocks[1], True)]
    for n in range(nq - 1):
        which = (n >= q_blocks[0]).astype(jnp.int32)
        items.append((which, n - which * q_blocks[0], False))
    p_refs = (p0_ref, p1_ref)
    for n, (which, kb, masked) in enumerate(items):
        probs(p_refs[n % 2], which, kb, masked)
        if n > 0:
            accumulate(p_refs[(n - 1) % 2], *items[n - 1][:2])
    accumulate(p_refs[(len(items) - 1) % 2], *items[-1][:2])

    for n, qb in enumerate(q_blocks):
        rows = block_rows(qb)
        acc = acc_ref[n]
        o_ref[rows, :] = (acc[:, 0:HEAD] / acc[:, HEAD:HEAD + 1]
                          * jax.nn.sigmoid(og_ref[rows, :].astype(F32))).astype(o_ref.dtype)


def _fox_attention_fixed_shift(qk3, v3, og3, fcum, c_row, n_heads, tq=512):
    nb, seq, _ = qk3.shape
    nq = seq // tq
    assert nq % 2 == 0
    full = lambda off: pl.BlockSpec((None, seq, HEAD), lambda b, h, i: (b, 0, off + h))
    return pl.pallas_call(
        functools.partial(_attn_fixed_shift_kernel, tq=tq),
        grid=(nb, n_heads, nq // 2),
        in_specs=[full(0), full(n_heads), full(0), full(0),
                  pl.BlockSpec((None, seq, HEAD), lambda b, h, i: (b, 0, 0)),
                  pl.BlockSpec((1, HEAD), lambda b, h, i: (0, 0))],
        out_specs=full(0),
        out_shape=jax.ShapeDtypeStruct((nb, seq, n_heads * HEAD), BF16),
        scratch_shapes=[pltpu.VMEM((seq, 2 * HEAD), BF16), pltpu.VMEM((seq, 2 * HEAD), BF16),
                        pltpu.VMEM((2, tq, 2 * HEAD), BF16),
                        pltpu.VMEM((tq, tq), BF16), pltpu.VMEM((tq, tq), BF16),
                        pltpu.VMEM((2, tq, 2 * HEAD), F32)],
        compiler_params=_cparams("parallel", "parallel", "arbitrary"),
        name="fox_attention_fixed_shift",
    )(qk3, qk3, v3, og3, fcum, c_row)


ITEM_STRIDE = 32
LIST_UNROLL = 4


def _attn_block_list_kernel(items_ref, count_ref, q_ref, k_ref, v_ref, og_ref, fc_ref, c_ref, o_ref,
                            ka_ref, va_ref, qa_ref, p0_ref, p1_ref, acc_ref, *, tq):
    h = pl.program_id(1)
    bh = pl.program_id(0) * pl.num_programs(1) + h
    seq = k_ref.shape[0]
    nq = seq // tq

    def block_rows(i):
        return pl.ds(pl.multiple_of(i * tq, tq), tq)

    lane1 = lax.broadcasted_iota(jnp.int32, (1, HEAD), 1)
    c_hi, c_mid, c_lo = _split3(c_ref[...])
    k_fixed = jnp.where((lane1 >= 3) & (lane1 < 6), 1.0,
                        jnp.where(lane1 == 6, -c_hi, jnp.where(lane1 == 7, -c_mid,
                                                               jnp.where(lane1 == 8, -c_lo, 0.0))))
    q_fixed = jnp.where((lane1 < 3) | ((lane1 >= 6) & (lane1 < 9)), 1.0, 0.0)

    def build(i, carry):
        rows = block_rows(i)
        lane = lax.broadcasted_iota(jnp.int32, (tq, HEAD), 1)
        f = jnp.sum(jnp.where(lane == h, fc_ref[rows, :], 0.0), axis=-1, keepdims=True)
        hi, mid, lo = _split3(f)
        k_cols = jnp.where(lane == 0, -hi, jnp.where(lane == 1, -mid, jnp.where(lane == 2, -lo, k_fixed)))
        q_cols = jnp.where(lane == 3, hi, jnp.where(lane == 4, mid, jnp.where(lane == 5, lo, q_fixed)))
        ka_ref[rows, 0:HEAD] = k_ref[rows, :]
        ka_ref[rows, HEAD:2 * HEAD] = k_cols.astype(BF16)
        va_ref[rows, 0:HEAD] = v_ref[rows, :]
        va_ref[rows, HEAD:2 * HEAD] = jnp.ones((tq, HEAD), BF16)
        qa_ref[rows, 0:HEAD] = q_ref[rows, :]
        qa_ref[rows, HEAD:2 * HEAD] = q_cols.astype(BF16)
        return carry

    lax.fori_loop(0, nq, build, 0)
    acc_ref[nq] = jnp.zeros(acc_ref.shape[1:], F32)

    def probs(p_ref, qb, kb, masked):
        qb = jnp.minimum(qb, nq - 1)
        s = _dot_nt(qa_ref[block_rows(qb), :], ka_ref[block_rows(kb), :])
        if masked:
            row = lax.broadcasted_iota(jnp.int32, (tq, tq), 0)
            col = lax.broadcasted_iota(jnp.int32, (tq, tq), 1)
            s = jnp.where(col <= row, s, -jnp.inf)
        p_ref[...] = jnp.exp2(s).astype(BF16)

    def product(p_ref, kb):
        return _dot(p_ref[...], va_ref[block_rows(kb), :])

    p_refs = (p0_ref, p1_ref)
    for i in range(nq):
        probs(p_refs[i % 2], i, i, True)
        if i > 0:
            acc_ref[i - 1] = product(p_refs[(i - 1) % 2], i - 1)
    acc_ref[nq - 1] = product(p_refs[(nq - 1) % 2], nq - 1)

    def item(n):
        code = items_ref[bh, n]
        return code // ITEM_STRIDE, code % ITEM_STRIDE

    def accumulate(p_ref, n):
        qb, kb = item(n)
        acc_ref[qb] += product(p_ref, kb)

    probs(p0_ref, *item(0), False)

    def group(m, carry):
        for u in range(LIST_UNROLL):
            n = LIST_UNROLL * m + u
            probs(p_refs[(u + 1) % 2], *item(n + 1), False)
            accumulate(p_refs[u % 2], n)
        return carry

    lax.fori_loop(0, count_ref[bh] // LIST_UNROLL, group, 0)

    def finish(i, carry):
        rows = block_rows(i)
        acc = acc_ref[i]
        o_ref[rows, :] = (acc[:, 0:HEAD] / acc[:, HEAD:HEAD + 1]
                          * jax.nn.sigmoid(og_ref[rows, :].astype(F32))).astype(o_ref.dtype)
        return carry

    lax.fori_loop(0, nq, finish, 0)


def _fox_attention_block_list(qk3, v3, og3, fcum, c_row, items, counts, n_heads, tq=512):
    nb, seq, _ = qk3.shape
    nq = seq // tq
    full = lambda off: pl.BlockSpec((None, seq, HEAD), lambda b, h, *_: (b, 0, off + h))
    grid_spec = pltpu.PrefetchScalarGridSpec(
        num_scalar_prefetch=2,
        grid=(nb, n_heads),
        in_specs=[full(0), full(n_heads), full(0), full(0),
                  pl.BlockSpec((None, seq, HEAD), lambda b, h, *_: (b, 0, 0), pipeline_mode=pl.Buffered(1)),
                  pl.BlockSpec((1, HEAD), lambda b, h, *_: (0, 0))],
        out_specs=full(0),
        scratch_shapes=[pltpu.VMEM((seq, 2 * HEAD), BF16), pltpu.VMEM((seq, 2 * HEAD), BF16),
                        pltpu.VMEM((seq, 2 * HEAD), BF16),
                        pltpu.VMEM((tq, tq), BF16), pltpu.VMEM((tq, tq), BF16),
                        pltpu.VMEM((nq + 1, tq, 2 * HEAD), F32)])
    return pl.pallas_call(
        functools.partial(_attn_block_list_kernel, tq=tq),
        grid_spec=grid_spec,
        out_shape=jax.ShapeDtypeStruct((nb, seq, n_heads * HEAD), BF16),
        compiler_params=_cparams("parallel", "arbitrary"),
        name="fox_attention_block_list",
    )(items, counts, qk3, qk3, v3, og3, fcum, c_row)


MAX_SHIFT_GAP = 100.0
SKIP_LOG2 = 136.0
LIST_MAX_FRACTION = 0.75


def _needed_blocks(fcum, n_heads, tq):
    nb, seq, _ = fcum.shape
    nq = seq // tq
    f_first = fcum[:, 0::tq, :n_heads]
    f_last = fcum[:, tq - 1::tq, :n_heads]
    bound = f_first[:, :, None, :] - f_last[:, None, :, :]
    i = jnp.arange(nq, dtype=jnp.int32)[:, None]
    j = jnp.arange(nq, dtype=jnp.int32)[None, :]
    need = (bound > -SKIP_LOG2) & (j < i)[None, :, :, None]
    need = jnp.transpose(need, (0, 3, 1, 2)).reshape(nb * n_heads, nq * nq)
    pad_code = nq * ITEM_STRIDE
    codes = jnp.where(need, (i * ITEM_STRIDE + j).reshape(1, nq * nq), pad_code)
    codes = jnp.sort(codes, axis=-1)
    codes = jnp.concatenate([codes, jnp.full((nb * n_heads, LIST_UNROLL), pad_code, jnp.int32)], axis=-1)
    counts = jnp.sum(need, axis=-1).astype(jnp.int32)
    return codes.astype(jnp.int32), counts + (-counts) % LIST_UNROLL, jnp.sum(counts)


def _fox_attention(qk3, v3, og3, fcum, gn, n_heads, tq=512):
    nb, seq, _ = qk3.shape
    nq = seq // tq
    qd = n_heads * HEAD
    c = (HEAD * 1.01) * jnp.max(jnp.abs(gn[:, :qd])) * jnp.max(jnp.abs(gn[:, qd:]))
    c_row = jnp.full((1, HEAD), c, F32)
    items, counts, total = _needed_blocks(fcum, n_heads, tq)
    all_pairs = nb * n_heads * (nq * (nq - 1) // 2)

    def block_list(qk3, v3, og3, fcum, c_row, items, counts):
        return _fox_attention_block_list(qk3, v3, og3, fcum, c_row, items, counts, n_heads, tq)

    def full_stream(qk3, v3, og3, fcum, c_row, items, counts):
        return _fox_attention_fixed_shift(qk3, v3, og3, fcum, c_row, n_heads, tq)

    def running_max(qk3, v3, og3, fcum, c_row, items, counts):
        fk = jnp.transpose(fcum[:, :, :n_heads], (0, 2, 1))
        return _fox_attention_running_max(qk3, v3, og3, fk, n_heads, tq)

    branch = jnp.where(2.0 * c > MAX_SHIFT_GAP, 2,
                       jnp.where(total <= LIST_MAX_FRACTION * all_pairs, 0, 1)).astype(jnp.int32)
    return lax.switch(branch, (block_list, full_stream, running_max), qk3, v3, og3, fcum, c_row, items, counts)


def _fox_attention_running_max(qk3, v3, og3, fk, n_heads, tq=512):
    nb, seq, _ = qk3.shape
    return pl.pallas_call(
        functools.partial(_attn_kernel, tq=tq),
        grid=(nb, n_heads, seq // tq),
        in_specs=[pl.BlockSpec((None, tq, HEAD), lambda b, h, i: (b, i, h)),
                  pl.BlockSpec((None, seq, HEAD), lambda b, h, i: (b, 0, n_heads + h)),
                  pl.BlockSpec((None, seq, HEAD), lambda b, h, i: (b, 0, h)),
                  pl.BlockSpec((None, tq, HEAD), lambda b, h, i: (b, i, h)),
                  pl.BlockSpec((None, n_heads, seq), lambda b, h, i: (b, 0, 0))],
        out_specs=pl.BlockSpec((None, tq, HEAD), lambda b, h, i: (b, i, h)),
        out_shape=jax.ShapeDtypeStruct((nb, seq, n_heads * HEAD), BF16),
        scratch_shapes=[pltpu.VMEM((tq, tq), F32), pltpu.VMEM((tq, tq), F32),
                        pltpu.VMEM((tq, 1), F32), pltpu.VMEM((tq, 1), F32), pltpu.VMEM((tq, HEAD), F32)],
        compiler_params=_cparams("parallel", "parallel", "arbitrary"),
        name="fox_attention",
    )(qk3, qk3, v3, og3, fk)


def _gdn_gate_kernel(ab_ref, alog_ref, dt_ref, o_ref, *, n_heads):
    lane = lax.broadcasted_iota(jnp.int32, (GATE_ROWS, HEAD), 1)
    r = lax.broadcasted_iota(jnp.int32, (GATE_ROWS, GATE_ROWS), 0)
    c = lax.broadcasted_iota(jnp.int32, (GATE_ROWS, GATE_ROWS), 1)
    mask = (r >= c) & ((r // CHUNK) == (c // CHUNK))
    for r0 in range(0, ab_ref.shape[0], GATE_ROWS):
        rows = slice(r0, r0 + GATE_ROWS)
        x = ab_ref[rows, :]
        g = -jnp.exp(alog_ref[...]) * _softplus(x + dt_ref[...])
        gc = _mask_dot(mask, jnp.where(lane < n_heads, g, 0.0))
        o_ref[rows, :] = jnp.where(lane < n_heads, gc, jax.nn.sigmoid(x))


def _gdn_gates(ab, alog_row, dt_row, n_heads, tm=1024):
    t = ab.shape[0]
    return pl.pallas_call(
        functools.partial(_gdn_gate_kernel, n_heads=n_heads),
        grid=(t // tm,),
        in_specs=[pl.BlockSpec((tm, HEAD), lambda i: (i, 0)),
                  pl.BlockSpec((1, HEAD), lambda i: (0, 0)),
                  pl.BlockSpec((1, HEAD), lambda i: (0, 0))],
        out_specs=pl.BlockSpec((tm, HEAD), lambda i: (i, 0)),
        out_shape=jax.ShapeDtypeStruct((t, HEAD), F32),
        compiler_params=_cparams("parallel"),
        name="gdn_gates",
    )(ab, alog_row, dt_row)


def _block_diag2(p, half):
    return jnp.concatenate([jnp.where(half, 0.0, p), jnp.where(half, p, 0.0)], axis=0)


def _gdn_chunk_kernel(q_ref, k_ref, v_ref, gb_ref, grow_ref, u_ref, w_ref, qd_ref, kd_ref, qk_ref, egl_ref,
                      *, n_vheads, chunks):
    hk = pl.program_id(1)
    lane = lax.broadcasted_iota(jnp.int32, (CHUNK, HEAD), 1)
    row = lax.broadcasted_iota(jnp.int32, (CHUNK, HEAD), 0)
    col = lane & (CHUNK - 1)
    half = lane >= CHUNK
    lower = row >= col
    strict = row > col
    eye = (row == col).astype(F32)
    level_masks = []
    for lvl in range(6):
        level_masks.append((((row >> lvl) & 1) == 1) & (((col >> lvl) & 1) == 0)
                           & ((row >> (lvl + 1)) == (col >> (lvl + 1))))
    zeros = jnp.zeros((CHUNK, HEAD), F32)

    def pick(gb, idx):
        return jnp.sum(jnp.where(lane == idx, gb, 0.0), axis=-1, keepdims=True)

    sls = [slice(c * CHUNK, (c + 1) * CHUNK) for c in range(chunks)]
    gcs, bes, a_s, xs = [], [], [], []
    for c, sl in enumerate(sls):
        q = q_ref[sl, :]
        k = k_ref[sl, :]
        gb = gb_ref[sl, :]
        gc0, gc1 = pick(gb, 2 * hk), pick(gb, 2 * hk + 1)
        be0, be1 = pick(gb, n_vheads + 2 * hk), pick(gb, n_vheads + 2 * hk + 1)
        gcol = jnp.where(half, gc1, gc0)
        bcol = jnp.where(half, be1, be0)
        decay = jnp.exp(jnp.where(lower, gcol - grow_ref[c:c + 1, :], -jnp.inf))
        qkk = _dot_nt(jnp.concatenate([q, k], axis=0), jnp.concatenate([k, k], axis=0))
        a = jnp.where(strict, qkk[CHUNK:, :] * decay * bcol, 0.0)
        qk_ref[sl, :] = (qkk[:CHUNK, :] * decay).astype(qk_ref.dtype)
        gcs.append((gc0, gc1))
        bes.append((be0, be1))
        a_s.append(a)
        xs.append(eye - jnp.where(level_masks[0], a, 0.0))

    for lvl in range(1, 6):
        ys = [_dot(xs[c].astype(BF16), _block_diag2(jnp.where(level_masks[lvl], a_s[c], 0.0), half).astype(BF16))
              for c in range(chunks)]
        xs = [xs[c] - _dot(ys[c].astype(BF16), _block_diag2(xs[c], half).astype(BF16)) for c in range(chunks)]

    for c, sl in enumerate(sls):
        (gc0, gc1), (be0, be1) = gcs[c], bes[c]
        kf = k_ref[sl, :].astype(F32)
        qf = q_ref[sl, :].astype(F32)
        v = v_ref[sl, :].astype(F32)
        eg0, eg1 = jnp.exp(gc0), jnp.exp(gc1)
        rhs = jnp.concatenate([
            jnp.concatenate([v[:, :HEAD] * be0, kf * (be0 * eg0), zeros, zeros], axis=1),
            jnp.concatenate([zeros, zeros, v[:, HEAD:] * be1, kf * (be1 * eg1)], axis=1)], axis=0)
        sol = _dot(xs[c].astype(BF16), rhs.astype(BF16))
        u_ref[sl, :] = jnp.concatenate([sol[:, 0:HEAD], sol[:, 2 * HEAD:3 * HEAD]], axis=1).astype(u_ref.dtype)
        w_ref[sl, :] = jnp.concatenate([sol[:, HEAD:2 * HEAD], sol[:, 3 * HEAD:]], axis=1).astype(w_ref.dtype)
        qd_ref[sl, :] = jnp.concatenate([qf * eg0, qf * eg1], axis=1).astype(qd_ref.dtype)
        gl0, gl1 = gc0[CHUNK - 1:CHUNK, :], gc1[CHUNK - 1:CHUNK, :]
        kd_ref[sl, :] = jnp.concatenate([kf * jnp.exp(gl0 - gc0), kf * jnp.exp(gl1 - gc1)],
                                        axis=1).astype(kd_ref.dtype)
        egl_ref[0, c:c + 1, :] = jnp.broadcast_to(jnp.exp(gl0), (1, HEAD))
        egl_ref[1, c:c + 1, :] = jnp.broadcast_to(jnp.exp(gl1), (1, HEAD))


def _gdn_chunks(qk3, v3, gb3, grow, n_kheads, n_vheads, tm=2048):
    nb, seq, _ = qk3.shape
    chunks = tm // CHUNK
    vd = n_vheads * HEAD
    big = jax.ShapeDtypeStruct((nb, seq, vd), BF16)
    pair_spec = pl.BlockSpec((None, tm, 2 * HEAD), lambda b, h, i: (b, i, h))
    return pl.pallas_call(
        functools.partial(_gdn_chunk_kernel, n_vheads=n_vheads, chunks=chunks),
        grid=(nb, n_kheads, seq // tm),
        in_specs=[pl.BlockSpec((None, tm, HEAD), lambda b, h, i: (b, i, h)),
                  pl.BlockSpec((None, tm, HEAD), lambda b, h, i: (b, i, n_kheads + h)),
                  pair_spec,
                  pl.BlockSpec((None, tm, HEAD), lambda b, h, i: (b, i, 0)),
                  pl.BlockSpec((None, None, chunks, HEAD), lambda b, h, i: (b, h, i, 0))],
        out_specs=[pair_spec, pair_spec, pair_spec, pair_spec,
                   pl.BlockSpec((None, tm, HEAD), lambda b, h, i: (b, i, h)),
                   pl.BlockSpec((None, 2, chunks, HEAD), lambda b, h, i: (b, h, i, 0))],
        out_shape=[big, big, big, big,
                   jax.ShapeDtypeStruct((nb, seq, n_kheads * HEAD), BF16),
                   jax.ShapeDtypeStruct((nb, n_vheads, seq // CHUNK, HEAD), F32)],
        compiler_params=_cparams("parallel", "parallel", "parallel"),
        name="gdn_chunks",
    )(qk3, qk3, v3, gb3, grow)


def _gdn_scan_kernel(u_ref, w_ref, qd_ref, kd_ref, qk_ref, egl_ref, z_ref, gn_ref, o_ref, s_ref,
                     *, heads, chunks):
    @pl.when(pl.program_id(2) == 0)
    def _():
        s_ref[...] = jnp.zeros_like(s_ref)

    zeros = jnp.zeros((CHUNK, HEAD), F32)
    gn = gn_ref[...]
    cols = [slice(g * HEAD, (g + 1) * HEAD) for g in range(heads)]
    for c in range(chunks):
        sl = slice(c * CHUNK, (c + 1) * CHUNK)
        rs = [_dot(jnp.concatenate([w_ref[sl, cols[g]], qd_ref[sl, cols[g]]], axis=0), s_ref[g].astype(BF16))
              for g in range(heads)]
        vnew = [u_ref[sl, cols[g]].astype(F32) - rs[g][:CHUNK, :] for g in range(heads)]
        intra = []
        for pr in range(heads // 2):
            v2 = jnp.concatenate([jnp.concatenate([vnew[2 * pr], zeros], axis=1),
                                  jnp.concatenate([zeros, vnew[2 * pr + 1]], axis=1)], axis=0).astype(BF16)
            intra.append(_dot(qk_ref[sl, pr * HEAD:(pr + 1) * HEAD], v2))
        for g in range(heads):
            s_ref[g] = s_ref[g] * egl_ref[g, c:c + 1, :] + _dot_tn(kd_ref[sl, cols[g]], vnew[g].astype(BF16))
        for g in range(heads):
            o = rs[g][CHUNK:, :] + intra[g // 2][:, (g % 2) * HEAD:(g % 2 + 1) * HEAD]
            ms = jnp.mean(o * o, axis=-1, keepdims=True)
            z = z_ref[sl, cols[g]].astype(F32)
            o_ref[sl, cols[g]] = (o * lax.rsqrt(ms + EPS) * gn * (z * jax.nn.sigmoid(z))).astype(o_ref.dtype)


def _gdn_scan(u, w, qd, kd, qkm, egl, z3, gn_row, n_vheads, heads=16, tm=512):
    nb, seq, vd = u.shape
    chunks = tm // CHUNK
    wide = pl.BlockSpec((None, tm, heads * HEAD), lambda b, g, i: (b, i, g))
    return pl.pallas_call(
        functools.partial(_gdn_scan_kernel, heads=heads, chunks=chunks),
        grid=(nb, n_vheads // heads, seq // tm),
        in_specs=[wide, wide, wide, wide,
                  pl.BlockSpec((None, tm, heads // 2 * HEAD), lambda b, g, i: (b, i, g)),
                  pl.BlockSpec((None, heads, chunks, HEAD), lambda b, g, i: (b, g, i, 0)),
                  wide,
                  pl.BlockSpec((1, HEAD), lambda b, g, i: (0, 0))],
        out_specs=wide,
        out_shape=jax.ShapeDtypeStruct((nb, seq, vd), BF16),
        scratch_shapes=[pltpu.VMEM((heads, HEAD, HEAD), F32)],
        compiler_params=_cparams("parallel", "parallel", "arbitrary"),
        name="gdn_scan",
    )(u, w, qd, kd, qkm, egl, z3, gn_row)


def _pad_cols(w, n):
    return jnp.pad(w, ((0, 0), (0, n - w.shape[1])))


def _row(v, n=HEAD):
    return jnp.pad(v.astype(F32), (0, n - v.shape[0])).reshape(1, n)


def _fox_layer(x2, nb, seq, g_pre, g_post, mod, w_in, f_bias, q_norm, k_norm, w_o, layer):
    d = x2.shape[1]
    nh = d // HEAD
    qd = nh * HEAD
    w_small = _pad_cols(w_in[:, 3 * qd:3 * qd + nh], HEAD)
    w_gate = w_in[:, 3 * qd + nh:]
    gn = jnp.concatenate([jnp.tile(q_norm.astype(F32) * (HEAD ** -0.5 * LOG2E), nh),
                          jnp.tile(k_norm.astype(F32), nh)]).reshape(1, 2 * qd)
    qk, f_logit, h = _fox_qk_proj(x2, g_pre, mod, w_in, gn, w_small, seq)
    v = _plain_proj(h, w_in, 2 * qd, qd)
    og = _plain_proj(h, w_gate, 0, qd)
    fcum = _fox_gate_cumsum(f_logit, _row(f_bias), nb, seq)
    attn = _fox_attention(qk.reshape(nb, seq, 2 * qd), v.reshape(nb, seq, qd), og.reshape(nb, seq, qd),
                          fcum, gn, nh)
    return _out_proj(attn.reshape(nb * seq, qd), w_o, layer, x2, g_post, mod, seq, tm=512)


def _gdn_layer(x2, nb, seq, g_pre, g_post, mod, w_in, conv_w, a_log, dt_bias, out_norm, w_o, layer):
    d = x2.shape[1]
    nk = d // HEAD
    nv = 2 * nk
    kd, vd = nk * HEAD, nv * HEAD
    n_main = 2 * kd + 2 * vd
    w_small = _pad_cols(w_in[:, n_main:], HEAD)
    conv_w = conv_w.astype(F32)
    qscale = jnp.concatenate([jnp.full((kd,), HEAD ** -0.5, F32), jnp.ones((kd,), F32)]).reshape(1, 2 * kd)
    qk, ab, h = _gdn_qk_proj(x2, g_pre, mod, w_in, conv_w, qscale, w_small, seq)
    v = _gdn_v_proj(h, w_in, conv_w, 2 * kd, vd, seq)
    z = _plain_proj(h, w_in, 2 * kd + vd, vd)
    gb = _gdn_gates(ab, _row(a_log), _row(dt_bias), nv)
    n_chunks = seq // CHUNK
    grow = gb[:, :nv].reshape(nb, n_chunks, CHUNK, nk, 2)
    grow = jnp.transpose(grow, (0, 3, 1, 4, 2)).reshape(nb, nk, n_chunks, 2 * CHUNK)
    u, w, qdec, kdec, qkm, egl = _gdn_chunks(qk.reshape(nb, seq, 2 * kd), v.reshape(nb, seq, vd),
                                             gb.reshape(nb, seq, HEAD), grow, nk, nv)
    a = _gdn_scan(u, w, qdec, kdec, qkm, egl, z.reshape(nb, seq, vd), _row(out_norm), nv)
    return _out_proj(a.reshape(nb * seq, vd), w_o, layer, x2, g_post, mod, seq, tm=512)


def _ffn_layer(x2, seq, g_pre, g_post, mod, w_up, conv_w, conv_b, w_down, layer):
    u = _ffn_up(x2, g_pre, mod, w_up, layer, conv_w.astype(F32), conv_b.reshape(1, -1).astype(F32), seq)
    return _out_proj(u, w_down, layer, x2, g_post, mod, seq, tm=256)


def kernel(x, c, ada_w, ada_b, norm_g, fox_w_in, fox_f_bias, fox_q_norm, fox_k_norm, fox_w_o,
           gdn_w_in, gdn_conv_w, gdn_a_log, gdn_dt_bias, gdn_out_norm, gdn_w_o,
           ffn_w_up, ffn_conv_w, ffn_conv_b, ffn_w_down):
    nb, seq, d = x.shape
    depth = ada_w.shape[0]
    mods = _ada_mods(c, ada_w, ada_b)
    fox_w_in, fox_w_o, gdn_w_in, gdn_w_o, ffn_w_up, ffn_w_down = (
        w.astype(BF16) for w in (fox_w_in, fox_w_o, gdn_w_in, gdn_w_o, ffn_w_up, ffn_w_down))
    x2 = x.reshape(nb * seq, d)
    for i in range(depth):
        g = [norm_g[i, r].reshape(1, d).astype(F32) for r in range(4)]
        j = i // 2
        if i % 2 == 0:
            x2 = _fox_layer(x2, nb, seq, g[0], g[1], mods[2 * i], fox_w_in[j], fox_f_bias[j],
                            fox_q_norm[j], fox_k_norm[j], fox_w_o, j)
        else:
            x2 = _gdn_layer(x2, nb, seq, g[0], g[1], mods[2 * i], gdn_w_in[j], gdn_conv_w[j],
                            gdn_a_log[j], gdn_dt_bias[j], gdn_out_norm[j], gdn_w_o, j)
        x2 = _ffn_layer(x2, seq, g[2], g[3], mods[2 * i + 1], ffn_w_up, ffn_conv_w[i],
                        ffn_conv_b[i], ffn_w_down, i)
    return x2.reshape(nb, seq, d)
```

```python
import functools

import jax
import jax.numpy as jnp
from jax import lax
from jax.experimental import pallas as pl
from jax.experimental.pallas import tpu as pltpu

F32 = jnp.float32
BF16 = jnp.bfloat16
EPS = 1e-6
HEAD = 128
CHUNK = 64
SUBLANES = 8
MXU_N = 256
VMEM_LIMIT = 56 * 1024 * 1024
INV_SQRT2 = 0.7071067811865476
LOG2E = 1.4426950408889634


def _cparams(*sem):
    return pltpu.CompilerParams(dimension_semantics=sem, vmem_limit_bytes=VMEM_LIMIT)


def _dot(a, b):
    return jnp.dot(a, b, preferred_element_type=F32)


def _dot_nt(a, b):
    return lax.dot_general(a, b, (((1,), (1,)), ((), ())), preferred_element_type=F32)


def _dot_tn(a, b):
    return lax.dot_general(a, b, (((0,), (0,)), ((), ())), preferred_element_type=F32)


def _split3(x):
    hi = x.astype(BF16).astype(F32)
    mid = (x - hi).astype(BF16).astype(F32)
    return hi, mid, x - hi - mid


def _mask_dot(mask, x):
    m = jnp.where(mask, 1.0, 0.0).astype(BF16)
    hi, mid, lo = _split3(x)
    return _dot(m, hi.astype(BF16)) + _dot(m, mid.astype(BF16)) + _dot(m, lo.astype(BF16))


def _softplus(x):
    return jnp.maximum(x, 0.0) + jnp.log1p(jnp.exp(-jnp.abs(x)))


def _ada_kernel(c_ref, w_ref, b_ref, o_ref):
    o_ref[0] = jnp.zeros(o_ref.shape[1:], F32)
    for b in range(c_ref.shape[0]):
        c = c_ref[b]
        row = jnp.sum(w_ref[0] * (c * jax.nn.sigmoid(c)), axis=0, keepdims=True)
        o_ref[0, b:b + 1, :] = row + b_ref[0]


def _ada_mods(c, ada_w, ada_b):
    nb, d = c.shape
    assert nb <= SUBLANES
    n = ada_w.shape[0] * ada_w.shape[1]
    w = ada_w.reshape(n, d, 3 * d)
    b = ada_b.reshape(n, 1, 3 * d)
    tn = 1024
    out = pl.pallas_call(
        _ada_kernel,
        grid=(n, 3 * d // tn),
        in_specs=[pl.BlockSpec((nb, d, 1), lambda s, j: (0, 0, 0)),
                  pl.BlockSpec((1, d, tn), lambda s, j: (s, 0, j)),
                  pl.BlockSpec((1, 1, tn), lambda s, j: (s, 0, j))],
        out_specs=pl.BlockSpec((1, SUBLANES, tn), lambda s, j: (s, 0, j)),
        out_shape=jax.ShapeDtypeStruct((n, SUBLANES, 3 * d), F32),
        compiler_params=_cparams("parallel", "parallel"),
        name="ada_mods",
    )(c.reshape(nb, d, 1), w, b)
    return out[:, :nb].reshape(n, nb, 1, 3 * d)


NORM_ROWS = 32


def _norm_mod(h_ref, x_ref, g_ref, sh_ref, sc_ref):
    gain = g_ref[...] * (1.0 + sc_ref[...])
    shift = sh_ref[...]

    def rows_pass(r, carry):
        rows = pl.ds(pl.multiple_of(r * NORM_ROWS, NORM_ROWS), NORM_ROWS)
        x = x_ref[rows, :]
        ms = jnp.mean(x * x, axis=-1, keepdims=True)
        h_ref[rows, :] = (x * lax.rsqrt(ms + EPS) * gain + shift).astype(BF16)
        return carry

    lax.fori_loop(0, x_ref.shape[0] // NORM_ROWS, rows_pass, 0, unroll=8)


def _sub_dots(h_ref, w_ref):
    return [_dot(h_ref[...], w_ref[:, c * MXU_N:(c + 1) * MXU_N]) for c in range(w_ref.shape[1] // MXU_N)]


def _head_norm_store(o_ref, c, y, gain_ref, mean):
    for g in range(MXU_N // HEAD):
        sl = slice(c * MXU_N + g * HEAD, c * MXU_N + (g + 1) * HEAD)
        t = y[:, g * HEAD:(g + 1) * HEAD]
        ss = jnp.sum(t * t, axis=-1, keepdims=True)
        if mean:
            ss = ss * (1.0 / HEAD)
        o_ref[:, sl] = (t * lax.rsqrt(ss + EPS) * gain_ref[:, sl]).astype(o_ref.dtype)


def _causal_conv(acc, cw, carry_ref, slot, first):
    tm = acc.shape[0]
    taps = cw.shape[0]
    prev = jnp.where(first, 0.0, carry_ref[slot])
    carry_ref[slot] = acc[tm - SUBLANES:tm, :]
    xe = jnp.concatenate([prev, acc], axis=0)
    ye = xe * cw[0:1, :]
    for k in range(1, taps):
        ye = xe * cw[k:k + 1, :] + pltpu.roll(ye, 1, axis=0)
    return ye[SUBLANES:, :]


def _zero_carry_at_start(carry_ref):
    @pl.when((pl.program_id(0) == 0) & (pl.program_id(1) == 0))
    def _():
        carry_ref[...] = jnp.zeros_like(carry_ref)


def _fox_qk_kernel(x_ref, g_ref, sh_ref, sc_ref, w_ref, gn_ref, ws_ref, o_ref, os_ref, h_ref):
    @pl.when(pl.program_id(1) == 0)
    def _():
        _norm_mod(h_ref, x_ref, g_ref, sh_ref, sc_ref)
        os_ref[...] = _dot(h_ref[...], ws_ref[...])

    for c, acc in enumerate(_sub_dots(h_ref, w_ref)):
        _head_norm_store(o_ref, c, acc, gn_ref, mean=True)


def _plain_proj_kernel(h_ref, w_ref, o_ref):
    o_ref[...] = _dot(h_ref[...], w_ref[...]).astype(o_ref.dtype)


def _gdn_qk_kernel(x_ref, g_ref, sh_ref, sc_ref, w_ref, cw_ref, ws_ref, o_ref, os_ref, h_ref,
                   carry_ref, *, blocks_per_seq):
    i = pl.program_id(0)
    j = pl.program_id(1)
    _zero_carry_at_start(carry_ref)

    @pl.when(j == 0)
    def _():
        _norm_mod(h_ref, x_ref, g_ref, sh_ref, sc_ref)
        os_ref[...] = _dot(h_ref[...], ws_ref[...])

    first = (i % blocks_per_seq) == 0
    accs = _sub_dots(h_ref, w_ref)
    for c, acc in enumerate(accs):
        cs = slice(c * MXU_N, (c + 1) * MXU_N)
        y = _causal_conv(acc, cw_ref[:, cs], carry_ref, j * len(accs) + c, first)
        o_ref[:, cs] = (y * jax.nn.sigmoid(y)).astype(o_ref.dtype)


def _gdn_v_kernel(h_ref, w_ref, cw_ref, o_ref, carry_ref, *, blocks_per_seq):
    i = pl.program_id(0)
    j = pl.program_id(1)
    _zero_carry_at_start(carry_ref)
    first = (i % blocks_per_seq) == 0
    accs = _sub_dots(h_ref, w_ref)
    for c, acc in enumerate(accs):
        cs = slice(c * MXU_N, (c + 1) * MXU_N)
        y = _causal_conv(acc, cw_ref[:, cs], carry_ref, j * len(accs) + c, first)
        o_ref[:, cs] = (y * jax.nn.sigmoid(y)).astype(o_ref.dtype)


def _ffn_up_kernel(x_ref, g_ref, sh_ref, sc_ref, wg_ref, wv_ref, cw_ref, cb_ref, o_ref,
                   h_ref, carry_ref, *, blocks_per_seq):
    i = pl.program_id(0)
    j = pl.program_id(1)
    _zero_carry_at_start(carry_ref)

    @pl.when(j == 0)
    def _():
        _norm_mod(h_ref, x_ref, g_ref, sh_ref, sc_ref)

    first = (i % blocks_per_seq) == 0
    n_sub = wg_ref.shape[1] // MXU_N
    pairs = []
    for c in range(n_sub):
        cs = slice(c * MXU_N, (c + 1) * MXU_N)
        pairs.append((_dot(h_ref[...], wg_ref[:, cs]), _dot(h_ref[...], wv_ref[:, cs])))
    for c, (gate, val) in enumerate(pairs):
        cs = slice(c * MXU_N, (c + 1) * MXU_N)
        gate = _causal_conv(gate, cw_ref[:, cs], carry_ref, j * n_sub + c, first) + cb_ref[:, cs]
        act = 0.5 * gate * (1.0 + lax.erf(gate * INV_SQRT2))
        o_ref[:, cs] = (act * val).astype(o_ref.dtype)


def _x_mod_specs(tm, d, bps):
    return [pl.BlockSpec((tm, d), lambda i, j: (i, 0)),
            pl.BlockSpec((1, d), lambda i, j: (0, 0)),
            pl.BlockSpec((None, 1, d), lambda i, j: (i // bps, 0, 0)),
            pl.BlockSpec((None, 1, d), lambda i, j: (i // bps, 0, 1))]


def _carry_scratch(tn, nj):
    return pltpu.VMEM((nj * (tn // MXU_N), SUBLANES, MXU_N), F32)


def _fox_qk_proj(x2, g_pre, mod, w_main, gn, w_small, seq, tm=1024, tn=1024):
    t, d = x2.shape
    n = gn.shape[1]
    return pl.pallas_call(
        _fox_qk_kernel,
        grid=(t // tm, n // tn),
        in_specs=_x_mod_specs(tm, d, seq // tm) + [
            pl.BlockSpec((d, tn), lambda i, j: (0, j)),
            pl.BlockSpec((1, tn), lambda i, j: (0, j)),
            pl.BlockSpec((d, HEAD), lambda i, j: (0, 0))],
        out_specs=[pl.BlockSpec((tm, tn), lambda i, j: (i, j)),
                   pl.BlockSpec((tm, HEAD), lambda i, j: (i, 0)),
                   pl.BlockSpec((tm, d), lambda i, j: (i, 0))],
        out_shape=[jax.ShapeDtypeStruct((t, n), BF16), jax.ShapeDtypeStruct((t, HEAD), F32),
                   jax.ShapeDtypeStruct((t, d), BF16)],
        compiler_params=_cparams("parallel", "arbitrary"),
        name="fox_qk_proj",
    )(x2, g_pre, mod, mod, w_main, gn, w_small)


def _plain_proj(h, w_main, col0, n, tm=1024, tn=1024):
    t, d = h.shape
    off = col0 // tn
    return pl.pallas_call(
        _plain_proj_kernel,
        grid=(t // tm, n // tn),
        in_specs=[pl.BlockSpec((tm, d), lambda i, j: (i, 0)),
                  pl.BlockSpec((d, tn), lambda i, j: (0, off + j))],
        out_specs=pl.BlockSpec((tm, tn), lambda i, j: (i, j)),
        out_shape=jax.ShapeDtypeStruct((t, n), BF16),
        compiler_params=_cparams("parallel", "arbitrary"),
        name="plain_proj",
    )(h, w_main)


def _gdn_qk_proj(x2, g_pre, mod, w_main, conv_w, n, w_small, seq, tm=1024, tn=1024):
    t, d = x2.shape
    taps = conv_w.shape[0]
    return pl.pallas_call(
        functools.partial(_gdn_qk_kernel, blocks_per_seq=seq // tm),
        grid=(t // tm, n // tn),
        in_specs=_x_mod_specs(tm, d, seq // tm) + [
            pl.BlockSpec((d, tn), lambda i, j: (0, j)),
            pl.BlockSpec((taps, tn), lambda i, j: (0, j)),
            pl.BlockSpec((d, HEAD), lambda i, j: (0, 0))],
        out_specs=[pl.BlockSpec((tm, tn), lambda i, j: (i, j)),
                   pl.BlockSpec((tm, HEAD), lambda i, j: (i, 0)),
                   pl.BlockSpec((tm, d), lambda i, j: (i, 0))],
        out_shape=[jax.ShapeDtypeStruct((t, n), BF16), jax.ShapeDtypeStruct((t, HEAD), F32),
                   jax.ShapeDtypeStruct((t, d), BF16)],
        scratch_shapes=[_carry_scratch(tn, n // tn)],
        compiler_params=_cparams("arbitrary", "arbitrary"),
        name="gdn_qk_proj",
    )(x2, g_pre, mod, mod, w_main, conv_w, w_small)


def _gdn_v_proj(h, w_main, conv_w, col0, n, seq, tm=1024, tn=1024):
    t, d = h.shape
    taps = conv_w.shape[0]
    off = col0 // tn
    return pl.pallas_call(
        functools.partial(_gdn_v_kernel, blocks_per_seq=seq // tm),
        grid=(t // tm, n // tn),
        in_specs=[pl.BlockSpec((tm, d), lambda i, j: (i, 0)),
                  pl.BlockSpec((d, tn), lambda i, j: (0, off + j)),
                  pl.BlockSpec((taps, tn), lambda i, j: (0, off + j))],
        out_specs=pl.BlockSpec((tm, tn), lambda i, j: (i, j)),
        out_shape=jax.ShapeDtypeStruct((t, n), BF16),
        scratch_shapes=[_carry_scratch(tn, n // tn)],
        compiler_params=_cparams("arbitrary", "arbitrary"),
        name="gdn_v_proj",
    )(h, w_main, conv_w)


def _ffn_up(x2, g_pre, mod, w_up, layer, conv_w, conv_b, seq, tm=1024, tn=512):
    t, d = x2.shape
    dff = w_up.shape[2] // 2
    nj = dff // tn
    taps = conv_w.shape[0]
    return pl.pallas_call(
        functools.partial(_ffn_up_kernel, blocks_per_seq=seq // tm),
        grid=(t // tm, nj),
        in_specs=_x_mod_specs(tm, d, seq // tm) + [
            pl.BlockSpec((None, d, tn), lambda i, j: (layer, 0, j)),
            pl.BlockSpec((None, d, tn), lambda i, j: (layer, 0, j + nj)),
            pl.BlockSpec((taps, tn), lambda i, j: (0, j)),
            pl.BlockSpec((1, tn), lambda i, j: (0, j))],
        out_specs=pl.BlockSpec((tm, tn), lambda i, j: (i, j)),
        out_shape=jax.ShapeDtypeStruct((t, dff), BF16),
        scratch_shapes=[pltpu.VMEM((tm, d), BF16), _carry_scratch(tn, nj)],
        compiler_params=_cparams("arbitrary", "arbitrary"),
        name="ffn_up",
    )(x2, g_pre, mod, mod, w_up, w_up, conv_w, conv_b)


def _out_proj_kernel(a_ref, w_ref, x_ref, g_ref, gate_ref, o_ref):
    y = _dot(a_ref[...], w_ref[...])
    ms = jnp.mean(y * y, axis=-1, keepdims=True)
    o_ref[...] = x_ref[...] + (y * lax.rsqrt(ms + EPS)) * (gate_ref[...] * g_ref[...])


def _out_proj(a, w, layer, x2, g_post, mod, seq, tm):
    t, kd = a.shape
    d = w.shape[2]
    bps = seq // tm
    return pl.pallas_call(
        _out_proj_kernel,
        grid=(t // tm,),
        in_specs=[pl.BlockSpec((tm, kd), lambda i: (i, 0)),
                  pl.BlockSpec((None, kd, d), lambda i: (layer, 0, 0), pipeline_mode=pl.Buffered(1)),
                  pl.BlockSpec((tm, d), lambda i: (i, 0)),
                  pl.BlockSpec((1, d), lambda i: (0, 0)),
                  pl.BlockSpec((None, 1, d), lambda i: (i // bps, 0, 2))],
        out_specs=pl.BlockSpec((tm, d), lambda i: (i, 0)),
        out_shape=jax.ShapeDtypeStruct((t, d), F32),
        compiler_params=_cparams("parallel"),
        name="out_proj",
    )(a, w, x2, g_post, mod)


GATE_ROWS = 256


def _fox_gate_kernel(f_ref, fb_ref, o_ref, carry_ref):
    @pl.when(pl.program_id(1) == 0)
    def _():
        carry_ref[...] = jnp.zeros_like(carry_ref)

    r = lax.broadcasted_iota(jnp.int32, (GATE_ROWS, GATE_ROWS), 0)
    c = lax.broadcasted_iota(jnp.int32, (GATE_ROWS, GATE_ROWS), 1)
    carry = carry_ref[0:1, :]
    for r0 in range(0, f_ref.shape[0], GATE_ROWS):
        rows = slice(r0, r0 + GATE_ROWS)
        x = f_ref[rows, :] + fb_ref[...]
        logf = (jnp.minimum(x, 0.0) - jnp.log1p(jnp.exp(-jnp.abs(x)))) * LOG2E
        cum = _mask_dot(r >= c, logf) + carry
        o_ref[rows, :] = cum
        carry = cum[GATE_ROWS - 1:GATE_ROWS, :]
    carry_ref[0:1, :] = carry


def _fox_gate_cumsum(f_logit, f_bias_row, nb, seq, tm=1024):
    return pl.pallas_call(
        _fox_gate_kernel,
        grid=(nb, seq // tm),
        in_specs=[pl.BlockSpec((None, tm, HEAD), lambda b, i: (b, i, 0)),
                  pl.BlockSpec((1, HEAD), lambda b, i: (0, 0))],
        out_specs=pl.BlockSpec((None, tm, HEAD), lambda b, i: (b, i, 0)),
        out_shape=jax.ShapeDtypeStruct((nb, seq, HEAD), F32),
        scratch_shapes=[pltpu.VMEM((SUBLANES, HEAD), F32)],
        compiler_params=_cparams("parallel", "arbitrary"),
        name="fox_gate_cumsum",
    )(f_logit.reshape(nb, seq, HEAD), f_bias_row)


def _attn_kernel(q_ref, k_ref, v_ref, og_ref, fk_ref, o_ref,
                 s0_ref, s1_ref, m_ref, l_ref, acc_ref, *, tq):
    h = pl.program_id(1)
    qi = pl.program_id(2)
    m_ref[...] = jnp.full(m_ref.shape, -jnp.inf, F32)
    l_ref[...] = jnp.zeros(l_ref.shape, F32)
    acc_ref[...] = jnp.zeros(acc_ref.shape, F32)

    def scores(s_ref, ki):
        ks = pl.multiple_of(ki * tq, tq)
        s_ref[...] = _dot_nt(q_ref[...], k_ref[pl.ds(ks, tq), :]) - fk_ref[pl.ds(h, 1), pl.ds(ks, tq)]

    def update(s_ref, ki, masked):
        ks = pl.multiple_of(ki * tq, tq)
        s = s_ref[...]
        if masked:
            row = lax.broadcasted_iota(jnp.int32, (tq, tq), 0)
            col = lax.broadcasted_iota(jnp.int32, (tq, tq), 1)
            s = jnp.where(col <= row, s, -jnp.inf)
        m_old = m_ref[...]
        m_new = jnp.maximum(m_old, jnp.max(s, axis=-1, keepdims=True))
        p = jnp.exp2(s - m_new)
        alpha = jnp.exp2(m_old - m_new)
        l_ref[...] = alpha * l_ref[...] + jnp.sum(p, axis=-1, keepdims=True)
        m_ref[...] = m_new
        acc_ref[...] = alpha * acc_ref[...] + _dot(p.astype(BF16), v_ref[pl.ds(ks, tq), :])

    scores(s0_ref, 0)

    def pair(pi, carry):
        scores(s1_ref, 2 * pi + 1)
        update(s0_ref, 2 * pi, False)
        scores(s0_ref, 2 * pi + 2)
        update(s1_ref, 2 * pi + 1, False)
        return carry

    lax.fori_loop(0, qi // 2, pair, 0)

    @pl.when(qi % 2 == 1)
    def _():
        scores(s1_ref, qi)
        update(s0_ref, qi - 1, False)
        update(s1_ref, qi, True)

    @pl.when(qi % 2 == 0)
    def _():
        update(s0_ref, qi, True)

    o_ref[...] = (acc_ref[...] / l_ref[...] * jax.nn.sigmoid(og_ref[...].astype(F32))).astype(o_ref.dtype)


def _attn_fixed_shift_kernel(q_ref, k_ref, v_ref, og_ref, fc_ref, c_ref, o_ref,
                             ka_ref, va_ref, qa_ref, p0_ref, p1_ref, acc_ref, *, tq):
    h = pl.program_id(1)
    step = pl.program_id(2)
    seq = k_ref.shape[0]
    nq = seq // tq
    q_blocks = (step, nq - 1 - step)

    def block_rows(i):
        return pl.ds(pl.multiple_of(i * tq, tq), tq)

    def head_col(rows):
        lane = lax.broadcasted_iota(jnp.int32, (tq, HEAD), 1)
        return jnp.sum(jnp.where(lane == h, fc_ref[rows, :], 0.0), axis=-1, keepdims=True)

    def extra_cols(first, second):
        lane = lax.broadcasted_iota(jnp.int32, (tq, HEAD), 1)
        cols = jnp.zeros((tq, HEAD), F32)
        for n, term in enumerate(first + second):
            cols = jnp.where(lane == n, term, cols)
        return cols.astype(BF16)

    ones3 = (1.0, 1.0, 1.0)

    @pl.when(step == 0)
    def _():
        def build(i, carry):
            rows = block_rows(i)
            hi, mid, lo = _split3(-head_col(rows))
            ka_ref[rows, 0:HEAD] = k_ref[rows, :]
            ka_ref[rows, HEAD:2 * HEAD] = extra_cols((hi, mid, lo), ones3)
            va_ref[rows, 0:HEAD] = v_ref[rows, :]
            va_ref[rows, HEAD:2 * HEAD] = jnp.ones((tq, HEAD), BF16)
            return carry
        lax.fori_loop(0, nq, build, 0)

    for n, qb in enumerate(q_blocks):
        rows = block_rows(qb)
        qa_ref[n] = jnp.concatenate(
            [q_ref[rows, :], extra_cols(ones3, _split3(head_col(rows) - c_ref[0:1, 0:1]))], axis=1)
    acc_ref[...] = jnp.zeros(acc_ref.shape, F32)

    def probs(p_ref, which, kb, masked):
        s = _dot_nt(qa_ref[which], ka_ref[block_rows(kb), :])
        if masked:
            row = lax.broadcasted_iota(jnp.int32, (tq, tq), 0)
            col = lax.broadcasted_iota(jnp.int32, (tq, tq), 1)
            s = jnp.where(col <= row, s, -jnp.inf)
        p_ref[...] = jnp.exp2(s).astype(BF16)

    def accumulate(p_ref, which, kb):
        acc_ref[which] += _dot(p_ref[...], va_ref[block_rows(kb), :])

    items = [(0, q_blocks[0], True), (1, q_blocks[1], True)]
    for n in range(nq - 1):
        which = (n >= q_blocks[0]).astype(jnp.int32)
        items.append((which, n - which * q_blocks[0], False))
    p_refs = (p0_ref, p1_ref)
    for n, (which, kb, masked) in enumerate(items):
        probs(p_refs[n % 2], which, kb, masked)
        if n > 0:
            accumulate(p_refs[(n - 1) % 2], *items[n - 1][:2])
    accumulate(p_refs[(len(items) - 1) % 2], *items[-1][:2])

    for n, qb in enumerate(q_blocks):
        rows = block_rows(qb)
        acc = acc_ref[n]
        o_ref[rows, :] = (acc[:, 0:HEAD] / acc[:, HEAD:HEAD + 1]
                          * jax.nn.sigmoid(og_ref[rows, :].astype(F32))).astype(o_ref.dtype)


def _fox_attention_fixed_shift(qk3, v3, og3, fcum, c_row, n_heads, tq=512):
    nb, seq, _ = qk3.shape
    nq = seq // tq
    assert nq % 2 == 0
    full = lambda off: pl.BlockSpec((None, seq, HEAD), lambda b, h, i: (b, 0, off + h))
    return pl.pallas_call(
        functools.partial(_attn_fixed_shift_kernel, tq=tq),
        grid=(nb, n_heads, nq // 2),
        in_specs=[full(0), full(n_heads), full(0), full(0),
                  pl.BlockSpec((None, seq, HEAD), lambda b, h, i: (b, 0, 0)),
                  pl.BlockSpec((1, HEAD), lambda b, h, i: (0, 0))],
        out_specs=full(0),
        out_shape=jax.ShapeDtypeStruct((nb, seq, n_heads * HEAD), BF16),
        scratch_shapes=[pltpu.VMEM((seq, 2 * HEAD), BF16), pltpu.VMEM((seq, 2 * HEAD), BF16),
                        pltpu.VMEM((2, tq, 2 * HEAD), BF16),
                        pltpu.VMEM((tq, tq), BF16), pltpu.VMEM((tq, tq), BF16),
                        pltpu.VMEM((2, tq, 2 * HEAD), F32)],
        compiler_params=_cparams("parallel", "parallel", "arbitrary"),
        name="fox_attention_fixed_shift",
    )(qk3, qk3, v3, og3, fcum, c_row)


ITEM_STRIDE = 32
LIST_UNROLL = 4


def _attn_block_list_kernel(items_ref, count_ref, q_ref, k_ref, v_ref, og_ref, fc_ref, c_ref, o_ref,
                            ka_ref, va_ref, qa_ref, p0_ref, p1_ref, acc_ref, *, tq):
    h = pl.program_id(1)
    bh = pl.program_id(0) * pl.num_programs(1) + h
    seq = k_ref.shape[0]
    nq = seq // tq

    def block_rows(i):
        return pl.ds(pl.multiple_of(i * tq, tq), tq)

    lane1 = lax.broadcasted_iota(jnp.int32, (1, HEAD), 1)
    c_hi, c_mid, c_lo = _split3(c_ref[...])
    k_fixed = jnp.where((lane1 >= 3) & (lane1 < 6), 1.0,
                        jnp.where(lane1 == 6, -c_hi, jnp.where(lane1 == 7, -c_mid,
                                                               jnp.where(lane1 == 8, -c_lo, 0.0))))
    q_fixed = jnp.where((lane1 < 3) | ((lane1 >= 6) & (lane1 < 9)), 1.0, 0.0)

    def build(i, carry):
        rows = block_rows(i)
        lane = lax.broadcasted_iota(jnp.int32, (tq, HEAD), 1)
        f = jnp.sum(jnp.where(lane == h, fc_ref[rows, :], 0.0), axis=-1, keepdims=True)
        hi, mid, lo = _split3(f)
        k_cols = jnp.where(lane == 0, -hi, jnp.where(lane == 1, -mid, jnp.where(lane == 2, -lo, k_fixed)))
        q_cols = jnp.where(lane == 3, hi, jnp.where(lane == 4, mid, jnp.where(lane == 5, lo, q_fixed)))
        ka_ref[rows, 0:HEAD] = k_ref[rows, :]
        ka_ref[rows, HEAD:2 * HEAD] = k_cols.astype(BF16)
        va_ref[rows, 0:HEAD] = v_ref[rows, :]
        va_ref[rows, HEAD:2 * HEAD] = jnp.ones((tq, HEAD), BF16)
        qa_ref[rows, 0:HEAD] = q_ref[rows, :]
        qa_ref[rows, HEAD:2 * HEAD] = q_cols.astype(BF16)
        return carry

    lax.fori_loop(0, nq, build, 0)
    acc_ref[nq] = jnp.zeros(acc_ref.shape[1:], F32)

    def probs(p_ref, qb, kb, masked):
        qb = jnp.minimum(qb, nq - 1)
        s = _dot_nt(qa_ref[block_rows(qb), :], ka_ref[block_rows(kb), :])
        if masked:
            row = lax.broadcasted_iota(jnp.int32, (tq, tq), 0)
            col = lax.broadcasted_iota(jnp.int32, (tq, tq), 1)
            s = jnp.where(col <= row, s, -jnp.inf)
        p_ref[...] = jnp.exp2(s).astype(BF16)

    def product(p_ref, kb):
        return _dot(p_ref[...], va_ref[block_rows(kb), :])

    p_refs = (p0_ref, p1_ref)
    for i in range(nq):
        probs(p_refs[i % 2], i, i, True)
        if i > 0:
            acc_ref[i - 1] = product(p_refs[(i - 1) % 2], i - 1)
    acc_ref[nq - 1] = product(p_refs[(nq - 1) % 2], nq - 1)

    def item(n):
        code = items_ref[bh, n]
        return code // ITEM_STRIDE, code % ITEM_STRIDE

    def accumulate(p_ref, n):
        qb, kb = item(n)
        acc_ref[qb] += product(p_ref, kb)

    probs(p0_ref, *item(0), False)

    def group(m, carry):
        for u in range(LIST_UNROLL):
            n = LIST_UNROLL * m + u
            probs(p_refs[(u + 1) % 2], *item(n + 1), False)
            accumulate(p_refs[u % 2], n)
        return carry

    lax.fori_loop(0, count_ref[bh] // LIST_UNROLL, group, 0)

    def finish(i, carry):
        rows = block_rows(i)
        acc = acc_ref[i]
        o_ref[rows, :] = (acc[:, 0:HEAD] / acc[:, HEAD:HEAD + 1]
                          * jax.nn.sigmoid(og_ref[rows, :].astype(F32))).astype(o_ref.dtype)
        return carry

    lax.fori_loop(0, nq, finish, 0)


def _fox_attention_block_list(qk3, v3, og3, fcum, c_row, items, counts, n_heads, tq=512):
    nb, seq, _ = qk3.shape
    nq = seq // tq
    full = lambda off: pl.BlockSpec((None, seq, HEAD), lambda b, h, *_: (b, 0, off + h))
    grid_spec = pltpu.PrefetchScalarGridSpec(
        num_scalar_prefetch=2,
        grid=(nb, n_heads),
        in_specs=[full(0), full(n_heads), full(0), full(0),
                  pl.BlockSpec((None, seq, HEAD), lambda b, h, *_: (b, 0, 0), pipeline_mode=pl.Buffered(1)),
                  pl.BlockSpec((1, HEAD), lambda b, h, *_: (0, 0))],
        out_specs=full(0),
        scratch_shapes=[pltpu.VMEM((seq, 2 * HEAD), BF16), pltpu.VMEM((seq, 2 * HEAD), BF16),
                        pltpu.VMEM((seq, 2 * HEAD), BF16),
                        pltpu.VMEM((tq, tq), BF16), pltpu.VMEM((tq, tq), BF16),
                        pltpu.VMEM((nq + 1, tq, 2 * HEAD), F32)])
    return pl.pallas_call(
        functools.partial(_attn_block_list_kernel, tq=tq),
        grid_spec=grid_spec,
        out_shape=jax.ShapeDtypeStruct((nb, seq, n_heads * HEAD), BF16),
        compiler_params=_cparams("parallel", "arbitrary"),
        name="fox_attention_block_list",
    )(items, counts, qk3, qk3, v3, og3, fcum, c_row)


MAX_SHIFT_GAP = 100.0
SKIP_LOG2 = 136.0
LIST_MAX_FRACTION = 0.75


def _needed_blocks(fcum, n_heads, tq):
    nb, seq, _ = fcum.shape
    nq = seq // tq
    f_first = fcum[:, 0::tq, :n_heads]
    f_last = fcum[:, tq - 1::tq, :n_heads]
    bound = f_first[:, :, None, :] - f_last[:, None, :, :]
    i = jnp.arange(nq, dtype=jnp.int32)[:, None]
    j = jnp.arange(nq, dtype=jnp.int32)[None, :]
    need = (bound > -SKIP_LOG2) & (j < i)[None, :, :, None]
    need = jnp.transpose(need, (0, 3, 1, 2)).reshape(nb * n_heads, nq * nq)
    pad_code = nq * ITEM_STRIDE
    codes = jnp.where(need, (i * ITEM_STRIDE + j).reshape(1, nq * nq), pad_code)
    codes = jnp.sort(codes, axis=-1)
    codes = jnp.concatenate([codes, jnp.full((nb * n_heads, LIST_UNROLL), pad_code, jnp.int32)], axis=-1)
    counts = jnp.sum(need, axis=-1).astype(jnp.int32)
    return codes.astype(jnp.int32), counts + (-counts) % LIST_UNROLL, jnp.sum(counts)


def _fox_attention(qk3, v3, og3, fcum, gn, n_heads, tq=512):
    nb, seq, _ = qk3.shape
    nq = seq // tq
    qd = n_heads * HEAD
    c = (HEAD * 1.01) * jnp.max(jnp.abs(gn[:, :qd])) * jnp.max(jnp.abs(gn[:, qd:]))
    c_row = jnp.full((1, HEAD), c, F32)
    items, counts, total = _needed_blocks(fcum, n_heads, tq)
    all_pairs = nb * n_heads * (nq * (nq - 1) // 2)

    def block_list(qk3, v3, og3, fcum, c_row, items, counts):
        return _fox_attention_block_list(qk3, v3, og3, fcum, c_row, items, counts, n_heads, tq)

    def full_stream(qk3, v3, og3, fcum, c_row, items, counts):
        return _fox_attention_fixed_shift(qk3, v3, og3, fcum, c_row, n_heads, tq)

    def running_max(qk3, v3, og3, fcum, c_row, items, counts):
        fk = jnp.transpose(fcum[:, :, :n_heads], (0, 2, 1))
        return _fox_attention_running_max(qk3, v3, og3, fk, n_heads, tq)

    branch = jnp.where(2.0 * c > MAX_SHIFT_GAP, 2,
                       jnp.where(total <= LIST_MAX_FRACTION * all_pairs, 0, 1)).astype(jnp.int32)
    return lax.switch(branch, (block_list, full_stream, running_max), qk3, v3, og3, fcum, c_row, items, counts)


def _fox_attention_running_max(qk3, v3, og3, fk, n_heads, tq=512):
    nb, seq, _ = qk3.shape
    return pl.pallas_call(
        functools.partial(_attn_kernel, tq=tq),
        grid=(nb, n_heads, seq // tq),
        in_specs=[pl.BlockSpec((None, tq, HEAD), lambda b, h, i: (b, i, h)),
                  pl.BlockSpec((None, seq, HEAD), lambda b, h, i: (b, 0, n_heads + h)),
                  pl.BlockSpec((None, seq, HEAD), lambda b, h, i: (b, 0, h)),
                  pl.BlockSpec((None, tq, HEAD), lambda b, h, i: (b, i, h)),
                  pl.BlockSpec((None, n_heads, seq), lambda b, h, i: (b, 0, 0))],
        out_specs=pl.BlockSpec((None, tq, HEAD), lambda b, h, i: (b, i, h)),
        out_shape=jax.ShapeDtypeStruct((nb, seq, n_heads * HEAD), BF16),
        scratch_shapes=[pltpu.VMEM((tq, tq), F32), pltpu.VMEM((tq, tq), F32),
                        pltpu.VMEM((tq, 1), F32), pltpu.VMEM((tq, 1), F32), pltpu.VMEM((tq, HEAD), F32)],
        compiler_params=_cparams("parallel", "parallel", "arbitrary"),
        name="fox_attention",
    )(qk3, qk3, v3, og3, fk)


def _gdn_gate_kernel(ab_ref, alog_ref, dt_ref, o_ref, *, n_heads):
    lane = lax.broadcasted_iota(jnp.int32, (GATE_ROWS, HEAD), 1)
    r = lax.broadcasted_iota(jnp.int32, (GATE_ROWS, GATE_ROWS), 0)
    c = lax.broadcasted_iota(jnp.int32, (GATE_ROWS, GATE_ROWS), 1)
    mask = (r >= c) & ((r // CHUNK) == (c // CHUNK))
    for r0 in range(0, ab_ref.shape[0], GATE_ROWS):
        rows = slice(r0, r0 + GATE_ROWS)
        x = ab_ref[rows, :]
        g = -jnp.exp(alog_ref[...]) * _softplus(x + dt_ref[...])
        gc = _mask_dot(mask, jnp.where(lane < n_heads, g, 0.0))
        o_ref[rows, :] = jnp.where(lane < n_heads, gc, jax.nn.sigmoid(x))


def _gdn_gates(ab, alog_row, dt_row, n_heads, tm=1024):
    t = ab.shape[0]
    return pl.pallas_call(
        functools.partial(_gdn_gate_kernel, n_heads=n_heads),
        grid=(t // tm,),
        in_specs=[pl.BlockSpec((tm, HEAD), lambda i: (i, 0)),
                  pl.BlockSpec((1, HEAD), lambda i: (0, 0)),
                  pl.BlockSpec((1, HEAD), lambda i: (0, 0))],
        out_specs=pl.BlockSpec((tm, HEAD), lambda i: (i, 0)),
        out_shape=jax.ShapeDtypeStruct((t, HEAD), F32),
        compiler_params=_cparams("parallel"),
        name="gdn_gates",
    )(ab, alog_row, dt_row)


def _block_diag2(p, half):
    return jnp.concatenate([jnp.where(half, 0.0, p), jnp.where(half, p, 0.0)], axis=0)


def _gdn_chunk_kernel(q_ref, k_ref, v_ref, gb_ref, grow_ref, u_ref, w_ref, qd_ref, kd_ref, qk_ref, egl_ref,
                      *, n_vheads, chunks):
    hk = pl.program_id(1)
    lane = lax.broadcasted_iota(jnp.int32, (CHUNK, HEAD), 1)
    row = lax.broadcasted_iota(jnp.int32, (CHUNK, HEAD), 0)
    col = lane & (CHUNK - 1)
    half = lane >= CHUNK
    lower = row >= col
    strict = row > col
    eye = (row == col).astype(F32)
    level_masks = []
    for lvl in range(6):
        level_masks.append((((row >> lvl) & 1) == 1) & (((col >> lvl) & 1) == 0)
                           & ((row >> (lvl + 1)) == (col >> (lvl + 1))))
    zeros = jnp.zeros((CHUNK, HEAD), F32)

    def pick(gb, idx):
        return jnp.sum(jnp.where(lane == idx, gb, 0.0), axis=-1, keepdims=True)

    sls = [slice(c * CHUNK, (c + 1) * CHUNK) for c in range(chunks)]
    def l2_normalised(t, scale):
        t = t.astype(F32)
        return (t * (lax.rsqrt(jnp.sum(t * t, axis=-1, keepdims=True) + EPS) * scale)).astype(BF16)

    gcs, bes, a_s, xs, qs, ks = [], [], [], [], [], []
    for c, sl in enumerate(sls):
        q = l2_normalised(q_ref[sl, :], HEAD ** -0.5)
        k = l2_normalised(k_ref[sl, :], 1.0)
        qs.append(q)
        ks.append(k)
        gb = gb_ref[sl, :]
        gc0, gc1 = pick(gb, 2 * hk), pick(gb, 2 * hk + 1)
        be0, be1 = pick(gb, n_vheads + 2 * hk), pick(gb, n_vheads + 2 * hk + 1)
        gcol = jnp.where(half, gc1, gc0)
        bcol = jnp.where(half, be1, be0)
        decay = jnp.exp(jnp.where(lower, gcol - grow_ref[c:c + 1, :], -jnp.inf))
        qkk = _dot_nt(jnp.concatenate([q, k], axis=0), jnp.concatenate([k, k], axis=0))
        a = jnp.where(strict, qkk[CHUNK:, :] * decay * bcol, 0.0)
        qk_ref[sl, :] = (qkk[:CHUNK, :] * decay).astype(qk_ref.dtype)
        gcs.append((gc0, gc1))
        bes.append((be0, be1))
        a_s.append(a)
        xs.append(eye - jnp.where(level_masks[0], a, 0.0))

    for lvl in range(1, 6):
        ys = [_dot(xs[c].astype(BF16), _block_diag2(jnp.where(level_masks[lvl], a_s[c], 0.0), half).astype(BF16))
              for c in range(chunks)]
        xs = [xs[c] - _dot(ys[c].astype(BF16), _block_diag2(xs[c], half).astype(BF16)) for c in range(chunks)]

    for c, sl in enumerate(sls):
        (gc0, gc1), (be0, be1) = gcs[c], bes[c]
        kf = ks[c].astype(F32)
        qf = qs[c].astype(F32)
        v = v_ref[sl, :].astype(F32)
        eg0, eg1 = jnp.exp(gc0), jnp.exp(gc1)
        rhs = jnp.concatenate([
            jnp.concatenate([v[:, :HEAD] * be0, kf * (be0 * eg0), zeros, zeros], axis=1),
            jnp.concatenate([zeros, zeros, v[:, HEAD:] * be1, kf * (be1 * eg1)], axis=1)], axis=0)
        sol = _dot(xs[c].astype(BF16), rhs.astype(BF16))
        u_ref[sl, :] = jnp.concatenate([sol[:, 0:HEAD], sol[:, 2 * HEAD:3 * HEAD]], axis=1).astype(u_ref.dtype)
        w_ref[sl, :] = jnp.concatenate([sol[:, HEAD:2 * HEAD], sol[:, 3 * HEAD:]], axis=1).astype(w_ref.dtype)
        qd_ref[sl, :] = jnp.concatenate([qf * eg0, qf * eg1], axis=1).astype(qd_ref.dtype)
        gl0, gl1 = gc0[CHUNK - 1:CHUNK, :], gc1[CHUNK - 1:CHUNK, :]
        kd_ref[sl, :] = jnp.concatenate([kf * jnp.exp(gl0 - gc0), kf * jnp.exp(gl1 - gc1)],
                                        axis=1).astype(kd_ref.dtype)
        egl_ref[0, c:c + 1, :] = jnp.broadcast_to(jnp.exp(gl0), (1, HEAD))
        egl_ref[1, c:c + 1, :] = jnp.broadcast_to(jnp.exp(gl1), (1, HEAD))


def _gdn_chunks(qk3, v3, gb3, grow, n_kheads, n_vheads, tm=2048):
    nb, seq, _ = qk3.shape
    chunks = tm // CHUNK
    vd = n_vheads * HEAD
    big = jax.ShapeDtypeStruct((nb, seq, vd), BF16)
    pair_spec = pl.BlockSpec((None, tm, 2 * HEAD), lambda b, h, i: (b, i, h))
    return pl.pallas_call(
        functools.partial(_gdn_chunk_kernel, n_vheads=n_vheads, chunks=chunks),
        grid=(nb, n_kheads, seq // tm),
        in_specs=[pl.BlockSpec((None, tm, HEAD), lambda b, h, i: (b, i, h)),
                  pl.BlockSpec((None, tm, HEAD), lambda b, h, i: (b, i, n_kheads + h)),
                  pair_spec,
                  pl.BlockSpec((None, tm, HEAD), lambda b, h, i: (b, i, 0)),
                  pl.BlockSpec((None, None, chunks, HEAD), lambda b, h, i: (b, h, i, 0))],
        out_specs=[pair_spec, pair_spec, pair_spec, pair_spec,
                   pl.BlockSpec((None, tm, HEAD), lambda b, h, i: (b, i, h)),
                   pl.BlockSpec((None, 2, chunks, HEAD), lambda b, h, i: (b, h, i, 0))],
        out_shape=[big, big, big, big,
                   jax.ShapeDtypeStruct((nb, seq, n_kheads * HEAD), BF16),
                   jax.ShapeDtypeStruct((nb, n_vheads, seq // CHUNK, HEAD), F32)],
        compiler_params=_cparams("parallel", "parallel", "parallel"),
        name="gdn_chunks",
    )(qk3, qk3, v3, gb3, grow)


def _gdn_scan_kernel(u_ref, w_ref, qd_ref, kd_ref, qk_ref, egl_ref, z_ref, gn_ref, o_ref, s_ref,
                     *, heads, chunks):
    @pl.when(pl.program_id(2) == 0)
    def _():
        s_ref[...] = jnp.zeros_like(s_ref)

    zeros = jnp.zeros((CHUNK, HEAD), F32)
    gn = gn_ref[...]
    cols = [slice(g * HEAD, (g + 1) * HEAD) for g in range(heads)]
    for c in range(chunks):
        sl = slice(c * CHUNK, (c + 1) * CHUNK)
        rs = [_dot(jnp.concatenate([w_ref[sl, cols[g]], qd_ref[sl, cols[g]]], axis=0), s_ref[g].astype(BF16))
              for g in range(heads)]
        vnew = [u_ref[sl, cols[g]].astype(F32) - rs[g][:CHUNK, :] for g in range(heads)]
        intra = []
        for pr in range(heads // 2):
            v2 = jnp.concatenate([jnp.concatenate([vnew[2 * pr], zeros], axis=1),
                                  jnp.concatenate([zeros, vnew[2 * pr + 1]], axis=1)], axis=0).astype(BF16)
            intra.append(_dot(qk_ref[sl, pr * HEAD:(pr + 1) * HEAD], v2))
        for g in range(heads):
            s_ref[g] = s_ref[g] * egl_ref[g, c:c + 1, :] + _dot_tn(kd_ref[sl, cols[g]], vnew[g].astype(BF16))
        for g in range(heads):
            o = rs[g][CHUNK:, :] + intra[g // 2][:, (g % 2) * HEAD:(g % 2 + 1) * HEAD]
            ms = jnp.mean(o * o, axis=-1, keepdims=True)
            z = z_ref[sl, cols[g]].astype(F32)
            o_ref[sl, cols[g]] = (o * lax.rsqrt(ms + EPS) * gn * (z * jax.nn.sigmoid(z))).astype(o_ref.dtype)


def _gdn_scan(u, w, qd, kd, qkm, egl, z3, gn_row, n_vheads, heads=16, tm=512):
    nb, seq, vd = u.shape
    chunks = tm // CHUNK
    wide = pl.BlockSpec((None, tm, heads * HEAD), lambda b, g, i: (b, i, g))
    return pl.pallas_call(
        functools.partial(_gdn_scan_kernel, heads=heads, chunks=chunks),
        grid=(nb, n_vheads // heads, seq // tm),
        in_specs=[wide, wide, wide, wide,
                  pl.BlockSpec((None, tm, heads // 2 * HEAD), lambda b, g, i: (b, i, g)),
                  pl.BlockSpec((None, heads, chunks, HEAD), lambda b, g, i: (b, g, i, 0)),
                  wide,
                  pl.BlockSpec((1, HEAD), lambda b, g, i: (0, 0))],
        out_specs=wide,
        out_shape=jax.ShapeDtypeStruct((nb, seq, vd), BF16),
        scratch_shapes=[pltpu.VMEM((heads, HEAD, HEAD), F32)],
        compiler_params=_cparams("parallel", "parallel", "arbitrary"),
        name="gdn_scan",
    )(u, w, qd, kd, qkm, egl, z3, gn_row)


def _pad_cols(w, n):
    return jnp.pad(w, ((0, 0), (0, n - w.shape[1])))


def _row(v, n=HEAD):
    return jnp.pad(v.astype(F32), (0, n - v.shape[0])).reshape(1, n)


def _fox_layer(x2, nb, seq, g_pre, g_post, mod, w_in, f_bias, q_norm, k_norm, w_o, layer):
    d = x2.shape[1]
    nh = d // HEAD
    qd = nh * HEAD
    w_small = _pad_cols(w_in[:, 3 * qd:3 * qd + nh], HEAD)
    w_gate = w_in[:, 3 * qd + nh:]
    gn = jnp.concatenate([jnp.tile(q_norm.astype(F32) * (HEAD ** -0.5 * LOG2E), nh),
                          jnp.tile(k_norm.astype(F32), nh)]).reshape(1, 2 * qd)
    qk, f_logit, h = _fox_qk_proj(x2, g_pre, mod, w_in, gn, w_small, seq)
    v = _plain_proj(h, w_in, 2 * qd, qd)
    og = _plain_proj(h, w_gate, 0, qd)
    fcum = _fox_gate_cumsum(f_logit, _row(f_bias), nb, seq)
    attn = _fox_attention(qk.reshape(nb, seq, 2 * qd), v.reshape(nb, seq, qd), og.reshape(nb, seq, qd),
                          fcum, gn, nh)
    return _out_proj(attn.reshape(nb * seq, qd), w_o, layer, x2, g_post, mod, seq, tm=512)


def _gdn_layer(x2, nb, seq, g_pre, g_post, mod, w_in, conv_w, a_log, dt_bias, out_norm, w_o, layer):
    d = x2.shape[1]
    nk = d // HEAD
    nv = 2 * nk
    kd, vd = nk * HEAD, nv * HEAD
    n_main = 2 * kd + 2 * vd
    w_small = _pad_cols(w_in[:, n_main:], HEAD)
    conv_w = conv_w.astype(F32)
    qk, ab, h = _gdn_qk_proj(x2, g_pre, mod, w_in, conv_w, 2 * kd, w_small, seq)
    v = _gdn_v_proj(h, w_in, conv_w, 2 * kd, vd, seq)
    z = _plain_proj(h, w_in, 2 * kd + vd, vd)
    gb = _gdn_gates(ab, _row(a_log), _row(dt_bias), nv)
    n_chunks = seq // CHUNK
    grow = gb[:, :nv].reshape(nb, n_chunks, CHUNK, nk, 2)
    grow = jnp.transpose(grow, (0, 3, 1, 4, 2)).reshape(nb, nk, n_chunks, 2 * CHUNK)
    u, w, qdec, kdec, qkm, egl = _gdn_chunks(qk.reshape(nb, seq, 2 * kd), v.reshape(nb, seq, vd),
                                             gb.reshape(nb, seq, HEAD), grow, nk, nv)
    a = _gdn_scan(u, w, qdec, kdec, qkm, egl, z.reshape(nb, seq, vd), _row(out_norm), nv)
    return _out_proj(a.reshape(nb * seq, vd), w_o, layer, x2, g_post, mod, seq, tm=512)


def _ffn_layer(x2, seq, g_pre, g_post, mod, w_up, conv_w, conv_b, w_down, layer):
    u = _ffn_up(x2, g_pre, mod, w_up, layer, conv_w.astype(F32), conv_b.reshape(1, -1).astype(F32), seq)
    return _out_proj(u, w_down, layer, x2, g_post, mod, seq, tm=256)


def kernel(x, c, ada_w, ada_b, norm_g, fox_w_in, fox_f_bias, fox_q_norm, fox_k_norm, fox_w_o,
           gdn_w_in, gdn_conv_w, gdn_a_log, gdn_dt_bias, gdn_out_norm, gdn_w_o,
           ffn_w_up, ffn_conv_w, ffn_conv_b, ffn_w_down):
    nb, seq, d = x.shape
    depth = ada_w.shape[0]
    mods = _ada_mods(c, ada_w, ada_b)
    fox_w_in, fox_w_o, gdn_w_in, gdn_w_o, ffn_w_up, ffn_w_down = (
        w.astype(BF16) for w in (fox_w_in, fox_w_o, gdn_w_in, gdn_w_o, ffn_w_up, ffn_w_down))
    x2 = x.reshape(nb * seq, d)
    for i in range(depth):
        g = [norm_g[i, r].reshape(1, d).astype(F32) for r in range(4)]
        j = i // 2
        if i % 2 == 0:
            x2 = _fox_layer(x2, nb, seq, g[0], g[1], mods[2 * i], fox_w_in[j], fox_f_bias[j],
                            fox_q_norm[j], fox_k_norm[j], fox_w_o, j)
        else:
            x2 = _gdn_layer(x2, nb, seq, g[0], g[1], mods[2 * i], gdn_w_in[j], gdn_conv_w[j],
                            gdn_a_log[j], gdn_dt_bias[j], gdn_out_norm[j], gdn_w_o, j)
        x2 = _ffn_layer(x2, seq, g[2], g[3], mods[2 * i + 1], ffn_w_up, ffn_conv_w[i],
                        ffn_conv_b[i], ffn_w_down, i)
    return x2.reshape(nb, seq, d)
```
